```python
import functools
import jax
import jax.numpy as jnp
from jax import lax
import numpy as np

D_MODEL = 1024
BATCH = 32
SEQ = 256
DEPTH = 1
DEC_BATCH = 8
DEC_SEQ = 2048
PAST_LEN = 512

GRID_W = 64
HEAD_DIM = 128
MIX_WIDTH = D_MODEL
DN_HEADS = MIX_WIDTH // (2 * HEAD_DIM)
DN_WIDTH = DN_HEADS * HEAD_DIM
DN_CONV = 5
DN_CHUNK = 64
ATT_HEADS = (MIX_WIDTH - DN_WIDTH) // HEAD_DIM
ATT_KV_HEADS = 2
ATT_GROUP = ATT_HEADS // ATT_KV_HEADS
ATT_Q = ATT_HEADS * HEAD_DIM
ATT_KV = ATT_KV_HEADS * HEAD_DIM
WINDOW = 128
ATT_BLOCK = 128
ROPE_THETA = 10000.0
N_EXPERTS = 256
TOP_K = 8
N_GROUPS = 8
TOPK_GROUPS = 4
EXPERT_FF = D_MODEL // 4
SHARED_FF = EXPERT_FF
ROUTED_SCALE = 2.5
EXPERT_ROWS = 128
NORM_EPS = 1e-6
NEG_INF = -1e30
SPLIT_SIZES = (DN_WIDTH, DN_WIDTH, DN_WIDTH, DN_WIDTH, 2 * DN_HEADS, 2 * DN_HEADS, ATT_Q, ATT_KV, ATT_KV)
IN_COLS = sum(SPLIT_SIZES)

kernel_name = "hybrid_deltanet_swa_moe_diffusion_step"


def rmsnorm(x, w):
    x32 = x.astype(jnp.float32)
    y = x32 * lax.rsqrt(jnp.mean(x32 * x32, axis=-1, keepdims=True) + NORM_EPS)
    return (y * w.astype(jnp.float32)).astype(x.dtype)


def l2norm(x):
    return x * lax.rsqrt(jnp.sum(x * x, axis=-1, keepdims=True) + NORM_EPS)


def split_columns(p):
    cuts = [int(v) for v in np.cumsum(SPLIT_SIZES)[:-1]]
    return jnp.split(p, cuts, axis=-1)


def centred_depthwise_conv(x, w):
    pad = DN_CONV // 2
    return lax.conv_general_dilated(
        x, w[:, None, :].astype(x.dtype), window_strides=(1,), padding=[(pad, pad)],
        dimension_numbers=("NWC", "WIO", "NWC"), feature_group_count=x.shape[-1])


def chunk_gated_delta(q, k, v, g, beta, s0):
    B, T, H, dk = q.shape
    dv = v.shape[-1]
    n = T // DN_CHUNK

    def to_chunks(t):
        t = t.reshape((B, n, DN_CHUNK, H) + t.shape[3:])
        return jnp.moveaxis(jnp.moveaxis(t, 3, 2), 1, 0)

    q = to_chunks(q * dk ** -0.5)
    k = to_chunks(k)
    v = to_chunks(v)
    beta = to_chunks(beta)
    g = jnp.cumsum(to_chunks(g), axis=-1)
    incl = jnp.tril(jnp.ones((DN_CHUNK, DN_CHUNK), bool))
    strict = jnp.tril(jnp.ones((DN_CHUNK, DN_CHUNK), bool), -1)
    diff = g[..., :, None] - g[..., None, :]
    decay = jnp.where(incl, jnp.exp(jnp.where(incl, diff, 0.0)), 0.0)
    kb = k * beta[..., None]
    a_mat = jnp.where(strict, jnp.einsum("nbhcd,nbhsd->nbhcs", kb, k) * decay, 0.0) + jnp.eye(DN_CHUNK, dtype=jnp.float32)
    rhs = jnp.concatenate([v * beta[..., None], kb * jnp.exp(g)[..., None]], axis=-1)
    sol = lax.linalg.triangular_solve(a_mat, rhs, left_side=True, lower=True, unit_diagonal=True)
    u, w = sol[..., :dv], sol[..., dv:]
    qk = jnp.einsum("nbhcd,nbhsd->nbhcs", q, k) * decay

    def step(s, xs):
        q_c, k_c, u_c, w_c, qk_c, g_c = xs
        v_new = u_c - jnp.einsum("bhcd,bhde->bhce", w_c, s)
        o_c = (jnp.einsum("bhcd,bhde->bhce", q_c * jnp.exp(g_c)[..., None], s)
               + jnp.einsum("bhcs,bhse->bhce", qk_c, v_new))
        g_last = g_c[..., -1:]
        s = (s * jnp.exp(g_last)[..., None]
             + jnp.einsum("bhcd,bhce->bhde", k_c * jnp.exp(g_last - g_c)[..., None], v_new))
        return s, o_c

    s_final, o = lax.scan(step, s0, (q, k, u, w, qk, g))
    o = jnp.moveaxis(jnp.moveaxis(o, 0, 1), 2, 3).reshape(B, T, H, dv)
    return o, s_final


def gated_deltanet(q, k, v, z, a, b, lp, s0):
    B, T, _ = q.shape
    out_dtype = q.dtype
    qkv = jax.nn.silu(centred_depthwise_conv(jnp.concatenate([q, k, v], axis=-1), lp["dn_conv"]))
    qkv = qkv.astype(jnp.float32).reshape(B, T, 3, DN_HEADS, HEAD_DIM)
    qh, kh, vh = l2norm(qkv[:, :, 0]), l2norm(qkv[:, :, 1]), qkv[:, :, 2]
    a = a.astype(jnp.float32).reshape(B, T, 2, DN_HEADS)
    b = b.astype(jnp.float32).reshape(B, T, 2, DN_HEADS)
    g = -jnp.exp(lp["dn_A_log"].astype(jnp.float32)) * jax.nn.softplus(a + lp["dn_dt_bias"].astype(jnp.float32))
    beta = jax.nn.sigmoid(b)
    s0 = s0.astype(jnp.float32)

    def flip(t):
        return jnp.flip(t, axis=1)

    o_f, s_f = chunk_gated_delta(qh, kh, vh, g[:, :, 0], beta[:, :, 0], s0[:, 0])
    o_b, s_b = chunk_gated_delta(flip(qh), flip(kh), flip(vh), flip(g[:, :, 1]), flip(beta[:, :, 1]), s0[:, 1])
    o = o_f + flip(o_b)
    zg = jax.nn.silu(z.astype(jnp.float32).reshape(B, T, DN_HEADS, HEAD_DIM))
    o = o * lax.rsqrt(jnp.mean(o * o, axis=-1, keepdims=True) + NORM_EPS) * lp["dn_norm"].astype(jnp.float32) * zg
    return o.reshape(B, T, DN_WIDTH).astype(out_dtype), jnp.stack([s_f, s_b], axis=1)


def axial_rope(x):
    B, T, H, D = x.shape
    rows = T // GRID_W
    row = jnp.repeat(jnp.arange(rows), GRID_W).astype(jnp.float32)
    col = (jnp.arange(rows * GRID_W) % GRID_W).astype(jnp.float32)
    n_freq = D // 4
    inv_freq = 1.0 / (ROPE_THETA ** (jnp.arange(n_freq, dtype=jnp.float32) / n_freq))
    ang = jnp.stack([row[:, None] * inv_freq, col[:, None] * inv_freq], axis=1)
    cos = jnp.cos(ang)[None, :, None]
    sin = jnp.sin(ang)[None, :, None]
    xr = x.astype(jnp.float32).reshape(B, T, H, 2, 2, n_freq)
    x1, x2 = xr[..., 0, :], xr[..., 1, :]
    out = jnp.stack([x1 * cos - x2 * sin, x2 * cos + x1 * sin], axis=-2)
    return out.reshape(B, T, H, D).astype(x.dtype)


def sink_logits(sinks, B):
    return jnp.broadcast_to(sinks.astype(jnp.float32).reshape(1, ATT_KV_HEADS, ATT_GROUP, 1, 1),
                            (B, ATT_KV_HEADS, ATT_GROUP, ATT_BLOCK, 1))


def context_attention(q, k, v, sinks):
    B, S, H, D = q.shape
    nb = S // ATT_BLOCK
    qb = jnp.moveaxis((q * D ** -0.5).reshape(B, nb, ATT_BLOCK, ATT_KV_HEADS, ATT_GROUP, D), 1, 0)
    sink = sink_logits(sinks, B)

    def attend(qi):
        s = jnp.einsum("bqkgd,bskd->bkgqs", qi, k, preferred_element_type=jnp.float32)
        p = jax.nn.softmax(jnp.concatenate([s, sink], axis=-1), axis=-1)[..., :S]
        return jnp.einsum("bkgqs,bskd->bqkgd", p.astype(v.dtype), v)

    out = lax.map(attend, qb)
    return jnp.moveaxis(out, 0, 1).reshape(B, S, H, D)


def latent_attention(q, k, v, ctx_k, ctx_v, sinks):
    B, T, H, D = q.shape
    P = ctx_k.shape[1]
    nb = T // ATT_BLOCK
    nl = 3 * ATT_BLOCK
    q = q * D ** -0.5
    pad = ((0, 0), (ATT_BLOCK, ATT_BLOCK), (0, 0), (0, 0))
    kp, vp = jnp.pad(k, pad), jnp.pad(v, pad)
    ctx_k = ctx_k.astype(q.dtype)
    ctx_v = ctx_v.astype(v.dtype)
    sink = sink_logits(sinks, B)
    offs_q = jnp.arange(ATT_BLOCK)
    offs_k = jnp.arange(nl) - ATT_BLOCK

    def attend(i):
        start = i * ATT_BLOCK
        qi = lax.dynamic_slice_in_dim(q, start, ATT_BLOCK, axis=1).reshape(B, ATT_BLOCK, ATT_KV_HEADS, ATT_GROUP, D)
        ki = lax.dynamic_slice_in_dim(kp, start, nl, axis=1)
        vi = lax.dynamic_slice_in_dim(vp, start, nl, axis=1)
        qpos = start + offs_q
        kpos = start + offs_k
        valid = ((jnp.abs(qpos[:, None] - kpos[None, :]) <= WINDOW)
                 & (kpos >= 0)[None, :] & (kpos < T)[None, :])
        s_loc = jnp.where(valid, jnp.einsum("bqkgd,bskd->bkgqs", qi, ki, preferred_element_type=jnp.float32), NEG_INF)
        s_ctx = jnp.einsum("bqkgd,bskd->bkgqs", qi, ctx_k, preferred_element_type=jnp.float32)
        p = jax.nn.softmax(jnp.concatenate([s_loc, s_ctx, sink], axis=-1), axis=-1)
        o = (jnp.einsum("bkgqs,bskd->bqkgd", p[..., :nl].astype(v.dtype), vi)
             + jnp.einsum("bkgqs,bskd->bqkgd", p[..., nl:nl + P].astype(v.dtype), ctx_v))
        return o.reshape(B, ATT_BLOCK, H, D)

    out = lax.map(attend, jnp.arange(nb))
    return jnp.moveaxis(out, 0, 1).reshape(B, T, H, D)


def swiglu(x, w1, w3, w2):
    return (jax.nn.silu(x @ w1) * (x @ w3)) @ w2


def routed_experts(x, idx, gates, w1, w3, w2):
    M, D = x.shape
    A = M * TOP_K
    e_flat = idx.reshape(A)
    tok = jnp.arange(A, dtype=jnp.int32) // TOP_K
    order = jnp.argsort(e_flat)
    e_sorted = e_flat[order]
    counts = jnp.zeros((N_EXPERTS,), jnp.int32).at[e_flat].add(1)
    starts = jnp.cumsum(counts) - counts
    padded = (counts + EXPERT_ROWS - 1) // EXPERT_ROWS * EXPERT_ROWS
    pad_ends = jnp.cumsum(padded)
    dest = (pad_ends - padded)[e_sorted] + jnp.arange(A, dtype=jnp.int32) - starts[e_sorted]
    n_blocks = (A + N_EXPERTS * (EXPERT_ROWS - 1)) // EXPERT_ROWS + 1
    n_slots = n_blocks * EXPERT_ROWS
    slot_tok = jnp.full((n_slots,), M, jnp.int32).at[dest].set(tok[order])
    slot_gate = jnp.zeros((n_slots,), x.dtype).at[dest].set(gates.reshape(A)[order].astype(x.dtype))
    block_expert = jnp.minimum(
        jnp.searchsorted(pad_ends, jnp.arange(n_blocks, dtype=jnp.int32) * EXPERT_ROWS, side="right"), N_EXPERTS - 1)
    x_pad = jnp.concatenate([x, jnp.zeros((1, D), x.dtype)], axis=0)

    def expert_block(args):
        tok_b, gate_b, e = args
        return swiglu(x_pad[tok_b], w1[e], w3[e], w2[e]) * gate_b[:, None]

    y = lax.map(expert_block, (slot_tok.reshape(n_blocks, EXPERT_ROWS),
                               slot_gate.reshape(n_blocks, EXPERT_ROWS), block_expert))
    out = jnp.zeros((M + 1, D), x.dtype).at[slot_tok].add(y.reshape(n_slots, D))
    return out[:M]


def moe_ffn(x, lp):
    shape = x.shape
    xf = x.reshape(-1, shape[-1])
    M = xf.shape[0]
    scores = jax.nn.sigmoid(jnp.einsum("md,de->me", xf, lp["router_w"], preferred_element_type=jnp.float32))
    biased = scores + lp["router_bias"].astype(jnp.float32)
    group_score = lax.top_k(biased.reshape(M, N_GROUPS, N_EXPERTS // N_GROUPS), 2)[0].sum(-1)
    _, group_sel = lax.top_k(group_score, TOPK_GROUPS)
    group_mask = jax.nn.one_hot(group_sel, N_GROUPS, dtype=jnp.float32).sum(1) > 0
    expert_mask = jnp.repeat(group_mask, N_EXPERTS // N_GROUPS, axis=1)
    _, idx = lax.top_k(jnp.where(expert_mask, biased, NEG_INF), TOP_K)
    gates = jnp.take_along_axis(scores, idx, axis=1)
    gates = gates / jnp.sum(gates, axis=-1, keepdims=True) * ROUTED_SCALE
    y = (routed_experts(xf, idx, gates, lp["expert_w1"], lp["expert_w3"], lp["expert_w2"])
         + swiglu(xf, lp["shared_w1"], lp["shared_w3"], lp["shared_w2"]))
    return y.reshape(shape)


def context_mixer(parts, lp):
    dq, dk, dv, dz, da, db, aq, ak, av = parts
    B, S, _ = aq.shape
    s0 = jnp.zeros((B, 2, DN_HEADS, HEAD_DIM, HEAD_DIM), jnp.float32)
    dn_out, dn_state = gated_deltanet(dq, dk, dv, dz, da, db, lp, s0)
    q = aq.reshape(B, S, ATT_HEADS, HEAD_DIM)
    k = ak.reshape(B, S, ATT_KV_HEADS, HEAD_DIM)
    v = av.reshape(B, S, ATT_KV_HEADS, HEAD_DIM)
    att = context_attention(q, k, v, lp["attn_sinks"]).reshape(B, S, ATT_Q)
    return jnp.concatenate([dn_out, att], axis=-1), (k, v, dn_state.astype(aq.dtype))


def latent_mixer(parts, lp, ctx_k, ctx_v, s0):
    dq, dk, dv, dz, da, db, aq, ak, av = parts
    B, T, _ = aq.shape
    dn_out, _ = gated_deltanet(dq, dk, dv, dz, da, db, lp, s0)
    q = axial_rope(aq.reshape(B, T, ATT_HEADS, HEAD_DIM))
    k = axial_rope(ak.reshape(B, T, ATT_KV_HEADS, HEAD_DIM))
    v = av.reshape(B, T, ATT_KV_HEADS, HEAD_DIM)
    att = latent_attention(q, k, v, ctx_k, ctx_v, lp["attn_sinks"]).reshape(B, T, ATT_Q)
    return jnp.concatenate([dn_out, att], axis=-1), None


def trunk_layer(x, mod, lp, mixer):
    shift1, scale1, gate1, shift2, scale2, gate2 = jnp.split(mod, 6, axis=-1)
    h = rmsnorm(x, lp["norm1"]) * (1 + scale1) + shift1
    mixed, ctx_tensors = mixer(split_columns(h @ lp["w_in"]))
    x = x + gate1 * (mixed @ lp["w_out"])
    h = rmsnorm(x, lp["norm2"]) * (1 + scale2) + shift2
    x = x + gate2 * moe_ffn(h, lp)
    return x, ctx_tensors


def setup_inputs(seed: int = 0) -> dict:
    key = jax.random.key(seed)
    ks = jax.random.split(key, 32)
    f32 = jnp.float32

    def nrm(k, shape, scale):
        return jax.random.normal(k, shape, f32) * scale

    L = DEPTH
    return {
        "x_prompt": nrm(ks[0], (BATCH, SEQ, D_MODEL), 1.0),
        "x_sample": nrm(ks[1], (DEC_BATCH, DEC_SEQ, D_MODEL), 1.0),
        "c": nrm(ks[2], (DEC_BATCH, D_MODEL), 1.0),
        "cache_k": nrm(ks[3], (DEC_BATCH, L, PAST_LEN, ATT_KV_HEADS, HEAD_DIM), 1.0),
        "cache_v": nrm(ks[4], (DEC_BATCH, L, PAST_LEN, ATT_KV_HEADS, HEAD_DIM), 1.0),
        "state_dn": nrm(ks[5], (DEC_BATCH, L, 2, DN_HEADS, HEAD_DIM, HEAD_DIM), HEAD_DIM ** -0.5),
        "c_ctx": nrm(ks[6], (D_MODEL,), 1.0),
        "w_ada": nrm(ks[7], (L, D_MODEL, 6 * D_MODEL), 0.5 * D_MODEL ** -0.5),
        "b_ada": nrm(ks[8], (L, 6 * D_MODEL), 0.02),
        "norm1": 1.0 + nrm(ks[9], (L, D_MODEL), 0.05),
        "norm2": 1.0 + nrm(ks[10], (L, D_MODEL), 0.05),
        "w_in": nrm(ks[11], (L, D_MODEL, IN_COLS), D_MODEL ** -0.5),
        "dn_conv": nrm(ks[12], (L, DN_CONV, 3 * DN_WIDTH), DN_CONV ** -0.5),
        "dn_A_log": jnp.log(jax.random.uniform(ks[13], (L, 2, DN_HEADS), f32, 1.0, 16.0)),
        "dn_dt_bias": nrm(ks[14], (L, 2, DN_HEADS), 0.1),
        "dn_norm": 1.0 + nrm(ks[15], (L, HEAD_DIM), 0.05),
        "attn_sinks": nrm(ks[16], (L, ATT_HEADS), 0.5),
        "w_out": nrm(ks[17], (L, MIX_WIDTH, D_MODEL), MIX_WIDTH ** -0.5),
        "router_w": nrm(ks[18], (L, D_MODEL, N_EXPERTS), D_MODEL ** -0.5),
        "router_bias": nrm(ks[19], (L, N_EXPERTS), 0.01),
        "expert_w1": nrm(ks[20], (L, N_EXPERTS, D_MODEL, EXPERT_FF), D_MODEL ** -0.5),
        "expert_w3": nrm(ks[21], (L, N_EXPERTS, D_MODEL, EXPERT_FF), D_MODEL ** -0.5),
        "expert_w2": nrm(ks[22], (L, N_EXPERTS, EXPERT_FF, D_MODEL), EXPERT_FF ** -0.5),
        "shared_w1": nrm(ks[23], (L, D_MODEL, SHARED_FF), D_MODEL ** -0.5),
        "shared_w3": nrm(ks[24], (L, D_MODEL, SHARED_FF), D_MODEL ** -0.5),
        "shared_w2": nrm(ks[25], (L, SHARED_FF, D_MODEL), SHARED_FF ** -0.5),
        "final_norm": 1.0 + nrm(ks[26], (D_MODEL,), 0.05),
    }


def reference(x_prompt, x_sample, c, cache_k, cache_v, state_dn, c_ctx, w_ada, b_ada, norm1, norm2, w_in,
              dn_conv, dn_A_log, dn_dt_bias, dn_norm, attn_sinks, w_out, router_w, router_bias,
              expert_w1, expert_w3, expert_w2, shared_w1, shared_w3, shared_w2, final_norm):
    hp = x_prompt
    hs = x_sample
    new_k, new_v, new_s = [], [], []
    for layer in range(DEPTH):
        lp = dict(norm1=norm1[layer], norm2=norm2[layer], w_in=w_in[layer], dn_conv=dn_conv[layer],
                  dn_A_log=dn_A_log[layer], dn_dt_bias=dn_dt_bias[layer], dn_norm=dn_norm[layer],
                  attn_sinks=attn_sinks[layer], w_out=w_out[layer], router_w=router_w[layer],
                  router_bias=router_bias[layer], expert_w1=expert_w1[layer], expert_w3=expert_w3[layer],
                  expert_w2=expert_w2[layer], shared_w1=shared_w1[layer], shared_w3=shared_w3[layer],
                  shared_w2=shared_w2[layer])
        mod_ctx = jax.nn.silu(c_ctx) @ w_ada[layer] + b_ada[layer]
        hp, (k_l, v_l, s_l) = trunk_layer(hp, mod_ctx, lp, functools.partial(context_mixer, lp=lp))
        new_k.append(k_l)
        new_v.append(v_l)
        new_s.append(s_l)
        mod_lat = (jax.nn.silu(c) @ w_ada[layer] + b_ada[layer])[:, None, :]
        hs, _ = trunk_layer(hs, mod_lat, lp, functools.partial(
            latent_mixer, lp=lp, ctx_k=cache_k[:, layer], ctx_v=cache_v[:, layer], s0=state_dn[:, layer]))
    y_prompt = rmsnorm(hp, final_norm)
    y_sample = rmsnorm(hs, final_norm)
    new_cache_k = jnp.stack(new_k, axis=1)
    new_cache_v = jnp.stack(new_v, axis=1)
    new_state_dn = jnp.stack(new_s, axis=1)
    return (y_prompt, y_sample, new_cache_k, new_cache_v, new_state_dn)
```

```python
import functools

import jax
import jax.numpy as jnp
import numpy as np
from jax import lax
from jax.experimental import pallas as pl
from jax.experimental.pallas import tpu as pltpu

D_MODEL = 1024
BATCH = 32
SEQ = 256
DEC_BATCH = 8
DEC_SEQ = 2048
PAST_LEN = 512
GRID_W = 64
HEAD_DIM = 128
DN_HEADS = 4
DN_WIDTH = DN_HEADS * HEAD_DIM
DN_CONV = 5
DN_CHUNK = 64
ATT_HEADS = 4
ATT_KV_HEADS = 2
ATT_GROUP = ATT_HEADS // ATT_KV_HEADS
ATT_Q = ATT_HEADS * HEAD_DIM
ATT_KV = ATT_KV_HEADS * HEAD_DIM
ATT_BLOCK = 128
ROPE_THETA = 10000.0
N_EXPERTS = 256
TOP_K = 8
N_GROUPS = 8
TOPK_GROUPS = 4
GROUP_SIZE = N_EXPERTS // N_GROUPS
EXPERT_FF = D_MODEL // 4
ROUTED_SCALE = 2.5
NORM_EPS = 1e-6
NEG_INF = -1e30

M_CTX = BATCH * SEQ
M_LAT = DEC_BATCH * DEC_SEQ
M_ALL = M_CTX + M_LAT
N_MOD = 16
TOK_TILE = 512
N_CTX_TILES = M_CTX // TOK_TILE
N_TILES = M_ALL // TOK_TILE
EXPERT_ROWS = 128
N_PAIRS = M_ALL * TOP_K
N_BLOCKS = (N_PAIRS + N_EXPERTS * (EXPERT_ROWS - 1)) // EXPERT_ROWS
N_SLOTS = N_BLOCKS * EXPERT_ROWS
COMBINE_TILE = 256
VMEM_LIMIT = 56 * 1024 * 1024

_BF16 = jnp.bfloat16
_F32 = jnp.float32


def _dot(a, b):
    return jnp.dot(a.astype(_BF16), b.astype(_BF16), preferred_element_type=_F32)


def _dot_nt(a, b):
    return lax.dot_general(a.astype(_BF16), b.astype(_BF16), (((1,), (1,)), ((), ())),
                           preferred_element_type=_F32)


def _dot_tn(a, b):
    return lax.dot_general(a.astype(_BF16), b.astype(_BF16), (((0,), (0,)), ((), ())),
                           preferred_element_type=_F32)


def _silu(x):
    return x * jax.nn.sigmoid(x)


def _rms(x, w):
    return x * lax.rsqrt(jnp.mean(x * x, axis=-1, keepdims=True) + NORM_EPS) * w


def _params(n_axes=1):
    return pltpu.CompilerParams(dimension_semantics=("arbitrary",) * n_axes, vmem_limit_bytes=VMEM_LIMIT)


def _tile_mod_row(i):
    return jnp.where(i < N_CTX_TILES, 0, 1 + (i - N_CTX_TILES) // (DEC_SEQ // TOK_TILE))


def _ctx_tile_mask(shape):
    limit = jnp.where(pl.program_id(0) < N_CTX_TILES, shape[0], 0)
    return lax.broadcasted_iota(jnp.int32, shape, 0) < limit


def _x_specs():
    return [
        pl.BlockSpec((TOK_TILE, D_MODEL), lambda i: (jnp.minimum(i, N_CTX_TILES - 1), 0)),
        pl.BlockSpec((TOK_TILE, D_MODEL), lambda i: (jnp.maximum(i - N_CTX_TILES, 0), 0)),
    ]


def _mod_kernel(c_ref, w_ref, b_ref, o_ref):
    o_ref[...] = _dot(_silu(c_ref[...]), w_ref[...]) + b_ref[...]


def _modulation(cvec, w_ada, b_ada):
    tn = 1024
    return pl.pallas_call(
        _mod_kernel,
        grid=(6 * D_MODEL // tn,),
        in_specs=[pl.BlockSpec((N_MOD, D_MODEL), lambda j: (0, 0)),
                  pl.BlockSpec((D_MODEL, tn), lambda j: (0, j)),
                  pl.BlockSpec((1, tn), lambda j: (0, j))],
        out_specs=pl.BlockSpec((N_MOD, tn), lambda j: (0, j)),
        out_shape=jax.ShapeDtypeStruct((N_MOD, 6 * D_MODEL), _F32),
        compiler_params=_params(),
        name="modulation",
    )(cvec, w_ada, b_ada)


def _rope(x, cos, sin, first_half):
    swapped = jnp.where(first_half, pltpu.roll(x, 96, 1), pltpu.roll(x, 32, 1))
    return x * cos + swapped * sin


def _inproj_kernel(xp_ref, xs_ref, mod_ref, n1_ref, cos_ref, sin_ref, wdn_ref, watt_ref, wab_ref,
                   dn_ref, aq_ref, ak_ref, av_ref, ab_ref):
    x = jnp.where(_ctx_tile_mask((TOK_TILE, D_MODEL)), xp_ref[...], xs_ref[...])
    shift = mod_ref[0, 0:1, :]
    scale = mod_ref[0, 1:2, :]
    h = (_rms(x, n1_ref[...]) * (1 + scale) + shift).astype(_BF16)
    dn_ref[...] = jnp.dot(h, wdn_ref[...], preferred_element_type=_F32).astype(_BF16)
    ab_ref[...] = jnp.dot(h, wab_ref[...], preferred_element_type=_F32)
    att = jnp.dot(h, watt_ref[...], preferred_element_type=_F32)
    cos = cos_ref[...]
    sin = sin_ref[...]
    lane = lax.broadcasted_iota(jnp.int32, (TOK_TILE, HEAD_DIM), 1)
    first_half = (lane % 64) < 32
    for hd in range(ATT_HEADS):
        q = att[:, hd * HEAD_DIM:(hd + 1) * HEAD_DIM]
        aq_ref[:, hd * HEAD_DIM:(hd + 1) * HEAD_DIM] = (
            _rope(q, cos, sin, first_half) * HEAD_DIM ** -0.5).astype(_BF16)
    for hd in range(ATT_KV_HEADS):
        k = att[:, ATT_Q + hd * HEAD_DIM:ATT_Q + (hd + 1) * HEAD_DIM]
        ak_ref[:, hd * HEAD_DIM:(hd + 1) * HEAD_DIM] = _rope(k, cos, sin, first_half)
    av_ref[...] = att[:, ATT_Q + ATT_KV:]


def _rope_tables():
    t = jnp.arange(DEC_SEQ)
    row = (t // GRID_W).astype(_F32)
    col = (t % GRID_W).astype(_F32)
    n_freq = HEAD_DIM // 4
    inv_freq = 1.0 / (ROPE_THETA ** (jnp.arange(n_freq, dtype=_F32) / n_freq))
    ang_r = row[:, None] * inv_freq
    ang_c = col[:, None] * inv_freq
    cos = jnp.concatenate([jnp.cos(ang_r), jnp.cos(ang_r), jnp.cos(ang_c), jnp.cos(ang_c)], axis=1)
    sin = jnp.concatenate([-jnp.sin(ang_r), jnp.sin(ang_r), -jnp.sin(ang_c), jnp.sin(ang_c)], axis=1)
    cos = jnp.concatenate([jnp.ones((TOK_TILE, HEAD_DIM), _F32), cos], axis=0)
    sin = jnp.concatenate([jnp.zeros((TOK_TILE, HEAD_DIM), _F32), sin], axis=0)
    return cos, sin


def _input_projection(xp, xs, mod3, norm1, cos, sin, w_dn, w_att, w_ab):
    def rope_idx(i):
        return (jnp.where(i < N_CTX_TILES, 0, 1 + (i - N_CTX_TILES) % (DEC_SEQ // TOK_TILE)), 0)

    const = lambda i: (0, 0)
    row = lambda i: (i, 0)
    return pl.pallas_call(
        _inproj_kernel,
        grid=(N_TILES,),
        in_specs=_x_specs() + [
            pl.BlockSpec((1, 6, D_MODEL), lambda i: (_tile_mod_row(i), 0, 0)),
            pl.BlockSpec((1, D_MODEL), const),
            pl.BlockSpec((TOK_TILE, HEAD_DIM), rope_idx),
            pl.BlockSpec((TOK_TILE, HEAD_DIM), rope_idx),
            pl.BlockSpec((D_MODEL, 4 * DN_WIDTH), const),
            pl.BlockSpec((D_MODEL, ATT_Q + 2 * ATT_KV), const),
            pl.BlockSpec((D_MODEL, 128), const),
        ],
        out_specs=[
            pl.BlockSpec((TOK_TILE, 4 * DN_WIDTH), row),
            pl.BlockSpec((TOK_TILE, ATT_Q), row),
            pl.BlockSpec((TOK_TILE, ATT_KV), row),
            pl.BlockSpec((TOK_TILE, ATT_KV), row),
            pl.BlockSpec((TOK_TILE, 128), row),
        ],
        out_shape=[
            jax.ShapeDtypeStruct((M_ALL, 4 * DN_WIDTH), _BF16),
            jax.ShapeDtypeStruct((M_ALL, ATT_Q), _BF16),
            jax.ShapeDtypeStruct((M_ALL, ATT_KV), _F32),
            jax.ShapeDtypeStruct((M_ALL, ATT_KV), _F32),
            jax.ShapeDtypeStruct((M_ALL, 128), _F32),
        ],
        compiler_params=_params(),
        name="input_projection",
    )(xp, xs, mod3, norm1, cos, sin, w_dn, w_att, w_ab)


def _inverse_unit_triangular(a, eye):
    p = eye - a
    x = a
    for _ in range(5):
        x = _dot(x, x)
        p = p + _dot(p, x)
    return p


def _dn_chunk(state, q, k, v, gc, gc_row, beta, g_tot, incl, strict, eye):
    decay = jnp.where(incl, jnp.exp(jnp.where(incl, gc - gc_row, 0.0)), 0.0)
    kb = k * beta
    a = jnp.where(strict, _dot_nt(kb, k) * decay, 0.0)
    t_inv = _inverse_unit_triangular(a, eye)
    eg = jnp.exp(gc)
    u = _dot(t_inv, v * beta)
    w = _dot(t_inv, kb * eg)
    qk = _dot_nt(q, k) * decay
    v_new = u - _dot(w, state)
    o = _dot(q * eg, state) + _dot(qk, v_new)
    state = state * jnp.exp(g_tot) + _dot_tn(k * jnp.exp(g_tot - gc), v_new)
    return state, o


def _dn_kernel(*refs, seq_len, has_s0, emit_state):
    dn_ref, ab_ref, cw_ref, alog_ref, dtb_ref, nw_ref = refs[:6]
    pos = 6
    s0_ref = None
    if has_s0:
        s0_ref = refs[pos]
        pos += 1
    o_ref = refs[pos]
    pos += 1
    st_ref = None
    if emit_state:
        st_ref = refs[pos]
        pos += 1
    pad_scr, q_scr, k_scr, v_scr, gate_scr, grow_scr, of_scr, ob_scr = refs[pos:]

    T = seq_len
    C = DN_CHUNK
    n_pair = T // (2 * C)

    gates = ab_ref[...]
    lane = lax.broadcasted_iota(jnp.int32, (T, 128), 1)
    tpos = lax.broadcasted_iota(jnp.int32, (T, 128), 0) % C
    sp_arg = gates + dtb_ref[...]
    softplus = jnp.maximum(sp_arg, 0.0) + jnp.log1p(jnp.exp(-jnp.abs(sp_arg)))
    g = -jnp.exp(alog_ref[...]) * softplus
    pre = g
    suf = g
    s = 1
    while s < C:
        pre = pre + jnp.where(tpos >= s, pltpu.roll(pre, s, 0), 0.0)
        suf = suf + jnp.where(tpos < C - s, pltpu.roll(suf, T - s, 0), 0.0)
        s *= 2
    gcum = jnp.where(lane < DN_HEADS, pre, suf)
    gate_scr[...] = jnp.where(lane < 2 * DN_HEADS, gcum, jax.nn.sigmoid(gates))
    gcum_t = gcum.T
    for p in range(n_pair):
        grow_scr[p] = gcum_t[0:8, p * 128:(p + 1) * 128]

    pad_scr[0:8, :] = jnp.zeros((8, HEAD_DIM), _F32)
    pad_scr[8 + T:16 + T, :] = jnp.zeros((8, HEAD_DIM), _F32)

    ri = lax.broadcasted_iota(jnp.int32, (C, C), 0)
    ci = lax.broadcasted_iota(jnp.int32, (C, C), 1)
    eye = (ri == ci).astype(_F32)
    masks = ((ci <= ri, ci < ri), (ci >= ri, ci > ri))

    def conv_silu(part, h):
        c0 = part * DN_WIDTH + h * HEAD_DIM
        pad_scr[8:8 + T, :] = dn_ref[:, c0:c0 + HEAD_DIM].astype(_F32)
        acc = cw_ref[0:1, c0:c0 + HEAD_DIM] * pad_scr[6:6 + T, :]
        for j in range(1, DN_CONV):
            acc = acc + cw_ref[j:j + 1, c0:c0 + HEAD_DIM] * pad_scr[6 + j:6 + j + T, :]
        return _silu(acc)

    def l2n(x):
        return x * lax.rsqrt(jnp.sum(x * x, axis=-1, keepdims=True) + NORM_EPS)

    for h in range(DN_HEADS):
        q_scr[...] = l2n(conv_silu(0, h)) * HEAD_DIM ** -0.5
        k_scr[...] = l2n(conv_silu(1, h))
        v_scr[...] = conv_silu(2, h)

        def chunk(state, r0, direction, rows, half, o_scr):
            col = direction * DN_HEADS + h
            bcol = 2 * DN_HEADS + col
            sl = pl.ds(r0, C)
            gc = gate_scr[sl, col:col + 1]
            beta = gate_scr[sl, bcol:bcol + 1]
            gc_row = rows[col:col + 1, half * C:(half + 1) * C]
            g_tot = gc[C - 1:C, :] if direction == 0 else gc[0:1, :]
            incl, strict = masks[direction]
            state, o = _dn_chunk(state, q_scr[sl, :], k_scr[sl, :], v_scr[sl, :], gc, gc_row, beta, g_tot,
                                 incl, strict, eye)
            o_scr[sl, :] = o
            return state

        def pair_body(p, carry):
            s_f, s_b = carry
            pb = n_pair - 1 - p
            rows_f = grow_scr[p]
            rows_b = grow_scr[pb]
            r_f = pl.multiple_of(p * 2 * C, 2 * C)
            r_b = pl.multiple_of(pb * 2 * C, 2 * C)
            s_f = chunk(s_f, r_f, 0, rows_f, 0, of_scr)
            s_b = chunk(s_b, r_b + C, 1, rows_b, 1, ob_scr)
            s_f = chunk(s_f, r_f + C, 0, rows_f, 1, of_scr)
            s_b = chunk(s_b, r_b, 1, rows_b, 0, ob_scr)
            return s_f, s_b

        if has_s0:
            init = (s0_ref[0, 0, h].astype(_F32), s0_ref[0, 1, h].astype(_F32))
        else:
            init = (jnp.zeros((HEAD_DIM, HEAD_DIM), _F32), jnp.zeros((HEAD_DIM, HEAD_DIM), _F32))
        s_f, s_b = lax.fori_loop(0, n_pair, pair_body, init)
        if emit_state:
            st_ref[0, 0, h] = s_f
            st_ref[0, 1, h] = s_b

        o = of_scr[...] + ob_scr[...]
        z = dn_ref[:, 3 * DN_WIDTH + h * HEAD_DIM:3 * DN_WIDTH + (h + 1) * HEAD_DIM].astype(_F32)
        o = o * lax.rsqrt(jnp.mean(o * o, axis=-1, keepdims=True) + NORM_EPS) * nw_ref[...] * _silu(z)
        o_ref[:, h * HEAD_DIM:(h + 1) * HEAD_DIM] = o.astype(_BF16)


def _deltanet(dn, ab, conv_w, a_log, dt_bias, dn_norm, s0, *, n_seq, seq_len, row_block0, emit_state):
    has_s0 = s0 is not None
    const = lambda b: (0, 0)
    state_spec = pl.BlockSpec((1, 2, DN_HEADS, HEAD_DIM, HEAD_DIM), lambda b: (b, 0, 0, 0, 0))
    in_specs = [
        pl.BlockSpec((seq_len, 4 * DN_WIDTH), lambda b: (row_block0 + b, 0)),
        pl.BlockSpec((seq_len, 128), lambda b: (row_block0 + b, 0)),
        pl.BlockSpec((8, 3 * DN_WIDTH), const),
        pl.BlockSpec((1, 128), const),
        pl.BlockSpec((1, 128), const),
        pl.BlockSpec((1, HEAD_DIM), const),
    ]
    args = [dn, ab, conv_w, a_log, dt_bias, dn_norm]
    if has_s0:
        in_specs.append(state_spec)
        args.append(s0)
    out_specs = [pl.BlockSpec((seq_len, DN_WIDTH), lambda b: (b, 0))]
    out_shape = [jax.ShapeDtypeStruct((n_seq * seq_len, DN_WIDTH), _BF16)]
    if emit_state:
        out_specs.append(state_spec)
        out_shape.append(jax.ShapeDtypeStruct((n_seq, 2, DN_HEADS, HEAD_DIM, HEAD_DIM), _F32))
    tile = (seq_len, HEAD_DIM)
    return pl.pallas_call(
        functools.partial(_dn_kernel, seq_len=seq_len, has_s0=has_s0, emit_state=emit_state),
        grid=(n_seq,),
        in_specs=in_specs,
        out_specs=out_specs,
        out_shape=out_shape,
        scratch_shapes=[
            pltpu.VMEM((seq_len + 16, HEAD_DIM), _F32),
            pltpu.VMEM(tile, _F32), pltpu.VMEM(tile, _F32), pltpu.VMEM(tile, _F32),
            pltpu.VMEM((seq_len, 128), _F32),
            pltpu.VMEM((seq_len // (2 * DN_CHUNK), 8, 128), _F32),
            pltpu.VMEM(tile, _F32), pltpu.VMEM(tile, _F32),
        ],
        compiler_params=_params(),
        name="deltanet_ctx" if emit_state else "deltanet_lat",
    )(*args)


def _softmax_av(scores, values, sink):
    m = sink
    for s in scores:
        m = jnp.maximum(m, jnp.max(s, axis=-1, keepdims=True))
    denom = jnp.exp(sink - m)
    acc = None
    for s, v in zip(scores, values):
        p = jnp.exp(s - m)
        denom = denom + jnp.sum(p, axis=-1, keepdims=True)
        pv = _dot(p, v)
        acc = pv if acc is None else acc + pv
    return acc / denom


def _ctx_attn_kernel(sink_ref, q_ref, k_ref, v_ref, o_ref):
    kvh = pl.program_id(1)
    k = k_ref[...]
    v = v_ref[...]
    for g in range(ATT_GROUP):
        q = q_ref[:, g * HEAD_DIM:(g + 1) * HEAD_DIM]
        sink = jnp.full((SEQ, 1), sink_ref[kvh * ATT_GROUP + g], _F32)
        o = _softmax_av([_dot_nt(q, k)], [v], sink)
        o_ref[:, g * HEAD_DIM:(g + 1) * HEAD_DIM] = o.astype(_BF16)


def _context_attention(sinks, aq, ak, av):
    return pl.pallas_call(
        _ctx_attn_kernel,
        grid=(BATCH, ATT_KV_HEADS),
        in_specs=[
            pl.BlockSpec(memory_space=pltpu.SMEM),
            pl.BlockSpec((SEQ, ATT_GROUP * HEAD_DIM), lambda b, h: (b, h)),
            pl.BlockSpec((SEQ, HEAD_DIM), lambda b, h: (b, h)),
            pl.BlockSpec((SEQ, HEAD_DIM), lambda b, h: (b, h)),
        ],
        out_specs=pl.BlockSpec((SEQ, ATT_GROUP * HEAD_DIM), lambda b, h: (b, h)),
        out_shape=jax.ShapeDtypeStruct((M_CTX, ATT_Q), _BF16),
        compiler_params=_params(2),
        name="context_attention",
    )(sinks, aq, ak, av)


def _lat_attn_kernel(sink_ref, q_ref, kp_ref, kc_ref, kn_ref, vp_ref, vc_ref, vn_ref, ck_ref, cv_ref, o_ref):
    i = pl.program_id(1)
    kvh = pl.program_id(2)
    nb = pl.num_programs(1)
    B = ATT_BLOCK
    rows = ATT_GROUP * B
    q = jnp.concatenate([q_ref[:, g * HEAD_DIM:(g + 1) * HEAD_DIM] for g in range(ATT_GROUP)], axis=0)
    r = lax.broadcasted_iota(jnp.int32, (rows, B), 0) % B
    c = lax.broadcasted_iota(jnp.int32, (rows, B), 1)
    s_prev = jnp.where(c >= r + jnp.where(i > 0, 0, B), _dot_nt(q, kp_ref[...]), NEG_INF)
    s_cur = _dot_nt(q, kc_ref[...])
    s_next = jnp.where(c <= r - jnp.where(i < nb - 1, 0, B), _dot_nt(q, kn_ref[...]), NEG_INF)
    s_ctx = _dot_nt(q, ck_ref[...])
    head = lax.broadcasted_iota(jnp.int32, (rows, 1), 0) // B
    sink = jnp.zeros((rows, 1), _F32)
    for g in range(ATT_GROUP):
        sink = jnp.where(head == g, sink_ref[kvh * ATT_GROUP + g], sink)
    o = _softmax_av([s_prev, s_cur, s_next, s_ctx], [vp_ref[...], vc_ref[...], vn_ref[...], cv_ref[...]], sink)
    for g in range(ATT_GROUP):
        o_ref[:, g * HEAD_DIM:(g + 1) * HEAD_DIM] = o[g * B:(g + 1) * B].astype(_BF16)


def _latent_attention(sinks, aq, ak, av, ctx_k, ctx_v):
    nb = DEC_SEQ // ATT_BLOCK
    blk0 = M_CTX // ATT_BLOCK
    cur = lambda b, i, h: (blk0 + b * nb + i, h)
    prev = lambda b, i, h: (blk0 + b * nb + jnp.maximum(i - 1, 0), h)
    nxt = lambda b, i, h: (blk0 + b * nb + jnp.minimum(i + 1, nb - 1), h)
    kv_spec = lambda f: pl.BlockSpec((ATT_BLOCK, HEAD_DIM), f)
    ctx_spec = pl.BlockSpec((PAST_LEN, HEAD_DIM), lambda b, i, h: (b, h))
    return pl.pallas_call(
        _lat_attn_kernel,
        grid=(DEC_BATCH, nb, ATT_KV_HEADS),
        in_specs=[
            pl.BlockSpec(memory_space=pltpu.SMEM),
            pl.BlockSpec((ATT_BLOCK, ATT_GROUP * HEAD_DIM), cur),
            kv_spec(prev), kv_spec(cur), kv_spec(nxt),
            kv_spec(prev), kv_spec(cur), kv_spec(nxt),
            ctx_spec, ctx_spec,
        ],
        out_specs=pl.BlockSpec((ATT_BLOCK, ATT_GROUP * HEAD_DIM), lambda b, i, h: (b * nb + i, h)),
        out_shape=jax.ShapeDtypeStruct((M_LAT, ATT_Q), _BF16),
        compiler_params=_params(3),
        name="latent_attention",
    )(sinks, aq, ak, ak, ak, av, av, av, ctx_k, ctx_v)


def _outproj_kernel(xp_ref, xs_ref, dnc_ref, dnl_ref, atc_ref, atl_ref, mod_ref, n2_ref, wo_dn_ref, wo_at_ref,
                    rw_ref, x1_ref, h2_ref, h2p_ref, lg_ref):
    x = jnp.where(_ctx_tile_mask((TOK_TILE, D_MODEL)), xp_ref[...], xs_ref[...])
    half_mask = _ctx_tile_mask((TOK_TILE, DN_WIDTH))
    dn = jnp.where(half_mask, dnc_ref[...], dnl_ref[...])
    at = jnp.where(half_mask, atc_ref[...], atl_ref[...])
    mixed = (jnp.dot(dn, wo_dn_ref[...], preferred_element_type=_F32)
             + jnp.dot(at, wo_at_ref[...], preferred_element_type=_F32))
    x1 = x + mod_ref[0, 2:3, :] * mixed
    x1_ref[...] = x1
    h = _rms(x1, n2_ref[...]) * (1 + mod_ref[0, 4:5, :]) + mod_ref[0, 3:4, :]
    h2_ref[...] = h.astype(_BF16)
    h2p_ref[...] = h
    lg_ref[...] = lax.dot_general(rw_ref[...], h, (((1,), (1,)), ((), ())),
                                  precision=lax.Precision.HIGHEST, preferred_element_type=_F32)


def _output_projection(xp, xs, dn_c, dn_l, at_c, at_l, mod3, norm2, wo_dn, wo_at, router_wt):
    const = lambda i: (0, 0)
    row = lambda i: (i, 0)
    ctx_row = lambda i: (jnp.minimum(i, N_CTX_TILES - 1), 0)
    lat_row = lambda i: (jnp.maximum(i - N_CTX_TILES, 0), 0)
    half = (TOK_TILE, DN_WIDTH)
    return pl.pallas_call(
        _outproj_kernel,
        grid=(N_TILES,),
        in_specs=_x_specs() + [
            pl.BlockSpec(half, ctx_row), pl.BlockSpec(half, lat_row),
            pl.BlockSpec(half, ctx_row), pl.BlockSpec(half, lat_row),
            pl.BlockSpec((1, 6, D_MODEL), lambda i: (_tile_mod_row(i), 0, 0)),
            pl.BlockSpec((1, D_MODEL), const),
            pl.BlockSpec((DN_WIDTH, D_MODEL), const),
            pl.BlockSpec((ATT_Q, D_MODEL), const),
            pl.BlockSpec((N_EXPERTS, D_MODEL), const),
        ],
        out_specs=[
            pl.BlockSpec((TOK_TILE, D_MODEL), row),
            pl.BlockSpec((TOK_TILE, D_MODEL), row),
            pl.BlockSpec((TOK_TILE, D_MODEL), row),
            pl.BlockSpec((N_EXPERTS, TOK_TILE), lambda i: (0, i)),
        ],
        out_shape=[
            jax.ShapeDtypeStruct((M_ALL, D_MODEL), _F32),
            jax.ShapeDtypeStruct((M_ALL, D_MODEL), _BF16),
            jax.ShapeDtypeStruct((M_ALL, D_MODEL), _F32),
            jax.ShapeDtypeStruct((N_EXPERTS, M_ALL), _F32),
        ],
        compiler_params=_params(),
        name="output_projection",
    )(xp, xs, dn_c, dn_l, at_c, at_l, mod3, norm2, wo_dn, wo_at, router_wt)


def _first_index_of(values, target, index, limit):
    return jnp.min(jnp.where(values == target, index, limit), axis=0, keepdims=True)


def _route_kernel(lg_ref, bias_ref, idx_ref, rank_ref, gate_ref, cnt_ref, carry_scr):
    i = pl.program_id(0)
    tm = TOK_TILE

    @pl.when(i == 0)
    def _():
        carry_scr[...] = jnp.zeros_like(carry_scr)

    scores = jax.nn.sigmoid(lg_ref[...])
    biased = scores + bias_ref[...]
    row = lax.broadcasted_iota(jnp.int32, (N_EXPERTS, tm), 0).astype(_F32)
    grow = lax.broadcasted_iota(jnp.int32, (GROUP_SIZE, tm), 0).astype(_F32)

    group_rows = []
    for g in range(N_GROUPS):
        blk = biased[g * GROUP_SIZE:(g + 1) * GROUP_SIZE]
        m1 = jnp.max(blk, axis=0, keepdims=True)
        i1 = _first_index_of(blk, m1, grow, GROUP_SIZE)
        m2 = jnp.max(jnp.where(grow == i1, -jnp.inf, blk), axis=0, keepdims=True)
        group_rows.append(m1 + m2)
    gs = jnp.concatenate(group_rows, axis=0)
    gi = lax.broadcasted_iota(jnp.int32, (N_GROUPS, tm), 0).astype(_F32)
    gself = jnp.zeros((N_GROUPS, tm), _F32)
    for _ in range(TOPK_GROUPS):
        hit = gi == _first_index_of(gs, jnp.max(gs, axis=0, keepdims=True), gi, N_GROUPS)
        gself = jnp.where(hit, 1.0, gself)
        gs = jnp.where(hit, -jnp.inf, gs)
    emask = jnp.concatenate(
        [jnp.broadcast_to(gself[g:g + 1], (GROUP_SIZE, tm)) for g in range(N_GROUPS)], axis=0) > 0.5
    masked = jnp.where(emask, biased, NEG_INF)

    selected = jnp.zeros((N_EXPERTS, tm), _F32)
    idxs, gates = [], []
    for _ in range(TOP_K):
        ei = _first_index_of(masked, jnp.max(masked, axis=0, keepdims=True), row, N_EXPERTS)
        hit = row == ei
        idxs.append(ei)
        gates.append(jnp.sum(jnp.where(hit, scores, 0.0), axis=0, keepdims=True))
        masked = jnp.where(hit, -jnp.inf, masked)
        selected = jnp.where(hit, 1.0, selected)
    gsum = gates[0]
    for gk in gates[1:]:
        gsum = gsum + gk
    gates = [gk / gsum * ROUTED_SCALE for gk in gates]

    si = lax.broadcasted_iota(jnp.int32, (tm, tm), 0)
    ti = lax.broadcasted_iota(jnp.int32, (tm, tm), 1)
    earlier = (si < ti).astype(_BF16)
    ranks_all = _dot(selected, earlier) + carry_scr[...]
    ranks = [jnp.sum(jnp.where(row == ei, ranks_all, 0.0), axis=0, keepdims=True) for ei in idxs]
    carry_scr[...] = carry_scr[...] + jnp.sum(selected, axis=1, keepdims=True)

    idx_ref[...] = jnp.concatenate(idxs, axis=0).astype(jnp.int32)
    rank_ref[...] = jnp.concatenate(ranks, axis=0).astype(jnp.int32)
    gate_rows = jnp.concatenate(gates + [jnp.zeros((128 - TOP_K, tm), _F32)], axis=0)
    gate_ref[...] = gate_rows.T
    cnt_ref[...] = jnp.broadcast_to(carry_scr[...], (N_EXPERTS, 128))


def _route(logits_t, bias_col):
    return pl.pallas_call(
        _route_kernel,
        grid=(N_TILES,),
        in_specs=[pl.BlockSpec((N_EXPERTS, TOK_TILE), lambda i: (0, i)),
                  pl.BlockSpec((N_EXPERTS, 1), lambda i: (0, 0))],
        out_specs=[
            pl.BlockSpec((TOP_K, TOK_TILE), lambda i: (0, i)),
            pl.BlockSpec((TOP_K, TOK_TILE), lambda i: (0, i)),
            pl.BlockSpec((TOK_TILE, 128), lambda i: (i, 0)),
            pl.BlockSpec((N_EXPERTS, 128), lambda i: (0, 0)),
        ],
        out_shape=[
            jax.ShapeDtypeStruct((TOP_K, M_ALL), jnp.int32),
            jax.ShapeDtypeStruct((TOP_K, M_ALL), jnp.int32),
            jax.ShapeDtypeStruct((M_ALL, 128), _F32),
            jax.ShapeDtypeStruct((N_EXPERTS, 128), _F32),
        ],
        scratch_shapes=[pltpu.VMEM((N_EXPERTS, 1), _F32)],
        compiler_params=_params(),
        name="route",
    )(logits_t, bias_col)


def _slot_kernel(idx_ref, rank_ref, start_ref, pos_ref):
    row = lax.broadcasted_iota(jnp.int32, (N_EXPERTS, TOK_TILE), 0)
    start = start_ref[...]
    rows = []
    for k in range(TOP_K):
        base = jnp.sum(jnp.where(row == idx_ref[k:k + 1, :], start, 0.0), axis=0, keepdims=True)
        rows.append(base.astype(jnp.int32) + rank_ref[k:k + 1, :])
    pos_ref[...] = jnp.concatenate(rows, axis=0)


def _slots(idx, rank, start_col):
    spec = pl.BlockSpec((TOP_K, TOK_TILE), lambda i: (0, i))
    return pl.pallas_call(
        _slot_kernel,
        grid=(N_TILES,),
        in_specs=[spec, spec, pl.BlockSpec((N_EXPERTS, 1), lambda i: (0, 0))],
        out_specs=spec,
        out_shape=jax.ShapeDtypeStruct((TOP_K, M_ALL), jnp.int32),
        compiler_params=_params(),
        name="slots",
    )(idx, rank, start_col)


def _dispatch_kernel(pos_ref, h2p_ref, xs_in_ref, xs_ref, sem):
    del xs_in_ref
    i = pl.program_id(0)
    base = i * TOK_TILE

    def body(t, carry):
        src = h2p_ref.at[pl.ds(base + t, 1)]
        for k in range(TOP_K):
            pltpu.make_async_copy(src, xs_ref.at[pl.ds(pos_ref[0, k, t], 1)], sem).start()
        return carry

    lax.fori_loop(0, TOK_TILE, body, 0)
    n = TOK_TILE * TOP_K
    pltpu.make_async_copy(h2p_ref.at[pl.ds(0, n)], xs_ref.at[pl.ds(0, n)], sem).wait()


def _dispatch(pos3, h2p, xs_zero):
    return pl.pallas_call(
        _dispatch_kernel,
        grid=(N_TILES,),
        in_specs=[pl.BlockSpec((1, TOP_K, TOK_TILE), lambda i: (i, 0, 0), memory_space=pltpu.SMEM),
                  pl.BlockSpec(memory_space=pl.ANY),
                  pl.BlockSpec(memory_space=pl.ANY)],
        out_specs=pl.BlockSpec(memory_space=pl.ANY),
        out_shape=jax.ShapeDtypeStruct((N_SLOTS, D_MODEL), _F32),
        scratch_shapes=[pltpu.SemaphoreType.DMA(())],
        input_output_aliases={2: 0},
        compiler_params=_params(),
        name="dispatch",
    )(pos3, h2p, xs_zero)


def _expert_kernel(be_ref, nused_ref, xs_ref, w1_ref, w3_ref, w2_ref, ys_ref, w1_scr, w3_scr, w2_scr):
    j = pl.program_id(0)

    @pl.when(j < nused_ref[0])
    def _():
        prev = be_ref[jnp.maximum(j - 1, 0)]

        @pl.when((j == 0) | (be_ref[j] != prev))
        def _():
            w1_scr[...] = w1_ref[0].astype(_BF16)
            w3_scr[...] = w3_ref[0].astype(_BF16)
            w2_scr[...] = w2_ref[0].astype(_BF16)

        x = xs_ref[...].astype(_BF16)
        h1 = jnp.dot(x, w1_scr[...], preferred_element_type=_F32)
        h3 = jnp.dot(x, w3_scr[...], preferred_element_type=_F32)
        act = (_silu(h1) * h3).astype(_BF16)
        ys_ref[...] = jnp.dot(act, w2_scr[...], preferred_element_type=_F32)

    @pl.when(j >= nused_ref[0])
    def _():
        ys_ref[...] = jnp.zeros_like(ys_ref)


def _experts(block_expert, n_used, xs, w1, w3, w2):
    def blk(j, be, nu):
        return (jnp.minimum(j, nu[0] - 1), 0)

    def wsel(j, be, nu):
        return (be[jnp.minimum(j, nu[0] - 1)], 0, 0)

    grid_spec = pltpu.PrefetchScalarGridSpec(
        num_scalar_prefetch=2,
        grid=(N_BLOCKS,),
        in_specs=[
            pl.BlockSpec((EXPERT_ROWS, D_MODEL), blk),
            pl.BlockSpec((1, D_MODEL, EXPERT_FF), wsel),
            pl.BlockSpec((1, D_MODEL, EXPERT_FF), wsel),
            pl.BlockSpec((1, EXPERT_FF, D_MODEL), wsel),
        ],
        out_specs=pl.BlockSpec((EXPERT_ROWS, D_MODEL), lambda j, be, nu: (j, 0)),
        scratch_shapes=[pltpu.VMEM((D_MODEL, EXPERT_FF), _BF16), pltpu.VMEM((D_MODEL, EXPERT_FF), _BF16),
                        pltpu.VMEM((EXPERT_FF, D_MODEL), _BF16)],
    )
    return pl.pallas_call(
        _expert_kernel,
        grid_spec=grid_spec,
        out_shape=jax.ShapeDtypeStruct((N_SLOTS, D_MODEL), _F32),
        compiler_params=_params(),
        name="experts",
    )(block_expert, n_used, xs, w1, w3, w2)


def _combine_kernel(pos_ref, gate_ref, h2_ref, x1_ref, mod_ref, sw1_ref, sw3_ref, sw2_ref, fn_ref, ys_ref,
                    y_ref, buf, sem):
    tm = COMBINE_TILE

    def body(t, carry):
        for k in range(TOP_K):
            pltpu.make_async_copy(ys_ref.at[pl.ds(pos_ref[0, k, t], 1)], buf.at[k, pl.ds(t, 1)], sem).start()
        return carry

    lax.fori_loop(0, tm, body, 0)
    hb = h2_ref[...]
    act = _silu(jnp.dot(hb, sw1_ref[...], preferred_element_type=_F32)) * jnp.dot(
        hb, sw3_ref[...], preferred_element_type=_F32)
    shared = _dot(act, sw2_ref[...])
    for k in range(TOP_K):
        pltpu.make_async_copy(ys_ref.at[pl.ds(0, tm)], buf.at[k], sem).wait()
    gates = gate_ref[...]
    routed = gates[:, 0:1] * buf[0]
    for k in range(1, TOP_K):
        routed = routed + gates[:, k:k + 1] * buf[k]
    y = x1_ref[...] + mod_ref[0, 5:6, :] * (routed + shared)
    y_ref[...] = _rms(y, fn_ref[...])


def _combine(pos3, gate_t, h2, x1, mod3, sw1, sw3, sw2, final_norm, ys, *, n_rows, tile0, mod_row):
    tm = COMBINE_TILE
    const = lambda i: (0, 0)
    row = lambda i: (tile0 + i, 0)
    return pl.pallas_call(
        _combine_kernel,
        grid=(n_rows // tm,),
        in_specs=[
            pl.BlockSpec((1, TOP_K, tm), lambda i: (tile0 + i, 0, 0), memory_space=pltpu.SMEM),
            pl.BlockSpec((tm, 128), row),
            pl.BlockSpec((tm, D_MODEL), row),
            pl.BlockSpec((tm, D_MODEL), row),
            pl.BlockSpec((1, 6, D_MODEL), lambda i: (mod_row(i), 0, 0)),
            pl.BlockSpec((D_MODEL, EXPERT_FF), const),
            pl.BlockSpec((D_MODEL, EXPERT_FF), const),
            pl.BlockSpec((EXPERT_FF, D_MODEL), const),
            pl.BlockSpec((1, D_MODEL), const),
            pl.BlockSpec(memory_space=pl.ANY),
        ],
        out_specs=pl.BlockSpec((tm, D_MODEL), lambda i: (i, 0)),
        out_shape=jax.ShapeDtypeStruct((n_rows, D_MODEL), _F32),
        scratch_shapes=[pltpu.VMEM((TOP_K, tm, D_MODEL), _F32), pltpu.SemaphoreType.DMA(())],
        compiler_params=_params(),
        name="combine",
    )(pos3, gate_t, h2, x1, mod3, sw1, sw3, sw2, final_norm, ys)


def _tile_major(a, tile):
    return a.reshape(TOP_K, -1, tile).transpose(1, 0, 2)


def kernel(x_prompt, x_sample, c, cache_k, cache_v, state_dn, c_ctx, w_ada, b_ada, norm1, norm2, w_in, dn_conv,
           dn_A_log, dn_dt_bias, dn_norm, attn_sinks, w_out, router_w, router_bias, expert_w1, expert_w3,
           expert_w2, shared_w1, shared_w3, shared_w2, final_norm):
    xp = x_prompt.reshape(M_CTX, D_MODEL)
    xs = x_sample.reshape(M_LAT, D_MODEL)

    cvec = jnp.concatenate([c_ctx[None, :], c, jnp.zeros((N_MOD - 1 - DEC_BATCH, D_MODEL), _F32)], axis=0)
    mod3 = _modulation(cvec, w_ada[0], b_ada).reshape(N_MOD, 6, D_MODEL)

    w = w_in[0]
    n_dn = 4 * DN_WIDTH
    w_dn = w[:, :n_dn].astype(_BF16)
    w_ab = jnp.pad(w[:, n_dn:n_dn + 4 * DN_HEADS], ((0, 0), (0, 128 - 4 * DN_HEADS))).astype(_BF16)
    w_att = w[:, n_dn + 4 * DN_HEADS:].astype(_BF16)
    cos, sin = _rope_tables()
    dn, aq, ak, av, ab = _input_projection(xp, xs, mod3, norm1, cos, sin, w_dn, w_att, w_ab)

    conv_w = jnp.pad(dn_conv[0], ((0, 8 - DN_CONV), (0, 0)))
    pad8 = lambda v: jnp.pad(v.reshape(1, 2 * DN_HEADS), ((0, 0), (0, 128 - 2 * DN_HEADS)))
    a_log = pad8(dn_A_log[0])
    dt_bias = pad8(dn_dt_bias[0])
    dn_c, new_state = _deltanet(dn, ab, conv_w, a_log, dt_bias, dn_norm, None,
                                n_seq=BATCH, seq_len=SEQ, row_block0=0, emit_state=True)
    (dn_l,) = _deltanet(dn, ab, conv_w, a_log, dt_bias, dn_norm, state_dn[:, 0],
                        n_seq=DEC_BATCH, seq_len=DEC_SEQ, row_block0=M_CTX // DEC_SEQ, emit_state=False)

    sinks = attn_sinks[0]
    at_c = _context_attention(sinks, aq, ak, av)
    ctx_k = cache_k[:, 0].reshape(DEC_BATCH * PAST_LEN, ATT_KV)
    ctx_v = cache_v[:, 0].reshape(DEC_BATCH * PAST_LEN, ATT_KV)
    at_l = _latent_attention(sinks, aq, ak, av, ctx_k, ctx_v)

    wo = w_out[0].astype(_BF16)
    x1, h2, h2p, logits_t = _output_projection(xp, xs, dn_c, dn_l, at_c, at_l, mod3, norm2,
                                               wo[:DN_WIDTH], wo[DN_WIDTH:], router_w[0].T)

    idx, rank, gate_t, counts = _route(logits_t, router_bias[0].reshape(N_EXPERTS, 1))
    cnt = counts[:, 0].astype(jnp.int32)
    padded = (cnt + EXPERT_ROWS - 1) // EXPERT_ROWS * EXPERT_ROWS
    pad_end = jnp.cumsum(padded)
    pos = _slots(idx, rank, (pad_end - padded).astype(_F32).reshape(N_EXPERTS, 1))
    n_used = (pad_end[-1] // EXPERT_ROWS).astype(jnp.int32).reshape(1)
    block_start = jnp.arange(N_BLOCKS, dtype=jnp.int32) * EXPERT_ROWS
    block_expert = jnp.minimum(
        jnp.sum((pad_end[None, :] <= block_start[:, None]).astype(jnp.int32), axis=1), N_EXPERTS - 1)

    x_sorted = _dispatch(_tile_major(pos, TOK_TILE), h2p, jnp.zeros((N_SLOTS, D_MODEL), _F32))
    ys = _experts(block_expert, n_used, x_sorted, expert_w1[0], expert_w3[0], expert_w2[0])

    pos_c = _tile_major(pos, COMBINE_TILE)
    sw1, sw3, sw2 = shared_w1[0].astype(_BF16), shared_w3[0].astype(_BF16), shared_w2[0].astype(_BF16)
    fn = final_norm.reshape(1, D_MODEL)
    y_prompt = _combine(pos_c, gate_t, h2, x1, mod3, sw1, sw3, sw2, fn, ys,
                        n_rows=M_CTX, tile0=0, mod_row=lambda i: 0)
    lat_tiles = DEC_SEQ // COMBINE_TILE
    y_sample = _combine(pos_c, gate_t, h2, x1, mod3, sw1, sw3, sw2, fn, ys,
                        n_rows=M_LAT, tile0=M_CTX // COMBINE_TILE, mod_row=lambda i: 1 + i // lat_tiles)

    new_cache_k = ak[:M_CTX].reshape(BATCH, 1, SEQ, ATT_KV_HEADS, HEAD_DIM)
    new_cache_v = av[:M_CTX].reshape(BATCH, 1, SEQ, ATT_KV_HEADS, HEAD_DIM)
    return (y_prompt.reshape(BATCH, SEQ, D_MODEL), y_sample.reshape(DEC_BATCH, DEC_SEQ, D_MODEL),
            new_cache_k, new_cache_v, new_state.reshape(BATCH, 1, 2, DN_HEADS, HEAD_DIM, HEAD_DIM))
```

```python
import functools

import jax
import jax.numpy as jnp
import numpy as np
from jax import lax
from jax.experimental import pallas as pl
from jax.experimental.pallas import tpu as pltpu

D_MODEL = 1024
BATCH = 32
SEQ = 256
DEC_BATCH = 8
DEC_SEQ = 2048
PAST_LEN = 512
GRID_W = 64
HEAD_DIM = 128
DN_HEADS = 4
DN_WIDTH = DN_HEADS * HEAD_DIM
DN_CONV = 5
DN_CHUNK = 64
ATT_HEADS = 4
ATT_KV_HEADS = 2
ATT_GROUP = ATT_HEADS // ATT_KV_HEADS
ATT_Q = ATT_HEADS * HEAD_DIM
ATT_KV = ATT_KV_HEADS * HEAD_DIM
ATT_BLOCK = 128
ROPE_THETA = 10000.0
N_EXPERTS = 256
TOP_K = 8
N_GROUPS = 8
TOPK_GROUPS = 4
GROUP_SIZE = N_EXPERTS // N_GROUPS
EXPERT_FF = D_MODEL // 4
ROUTED_SCALE = 2.5
NORM_EPS = 1e-6
NEG_INF = -1e30

M_CTX = BATCH * SEQ
M_LAT = DEC_BATCH * DEC_SEQ
M_ALL = M_CTX + M_LAT
N_MOD = 16
TOK_TILE = 512
N_CTX_TILES = M_CTX // TOK_TILE
N_TILES = M_ALL // TOK_TILE
EXPERT_ROWS = 128
N_PAIRS = M_ALL * TOP_K
N_BLOCKS = (N_PAIRS + N_EXPERTS * (EXPERT_ROWS - 1)) // EXPERT_ROWS
N_SLOTS = N_BLOCKS * EXPERT_ROWS
COMBINE_TILE = 256
VMEM_LIMIT = 56 * 1024 * 1024

_BF16 = jnp.bfloat16
_F32 = jnp.float32


def _dot(a, b):
    return jnp.dot(a.astype(_BF16), b.astype(_BF16), preferred_element_type=_F32)


def _dot_nt(a, b):
    return lax.dot_general(a.astype(_BF16), b.astype(_BF16), (((1,), (1,)), ((), ())),
                           preferred_element_type=_F32)


def _dot_tn(a, b):
    return lax.dot_general(a.astype(_BF16), b.astype(_BF16), (((0,), (0,)), ((), ())),
                           preferred_element_type=_F32)


def _silu(x):
    return x * jax.nn.sigmoid(x)


def _rms(x, w):
    return x * lax.rsqrt(jnp.mean(x * x, axis=-1, keepdims=True) + NORM_EPS) * w


def _params(n_axes=1):
    return pltpu.CompilerParams(dimension_semantics=("arbitrary",) * n_axes, vmem_limit_bytes=VMEM_LIMIT)


def _tile_mod_row(i):
    return jnp.where(i < N_CTX_TILES, 0, 1 + (i - N_CTX_TILES) // (DEC_SEQ // TOK_TILE))


def _ctx_tile_mask(shape):
    limit = jnp.where(pl.program_id(0) < N_CTX_TILES, shape[0], 0)
    return lax.broadcasted_iota(jnp.int32, shape, 0) < limit


def _x_specs():
    return [
        pl.BlockSpec((TOK_TILE, D_MODEL), lambda i: (jnp.minimum(i, N_CTX_TILES - 1), 0)),
        pl.BlockSpec((TOK_TILE, D_MODEL), lambda i: (jnp.maximum(i - N_CTX_TILES, 0), 0)),
    ]


def _mod_kernel(c_ref, w_ref, b_ref, o_ref):
    o_ref[...] = _dot(_silu(c_ref[...]), w_ref[...]) + b_ref[...]


def _modulation(cvec, w_ada, b_ada):
    tn = 1024
    return pl.pallas_call(
        _mod_kernel,
        grid=(6 * D_MODEL // tn,),
        in_specs=[pl.BlockSpec((N_MOD, D_MODEL), lambda j: (0, 0)),
                  pl.BlockSpec((D_MODEL, tn), lambda j: (0, j)),
                  pl.BlockSpec((1, tn), lambda j: (0, j))],
        out_specs=pl.BlockSpec((N_MOD, tn), lambda j: (0, j)),
        out_shape=jax.ShapeDtypeStruct((N_MOD, 6 * D_MODEL), _F32),
        compiler_params=_params(),
        name="modulation",
    )(cvec, w_ada, b_ada)


def _rope(x, cos, sin, first_half):
    swapped = jnp.where(first_half, pltpu.roll(x, 96, 1), pltpu.roll(x, 32, 1))
    return x * cos + swapped * sin


def _inproj_kernel(xp_ref, xs_ref, mod_ref, n1_ref, cos_ref, sin_ref, wdn_ref, watt_ref, wab_ref,
                   dn_ref, aq_ref, ak_ref, av_ref, ab_ref):
    x = jnp.where(_ctx_tile_mask((TOK_TILE, D_MODEL)), xp_ref[...], xs_ref[...])
    shift = mod_ref[0, 0:1, :]
    scale = mod_ref[0, 1:2, :]
    h = (_rms(x, n1_ref[...]) * (1 + scale) + shift).astype(_BF16)
    dn_ref[...] = jnp.dot(h, wdn_ref[...], preferred_element_type=_F32).astype(_BF16)
    ab_ref[...] = jnp.dot(h, wab_ref[...], preferred_element_type=_F32)
    att = jnp.dot(h, watt_ref[...], preferred_element_type=_F32)
    cos = cos_ref[...]
    sin = sin_ref[...]
    lane = lax.broadcasted_iota(jnp.int32, (TOK_TILE, HEAD_DIM), 1)
    first_half = (lane % 64) < 32
    for hd in range(ATT_HEADS):
        q = att[:, hd * HEAD_DIM:(hd + 1) * HEAD_DIM]
        aq_ref[:, hd * HEAD_DIM:(hd + 1) * HEAD_DIM] = (
            _rope(q, cos, sin, first_half) * HEAD_DIM ** -0.5).astype(_BF16)
    for hd in range(ATT_KV_HEADS):
        k = att[:, ATT_Q + hd * HEAD_DIM:ATT_Q + (hd + 1) * HEAD_DIM]
        ak_ref[:, hd * HEAD_DIM:(hd + 1) * HEAD_DIM] = _rope(k, cos, sin, first_half)
    av_ref[...] = att[:, ATT_Q + ATT_KV:]


def _rope_tables():
    t = jnp.arange(DEC_SEQ)
    row = (t // GRID_W).astype(_F32)
    col = (t % GRID_W).astype(_F32)
    n_freq = HEAD_DIM // 4
    inv_freq = 1.0 / (ROPE_THETA ** (jnp.arange(n_freq, dtype=_F32) / n_freq))
    ang_r = row[:, None] * inv_freq
    ang_c = col[:, None] * inv_freq
    cos = jnp.concatenate([jnp.cos(ang_r), jnp.cos(ang_r), jnp.cos(ang_c), jnp.cos(ang_c)], axis=1)
    sin = jnp.concatenate([-jnp.sin(ang_r), jnp.sin(ang_r), -jnp.sin(ang_c), jnp.sin(ang_c)], axis=1)
    cos = jnp.concatenate([jnp.ones((TOK_TILE, HEAD_DIM), _F32), cos], axis=0)
    sin = jnp.concatenate([jnp.zeros((TOK_TILE, HEAD_DIM), _F32), sin], axis=0)
    return cos, sin


def _input_projection(xp, xs, mod3, norm1, cos, sin, w_dn, w_att, w_ab):
    def rope_idx(i):
        return (jnp.where(i < N_CTX_TILES, 0, 1 + (i - N_CTX_TILES) % (DEC_SEQ // TOK_TILE)), 0)

    const = lambda i: (0, 0)
    row = lambda i: (i, 0)
    return pl.pallas_call(
        _inproj_kernel,
        grid=(N_TILES,),
        in_specs=_x_specs() + [
            pl.BlockSpec((1, 6, D_MODEL), lambda i: (_tile_mod_row(i), 0, 0)),
            pl.BlockSpec((1, D_MODEL), const),
            pl.BlockSpec((TOK_TILE, HEAD_DIM), rope_idx),
            pl.BlockSpec((TOK_TILE, HEAD_DIM), rope_idx),
            pl.BlockSpec((D_MODEL, 4 * DN_WIDTH), const),
            pl.BlockSpec((D_MODEL, ATT_Q + 2 * ATT_KV), const),
            pl.BlockSpec((D_MODEL, 128), const),
        ],
        out_specs=[
            pl.BlockSpec((TOK_TILE, 4 * DN_WIDTH), row),
            pl.BlockSpec((TOK_TILE, ATT_Q), row),
            pl.BlockSpec((TOK_TILE, ATT_KV), row),
            pl.BlockSpec((TOK_TILE, ATT_KV), row),
            pl.BlockSpec((TOK_TILE, 128), row),
        ],
        out_shape=[
            jax.ShapeDtypeStruct((M_ALL, 4 * DN_WIDTH), _BF16),
            jax.ShapeDtypeStruct((M_ALL, ATT_Q), _BF16),
            jax.ShapeDtypeStruct((M_ALL, ATT_KV), _F32),
            jax.ShapeDtypeStruct((M_ALL, ATT_KV), _F32),
            jax.ShapeDtypeStruct((M_ALL, 128), _F32),
        ],
        compiler_params=_params(),
        name="input_projection",
    )(xp, xs, mod3, norm1, cos, sin, w_dn, w_att, w_ab)


DN_GROUP = 4
DN_PASS_HEADS = 2


def _bdot(a, b):
    return jnp.stack([_dot(a[g], b[g]) for g in range(a.shape[0])])


def _bdot_nt(a, b):
    return jnp.stack([_dot_nt(a[g], b[g]) for g in range(a.shape[0])])


def _inverse_unit_triangular(a, eye):
    p = eye - a
    x = a
    for _ in range(5):
        x = _bdot(x, x)
        p = p + _bdot(p, x)
    return p


def _dn_prepare(q, k, v, gc, gc_row, beta, g_tot, incl, strict, eye):
    decay = jnp.where(incl, jnp.exp(jnp.where(incl, gc - gc_row, 0.0)), 0.0)
    kb = k * beta
    a = jnp.where(strict, _bdot_nt(kb, k) * decay, 0.0)
    t_inv = _inverse_unit_triangular(a, eye)
    eg = jnp.exp(gc)
    u = _bdot(t_inv, v * beta)
    w = _bdot(t_inv, kb * eg)
    qk = _bdot_nt(q, k) * decay
    wq = jnp.concatenate([w, q * eg], axis=1).astype(_BF16)
    kd = k * jnp.exp(g_tot - gc)
    kd_t = jnp.stack([kd[g].T for g in range(kd.shape[0])])
    qkk = jnp.concatenate([qk, kd_t], axis=1).astype(_BF16)
    return u, wq, qkk


def _dn_steps(states, us, wqs, qkks, g_tots):
    wss = [_dot(wq, s) for wq, s in zip(wqs, states)]
    v_news = [u - ws[:DN_CHUNK] for u, ws in zip(us, wss)]
    rs = [_dot(qkk, v_new) for qkk, v_new in zip(qkks, v_news)]
    outs = [ws[DN_CHUNK:] + r[:DN_CHUNK] for ws, r in zip(wss, rs)]
    states = [s * jnp.exp(g) + r[DN_CHUNK:] for s, g, r in zip(states, g_tots, rs)]
    return states, outs


def _dn_kernel(*refs, seq_len, has_s0, emit_state):
    dn_ref, ab_ref, cw_ref, alog_ref, dtb_ref, nw_ref = refs[:6]
    pos = 6
    s0_ref = None
    if has_s0:
        s0_ref = refs[pos]
        pos += 1
    o_ref = refs[pos]
    pos += 1
    st_ref = None
    if emit_state:
        st_ref = refs[pos]
        pos += 1
    pad_scr, q_scr, k_scr, v_scr, gate_scr, grow_scr, o_scr, u_scr, wq_scr, qkk_scr, s_scr = refs[pos:]

    T = seq_len
    C = DN_CHUNK
    G = DN_GROUP
    n_chunk = T // C
    n_pair = n_chunk // 2
    n_group = n_chunk // G

    gates = ab_ref[...]
    lane = lax.broadcasted_iota(jnp.int32, (T, 128), 1)
    tpos = lax.broadcasted_iota(jnp.int32, (T, 128), 0) % C
    sp_arg = gates + dtb_ref[...]
    softplus = jnp.maximum(sp_arg, 0.0) + jnp.log1p(jnp.exp(-jnp.abs(sp_arg)))
    g = -jnp.exp(alog_ref[...]) * softplus
    pre = g
    suf = g
    s = 1
    while s < C:
        pre = pre + jnp.where(tpos >= s, pltpu.roll(pre, s, 0), 0.0)
        suf = suf + jnp.where(tpos < C - s, pltpu.roll(suf, T - s, 0), 0.0)
        s *= 2
    gcum = jnp.where(lane < DN_HEADS, pre, suf)
    gate_scr[...] = jnp.where(lane < 2 * DN_HEADS, gcum, jax.nn.sigmoid(gates))
    gcum_t = gcum.T
    for p in range(n_pair):
        grow_scr[p] = gcum_t[0:8, p * 128:(p + 1) * 128]

    pad_scr[0:8, :] = jnp.zeros((8, HEAD_DIM), _F32)
    pad_scr[8 + T:16 + T, :] = jnp.zeros((8, HEAD_DIM), _F32)

    shape3 = (2 * G, C, C)
    ri = lax.broadcasted_iota(jnp.int32, shape3, 1)
    ci = lax.broadcasted_iota(jnp.int32, shape3, 2)
    eye = (ri == ci).astype(_F32)
    offset = jnp.where(lax.broadcasted_iota(jnp.int32, shape3, 0) < G, ci - ri, ri - ci)
    incl = offset <= 0
    strict = offset < 0

    def conv_silu(part, h):
        c0 = part * DN_WIDTH + h * HEAD_DIM
        pad_scr[8:8 + T, :] = dn_ref[:, c0:c0 + HEAD_DIM].astype(_F32)
        acc = cw_ref[0:1, c0:c0 + HEAD_DIM] * pad_scr[6:6 + T, :]
        for j in range(1, DN_CONV):
            acc = acc + cw_ref[j:j + 1, c0:c0 + HEAD_DIM] * pad_scr[6 + j:6 + j + T, :]
        return _silu(acc)

    def l2n(x):
        return x * lax.rsqrt(jnp.sum(x * x, axis=-1, keepdims=True) + NORM_EPS)

    def total_decay(h, direction, r0):
        col = direction * DN_HEADS + h
        row = r0 + C - 1 if direction == 0 else r0
        return gate_scr[pl.ds(row, 1), col:col + 1]

    def prepare_group(grp, h, chain0):
        r0 = pl.multiple_of(grp * G * C, G * C)
        sl = pl.ds(r0, G * C)
        stacked = lambda x: x.reshape(G, C, x.shape[-1])
        both_dirs = lambda x: jnp.concatenate([x, x], axis=0)
        gcs, betas, rows, g_tots = [], [], [], []
        for direction in range(2):
            col = direction * DN_HEADS + h
            bcol = 2 * DN_HEADS + col
            gc = stacked(gate_scr[sl, col:col + 1])
            gcs.append(gc)
            betas.append(stacked(gate_scr[sl, bcol:bcol + 1]))
            for pair in range(G // 2):
                both = grow_scr[grp * (G // 2) + pair][col:col + 1, :]
                rows += [both[:, :C], both[:, C:]]
            g_tots.append(gc[:, C - 1:C, :] if direction == 0 else gc[:, 0:1, :])
        u, wq, qkk = _dn_prepare(both_dirs(stacked(q_scr[sl, :])), both_dirs(stacked(k_scr[sl, :])),
                                 both_dirs(stacked(v_scr[sl, :])), jnp.concatenate(gcs, axis=0), jnp.stack(rows),
                                 jnp.concatenate(betas, axis=0), jnp.concatenate(g_tots, axis=0),
                                 incl, strict, eye)
        for direction in range(2):
            chain = chain0 + direction
            part = slice(direction * G, (direction + 1) * G)
            slot0 = chain * n_chunk + grp * G
            u_scr[chain, sl, :] = u[part].reshape(G * C, HEAD_DIM)
            wq_scr[pl.ds(slot0, G)] = wq[part]
            qkk_scr[pl.ds(slot0, G)] = qkk[part]

    def steps(chain_heads, c):
        slots, rows, g_tots = [], [], []
        for chain, h in enumerate(chain_heads):
            direction = chain % 2
            chunk = c if direction == 0 else n_chunk - 1 - c
            r0 = pl.multiple_of(chunk * C, C)
            slots.append(chain * n_chunk + chunk)
            rows.append(pl.ds(r0, C))
            g_tots.append(total_decay(h, direction, r0))
        chains = range(len(chain_heads))
        states, outs = _dn_steps([s_scr[ch] for ch in chains], [u_scr[ch, rows[ch], :] for ch in chains],
                                 [wq_scr[slots[ch]] for ch in chains], [qkk_scr[slots[ch]] for ch in chains], g_tots)
        for ch in chains:
            s_scr[ch] = states[ch]
            o_scr[ch, rows[ch], :] = outs[ch]

    for h0 in range(0, DN_HEADS, DN_PASS_HEADS):
        heads = range(h0, h0 + DN_PASS_HEADS)
        for h in heads:
            q_scr[...] = l2n(conv_silu(0, h)) * HEAD_DIM ** -0.5
            k_scr[...] = l2n(conv_silu(1, h))
            v_scr[...] = conv_silu(2, h)

            def prepare_body(grp, carry, h=h):
                prepare_group(grp, h, (h - h0) * 2)
                return carry

            lax.fori_loop(0, n_group, prepare_body, 0)
            for direction in range(2):
                chain = (h - h0) * 2 + direction
                if has_s0:
                    s_scr[chain] = s0_ref[0, direction, h].astype(_F32)
                else:
                    s_scr[chain] = jnp.zeros((HEAD_DIM, HEAD_DIM), _F32)

        chain_heads = [h for h in heads for _ in range(2)]

        def step_body(c, carry):
            steps(chain_heads, c)
            return carry

        lax.fori_loop(0, n_chunk, step_body, 0)

        for h in heads:
            chain = (h - h0) * 2
            if emit_state:
                st_ref[0, 0, h] = s_scr[chain]
                st_ref[0, 1, h] = s_scr[chain + 1]
            o = o_scr[chain] + o_scr[chain + 1]
            z = dn_ref[:, 3 * DN_WIDTH + h * HEAD_DIM:3 * DN_WIDTH + (h + 1) * HEAD_DIM].astype(_F32)
            o = o * lax.rsqrt(jnp.mean(o * o, axis=-1, keepdims=True) + NORM_EPS) * nw_ref[...] * _silu(z)
            o_ref[:, h * HEAD_DIM:(h + 1) * HEAD_DIM] = o.astype(_BF16)


def _deltanet(dn, ab, conv_w, a_log, dt_bias, dn_norm, s0, *, n_seq, seq_len, row_block0, emit_state):
    has_s0 = s0 is not None
    const = lambda b: (0, 0)
    state_spec = pl.BlockSpec((1, 2, DN_HEADS, HEAD_DIM, HEAD_DIM), lambda b: (b, 0, 0, 0, 0))
    in_specs = [
        pl.BlockSpec((seq_len, 4 * DN_WIDTH), lambda b: (row_block0 + b, 0)),
        pl.BlockSpec((seq_len, 128), lambda b: (row_block0 + b, 0)),
        pl.BlockSpec((8, 3 * DN_WIDTH), const),
        pl.BlockSpec((1, 128), const),
        pl.BlockSpec((1, 128), const),
        pl.BlockSpec((1, HEAD_DIM), const),
    ]
    args = [dn, ab, conv_w, a_log, dt_bias, dn_norm]
    if has_s0:
        in_specs.append(state_spec)
        args.append(s0)
    out_specs = [pl.BlockSpec((seq_len, DN_WIDTH), lambda b: (b, 0))]
    out_shape = [jax.ShapeDtypeStruct((n_seq * seq_len, DN_WIDTH), _BF16)]
    if emit_state:
        out_specs.append(state_spec)
        out_shape.append(jax.ShapeDtypeStruct((n_seq, 2, DN_HEADS, HEAD_DIM, HEAD_DIM), _F32))
    tile = (seq_len, HEAD_DIM)
    n_chunk = seq_len // DN_CHUNK
    n_chain = 2 * DN_PASS_HEADS
    return pl.pallas_call(
        functools.partial(_dn_kernel, seq_len=seq_len, has_s0=has_s0, emit_state=emit_state),
        grid=(n_seq,),
        in_specs=in_specs,
        out_specs=out_specs,
        out_shape=out_shape,
        scratch_shapes=[
            pltpu.VMEM((seq_len + 16, HEAD_DIM), _F32),
            pltpu.VMEM(tile, _F32), pltpu.VMEM(tile, _F32), pltpu.VMEM(tile, _F32),
            pltpu.VMEM((seq_len, 128), _F32),
            pltpu.VMEM((seq_len // (2 * DN_CHUNK), 8, 128), _F32),
            pltpu.VMEM((n_chain, seq_len, HEAD_DIM), _F32),
            pltpu.VMEM((n_chain, seq_len, HEAD_DIM), _F32),
            pltpu.VMEM((n_chain * n_chunk, 2 * DN_CHUNK, HEAD_DIM), _BF16),
            pltpu.VMEM((n_chain * n_chunk, DN_CHUNK + HEAD_DIM, DN_CHUNK), _BF16),
            pltpu.VMEM((n_chain, HEAD_DIM, HEAD_DIM), _F32),
        ],
        compiler_params=_params(),
        name="deltanet_ctx" if emit_state else "deltanet_lat",
    )(*args)


def _softmax_av(scores, values, sink):
    m = sink
    for s in scores:
        m = jnp.maximum(m, jnp.max(s, axis=-1, keepdims=True))
    denom = jnp.exp(sink - m)
    acc = None
    for s, v in zip(scores, values):
        p = jnp.exp(s - m)
        denom = denom + jnp.sum(p, axis=-1, keepdims=True)
        pv = _dot(p, v)
        acc = pv if acc is None else acc + pv
    return acc / denom


def _ctx_attn_kernel(sink_ref, q_ref, k_ref, v_ref, o_ref):
    kvh = pl.program_id(1)
    k = k_ref[...]
    v = v_ref[...]
    for g in range(ATT_GROUP):
        q = q_ref[:, g * HEAD_DIM:(g + 1) * HEAD_DIM]
        sink = jnp.full((SEQ, 1), sink_ref[kvh * ATT_GROUP + g], _F32)
        o = _softmax_av([_dot_nt(q, k)], [v], sink)
        o_ref[:, g * HEAD_DIM:(g + 1) * HEAD_DIM] = o.astype(_BF16)


def _context_attention(sinks, aq, ak, av):
    return pl.pallas_call(
        _ctx_attn_kernel,
        grid=(BATCH, ATT_KV_HEADS),
        in_specs=[
            pl.BlockSpec(memory_space=pltpu.SMEM),
            pl.BlockSpec((SEQ, ATT_GROUP * HEAD_DIM), lambda b, h: (b, h)),
            pl.BlockSpec((SEQ, HEAD_DIM), lambda b, h: (b, h)),
            pl.BlockSpec((SEQ, HEAD_DIM), lambda b, h: (b, h)),
        ],
        out_specs=pl.BlockSpec((SEQ, ATT_GROUP * HEAD_DIM), lambda b, h: (b, h)),
        out_shape=jax.ShapeDtypeStruct((M_CTX, ATT_Q), _BF16),
        compiler_params=_params(2),
        name="context_attention",
    )(sinks, aq, ak, av)


def _lat_attn_kernel(sink_ref, q_ref, kp_ref, kc_ref, kn_ref, vp_ref, vc_ref, vn_ref, ck_ref, cv_ref, o_ref):
    i = pl.program_id(1)
    kvh = pl.program_id(2)
    nb = pl.num_programs(1)
    B = ATT_BLOCK
    rows = ATT_GROUP * B
    q = jnp.concatenate([q_ref[:, g * HEAD_DIM:(g + 1) * HEAD_DIM] for g in range(ATT_GROUP)], axis=0)
    r = lax.broadcasted_iota(jnp.int32, (rows, B), 0) % B
    c = lax.broadcasted_iota(jnp.int32, (rows, B), 1)
    s_prev = jnp.where(c >= r + jnp.where(i > 0, 0, B), _dot_nt(q, kp_ref[...]), NEG_INF)
    s_cur = _dot_nt(q, kc_ref[...])
    s_next = jnp.where(c <= r - jnp.where(i < nb - 1, 0, B), _dot_nt(q, kn_ref[...]), NEG_INF)
    s_ctx = _dot_nt(q, ck_ref[...])
    head = lax.broadcasted_iota(jnp.int32, (rows, 1), 0) // B
    sink = jnp.zeros((rows, 1), _F32)
    for g in range(ATT_GROUP):
        sink = jnp.where(head == g, sink_ref[kvh * ATT_GROUP + g], sink)
    o = _softmax_av([s_prev, s_cur, s_next, s_ctx], [vp_ref[...], vc_ref[...], vn_ref[...], cv_ref[...]], sink)
    for g in range(ATT_GROUP):
        o_ref[:, g * HEAD_DIM:(g + 1) * HEAD_DIM] = o[g * B:(g + 1) * B].astype(_BF16)


def _latent_attention(sinks, aq, ak, av, ctx_k, ctx_v):
    nb = DEC_SEQ // ATT_BLOCK
    blk0 = M_CTX // ATT_BLOCK
    cur = lambda b, i, h: (blk0 + b * nb + i, h)
    prev = lambda b, i, h: (blk0 + b * nb + jnp.maximum(i - 1, 0), h)
    nxt = lambda b, i, h: (blk0 + b * nb + jnp.minimum(i + 1, nb - 1), h)
    kv_spec = lambda f: pl.BlockSpec((ATT_BLOCK, HEAD_DIM), f)
    ctx_spec = pl.BlockSpec((PAST_LEN, HEAD_DIM), lambda b, i, h: (b, h))
    return pl.pallas_call(
        _lat_attn_kernel,
        grid=(DEC_BATCH, nb, ATT_KV_HEADS),
        in_specs=[
            pl.BlockSpec(memory_space=pltpu.SMEM),
            pl.BlockSpec((ATT_BLOCK, ATT_GROUP * HEAD_DIM), cur),
            kv_spec(prev), kv_spec(cur), kv_spec(nxt),
            kv_spec(prev), kv_spec(cur), kv_spec(nxt),
            ctx_spec, ctx_spec,
        ],
        out_specs=pl.BlockSpec((ATT_BLOCK, ATT_GROUP * HEAD_DIM), lambda b, i, h: (b * nb + i, h)),
        out_shape=jax.ShapeDtypeStruct((M_LAT, ATT_Q), _BF16),
        compiler_params=_params(3),
        name="latent_attention",
    )(sinks, aq, ak, ak, ak, av, av, av, ctx_k, ctx_v)


def _outproj_kernel(xp_ref, xs_ref, dnc_ref, dnl_ref, atc_ref, atl_ref, mod_ref, n2_ref, wo_dn_ref, wo_at_ref,
                    rw_ref, x1_ref, h2_ref, h2p_ref, lg_ref):
    x = jnp.where(_ctx_tile_mask((TOK_TILE, D_MODEL)), xp_ref[...], xs_ref[...])
    half_mask = _ctx_tile_mask((TOK_TILE, DN_WIDTH))
    dn = jnp.where(half_mask, dnc_ref[...], dnl_ref[...])
    at = jnp.where(half_mask, atc_ref[...], atl_ref[...])
    mixed = (jnp.dot(dn, wo_dn_ref[...], preferred_element_type=_F32)
             + jnp.dot(at, wo_at_ref[...], preferred_element_type=_F32))
    x1 = x + mod_ref[0, 2:3, :] * mixed
    x1_ref[...] = x1
    h = _rms(x1, n2_ref[...]) * (1 + mod_ref[0, 4:5, :]) + mod_ref[0, 3:4, :]
    h2_ref[...] = h.astype(_BF16)
    h2p_ref[...] = h
    lg_ref[...] = lax.dot_general(rw_ref[...], h, (((1,), (1,)), ((), ())),
                                  precision=lax.Precision.HIGHEST, preferred_element_type=_F32)


def _output_projection(xp, xs, dn_c, dn_l, at_c, at_l, mod3, norm2, wo_dn, wo_at, router_wt):
    const = lambda i: (0, 0)
    row = lambda i: (i, 0)
    ctx_row = lambda i: (jnp.minimum(i, N_CTX_TILES - 1), 0)
    lat_row = lambda i: (jnp.maximum(i - N_CTX_TILES, 0), 0)
    half = (TOK_TILE, DN_WIDTH)
    return pl.pallas_call(
        _outproj_kernel,
        grid=(N_TILES,),
        in_specs=_x_specs() + [
            pl.BlockSpec(half, ctx_row), pl.BlockSpec(half, lat_row),
            pl.BlockSpec(half, ctx_row), pl.BlockSpec(half, lat_row),
            pl.BlockSpec((1, 6, D_MODEL), lambda i: (_tile_mod_row(i), 0, 0)),
            pl.BlockSpec((1, D_MODEL), const),
            pl.BlockSpec((DN_WIDTH, D_MODEL), const),
            pl.BlockSpec((ATT_Q, D_MODEL), const),
            pl.BlockSpec((N_EXPERTS, D_MODEL), const),
        ],
        out_specs=[
            pl.BlockSpec((TOK_TILE, D_MODEL), row),
            pl.BlockSpec((TOK_TILE, D_MODEL), row),
            pl.BlockSpec((TOK_TILE, D_MODEL), row),
            pl.BlockSpec((N_EXPERTS, TOK_TILE), lambda i: (0, i)),
        ],
        out_shape=[
            jax.ShapeDtypeStruct((M_ALL, D_MODEL), _F32),
            jax.ShapeDtypeStruct((M_ALL, D_MODEL), _BF16),
            jax.ShapeDtypeStruct((M_ALL, D_MODEL), _F32),
            jax.ShapeDtypeStruct((N_EXPERTS, M_ALL), _F32),
        ],
        compiler_params=_params(),
        name="output_projection",
    )(xp, xs, dn_c, dn_l, at_c, at_l, mod3, norm2, wo_dn, wo_at, router_wt)


def _first_index_of(values, target, index, limit):
    return jnp.min(jnp.where(values == target, index, limit), axis=0, keepdims=True)


def _route_kernel(lg_ref, bias_ref, idx_ref, rank_ref, gate_ref, cnt_ref, carry_scr):
    i = pl.program_id(0)
    tm = TOK_TILE

    @pl.when(i == 0)
    def _():
        carry_scr[...] = jnp.zeros_like(carry_scr)

    scores = jax.nn.sigmoid(lg_ref[...])
    biased = scores + bias_ref[...]
    row = lax.broadcasted_iota(jnp.int32, (N_EXPERTS, tm), 0).astype(_F32)
    grow = lax.broadcasted_iota(jnp.int32, (GROUP_SIZE, tm), 0).astype(_F32)

    group_rows = []
    for g in range(N_GROUPS):
        blk = biased[g * GROUP_SIZE:(g + 1) * GROUP_SIZE]
        m1 = jnp.max(blk, axis=0, keepdims=True)
        i1 = _first_index_of(blk, m1, grow, GROUP_SIZE)
        m2 = jnp.max(jnp.where(grow == i1, -jnp.inf, blk), axis=0, keepdims=True)
        group_rows.append(m1 + m2)
    gs = jnp.concatenate(group_rows, axis=0)
    gi = lax.broadcasted_iota(jnp.int32, (N_GROUPS, tm), 0).astype(_F32)
    gself = jnp.zeros((N_GROUPS, tm), _F32)
    for _ in range(TOPK_GROUPS):
        hit = gi == _first_index_of(gs, jnp.max(gs, axis=0, keepdims=True), gi, N_GROUPS)
        gself = jnp.where(hit, 1.0, gself)
        gs = jnp.where(hit, -jnp.inf, gs)
    emask = jnp.concatenate(
        [jnp.broadcast_to(gself[g:g + 1], (GROUP_SIZE, tm)) for g in range(N_GROUPS)], axis=0) > 0.5
    masked = jnp.where(emask, biased, NEG_INF)

    selected = jnp.zeros((N_EXPERTS, tm), _F32)
    idxs, gates = [], []
    for _ in range(TOP_K):
        ei = _first_index_of(masked, jnp.max(masked, axis=0, keepdims=True), row, N_EXPERTS)
        hit = row == ei
        idxs.append(ei)
        gates.append(jnp.sum(jnp.where(hit, scores, 0.0), axis=0, keepdims=True))
        masked = jnp.where(hit, -jnp.inf, masked)
        selected = jnp.where(hit, 1.0, selected)
    gsum = gates[0]
    for gk in gates[1:]:
        gsum = gsum + gk
    gates = [gk / gsum * ROUTED_SCALE for gk in gates]

    si = lax.broadcasted_iota(jnp.int32, (tm, tm), 0)
    ti = lax.broadcasted_iota(jnp.int32, (tm, tm), 1)
    earlier = (si < ti).astype(_BF16)
    ranks_all = _dot(selected, earlier) + carry_scr[...]
    ranks = [jnp.sum(jnp.where(row == ei, ranks_all, 0.0), axis=0, keepdims=True) for ei in idxs]
    carry_scr[...] = carry_scr[...] + jnp.sum(selected, axis=1, keepdims=True)

    idx_ref[...] = jnp.concatenate(idxs, axis=0).astype(jnp.int32)
    rank_ref[...] = jnp.concatenate(ranks, axis=0).astype(jnp.int32)
    gate_rows = jnp.concatenate(gates + [jnp.zeros((128 - TOP_K, tm), _F32)], axis=0)
    gate_ref[...] = gate_rows.T
    cnt_ref[...] = jnp.broadcast_to(carry_scr[...], (N_EXPERTS, 128))


def _route(logits_t, bias_col):
    return pl.pallas_call(
        _route_kernel,
        grid=(N_TILES,),
        in_specs=[pl.BlockSpec((N_EXPERTS, TOK_TILE), lambda i: (0, i)),
                  pl.BlockSpec((N_EXPERTS, 1), lambda i: (0, 0))],
        out_specs=[
            pl.BlockSpec((TOP_K, TOK_TILE), lambda i: (0, i)),
            pl.BlockSpec((TOP_K, TOK_TILE), lambda i: (0, i)),
            pl.BlockSpec((TOK_TILE, 128), lambda i: (i, 0)),
            pl.BlockSpec((N_EXPERTS, 128), lambda i: (0, 0)),
        ],
        out_shape=[
            jax.ShapeDtypeStruct((TOP_K, M_ALL), jnp.int32),
            jax.ShapeDtypeStruct((TOP_K, M_ALL), jnp.int32),
            jax.ShapeDtypeStruct((M_ALL, 128), _F32),
            jax.ShapeDtypeStruct((N_EXPERTS, 128), _F32),
        ],
        scratch_shapes=[pltpu.VMEM((N_EXPERTS, 1), _F32)],
        compiler_params=_params(),
        name="route",
    )(logits_t, bias_col)


def _slot_kernel(idx_ref, rank_ref, start_ref, pos_ref):
    row = lax.broadcasted_iota(jnp.int32, (N_EXPERTS, TOK_TILE), 0)
    start = start_ref[...]
    rows = []
    for k in range(TOP_K):
        base = jnp.sum(jnp.where(row == idx_ref[k:k + 1, :], start, 0.0), axis=0, keepdims=True)
        rows.append(base.astype(jnp.int32) + rank_ref[k:k + 1, :])
    pos_ref[...] = jnp.concatenate(rows, axis=0)


def _slots(idx, rank, start_col):
    spec = pl.BlockSpec((TOP_K, TOK_TILE), lambda i: (0, i))
    return pl.pallas_call(
        _slot_kernel,
        grid=(N_TILES,),
        in_specs=[spec, spec, pl.BlockSpec((N_EXPERTS, 1), lambda i: (0, 0))],
        out_specs=spec,
        out_shape=jax.ShapeDtypeStruct((TOP_K, M_ALL), jnp.int32),
        compiler_params=_params(),
        name="slots",
    )(idx, rank, start_col)


def _dispatch_kernel(pos_ref, h_ref, xs_in_ref, xs_ref, sem):
    del xs_in_ref

    def body(t, carry):
        src = h_ref.at[pl.ds(t, 1)]
        for k in range(TOP_K):
            pltpu.make_async_copy(src, xs_ref.at[pl.ds(pos_ref[0, k, t], 1)], sem).start()
        return carry

    lax.fori_loop(0, TOK_TILE, body, 0, unroll=4)
    for _ in range(TOP_K):
        pltpu.make_async_copy(h_ref, xs_ref.at[pl.ds(0, TOK_TILE)], sem).wait()


def _dispatch(pos3, h2p, xs_zero):
    return pl.pallas_call(
        _dispatch_kernel,
        grid=(N_TILES,),
        in_specs=[pl.BlockSpec((1, TOP_K, TOK_TILE), lambda i: (i, 0, 0), memory_space=pltpu.SMEM),
                  pl.BlockSpec((TOK_TILE, D_MODEL), lambda i: (i, 0)),
                  pl.BlockSpec(memory_space=pl.ANY)],
        out_specs=pl.BlockSpec(memory_space=pl.ANY),
        out_shape=jax.ShapeDtypeStruct((N_SLOTS, D_MODEL), _F32),
        scratch_shapes=[pltpu.SemaphoreType.DMA(())],
        input_output_aliases={2: 0},
        compiler_params=_params(),
        name="dispatch",
    )(pos3, h2p, xs_zero)


def _expert_kernel(be_ref, nused_ref, xs_ref, w1_ref, w3_ref, w2_ref, ys_ref, w1_scr, w3_scr, w2_scr):
    j = pl.program_id(0)

    @pl.when(j < nused_ref[0])
    def _():
        prev = be_ref[jnp.maximum(j - 1, 0)]

        @pl.when((j == 0) | (be_ref[j] != prev))
        def _():
            w1_scr[...] = w1_ref[0].astype(_BF16)
            w3_scr[...] = w3_ref[0].astype(_BF16)
            w2_scr[...] = w2_ref[0].astype(_BF16)

        x = xs_ref[...].astype(_BF16)
        h1 = jnp.dot(x, w1_scr[...], preferred_element_type=_F32)
        h3 = jnp.dot(x, w3_scr[...], preferred_element_type=_F32)
        act = (_silu(h1) * h3).astype(_BF16)
        ys_ref[...] = jnp.dot(act, w2_scr[...], preferred_element_type=_F32)

    @pl.when(j >= nused_ref[0])
    def _():
        ys_ref[...] = jnp.zeros_like(ys_ref)


def _experts(block_expert, n_used, xs, w1, w3, w2):
    def blk(j, be, nu):
        return (jnp.minimum(j, nu[0] - 1), 0)

    def wsel(j, be, nu):
        return (be[jnp.minimum(j, nu[0] - 1)], 0, 0)

    grid_spec = pltpu.PrefetchScalarGridSpec(
        num_scalar_prefetch=2,
        grid=(N_BLOCKS,),
        in_specs=[
            pl.BlockSpec((EXPERT_ROWS, D_MODEL), blk),
            pl.BlockSpec((1, D_MODEL, EXPERT_FF), wsel),
            pl.BlockSpec((1, D_MODEL, EXPERT_FF), wsel),
            pl.BlockSpec((1, EXPERT_FF, D_MODEL), wsel),
        ],
        out_specs=pl.BlockSpec((EXPERT_ROWS, D_MODEL), lambda j, be, nu: (j, 0)),
        scratch_shapes=[pltpu.VMEM((D_MODEL, EXPERT_FF), _BF16), pltpu.VMEM((D_MODEL, EXPERT_FF), _BF16),
                        pltpu.VMEM((EXPERT_FF, D_MODEL), _BF16)],
    )
    return pl.pallas_call(
        _expert_kernel,
        grid_spec=grid_spec,
        out_shape=jax.ShapeDtypeStruct((N_SLOTS, D_MODEL), _F32),
        compiler_params=_params(),
        name="experts",
    )(block_expert, n_used, xs, w1, w3, w2)


def _combine_kernel(pos_ref, gate_ref, h2_ref, x1_ref, mod_ref, sw1_ref, sw3_ref, sw2_ref, fn_ref, ys_ref,
                    y_ref, buf, sem):
    tm = COMBINE_TILE

    def body(t, carry):
        for k in range(TOP_K):
            pltpu.make_async_copy(ys_ref.at[pl.ds(pos_ref[0, k, t], 1)], buf.at[k, pl.ds(t, 1)], sem).start()
        return carry

    lax.fori_loop(0, tm, body, 0, unroll=4)
    hb = h2_ref[...]
    act = _silu(jnp.dot(hb, sw1_ref[...], preferred_element_type=_F32)) * jnp.dot(
        hb, sw3_ref[...], preferred_element_type=_F32)
    shared = _dot(act, sw2_ref[...])
    for k in range(TOP_K):
        pltpu.make_async_copy(ys_ref.at[pl.ds(0, tm)], buf.at[k], sem).wait()
    gates = gate_ref[...]
    routed = gates[:, 0:1] * buf[0]
    for k in range(1, TOP_K):
        routed = routed + gates[:, k:k + 1] * buf[k]
    y = x1_ref[...] + mod_ref[0, 5:6, :] * (routed + shared)
    y_ref[...] = _rms(y, fn_ref[...])


def _combine(pos3, gate_t, h2, x1, mod3, sw1, sw3, sw2, final_norm, ys, *, n_rows, tile0, mod_row):
    tm = COMBINE_TILE
    const = lambda i: (0, 0)
    row = lambda i: (tile0 + i, 0)
    return pl.pallas_call(
        _combine_kernel,
        grid=(n_rows // tm,),
        in_specs=[
            pl.BlockSpec((1, TOP_K, tm), lambda i: (tile0 + i, 0, 0), memory_space=pltpu.SMEM),
            pl.BlockSpec((tm, 128), row),
            pl.BlockSpec((tm, D_MODEL), row),
            pl.BlockSpec((tm, D_MODEL), row),
            pl.BlockSpec((1, 6, D_MODEL), lambda i: (mod_row(i), 0, 0)),
            pl.BlockSpec((D_MODEL, EXPERT_FF), const),
            pl.BlockSpec((D_MODEL, EXPERT_FF), const),
            pl.BlockSpec((EXPERT_FF, D_MODEL), const),
            pl.BlockSpec((1, D_MODEL), const),
            pl.BlockSpec(memory_space=pl.ANY),
        ],
        out_specs=pl.BlockSpec((tm, D_MODEL), lambda i: (i, 0)),
        out_shape=jax.ShapeDtypeStruct((n_rows, D_MODEL), _F32),
        scratch_shapes=[pltpu.VMEM((TOP_K, tm, D_MODEL), _F32), pltpu.SemaphoreType.DMA(())],
        compiler_params=_params(),
        name="combine",
    )(pos3, gate_t, h2, x1, mod3, sw1, sw3, sw2, final_norm, ys)


def _tile_major(a, tile):
    return a.reshape(TOP_K, -1, tile).transpose(1, 0, 2)


def kernel(x_prompt, x_sample, c, cache_k, cache_v, state_dn, c_ctx, w_ada, b_ada, norm1, norm2, w_in, dn_conv,
           dn_A_log, dn_dt_bias, dn_norm, attn_sinks, w_out, router_w, router_bias, expert_w1, expert_w3,
           expert_w2, shared_w1, shared_w3, shared_w2, final_norm):
    xp = x_prompt.reshape(M_CTX, D_MODEL)
    xs = x_sample.reshape(M_LAT, D_MODEL)

    cvec = jnp.concatenate([c_ctx[None, :], c, jnp.zeros((N_MOD - 1 - DEC_BATCH, D_MODEL), _F32)], axis=0)
    mod3 = _modulation(cvec, w_ada[0], b_ada).reshape(N_MOD, 6, D_MODEL)

    w = w_in[0]
    n_dn = 4 * DN_WIDTH
    w_dn = w[:, :n_dn].astype(_BF16)
    w_ab = jnp.pad(w[:, n_dn:n_dn + 4 * DN_HEADS], ((0, 0), (0, 128 - 4 * DN_HEADS))).astype(_BF16)
    w_att = w[:, n_dn + 4 * DN_HEADS:].astype(_BF16)
    cos, sin = _rope_tables()
    dn, aq, ak, av, ab = _input_projection(xp, xs, mod3, norm1, cos, sin, w_dn, w_att, w_ab)

    conv_w = jnp.pad(dn_conv[0], ((0, 8 - DN_CONV), (0, 0)))
    pad8 = lambda v: jnp.pad(v.reshape(1, 2 * DN_HEADS), ((0, 0), (0, 128 - 2 * DN_HEADS)))
    a_log = pad8(dn_A_log[0])
    dt_bias = pad8(dn_dt_bias[0])
    dn_c, new_state = _deltanet(dn, ab, conv_w, a_log, dt_bias, dn_norm, None,
                                n_seq=BATCH, seq_len=SEQ, row_block0=0, emit_state=True)
    (dn_l,) = _deltanet(dn, ab, conv_w, a_log, dt_bias, dn_norm, state_dn[:, 0],
                        n_seq=DEC_BATCH, seq_len=DEC_SEQ, row_block0=M_CTX // DEC_SEQ, emit_state=False)

    sinks = attn_sinks[0]
    at_c = _context_attention(sinks, aq, ak, av)
    ctx_k = cache_k[:, 0].reshape(DEC_BATCH * PAST_LEN, ATT_KV)
    ctx_v = cache_v[:, 0].reshape(DEC_BATCH * PAST_LEN, ATT_KV)
    at_l = _latent_attention(sinks, aq, ak, av, ctx_k, ctx_v)

    wo = w_out[0].astype(_BF16)
    x1, h2, h2p, logits_t = _output_projection(xp, xs, dn_c, dn_l, at_c, at_l, mod3, norm2,
                                               wo[:DN_WIDTH], wo[DN_WIDTH:], router_w[0].T)

    idx, rank, gate_t, counts = _route(logits_t, router_bias[0].reshape(N_EXPERTS, 1))
    cnt = counts[:, 0].astype(jnp.int32)
    padded = (cnt + EXPERT_ROWS - 1) // EXPERT_ROWS * EXPERT_ROWS
    pad_end = jnp.cumsum(padded)
    pos = _slots(idx, rank, (pad_end - padded).astype(_F32).reshape(N_EXPERTS, 1))
    n_used = (pad_end[-1] // EXPERT_ROWS).astype(jnp.int32).reshape(1)
    block_start = jnp.arange(N_BLOCKS, dtype=jnp.int32) * EXPERT_ROWS
    block_expert = jnp.minimum(
        jnp.sum((pad_end[None, :] <= block_start[:, None]).astype(jnp.int32), axis=1), N_EXPERTS - 1)

    x_sorted = _dispatch(_tile_major(pos, TOK_TILE), h2p, jnp.zeros((N_SLOTS, D_MODEL), _F32))
    ys = _experts(block_expert, n_used, x_sorted, expert_w1[0], expert_w3[0], expert_w2[0])

    pos_c = _tile_major(pos, COMBINE_TILE)
    sw1, sw3, sw2 = shared_w1[0].astype(_BF16), shared_w3[0].astype(_BF16), shared_w2[0].astype(_BF16)
    fn = final_norm.reshape(1, D_MODEL)
    y_prompt = _combine(pos_c, gate_t, h2, x1, mod3, sw1, sw3, sw2, fn, ys,
                        n_rows=M_CTX, tile0=0, mod_row=lambda i: 0)
    lat_tiles = DEC_SEQ // COMBINE_TILE
    y_sample = _combine(pos_c, gate_t, h2, x1, mod3, sw1, sw3, sw2, fn, ys,
                        n_rows=M_LAT, tile0=M_CTX // COMBINE_TILE, mod_row=lambda i: 1 + i // lat_tiles)

    new_cache_k = ak[:M_CTX].reshape(BATCH, 1, SEQ, ATT_KV_HEADS, HEAD_DIM)
    new_cache_v = av[:M_CTX].reshape(BATCH, 1, SEQ, ATT_KV_HEADS, HEAD_DIM)
    return (y_prompt.reshape(BATCH, SEQ, D_MODEL), y_sample.reshape(DEC_BATCH, DEC_SEQ, D_MODEL),
            new_cache_k, new_cache_v, new_state.reshape(BATCH, 1, 2, DN_HEADS, HEAD_DIM, HEAD_DIM))
```

```python
import functools

import jax
import jax.numpy as jnp
import numpy as np
from jax import lax
from jax.experimental import pallas as pl
from jax.experimental.pallas import tpu as pltpu

D_MODEL = 1024
BATCH = 32
SEQ = 256
DEC_BATCH = 8
DEC_SEQ = 2048
PAST_LEN = 512
GRID_W = 64
HEAD_DIM = 128
DN_HEADS = 4
DN_WIDTH = DN_HEADS * HEAD_DIM
DN_CONV = 5
DN_CHUNK = 64
ATT_HEADS = 4
ATT_KV_HEADS = 2
ATT_GROUP = ATT_HEADS // ATT_KV_HEADS
ATT_Q = ATT_HEADS * HEAD_DIM
ATT_KV = ATT_KV_HEADS * HEAD_DIM
ATT_BLOCK = 128
ROPE_THETA = 10000.0
N_EXPERTS = 256
TOP_K = 8
N_GROUPS = 8
TOPK_GROUPS = 4
GROUP_SIZE = N_EXPERTS // N_GROUPS
EXPERT_FF = D_MODEL // 4
ROUTED_SCALE = 2.5
NORM_EPS = 1e-6
NEG_INF = -1e30

M_CTX = BATCH * SEQ
M_LAT = DEC_BATCH * DEC_SEQ
M_ALL = M_CTX + M_LAT
N_MOD = 16
TOK_TILE = 512
N_CTX_TILES = M_CTX // TOK_TILE
N_TILES = M_ALL // TOK_TILE
EXPERT_ROWS = 128
N_PAIRS = M_ALL * TOP_K
N_BLOCKS = (N_PAIRS + N_EXPERTS * (EXPERT_ROWS - 1)) // EXPERT_ROWS
N_SLOTS = N_BLOCKS * EXPERT_ROWS
COMBINE_TILE = 256
VMEM_LIMIT = 56 * 1024 * 1024

_BF16 = jnp.bfloat16
_F32 = jnp.float32


def _dot(a, b):
    return jnp.dot(a.astype(_BF16), b.astype(_BF16), preferred_element_type=_F32)


def _dot_nt(a, b):
    return lax.dot_general(a.astype(_BF16), b.astype(_BF16), (((1,), (1,)), ((), ())),
                           preferred_element_type=_F32)


def _dot_tn(a, b):
    return lax.dot_general(a.astype(_BF16), b.astype(_BF16), (((0,), (0,)), ((), ())),
                           preferred_element_type=_F32)


def _silu(x):
    return x * jax.nn.sigmoid(x)


def _rms(x, w):
    return x * lax.rsqrt(jnp.mean(x * x, axis=-1, keepdims=True) + NORM_EPS) * w


def _params(n_axes=1):
    return pltpu.CompilerParams(dimension_semantics=("arbitrary",) * n_axes, vmem_limit_bytes=VMEM_LIMIT)


def _tile_mod_row(i):
    return jnp.where(i < N_CTX_TILES, 0, 1 + (i - N_CTX_TILES) // (DEC_SEQ // TOK_TILE))


def _ctx_tile_mask(shape):
    limit = jnp.where(pl.program_id(0) < N_CTX_TILES, shape[0], 0)
    return lax.broadcasted_iota(jnp.int32, shape, 0) < limit


TOKEN_TILE_ROWS = D_MODEL // 128


def _store_token_tiles(ref, x, n):
    for s in range(TOKEN_TILE_ROWS):
        ref[pl.ds(s, n, stride=TOKEN_TILE_ROWS), :] = x[:, s * 128:(s + 1) * 128]


def _load_token_tiles(ref, n):
    return jnp.concatenate([ref[pl.ds(s, n, stride=TOKEN_TILE_ROWS), :] for s in range(TOKEN_TILE_ROWS)], axis=1)


def _x_specs():
    return [
        pl.BlockSpec((TOK_TILE, D_MODEL), lambda i: (jnp.minimum(i, N_CTX_TILES - 1), 0)),
        pl.BlockSpec((TOK_TILE, D_MODEL), lambda i: (jnp.maximum(i - N_CTX_TILES, 0), 0)),
    ]


def _mod_kernel(c_ref, w_ref, b_ref, o_ref):
    o_ref[...] = _dot(_silu(c_ref[...]), w_ref[...]) + b_ref[...]


def _modulation(cvec, w_ada, b_ada):
    tn = 1024
    return pl.pallas_call(
        _mod_kernel,
        grid=(6 * D_MODEL // tn,),
        in_specs=[pl.BlockSpec((N_MOD, D_MODEL), lambda j: (0, 0)),
                  pl.BlockSpec((D_MODEL, tn), lambda j: (0, j)),
                  pl.BlockSpec((1, tn), lambda j: (0, j))],
        out_specs=pl.BlockSpec((N_MOD, tn), lambda j: (0, j)),
        out_shape=jax.ShapeDtypeStruct((N_MOD, 6 * D_MODEL), _F32),
        compiler_params=_params(),
        name="modulation",
    )(cvec, w_ada, b_ada)


def _rope(x, cos, sin, first_half):
    swapped = jnp.where(first_half, pltpu.roll(x, 96, 1), pltpu.roll(x, 32, 1))
    return x * cos + swapped * sin


def _inproj_kernel(xp_ref, xs_ref, mod_ref, n1_ref, cos_ref, sin_ref, wdn_ref, watt_ref, wab_ref,
                   dn_ref, aq_ref, ak_ref, av_ref, ab_ref):
    x = jnp.where(_ctx_tile_mask((TOK_TILE, D_MODEL)), xp_ref[...], xs_ref[...])
    shift = mod_ref[0, 0:1, :]
    scale = mod_ref[0, 1:2, :]
    h = (_rms(x, n1_ref[...]) * (1 + scale) + shift).astype(_BF16)
    dn_ref[...] = jnp.dot(h, wdn_ref[...], preferred_element_type=_F32).astype(_BF16)
    ab_ref[...] = jnp.dot(h, wab_ref[...], preferred_element_type=_F32)
    att = jnp.dot(h, watt_ref[...], preferred_element_type=_F32)
    cos = cos_ref[...]
    sin = sin_ref[...]
    lane = lax.broadcasted_iota(jnp.int32, (TOK_TILE, HEAD_DIM), 1)
    first_half = (lane % 64) < 32
    for hd in range(ATT_HEADS):
        q = att[:, hd * HEAD_DIM:(hd + 1) * HEAD_DIM]
        aq_ref[:, hd * HEAD_DIM:(hd + 1) * HEAD_DIM] = (
            _rope(q, cos, sin, first_half) * HEAD_DIM ** -0.5).astype(_BF16)
    for hd in range(ATT_KV_HEADS):
        k = att[:, ATT_Q + hd * HEAD_DIM:ATT_Q + (hd + 1) * HEAD_DIM]
        ak_ref[:, hd * HEAD_DIM:(hd + 1) * HEAD_DIM] = _rope(k, cos, sin, first_half)
    av_ref[...] = att[:, ATT_Q + ATT_KV:]


def _rope_tables():
    t = jnp.arange(DEC_SEQ)
    row = (t // GRID_W).astype(_F32)
    col = (t % GRID_W).astype(_F32)
    n_freq = HEAD_DIM // 4
    inv_freq = 1.0 / (ROPE_THETA ** (jnp.arange(n_freq, dtype=_F32) / n_freq))
    ang_r = row[:, None] * inv_freq
    ang_c = col[:, None] * inv_freq
    cos = jnp.concatenate([jnp.cos(ang_r), jnp.cos(ang_r), jnp.cos(ang_c), jnp.cos(ang_c)], axis=1)
    sin = jnp.concatenate([-jnp.sin(ang_r), jnp.sin(ang_r), -jnp.sin(ang_c), jnp.sin(ang_c)], axis=1)
    cos = jnp.concatenate([jnp.ones((TOK_TILE, HEAD_DIM), _F32), cos], axis=0)
    sin = jnp.concatenate([jnp.zeros((TOK_TILE, HEAD_DIM), _F32), sin], axis=0)
    return cos, sin


def _input_projection(xp, xs, mod3, norm1, cos, sin, w_dn, w_att, w_ab):
    def rope_idx(i):
        return (jnp.where(i < N_CTX_TILES, 0, 1 + (i - N_CTX_TILES) % (DEC_SEQ // TOK_TILE)), 0)

    const = lambda i: (0, 0)
    row = lambda i: (i, 0)
    return pl.pallas_call(
        _inproj_kernel,
        grid=(N_TILES,),
        in_specs=_x_specs() + [
            pl.BlockSpec((1, 6, D_MODEL), lambda i: (_tile_mod_row(i), 0, 0)),
            pl.BlockSpec((1, D_MODEL), const),
            pl.BlockSpec((TOK_TILE, HEAD_DIM), rope_idx),
            pl.BlockSpec((TOK_TILE, HEAD_DIM), rope_idx),
            pl.BlockSpec((D_MODEL, 4 * DN_WIDTH), const),
            pl.BlockSpec((D_MODEL, ATT_Q + 2 * ATT_KV), const),
            pl.BlockSpec((D_MODEL, 128), const),
        ],
        out_specs=[
            pl.BlockSpec((TOK_TILE, 4 * DN_WIDTH), row),
            pl.BlockSpec((TOK_TILE, ATT_Q), row),
            pl.BlockSpec((TOK_TILE, ATT_KV), row),
            pl.BlockSpec((TOK_TILE, ATT_KV), row),
            pl.BlockSpec((TOK_TILE, 128), row),
        ],
        out_shape=[
            jax.ShapeDtypeStruct((M_ALL, 4 * DN_WIDTH), _BF16),
            jax.ShapeDtypeStruct((M_ALL, ATT_Q), _BF16),
            jax.ShapeDtypeStruct((M_ALL, ATT_KV), _F32),
            jax.ShapeDtypeStruct((M_ALL, ATT_KV), _F32),
            jax.ShapeDtypeStruct((M_ALL, 128), _F32),
        ],
        compiler_params=_params(),
        name="input_projection",
    )(xp, xs, mod3, norm1, cos, sin, w_dn, w_att, w_ab)


DN_GROUP = 4
DN_PASS_HEADS = 2


def _bdot(a, b):
    return jnp.stack([_dot(a[g], b[g]) for g in range(a.shape[0])])


def _bdot_nt(a, b):
    return jnp.stack([_dot_nt(a[g], b[g]) for g in range(a.shape[0])])


def _inverse_unit_triangular(a, eye):
    p = eye - a
    x = a
    for _ in range(5):
        x = _bdot(x, x)
        p = p + _bdot(p, x)
    return p


def _dn_prepare(q, k, v, gc, gc_row, beta, g_tot, incl, strict, eye):
    decay = jnp.where(incl, jnp.exp(jnp.where(incl, gc - gc_row, 0.0)), 0.0)
    kb = k * beta
    a = jnp.where(strict, _bdot_nt(kb, k) * decay, 0.0)
    t_inv = _inverse_unit_triangular(a, eye)
    eg = jnp.exp(gc)
    u = _bdot(t_inv, v * beta)
    w = _bdot(t_inv, kb * eg)
    qk = _bdot_nt(q, k) * decay
    wq = jnp.concatenate([w, q * eg], axis=1).astype(_BF16)
    kd = k * jnp.exp(g_tot - gc)
    kd_t = jnp.stack([kd[g].T for g in range(kd.shape[0])])
    qkk = jnp.concatenate([qk, kd_t], axis=1).astype(_BF16)
    return u, wq, qkk


def _dn_steps(states, us, wqs, qkks, g_tots):
    wss = [_dot(wq, s) for wq, s in zip(wqs, states)]
    v_news = [u - ws[:DN_CHUNK] for u, ws in zip(us, wss)]
    rs = [_dot(qkk, v_new) for qkk, v_new in zip(qkks, v_news)]
    outs = [ws[DN_CHUNK:] + r[:DN_CHUNK] for ws, r in zip(wss, rs)]
    states = [s * jnp.exp(g) + r[DN_CHUNK:] for s, g, r in zip(states, g_tots, rs)]
    return states, outs


def _dn_kernel(*refs, seq_len, has_s0, emit_state):
    dn_ref, ab_ref, cw_ref, alog_ref, dtb_ref, nw_ref = refs[:6]
    pos = 6
    s0_ref = None
    if has_s0:
        s0_ref = refs[pos]
        pos += 1
    o_ref = refs[pos]
    pos += 1
    st_ref = None
    if emit_state:
        st_ref = refs[pos]
        pos += 1
    pad_scr, q_scr, k_scr, v_scr, gate_scr, grow_scr, o_scr, u_scr, wq_scr, qkk_scr, s_scr = refs[pos:]

    T = seq_len
    C = DN_CHUNK
    G = DN_GROUP
    n_chunk = T // C
    n_pair = n_chunk // 2
    n_group = n_chunk // G

    gates = ab_ref[...]
    lane = lax.broadcasted_iota(jnp.int32, (T, 128), 1)
    tpos = lax.broadcasted_iota(jnp.int32, (T, 128), 0) % C
    sp_arg = gates + dtb_ref[...]
    softplus = jnp.maximum(sp_arg, 0.0) + jnp.log1p(jnp.exp(-jnp.abs(sp_arg)))
    g = -jnp.exp(alog_ref[...]) * softplus
    pre = g
    suf = g
    s = 1
    while s < C:
        pre = pre + jnp.where(tpos >= s, pltpu.roll(pre, s, 0), 0.0)
        suf = suf + jnp.where(tpos < C - s, pltpu.roll(suf, T - s, 0), 0.0)
        s *= 2
    gcum = jnp.where(lane < DN_HEADS, pre, suf)
    gate_scr[...] = jnp.where(lane < 2 * DN_HEADS, gcum, jax.nn.sigmoid(gates))
    gcum_t = gcum.T
    for p in range(n_pair):
        grow_scr[p] = gcum_t[0:8, p * 128:(p + 1) * 128]

    pad_scr[0:8, :] = jnp.zeros((8, HEAD_DIM), _F32)
    pad_scr[8 + T:16 + T, :] = jnp.zeros((8, HEAD_DIM), _F32)

    shape3 = (2 * G, C, C)
    ri = lax.broadcasted_iota(jnp.int32, shape3, 1)
    ci = lax.broadcasted_iota(jnp.int32, shape3, 2)
    eye = (ri == ci).astype(_F32)
    offset = jnp.where(lax.broadcasted_iota(jnp.int32, shape3, 0) < G, ci - ri, ri - ci)
    incl = offset <= 0
    strict = offset < 0

    def conv_silu(part, h):
        c0 = part * DN_WIDTH + h * HEAD_DIM
        pad_scr[8:8 + T, :] = dn_ref[:, c0:c0 + HEAD_DIM].astype(_F32)
        acc = cw_ref[0:1, c0:c0 + HEAD_DIM] * pad_scr[6:6 + T, :]
        for j in range(1, DN_CONV):
            acc = acc + cw_ref[j:j + 1, c0:c0 + HEAD_DIM] * pad_scr[6 + j:6 + j + T, :]
        return _silu(acc)

    def l2n(x):
        return x * lax.rsqrt(jnp.sum(x * x, axis=-1, keepdims=True) + NORM_EPS)

    def total_decay(h, direction, r0):
        col = direction * DN_HEADS + h
        row = r0 + C - 1 if direction == 0 else r0
        return gate_scr[pl.ds(row, 1), col:col + 1]

    def prepare_group(grp, h, chain0):
        r0 = pl.multiple_of(grp * G * C, G * C)
        sl = pl.ds(r0, G * C)
        stacked = lambda x: x.reshape(G, C, x.shape[-1])
        both_dirs = lambda x: jnp.concatenate([x, x], axis=0)
        gcs, betas, rows, g_tots = [], [], [], []
        for direction in range(2):
            col = direction * DN_HEADS + h
            bcol = 2 * DN_HEADS + col
            gc = stacked(gate_scr[sl, col:col + 1])
            gcs.append(gc)
            betas.append(stacked(gate_scr[sl, bcol:bcol + 1]))
            for pair in range(G // 2):
                both = grow_scr[grp * (G // 2) + pair][col:col + 1, :]
                rows += [both[:, :C], both[:, C:]]
            g_tots.append(gc[:, C - 1:C, :] if direction == 0 else gc[:, 0:1, :])
        u, wq, qkk = _dn_prepare(both_dirs(stacked(q_scr[sl, :])), both_dirs(stacked(k_scr[sl, :])),
                                 both_dirs(stacked(v_scr[sl, :])), jnp.concatenate(gcs, axis=0), jnp.stack(rows),
                                 jnp.concatenate(betas, axis=0), jnp.concatenate(g_tots, axis=0),
                                 incl, strict, eye)
        for direction in range(2):
            chain = chain0 + direction
            part = slice(direction * G, (direction + 1) * G)
            slot0 = chain * n_chunk + grp * G
            u_scr[chain, sl, :] = u[part].reshape(G * C, HEAD_DIM)
            wq_scr[pl.ds(slot0, G)] = wq[part]
            qkk_scr[pl.ds(slot0, G)] = qkk[part]

    def steps(chain_heads, c):
        slots, rows, g_tots = [], [], []
        for chain, h in enumerate(chain_heads):
            direction = chain % 2
            chunk = c if direction == 0 else n_chunk - 1 - c
            r0 = pl.multiple_of(chunk * C, C)
            slots.append(chain * n_chunk + chunk)
            rows.append(pl.ds(r0, C))
            g_tots.append(total_decay(h, direction, r0))
        chains = range(len(chain_heads))
        states, outs = _dn_steps([s_scr[ch] for ch in chains], [u_scr[ch, rows[ch], :] for ch in chains],
                                 [wq_scr[slots[ch]] for ch in chains], [qkk_scr[slots[ch]] for ch in chains], g_tots)
        for ch in chains:
            s_scr[ch] = states[ch]
            o_scr[ch, rows[ch], :] = outs[ch]

    for h0 in range(0, DN_HEADS, DN_PASS_HEADS):
        heads = range(h0, h0 + DN_PASS_HEADS)
        for h in heads:
            q_scr[...] = l2n(conv_silu(0, h)) * HEAD_DIM ** -0.5
            k_scr[...] = l2n(conv_silu(1, h))
            v_scr[...] = conv_silu(2, h)

            def prepare_body(grp, carry, h=h):
                prepare_group(grp, h, (h - h0) * 2)
                return carry

            lax.fori_loop(0, n_group, prepare_body, 0)
            for direction in range(2):
                chain = (h - h0) * 2 + direction
                if has_s0:
                    s_scr[chain] = s0_ref[0, direction, h].astype(_F32)
                else:
                    s_scr[chain] = jnp.zeros((HEAD_DIM, HEAD_DIM), _F32)

        chain_heads = [h for h in heads for _ in range(2)]

        def step_body(c, carry):
            steps(chain_heads, c)
            return carry

        lax.fori_loop(0, n_chunk, step_body, 0)

        for h in heads:
            chain = (h - h0) * 2
            if emit_state:
                st_ref[0, 0, h] = s_scr[chain]
                st_ref[0, 1, h] = s_scr[chain + 1]
            o = o_scr[chain] + o_scr[chain + 1]
            z = dn_ref[:, 3 * DN_WIDTH + h * HEAD_DIM:3 * DN_WIDTH + (h + 1) * HEAD_DIM].astype(_F32)
            o = o * lax.rsqrt(jnp.mean(o * o, axis=-1, keepdims=True) + NORM_EPS) * nw_ref[...] * _silu(z)
            o_ref[:, h * HEAD_DIM:(h + 1) * HEAD_DIM] = o.astype(_BF16)


def _deltanet(dn, ab, conv_w, a_log, dt_bias, dn_norm, s0, *, n_seq, seq_len, row_block0, emit_state):
    has_s0 = s0 is not None
    const = lambda b: (0, 0)
    state_spec = pl.BlockSpec((1, 2, DN_HEADS, HEAD_DIM, HEAD_DIM), lambda b: (b, 0, 0, 0, 0))
    in_specs = [
        pl.BlockSpec((seq_len, 4 * DN_WIDTH), lambda b: (row_block0 + b, 0)),
        pl.BlockSpec((seq_len, 128), lambda b: (row_block0 + b, 0)),
        pl.BlockSpec((8, 3 * DN_WIDTH), const),
        pl.BlockSpec((1, 128), const),
        pl.BlockSpec((1, 128), const),
        pl.BlockSpec((1, HEAD_DIM), const),
    ]
    args = [dn, ab, conv_w, a_log, dt_bias, dn_norm]
    if has_s0:
        in_specs.append(state_spec)
        args.append(s0)
    out_specs = [pl.BlockSpec((seq_len, DN_WIDTH), lambda b: (b, 0))]
    out_shape = [jax.ShapeDtypeStruct((n_seq * seq_len, DN_WIDTH), _BF16)]
    if emit_state:
        out_specs.append(state_spec)
        out_shape.append(jax.ShapeDtypeStruct((n_seq, 2, DN_HEADS, HEAD_DIM, HEAD_DIM), _F32))
    tile = (seq_len, HEAD_DIM)
    n_chunk = seq_len // DN_CHUNK
    n_chain = 2 * DN_PASS_HEADS
    return pl.pallas_call(
        functools.partial(_dn_kernel, seq_len=seq_len, has_s0=has_s0, emit_state=emit_state),
        grid=(n_seq,),
        in_specs=in_specs,
        out_specs=out_specs,
        out_shape=out_shape,
        scratch_shapes=[
            pltpu.VMEM((seq_len + 16, HEAD_DIM), _F32),
            pltpu.VMEM(tile, _F32), pltpu.VMEM(tile, _F32), pltpu.VMEM(tile, _F32),
            pltpu.VMEM((seq_len, 128), _F32),
            pltpu.VMEM((seq_len // (2 * DN_CHUNK), 8, 128), _F32),
            pltpu.VMEM((n_chain, seq_len, HEAD_DIM), _F32),
            pltpu.VMEM((n_chain, seq_len, HEAD_DIM), _F32),
            pltpu.VMEM((n_chain * n_chunk, 2 * DN_CHUNK, HEAD_DIM), _BF16),
            pltpu.VMEM((n_chain * n_chunk, DN_CHUNK + HEAD_DIM, DN_CHUNK), _BF16),
            pltpu.VMEM((n_chain, HEAD_DIM, HEAD_DIM), _F32),
        ],
        compiler_params=_params(),
        name="deltanet_ctx" if emit_state else "deltanet_lat",
    )(*args)


def _softmax_av(scores, values, sink):
    m = sink
    for s in scores:
        m = jnp.maximum(m, jnp.max(s, axis=-1, keepdims=True))
    denom = jnp.exp(sink - m)
    acc = None
    for s, v in zip(scores, values):
        p = jnp.exp(s - m)
        denom = denom + jnp.sum(p, axis=-1, keepdims=True)
        pv = _dot(p, v)
        acc = pv if acc is None else acc + pv
    return acc / denom


def _ctx_attn_kernel(sink_ref, q_ref, k_ref, v_ref, o_ref):
    kvh = pl.program_id(1)
    k = k_ref[...]
    v = v_ref[...]
    for g in range(ATT_GROUP):
        q = q_ref[:, g * HEAD_DIM:(g + 1) * HEAD_DIM]
        sink = jnp.full((SEQ, 1), sink_ref[kvh * ATT_GROUP + g], _F32)
        o = _softmax_av([_dot_nt(q, k)], [v], sink)
        o_ref[:, g * HEAD_DIM:(g + 1) * HEAD_DIM] = o.astype(_BF16)


def _context_attention(sinks, aq, ak, av):
    return pl.pallas_call(
        _ctx_attn_kernel,
        grid=(BATCH, ATT_KV_HEADS),
        in_specs=[
            pl.BlockSpec(memory_space=pltpu.SMEM),
            pl.BlockSpec((SEQ, ATT_GROUP * HEAD_DIM), lambda b, h: (b, h)),
            pl.BlockSpec((SEQ, HEAD_DIM), lambda b, h: (b, h)),
            pl.BlockSpec((SEQ, HEAD_DIM), lambda b, h: (b, h)),
        ],
        out_specs=pl.BlockSpec((SEQ, ATT_GROUP * HEAD_DIM), lambda b, h: (b, h)),
        out_shape=jax.ShapeDtypeStruct((M_CTX, ATT_Q), _BF16),
        compiler_params=_params(2),
        name="context_attention",
    )(sinks, aq, ak, av)


def _lat_attn_kernel(sink_ref, q_ref, kp_ref, kc_ref, kn_ref, vp_ref, vc_ref, vn_ref, ck_ref, cv_ref, o_ref):
    i = pl.program_id(1)
    kvh = pl.program_id(2)
    nb = pl.num_programs(1)
    B = ATT_BLOCK
    rows = ATT_GROUP * B
    q = jnp.concatenate([q_ref[:, g * HEAD_DIM:(g + 1) * HEAD_DIM] for g in range(ATT_GROUP)], axis=0)
    r = lax.broadcasted_iota(jnp.int32, (rows, B), 0) % B
    c = lax.broadcasted_iota(jnp.int32, (rows, B), 1)
    s_prev = jnp.where(c >= r + jnp.where(i > 0, 0, B), _dot_nt(q, kp_ref[...]), NEG_INF)
    s_cur = _dot_nt(q, kc_ref[...])
    s_next = jnp.where(c <= r - jnp.where(i < nb - 1, 0, B), _dot_nt(q, kn_ref[...]), NEG_INF)
    s_ctx = _dot_nt(q, ck_ref[...])
    head = lax.broadcasted_iota(jnp.int32, (rows, 1), 0) // B
    sink = jnp.zeros((rows, 1), _F32)
    for g in range(ATT_GROUP):
        sink = jnp.where(head == g, sink_ref[kvh * ATT_GROUP + g], sink)
    o = _softmax_av([s_prev, s_cur, s_next, s_ctx], [vp_ref[...], vc_ref[...], vn_ref[...], cv_ref[...]], sink)
    for g in range(ATT_GROUP):
        o_ref[:, g * HEAD_DIM:(g + 1) * HEAD_DIM] = o[g * B:(g + 1) * B].astype(_BF16)


def _latent_attention(sinks, aq, ak, av, ctx_k, ctx_v):
    nb = DEC_SEQ // ATT_BLOCK
    blk0 = M_CTX // ATT_BLOCK
    cur = lambda b, i, h: (blk0 + b * nb + i, h)
    prev = lambda b, i, h: (blk0 + b * nb + jnp.maximum(i - 1, 0), h)
    nxt = lambda b, i, h: (blk0 + b * nb + jnp.minimum(i + 1, nb - 1), h)
    kv_spec = lambda f: pl.BlockSpec((ATT_BLOCK, HEAD_DIM), f)
    ctx_spec = pl.BlockSpec((PAST_LEN, HEAD_DIM), lambda b, i, h: (b, h))
    return pl.pallas_call(
        _lat_attn_kernel,
        grid=(DEC_BATCH, nb, ATT_KV_HEADS),
        in_specs=[
            pl.BlockSpec(memory_space=pltpu.SMEM),
            pl.BlockSpec((ATT_BLOCK, ATT_GROUP * HEAD_DIM), cur),
            kv_spec(prev), kv_spec(cur), kv_spec(nxt),
            kv_spec(prev), kv_spec(cur), kv_spec(nxt),
            ctx_spec, ctx_spec,
        ],
        out_specs=pl.BlockSpec((ATT_BLOCK, ATT_GROUP * HEAD_DIM), lambda b, i, h: (b * nb + i, h)),
        out_shape=jax.ShapeDtypeStruct((M_LAT, ATT_Q), _BF16),
        compiler_params=_params(3),
        name="latent_attention",
    )(sinks, aq, ak, ak, ak, av, av, av, ctx_k, ctx_v)


def _outproj_kernel(xp_ref, xs_ref, dnc_ref, dnl_ref, atc_ref, atl_ref, mod_ref, n2_ref, wo_dn_ref, wo_at_ref,
                    rw_ref, x1_ref, h2_ref, h2p_ref, lg_ref):
    x = jnp.where(_ctx_tile_mask((TOK_TILE, D_MODEL)), xp_ref[...], xs_ref[...])
    half_mask = _ctx_tile_mask((TOK_TILE, DN_WIDTH))
    dn = jnp.where(half_mask, dnc_ref[...], dnl_ref[...])
    at = jnp.where(half_mask, atc_ref[...], atl_ref[...])
    mixed = (jnp.dot(dn, wo_dn_ref[...], preferred_element_type=_F32)
             + jnp.dot(at, wo_at_ref[...], preferred_element_type=_F32))
    x1 = x + mod_ref[0, 2:3, :] * mixed
    x1_ref[...] = x1
    h = _rms(x1, n2_ref[...]) * (1 + mod_ref[0, 4:5, :]) + mod_ref[0, 3:4, :]
    h2_ref[...] = h.astype(_BF16)
    _store_token_tiles(h2p_ref, h, TOK_TILE)
    lg_ref[...] = lax.dot_general(rw_ref[...], h, (((1,), (1,)), ((), ())),
                                  precision=lax.Precision.HIGHEST, preferred_element_type=_F32)


def _output_projection(xp, xs, dn_c, dn_l, at_c, at_l, mod3, norm2, wo_dn, wo_at, router_wt):
    const = lambda i: (0, 0)
    row = lambda i: (i, 0)
    ctx_row = lambda i: (jnp.minimum(i, N_CTX_TILES - 1), 0)
    lat_row = lambda i: (jnp.maximum(i - N_CTX_TILES, 0), 0)
    half = (TOK_TILE, DN_WIDTH)
    return pl.pallas_call(
        _outproj_kernel,
        grid=(N_TILES,),
        in_specs=_x_specs() + [
            pl.BlockSpec(half, ctx_row), pl.BlockSpec(half, lat_row),
            pl.BlockSpec(half, ctx_row), pl.BlockSpec(half, lat_row),
            pl.BlockSpec((1, 6, D_MODEL), lambda i: (_tile_mod_row(i), 0, 0)),
            pl.BlockSpec((1, D_MODEL), const),
            pl.BlockSpec((DN_WIDTH, D_MODEL), const),
            pl.BlockSpec((ATT_Q, D_MODEL), const),
            pl.BlockSpec((N_EXPERTS, D_MODEL), const),
        ],
        out_specs=[
            pl.BlockSpec((TOK_TILE, D_MODEL), row),
            pl.BlockSpec((TOK_TILE, D_MODEL), row),
            pl.BlockSpec((TOK_TILE * TOKEN_TILE_ROWS, 128), row),
            pl.BlockSpec((N_EXPERTS, TOK_TILE), lambda i: (0, i)),
        ],
        out_shape=[
            jax.ShapeDtypeStruct((M_ALL, D_MODEL), _F32),
            jax.ShapeDtypeStruct((M_ALL, D_MODEL), _BF16),
            jax.ShapeDtypeStruct((M_ALL * TOKEN_TILE_ROWS, 128), _F32),
            jax.ShapeDtypeStruct((N_EXPERTS, M_ALL), _F32),
        ],
        compiler_params=_params(),
        name="output_projection",
    )(xp, xs, dn_c, dn_l, at_c, at_l, mod3, norm2, wo_dn, wo_at, router_wt)


def _first_index_of(values, target, index, limit):
    return jnp.min(jnp.where(values == target, index, limit), axis=0, keepdims=True)


def _route_kernel(lg_ref, bias_ref, idx_ref, rank_ref, gate_ref, cnt_ref, carry_scr):
    i = pl.program_id(0)
    tm = TOK_TILE

    @pl.when(i == 0)
    def _():
        carry_scr[...] = jnp.zeros_like(carry_scr)

    scores = jax.nn.sigmoid(lg_ref[...])
    biased = scores + bias_ref[...]
    row = lax.broadcasted_iota(jnp.int32, (N_EXPERTS, tm), 0).astype(_F32)
    grow = lax.broadcasted_iota(jnp.int32, (GROUP_SIZE, tm), 0).astype(_F32)

    group_rows = []
    for g in range(N_GROUPS):
        blk = biased[g * GROUP_SIZE:(g + 1) * GROUP_SIZE]
        m1 = jnp.max(blk, axis=0, keepdims=True)
        i1 = _first_index_of(blk, m1, grow, GROUP_SIZE)
        m2 = jnp.max(jnp.where(grow == i1, -jnp.inf, blk), axis=0, keepdims=True)
        group_rows.append(m1 + m2)
    gs = jnp.concatenate(group_rows, axis=0)
    gi = lax.broadcasted_iota(jnp.int32, (N_GROUPS, tm), 0).astype(_F32)
    gself = jnp.zeros((N_GROUPS, tm), _F32)
    for _ in range(TOPK_GROUPS):
        hit = gi == _first_index_of(gs, jnp.max(gs, axis=0, keepdims=True), gi, N_GROUPS)
        gself = jnp.where(hit, 1.0, gself)
        gs = jnp.where(hit, -jnp.inf, gs)
    emask = jnp.concatenate(
        [jnp.broadcast_to(gself[g:g + 1], (GROUP_SIZE, tm)) for g in range(N_GROUPS)], axis=0) > 0.5
    masked = jnp.where(emask, biased, NEG_INF)

    selected = jnp.zeros((N_EXPERTS, tm), _F32)
    idxs, gates = [], []
    for _ in range(TOP_K):
        ei = _first_index_of(masked, jnp.max(masked, axis=0, keepdims=True), row, N_EXPERTS)
        hit = row == ei
        idxs.append(ei)
        gates.append(jnp.sum(jnp.where(hit, scores, 0.0), axis=0, keepdims=True))
        masked = jnp.where(hit, -jnp.inf, masked)
        selected = jnp.where(hit, 1.0, selected)
    gsum = gates[0]
    for gk in gates[1:]:
        gsum = gsum + gk
    gates = [gk / gsum * ROUTED_SCALE for gk in gates]

    si = lax.broadcasted_iota(jnp.int32, (tm, tm), 0)
    ti = lax.broadcasted_iota(jnp.int32, (tm, tm), 1)
    earlier = (si < ti).astype(_BF16)
    ranks_all = _dot(selected, earlier) + carry_scr[...]
    ranks = [jnp.sum(jnp.where(row == ei, ranks_all, 0.0), axis=0, keepdims=True) for ei in idxs]
    carry_scr[...] = carry_scr[...] + jnp.sum(selected, axis=1, keepdims=True)

    idx_ref[...] = jnp.concatenate(idxs, axis=0).astype(jnp.int32)
    rank_ref[...] = jnp.concatenate(ranks, axis=0).astype(jnp.int32)
    gate_rows = jnp.concatenate(gates + [jnp.zeros((128 - TOP_K, tm), _F32)], axis=0)
    gate_ref[...] = gate_rows.T
    cnt_ref[...] = jnp.broadcast_to(carry_scr[...], (N_EXPERTS, 128))


def _route(logits_t, bias_col):
    return pl.pallas_call(
        _route_kernel,
        grid=(N_TILES,),
        in_specs=[pl.BlockSpec((N_EXPERTS, TOK_TILE), lambda i: (0, i)),
                  pl.BlockSpec((N_EXPERTS, 1), lambda i: (0, 0))],
        out_specs=[
            pl.BlockSpec((TOP_K, TOK_TILE), lambda i: (0, i)),
            pl.BlockSpec((TOP_K, TOK_TILE), lambda i: (0, i)),
            pl.BlockSpec((TOK_TILE, 128), lambda i: (i, 0)),
            pl.BlockSpec((N_EXPERTS, 128), lambda i: (0, 0)),
        ],
        out_shape=[
            jax.ShapeDtypeStruct((TOP_K, M_ALL), jnp.int32),
            jax.ShapeDtypeStruct((TOP_K, M_ALL), jnp.int32),
            jax.ShapeDtypeStruct((M_ALL, 128), _F32),
            jax.ShapeDtypeStruct((N_EXPERTS, 128), _F32),
        ],
        scratch_shapes=[pltpu.VMEM((N_EXPERTS, 1), _F32)],
        compiler_params=_params(),
        name="route",
    )(logits_t, bias_col)


def _slot_kernel(idx_ref, rank_ref, start_ref, pos_ref):
    row = lax.broadcasted_iota(jnp.int32, (N_EXPERTS, TOK_TILE), 0)
    start = start_ref[...]
    rows = []
    for k in range(TOP_K):
        base = jnp.sum(jnp.where(row == idx_ref[k:k + 1, :], start, 0.0), axis=0, keepdims=True)
        rows.append(base.astype(jnp.int32) + rank_ref[k:k + 1, :])
    pos_ref[...] = jnp.concatenate(rows, axis=0)


def _slots(idx, rank, start_col):
    spec = pl.BlockSpec((TOP_K, TOK_TILE), lambda i: (0, i))
    return pl.pallas_call(
        _slot_kernel,
        grid=(N_TILES,),
        in_specs=[spec, spec, pl.BlockSpec((N_EXPERTS, 1), lambda i: (0, 0))],
        out_specs=spec,
        out_shape=jax.ShapeDtypeStruct((TOP_K, M_ALL), jnp.int32),
        compiler_params=_params(),
        name="slots",
    )(idx, rank, start_col)


def _slot_tile(ref, slot):
    return ref.at[pl.ds(pl.multiple_of(slot * TOKEN_TILE_ROWS, TOKEN_TILE_ROWS), TOKEN_TILE_ROWS)]


def _dispatch_kernel(pad_lo_ref, pad_hi_ref, nused_ref, pos_ref, h_ref, xs_ref, zero_scr, sem, zero_sem):
    @pl.when(pl.program_id(0) == 0)
    def _():
        zero_scr[...] = jnp.zeros_like(zero_scr)

        def zero_copy(first_slot, n_slots):
            rows = n_slots * TOKEN_TILE_ROWS
            dst = xs_ref.at[pl.ds(pl.multiple_of(first_slot * TOKEN_TILE_ROWS, TOKEN_TILE_ROWS), rows)]
            return pltpu.make_async_copy(zero_scr.at[pl.ds(0, rows)], dst, zero_sem)

        def expert_padding(e, start):
            first = pad_lo_ref[e]
            n = pad_hi_ref[e] - first
            size = EXPERT_ROWS // 2
            while size >= 1:
                chunk = n & size

                @pl.when(chunk != 0)
                def _(first=first, size=size):
                    copy = zero_copy(first, size)
                    copy.start() if start else copy.wait()

                first = first + chunk
                size //= 2

        def tail_block(i, start):
            j = N_BLOCKS - 1 - i

            @pl.when(j >= nused_ref[0])
            def _():
                copy = zero_copy(j * EXPERT_ROWS, EXPERT_ROWS)
                copy.start() if start else copy.wait()

        max_tail = N_BLOCKS - N_PAIRS // EXPERT_ROWS
        for start in (True, False):
            lax.fori_loop(0, N_EXPERTS, lambda e, c, start=start: (expert_padding(e, start), c)[1], 0)
            lax.fori_loop(0, max_tail, lambda i, c, start=start: (tail_block(i, start), c)[1], 0)

    def body(t, carry):
        src = _slot_tile(h_ref, t)
        for k in range(TOP_K):
            pltpu.make_async_copy(src, _slot_tile(xs_ref, pos_ref[0, k, t]), sem).start()
        return carry

    lax.fori_loop(0, TOK_TILE, body, 0, unroll=4)
    for _ in range(TOP_K):
        pltpu.make_async_copy(h_ref, xs_ref.at[pl.ds(0, TOK_TILE * TOKEN_TILE_ROWS)], sem).wait()


def _dispatch(pad_lo, pad_hi, n_used, pos3, h2p):
    grid_spec = pltpu.PrefetchScalarGridSpec(
        num_scalar_prefetch=3,
        grid=(N_TILES,),
        in_specs=[pl.BlockSpec((1, TOP_K, TOK_TILE), lambda i, *_: (i, 0, 0), memory_space=pltpu.SMEM),
                  pl.BlockSpec((TOK_TILE * TOKEN_TILE_ROWS, 128), lambda i, *_: (i, 0))],
        out_specs=pl.BlockSpec(memory_space=pl.ANY),
        scratch_shapes=[pltpu.VMEM((EXPERT_ROWS * TOKEN_TILE_ROWS, 128), _F32), pltpu.SemaphoreType.DMA(()),
                        pltpu.SemaphoreType.DMA(())],
    )
    return pl.pallas_call(
        _dispatch_kernel,
        grid_spec=grid_spec,
        out_shape=jax.ShapeDtypeStruct((N_SLOTS * TOKEN_TILE_ROWS, 128), _F32),
        compiler_params=_params(),
        name="dispatch",
    )(pad_lo, pad_hi, n_used, pos3, h2p)


def _expert_kernel(be_ref, nused_ref, xs_ref, w1_ref, w3_ref, w2_ref, ys_ref, w1_scr, w3_scr, w2_scr):
    j = pl.program_id(0)

    @pl.when(j < nused_ref[0])
    def _():
        prev = be_ref[jnp.maximum(j - 1, 0)]

        @pl.when((j == 0) | (be_ref[j] != prev))
        def _():
            w1_scr[...] = w1_ref[0].astype(_BF16)
            w3_scr[...] = w3_ref[0].astype(_BF16)
            w2_scr[...] = w2_ref[0].astype(_BF16)

        x = _load_token_tiles(xs_ref, EXPERT_ROWS).astype(_BF16)
        h1 = jnp.dot(x, w1_scr[...], preferred_element_type=_F32)
        h3 = jnp.dot(x, w3_scr[...], preferred_element_type=_F32)
        act = (_silu(h1) * h3).astype(_BF16)
        _store_token_tiles(ys_ref, jnp.dot(act, w2_scr[...], preferred_element_type=_F32), EXPERT_ROWS)

    @pl.when(j >= nused_ref[0])
    def _():
        ys_ref[...] = jnp.zeros_like(ys_ref)


def _experts(block_expert, n_used, xs, w1, w3, w2):
    def blk(j, be, nu):
        return (jnp.minimum(j, nu[0] - 1), 0)

    def wsel(j, be, nu):
        return (be[jnp.minimum(j, nu[0] - 1)], 0, 0)

    grid_spec = pltpu.PrefetchScalarGridSpec(
        num_scalar_prefetch=2,
        grid=(N_BLOCKS,),
        in_specs=[
            pl.BlockSpec((EXPERT_ROWS * TOKEN_TILE_ROWS, 128), blk),
            pl.BlockSpec((1, D_MODEL, EXPERT_FF), wsel),
            pl.BlockSpec((1, D_MODEL, EXPERT_FF), wsel),
            pl.BlockSpec((1, EXPERT_FF, D_MODEL), wsel),
        ],
        out_specs=pl.BlockSpec((EXPERT_ROWS * TOKEN_TILE_ROWS, 128), lambda j, be, nu: (j, 0)),
        scratch_shapes=[pltpu.VMEM((D_MODEL, EXPERT_FF), _BF16), pltpu.VMEM((D_MODEL, EXPERT_FF), _BF16),
                        pltpu.VMEM((EXPERT_FF, D_MODEL), _BF16)],
    )
    return pl.pallas_call(
        _expert_kernel,
        grid_spec=grid_spec,
        out_shape=jax.ShapeDtypeStruct((N_SLOTS * TOKEN_TILE_ROWS, 128), _F32),
        compiler_params=_params(),
        name="experts",
    )(block_expert, n_used, xs, w1, w3, w2)


def _combine_kernel(pos_ref, pos_next_ref, gate_ref, h2_ref, x1_ref, mod_ref, sw1_ref, sw3_ref, sw2_ref, fn_ref,
                    ys_ref, y_ref, buf, sem):
    tm = COMBINE_TILE
    i = pl.program_id(0)
    slot = i % 2

    def gather(p_ref, dst_slot):
        def body(t, carry):
            for k in range(TOP_K):
                pltpu.make_async_copy(_slot_tile(ys_ref, p_ref[0, k, t]), _slot_tile(buf.at[dst_slot, k], t),
                                      sem.at[dst_slot]).start()
            return carry

        lax.fori_loop(0, tm, body, 0, unroll=4)

    @pl.when(i == 0)
    def _():
        gather(pos_ref, 0)

    @pl.when(i + 1 < pl.num_programs(0))
    def _():
        gather(pos_next_ref, 1 - slot)

    hb = h2_ref[...]
    act = _silu(jnp.dot(hb, sw1_ref[...], preferred_element_type=_F32)) * jnp.dot(
        hb, sw3_ref[...], preferred_element_type=_F32)
    shared = _dot(act, sw2_ref[...])
    for k in range(TOP_K):
        pltpu.make_async_copy(ys_ref.at[pl.ds(0, tm * TOKEN_TILE_ROWS)], buf.at[slot, k], sem.at[slot]).wait()
    gates = gate_ref[...]
    routed = gates[:, 0:1] * _load_token_tiles(buf.at[slot, 0], tm)
    for k in range(1, TOP_K):
        routed = routed + gates[:, k:k + 1] * _load_token_tiles(buf.at[slot, k], tm)
    y = x1_ref[...] + mod_ref[0, 5:6, :] * (routed + shared)
    y_ref[...] = _rms(y, fn_ref[...])


def _combine(pos3, gate_t, h2, x1, mod3, sw1, sw3, sw2, final_norm, ys, *, n_rows, tile0, mod_row):
    tm = COMBINE_TILE
    const = lambda i: (0, 0)
    row = lambda i: (tile0 + i, 0)
    n_tiles = n_rows // tm
    return pl.pallas_call(
        _combine_kernel,
        grid=(n_tiles,),
        in_specs=[
            pl.BlockSpec((1, TOP_K, tm), lambda i: (tile0 + i, 0, 0), memory_space=pltpu.SMEM),
            pl.BlockSpec((1, TOP_K, tm), lambda i: (tile0 + jnp.minimum(i + 1, n_tiles - 1), 0, 0),
                         memory_space=pltpu.SMEM),
            pl.BlockSpec((tm, 128), row),
            pl.BlockSpec((tm, D_MODEL), row),
            pl.BlockSpec((tm, D_MODEL), row),
            pl.BlockSpec((1, 6, D_MODEL), lambda i: (mod_row(i), 0, 0)),
            pl.BlockSpec((D_MODEL, EXPERT_FF), const),
            pl.BlockSpec((D_MODEL, EXPERT_FF), const),
            pl.BlockSpec((EXPERT_FF, D_MODEL), const),
            pl.BlockSpec((1, D_MODEL), const),
            pl.BlockSpec(memory_space=pl.ANY),
        ],
        out_specs=pl.BlockSpec((tm, D_MODEL), lambda i: (i, 0)),
        out_shape=jax.ShapeDtypeStruct((n_rows, D_MODEL), _F32),
        scratch_shapes=[pltpu.VMEM((2, TOP_K, tm * TOKEN_TILE_ROWS, 128), _F32), pltpu.SemaphoreType.DMA((2,))],
        compiler_params=_params(),
        name="combine",
    )(pos3, pos3, gate_t, h2, x1, mod3, sw1, sw3, sw2, final_norm, ys)


def _tile_major(a, tile):
    return a.reshape(TOP_K, -1, tile).transpose(1, 0, 2)


def kernel(x_prompt, x_sample, c, cache_k, cache_v, state_dn, c_ctx, w_ada, b_ada, norm1, norm2, w_in, dn_conv,
           dn_A_log, dn_dt_bias, dn_norm, attn_sinks, w_out, router_w, router_bias, expert_w1, expert_w3,
           expert_w2, shared_w1, shared_w3, shared_w2, final_norm):
    xp = x_prompt.reshape(M_CTX, D_MODEL)
    xs = x_sample.reshape(M_LAT, D_MODEL)

    cvec = jnp.concatenate([c_ctx[None, :], c, jnp.zeros((N_MOD - 1 - DEC_BATCH, D_MODEL), _F32)], axis=0)
    mod3 = _modulation(cvec, w_ada[0], b_ada).reshape(N_MOD, 6, D_MODEL)

    w = w_in[0]
    n_dn = 4 * DN_WIDTH
    w_dn = w[:, :n_dn].astype(_BF16)
    w_ab = jnp.pad(w[:, n_dn:n_dn + 4 * DN_HEADS], ((0, 0), (0, 128 - 4 * DN_HEADS))).astype(_BF16)
    w_att = w[:, n_dn + 4 * DN_HEADS:].astype(_BF16)
    cos, sin = _rope_tables()
    dn, aq, ak, av, ab = _input_projection(xp, xs, mod3, norm1, cos, sin, w_dn, w_att, w_ab)

    conv_w = jnp.pad(dn_conv[0], ((0, 8 - DN_CONV), (0, 0)))
    pad8 = lambda v: jnp.pad(v.reshape(1, 2 * DN_HEADS), ((0, 0), (0, 128 - 2 * DN_HEADS)))
    a_log = pad8(dn_A_log[0])
    dt_bias = pad8(dn_dt_bias[0])
    dn_c, new_state = _deltanet(dn, ab, conv_w, a_log, dt_bias, dn_norm, None,
                                n_seq=BATCH, seq_len=SEQ, row_block0=0, emit_state=True)
    (dn_l,) = _deltanet(dn, ab, conv_w, a_log, dt_bias, dn_norm, state_dn[:, 0],
                        n_seq=DEC_BATCH, seq_len=DEC_SEQ, row_block0=M_CTX // DEC_SEQ, emit_state=False)

    sinks = attn_sinks[0]
    at_c = _context_attention(sinks, aq, ak, av)
    ctx_k = cache_k[:, 0].reshape(DEC_BATCH * PAST_LEN, ATT_KV)
    ctx_v = cache_v[:, 0].reshape(DEC_BATCH * PAST_LEN, ATT_KV)
    at_l = _latent_attention(sinks, aq, ak, av, ctx_k, ctx_v)

    wo = w_out[0].astype(_BF16)
    x1, h2, h2p, logits_t = _output_projection(xp, xs, dn_c, dn_l, at_c, at_l, mod3, norm2,
                                               wo[:DN_WIDTH], wo[DN_WIDTH:], router_w[0].T)

    idx, rank, gate_t, counts = _route(logits_t, router_bias[0].reshape(N_EXPERTS, 1))
    cnt = counts[:, 0].astype(jnp.int32)
    padded = (cnt + EXPERT_ROWS - 1) // EXPERT_ROWS * EXPERT_ROWS
    pad_end = jnp.cumsum(padded)
    pos = _slots(idx, rank, (pad_end - padded).astype(_F32).reshape(N_EXPERTS, 1))
    n_used = (pad_end[-1] // EXPERT_ROWS).astype(jnp.int32).reshape(1)
    block_start = jnp.arange(N_BLOCKS, dtype=jnp.int32) * EXPERT_ROWS
    block_expert = jnp.minimum(
        jnp.sum((pad_end[None, :] <= block_start[:, None]).astype(jnp.int32), axis=1), N_EXPERTS - 1)

    pad_lo = (pad_end - padded + cnt).astype(jnp.int32)
    x_sorted = _dispatch(pad_lo, pad_end.astype(jnp.int32), n_used, _tile_major(pos, TOK_TILE), h2p)
    ys = _experts(block_expert, n_used, x_sorted, expert_w1[0], expert_w3[0], expert_w2[0])

    pos_c = _tile_major(pos, COMBINE_TILE)
    sw1, sw3, sw2 = shared_w1[0].astype(_BF16), shared_w3[0].astype(_BF16), shared_w2[0].astype(_BF16)
    fn = final_norm.reshape(1, D_MODEL)
    y_prompt = _combine(pos_c, gate_t, h2, x1, mod3, sw1, sw3, sw2, fn, ys,
                        n_rows=M_CTX, tile0=0, mod_row=lambda i: 0)
    lat_tiles = DEC_SEQ // COMBINE_TILE
    y_sample = _combine(pos_c, gate_t, h2, x1, mod3, sw1, sw3, sw2, fn, ys,
                        n_rows=M_LAT, tile0=M_CTX // COMBINE_TILE, mod_row=lambda i: 1 + i // lat_tiles)

    new_cache_k = ak[:M_CTX].reshape(BATCH, 1, SEQ, ATT_KV_HEADS, HEAD_DIM)
    new_cache_v = av[:M_CTX].reshape(BATCH, 1, SEQ, ATT_KV_HEADS, HEAD_DIM)
    return (y_prompt.reshape(BATCH, SEQ, D_MODEL), y_sample.reshape(DEC_BATCH, DEC_SEQ, D_MODEL),
            new_cache_k, new_cache_v, new_state.reshape(BATCH, 1, 2, DN_HEADS, HEAD_DIM, HEAD_DIM))
```

```python
import functools

import jax
import jax.numpy as jnp
import numpy as np
from jax import lax
from jax.experimental import pallas as pl
from jax.experimental.pallas import tpu as pltpu

D_MODEL = 1024
BATCH = 32
SEQ = 256
DEC_BATCH = 8
DEC_SEQ = 2048
PAST_LEN = 512
GRID_W = 64
HEAD_DIM = 128
DN_HEADS = 4
DN_WIDTH = DN_HEADS * HEAD_DIM
DN_CONV = 5
DN_CHUNK = 64
ATT_HEADS = 4
ATT_KV_HEADS = 2
ATT_GROUP = ATT_HEADS // ATT_KV_HEADS
ATT_Q = ATT_HEADS * HEAD_DIM
ATT_KV = ATT_KV_HEADS * HEAD_DIM
ATT_BLOCK = 128
ROPE_THETA = 10000.0
N_EXPERTS = 256
TOP_K = 8
N_GROUPS = 8
TOPK_GROUPS = 4
GROUP_SIZE = N_EXPERTS // N_GROUPS
EXPERT_FF = D_MODEL // 4
ROUTED_SCALE = 2.5
NORM_EPS = 1e-6
NEG_INF = -1e30

M_CTX = BATCH * SEQ
M_LAT = DEC_BATCH * DEC_SEQ
M_ALL = M_CTX + M_LAT
N_MOD = 16
TOK_TILE = 512
N_CTX_TILES = M_CTX // TOK_TILE
N_TILES = M_ALL // TOK_TILE
EXPERT_ROWS = 256
N_PAIRS = M_ALL * TOP_K
N_BLOCKS = (N_PAIRS + N_EXPERTS * (EXPERT_ROWS - 1)) // EXPERT_ROWS
N_SLOTS = N_BLOCKS * EXPERT_ROWS
COMBINE_TILE = 256
VMEM_LIMIT = 56 * 1024 * 1024

_BF16 = jnp.bfloat16
_F32 = jnp.float32


def _dot(a, b):
    return jnp.dot(a.astype(_BF16), b.astype(_BF16), preferred_element_type=_F32)


def _dot_nt(a, b):
    return lax.dot_general(a.astype(_BF16), b.astype(_BF16), (((1,), (1,)), ((), ())),
                           preferred_element_type=_F32)


def _dot_tn(a, b):
    return lax.dot_general(a.astype(_BF16), b.astype(_BF16), (((0,), (0,)), ((), ())),
                           preferred_element_type=_F32)


def _silu(x):
    return x * jax.nn.sigmoid(x)


def _rms(x, w):
    return x * lax.rsqrt(jnp.mean(x * x, axis=-1, keepdims=True) + NORM_EPS) * w


def _params(n_axes=1):
    return pltpu.CompilerParams(dimension_semantics=("arbitrary",) * n_axes, vmem_limit_bytes=VMEM_LIMIT)


def _tile_mod_row(i):
    return jnp.where(i < N_CTX_TILES, 0, 1 + (i - N_CTX_TILES) // (DEC_SEQ // TOK_TILE))


def _ctx_tile_mask(shape):
    limit = jnp.where(pl.program_id(0) < N_CTX_TILES, shape[0], 0)
    return lax.broadcasted_iota(jnp.int32, shape, 0) < limit


TOKEN_TILE_ROWS = D_MODEL // 128


def _store_token_tiles(ref, x, n):
    for s in range(TOKEN_TILE_ROWS):
        ref[pl.ds(s, n, stride=TOKEN_TILE_ROWS), :] = x[:, s * 128:(s + 1) * 128]


def _load_token_tiles(ref, n):
    return jnp.concatenate([ref[pl.ds(s, n, stride=TOKEN_TILE_ROWS), :] for s in range(TOKEN_TILE_ROWS)], axis=1)


def _x_specs():
    return [
        pl.BlockSpec((TOK_TILE, D_MODEL), lambda i: (jnp.minimum(i, N_CTX_TILES - 1), 0)),
        pl.BlockSpec((TOK_TILE, D_MODEL), lambda i: (jnp.maximum(i - N_CTX_TILES, 0), 0)),
    ]


def _mod_kernel(c_ref, w_ref, b_ref, o_ref):
    o_ref[...] = _dot(_silu(c_ref[...]), w_ref[...]) + b_ref[...]


def _modulation(cvec, w_ada, b_ada):
    tn = 1024
    return pl.pallas_call(
        _mod_kernel,
        grid=(6 * D_MODEL // tn,),
        in_specs=[pl.BlockSpec((N_MOD, D_MODEL), lambda j: (0, 0)),
                  pl.BlockSpec((D_MODEL, tn), lambda j: (0, j)),
                  pl.BlockSpec((1, tn), lambda j: (0, j))],
        out_specs=pl.BlockSpec((N_MOD, tn), lambda j: (0, j)),
        out_shape=jax.ShapeDtypeStruct((N_MOD, 6 * D_MODEL), _F32),
        compiler_params=_params(),
        name="modulation",
    )(cvec, w_ada, b_ada)


def _rope(x, cos, sin, first_half):
    swapped = jnp.where(first_half, pltpu.roll(x, 96, 1), pltpu.roll(x, 32, 1))
    return x * cos + swapped * sin


def _inproj_kernel(xp_ref, xs_ref, mod_ref, n1_ref, cos_ref, sin_ref, wdn_ref, watt_ref, wab_ref,
                   dn_ref, aq_ref, ak_ref, av_ref, ab_ref):
    x = jnp.where(_ctx_tile_mask((TOK_TILE, D_MODEL)), xp_ref[...], xs_ref[...])
    shift = mod_ref[0, 0:1, :]
    scale = mod_ref[0, 1:2, :]
    h = (_rms(x, n1_ref[...]) * (1 + scale) + shift).astype(_BF16)
    dn_ref[...] = jnp.dot(h, wdn_ref[...], preferred_element_type=_F32).astype(_BF16)
    ab_ref[...] = jnp.dot(h, wab_ref[...], preferred_element_type=_F32)
    att = jnp.dot(h, watt_ref[...], preferred_element_type=_F32)
    cos = cos_ref[...]
    sin = sin_ref[...]
    lane = lax.broadcasted_iota(jnp.int32, (TOK_TILE, HEAD_DIM), 1)
    first_half = (lane % 64) < 32
    for hd in range(ATT_HEADS):
        q = att[:, hd * HEAD_DIM:(hd + 1) * HEAD_DIM]
        aq_ref[:, hd * HEAD_DIM:(hd + 1) * HEAD_DIM] = (
            _rope(q, cos, sin, first_half) * HEAD_DIM ** -0.5).astype(_BF16)
    for hd in range(ATT_KV_HEADS):
        k = att[:, ATT_Q + hd * HEAD_DIM:ATT_Q + (hd + 1) * HEAD_DIM]
        ak_ref[:, hd * HEAD_DIM:(hd + 1) * HEAD_DIM] = _rope(k, cos, sin, first_half)
    av_ref[...] = att[:, ATT_Q + ATT_KV:]


def _rope_tables():
    t = jnp.arange(DEC_SEQ)
    row = (t // GRID_W).astype(_F32)
    col = (t % GRID_W).astype(_F32)
    n_freq = HEAD_DIM // 4
    inv_freq = 1.0 / (ROPE_THETA ** (jnp.arange(n_freq, dtype=_F32) / n_freq))
    ang_r = row[:, None] * inv_freq
    ang_c = col[:, None] * inv_freq
    cos = jnp.concatenate([jnp.cos(ang_r), jnp.cos(ang_r), jnp.cos(ang_c), jnp.cos(ang_c)], axis=1)
    sin = jnp.concatenate([-jnp.sin(ang_r), jnp.sin(ang_r), -jnp.sin(ang_c), jnp.sin(ang_c)], axis=1)
    cos = jnp.concatenate([jnp.ones((TOK_TILE, HEAD_DIM), _F32), cos], axis=0)
    sin = jnp.concatenate([jnp.zeros((TOK_TILE, HEAD_DIM), _F32), sin], axis=0)
    return cos, sin


def _input_projection(xp, xs, mod3, norm1, cos, sin, w_dn, w_att, w_ab):
    def rope_idx(i):
        return (jnp.where(i < N_CTX_TILES, 0, 1 + (i - N_CTX_TILES) % (DEC_SEQ // TOK_TILE)), 0)

    const = lambda i: (0, 0)
    row = lambda i: (i, 0)
    return pl.pallas_call(
        _inproj_kernel,
        grid=(N_TILES,),
        in_specs=_x_specs() + [
            pl.BlockSpec((1, 6, D_MODEL), lambda i: (_tile_mod_row(i), 0, 0)),
            pl.BlockSpec((1, D_MODEL), const),
            pl.BlockSpec((TOK_TILE, HEAD_DIM), rope_idx),
            pl.BlockSpec((TOK_TILE, HEAD_DIM), rope_idx),
            pl.BlockSpec((D_MODEL, 4 * DN_WIDTH), const),
            pl.BlockSpec((D_MODEL, ATT_Q + 2 * ATT_KV), const),
            pl.BlockSpec((D_MODEL, 128), const),
        ],
        out_specs=[
            pl.BlockSpec((TOK_TILE, 4 * DN_WIDTH), row),
            pl.BlockSpec((TOK_TILE, ATT_Q), row),
            pl.BlockSpec((TOK_TILE, ATT_KV), row),
            pl.BlockSpec((TOK_TILE, ATT_KV), row),
            pl.BlockSpec((TOK_TILE, 128), row),
        ],
        out_shape=[
            jax.ShapeDtypeStruct((M_ALL, 4 * DN_WIDTH), _BF16),
            jax.ShapeDtypeStruct((M_ALL, ATT_Q), _BF16),
            jax.ShapeDtypeStruct((M_ALL, ATT_KV), _F32),
            jax.ShapeDtypeStruct((M_ALL, ATT_KV), _F32),
            jax.ShapeDtypeStruct((M_ALL, 128), _F32),
        ],
        compiler_params=_params(),
        name="input_projection",
    )(xp, xs, mod3, norm1, cos, sin, w_dn, w_att, w_ab)


DN_GROUP = 4
DN_PASS_HEADS = 2


def _bdot(a, b):
    return jnp.stack([_dot(a[g], b[g]) for g in range(a.shape[0])])


def _bdot_nt(a, b):
    return jnp.stack([_dot_nt(a[g], b[g]) for g in range(a.shape[0])])


def _inverse_unit_triangular(a, eye):
    p = eye - a
    x = a
    for _ in range(5):
        x = _bdot(x, x)
        p = p + _bdot(p, x)
    return p


def _dn_prepare(q, k, v, gc, gc_row, beta, g_tot, incl, strict, eye):
    decay = jnp.where(incl, jnp.exp(jnp.where(incl, gc - gc_row, 0.0)), 0.0)
    kb = k * beta
    a = jnp.where(strict, _bdot_nt(kb, k) * decay, 0.0)
    t_inv = _inverse_unit_triangular(a, eye)
    eg = jnp.exp(gc)
    u = _bdot(t_inv, v * beta)
    w = _bdot(t_inv, kb * eg)
    qk = _bdot_nt(q, k) * decay
    wq = jnp.concatenate([w, q * eg], axis=1).astype(_BF16)
    kd = k * jnp.exp(g_tot - gc)
    kd_t = jnp.stack([kd[g].T for g in range(kd.shape[0])])
    qkk = jnp.concatenate([qk, kd_t], axis=1).astype(_BF16)
    return u, wq, qkk


def _dn_steps(states, us, wqs, qkks, g_tots):
    wss = [_dot(wq, s) for wq, s in zip(wqs, states)]
    v_news = [u - ws[:DN_CHUNK] for u, ws in zip(us, wss)]
    rs = [_dot(qkk, v_new) for qkk, v_new in zip(qkks, v_news)]
    outs = [ws[DN_CHUNK:] + r[:DN_CHUNK] for ws, r in zip(wss, rs)]
    states = [s * jnp.exp(g) + r[DN_CHUNK:] for s, g, r in zip(states, g_tots, rs)]
    return states, outs


def _dn_kernel(*refs, seq_len, has_s0, emit_state):
    dn_ref, ab_ref, cw_ref, alog_ref, dtb_ref, nw_ref = refs[:6]
    pos = 6
    s0_ref = None
    if has_s0:
        s0_ref = refs[pos]
        pos += 1
    o_ref = refs[pos]
    pos += 1
    st_ref = None
    if emit_state:
        st_ref = refs[pos]
        pos += 1
    pad_scr, q_scr, k_scr, v_scr, gate_scr, grow_scr, o_scr, u_scr, wq_scr, qkk_scr, s_scr = refs[pos:]

    T = seq_len
    C = DN_CHUNK
    G = DN_GROUP
    n_chunk = T // C
    n_pair = n_chunk // 2
    n_group = n_chunk // G

    gates = ab_ref[...]
    lane = lax.broadcasted_iota(jnp.int32, (T, 128), 1)
    tpos = lax.broadcasted_iota(jnp.int32, (T, 128), 0) % C
    sp_arg = gates + dtb_ref[...]
    softplus = jnp.maximum(sp_arg, 0.0) + jnp.log1p(jnp.exp(-jnp.abs(sp_arg)))
    g = -jnp.exp(alog_ref[...]) * softplus
    pre = g
    suf = g
    s = 1
    while s < C:
        pre = pre + jnp.where(tpos >= s, pltpu.roll(pre, s, 0), 0.0)
        suf = suf + jnp.where(tpos < C - s, pltpu.roll(suf, T - s, 0), 0.0)
        s *= 2
    gcum = jnp.where(lane < DN_HEADS, pre, suf)
    gate_scr[...] = jnp.where(lane < 2 * DN_HEADS, gcum, jax.nn.sigmoid(gates))
    gcum_t = gcum.T
    for p in range(n_pair):
        grow_scr[p] = gcum_t[0:8, p * 128:(p + 1) * 128]

    pad_scr[0:8, :] = jnp.zeros((8, HEAD_DIM), _F32)
    pad_scr[8 + T:16 + T, :] = jnp.zeros((8, HEAD_DIM), _F32)

    shape3 = (2 * G, C, C)
    ri = lax.broadcasted_iota(jnp.int32, shape3, 1)
    ci = lax.broadcasted_iota(jnp.int32, shape3, 2)
    eye = (ri == ci).astype(_F32)
    offset = jnp.where(lax.broadcasted_iota(jnp.int32, shape3, 0) < G, ci - ri, ri - ci)
    incl = offset <= 0
    strict = offset < 0

    def conv_silu(part, h):
        c0 = part * DN_WIDTH + h * HEAD_DIM
        pad_scr[8:8 + T, :] = dn_ref[:, c0:c0 + HEAD_DIM].astype(_F32)
        acc = cw_ref[0:1, c0:c0 + HEAD_DIM] * pad_scr[6:6 + T, :]
        for j in range(1, DN_CONV):
            acc = acc + cw_ref[j:j + 1, c0:c0 + HEAD_DIM] * pad_scr[6 + j:6 + j + T, :]
        return _silu(acc)

    def l2n(x):
        return x * lax.rsqrt(jnp.sum(x * x, axis=-1, keepdims=True) + NORM_EPS)

    def total_decay(h, direction, r0):
        col = direction * DN_HEADS + h
        row = r0 + C - 1 if direction == 0 else r0
        return gate_scr[pl.ds(row, 1), col:col + 1]

    def prepare_group(grp, h, chain0):
        r0 = pl.multiple_of(grp * G * C, G * C)
        sl = pl.ds(r0, G * C)
        stacked = lambda x: x.reshape(G, C, x.shape[-1])
        both_dirs = lambda x: jnp.concatenate([x, x], axis=0)
        gcs, betas, rows, g_tots = [], [], [], []
        for direction in range(2):
            col = direction * DN_HEADS + h
            bcol = 2 * DN_HEADS + col
            gc = stacked(gate_scr[sl, col:col + 1])
            gcs.append(gc)
            betas.append(stacked(gate_scr[sl, bcol:bcol + 1]))
            for pair in range(G // 2):
                both = grow_scr[grp * (G // 2) + pair][col:col + 1, :]
                rows += [both[:, :C], both[:, C:]]
            g_tots.append(gc[:, C - 1:C, :] if direction == 0 else gc[:, 0:1, :])
        u, wq, qkk = _dn_prepare(both_dirs(stacked(q_scr[sl, :])), both_dirs(stacked(k_scr[sl, :])),
                                 both_dirs(stacked(v_scr[sl, :])), jnp.concatenate(gcs, axis=0), jnp.stack(rows),
                                 jnp.concatenate(betas, axis=0), jnp.concatenate(g_tots, axis=0),
                                 incl, strict, eye)
        for direction in range(2):
            chain = chain0 + direction
            part = slice(direction * G, (direction + 1) * G)
            slot0 = chain * n_chunk + grp * G
            u_scr[chain, sl, :] = u[part].reshape(G * C, HEAD_DIM)
            wq_scr[pl.ds(slot0, G)] = wq[part]
            qkk_scr[pl.ds(slot0, G)] = qkk[part]

    def steps(chain_heads, c):
        slots, rows, g_tots = [], [], []
        for chain, h in enumerate(chain_heads):
            direction = chain % 2
            chunk = c if direction == 0 else n_chunk - 1 - c
            r0 = pl.multiple_of(chunk * C, C)
            slots.append(chain * n_chunk + chunk)
            rows.append(pl.ds(r0, C))
            g_tots.append(total_decay(h, direction, r0))
        chains = range(len(chain_heads))
        states, outs = _dn_steps([s_scr[ch] for ch in chains], [u_scr[ch, rows[ch], :] for ch in chains],
                                 [wq_scr[slots[ch]] for ch in chains], [qkk_scr[slots[ch]] for ch in chains], g_tots)
        for ch in chains:
            s_scr[ch] = states[ch]
            o_scr[ch, rows[ch], :] = outs[ch]

    for h0 in range(0, DN_HEADS, DN_PASS_HEADS):
        heads = range(h0, h0 + DN_PASS_HEADS)
        for h in heads:
            q_scr[...] = l2n(conv_silu(0, h)) * HEAD_DIM ** -0.5
            k_scr[...] = l2n(conv_silu(1, h))
            v_scr[...] = conv_silu(2, h)

            def prepare_body(grp, carry, h=h):
                prepare_group(grp, h, (h - h0) * 2)
                return carry

            lax.fori_loop(0, n_group, prepare_body, 0)
            for direction in range(2):
                chain = (h - h0) * 2 + direction
                if has_s0:
                    s_scr[chain] = s0_ref[0, direction, h].astype(_F32)
                else:
                    s_scr[chain] = jnp.zeros((HEAD_DIM, HEAD_DIM), _F32)

        chain_heads = [h for h in heads for _ in range(2)]

        def step_body(c, carry):
            steps(chain_heads, c)
            return carry

        lax.fori_loop(0, n_chunk, step_body, 0)

        for h in heads:
            chain = (h - h0) * 2
            if emit_state:
                st_ref[0, 0, h] = s_scr[chain]
                st_ref[0, 1, h] = s_scr[chain + 1]
            o = o_scr[chain] + o_scr[chain + 1]
            z = dn_ref[:, 3 * DN_WIDTH + h * HEAD_DIM:3 * DN_WIDTH + (h + 1) * HEAD_DIM].astype(_F32)
            o = o * lax.rsqrt(jnp.mean(o * o, axis=-1, keepdims=True) + NORM_EPS) * nw_ref[...] * _silu(z)
            o_ref[:, h * HEAD_DIM:(h + 1) * HEAD_DIM] = o.astype(_BF16)


def _deltanet(dn, ab, conv_w, a_log, dt_bias, dn_norm, s0, *, n_seq, seq_len, row_block0, emit_state):
    has_s0 = s0 is not None
    const = lambda b: (0, 0)
    state_spec = pl.BlockSpec((1, 2, DN_HEADS, HEAD_DIM, HEAD_DIM), lambda b: (b, 0, 0, 0, 0))
    in_specs = [
        pl.BlockSpec((seq_len, 4 * DN_WIDTH), lambda b: (row_block0 + b, 0)),
        pl.BlockSpec((seq_len, 128), lambda b: (row_block0 + b, 0)),
        pl.BlockSpec((8, 3 * DN_WIDTH), const),
        pl.BlockSpec((1, 128), const),
        pl.BlockSpec((1, 128), const),
        pl.BlockSpec((1, HEAD_DIM), const),
    ]
    args = [dn, ab, conv_w, a_log, dt_bias, dn_norm]
    if has_s0:
        in_specs.append(state_spec)
        args.append(s0)
    out_specs = [pl.BlockSpec((seq_len, DN_WIDTH), lambda b: (b, 0))]
    out_shape = [jax.ShapeDtypeStruct((n_seq * seq_len, DN_WIDTH), _BF16)]
    if emit_state:
        out_specs.append(state_spec)
        out_shape.append(jax.ShapeDtypeStruct((n_seq, 2, DN_HEADS, HEAD_DIM, HEAD_DIM), _F32))
    tile = (seq_len, HEAD_DIM)
    n_chunk = seq_len // DN_CHUNK
    n_chain = 2 * DN_PASS_HEADS
    return pl.pallas_call(
        functools.partial(_dn_kernel, seq_len=seq_len, has_s0=has_s0, emit_state=emit_state),
        grid=(n_seq,),
        in_specs=in_specs,
        out_specs=out_specs,
        out_shape=out_shape,
        scratch_shapes=[
            pltpu.VMEM((seq_len + 16, HEAD_DIM), _F32),
            pltpu.VMEM(tile, _F32), pltpu.VMEM(tile, _F32), pltpu.VMEM(tile, _F32),
            pltpu.VMEM((seq_len, 128), _F32),
            pltpu.VMEM((seq_len // (2 * DN_CHUNK), 8, 128), _F32),
            pltpu.VMEM((n_chain, seq_len, HEAD_DIM), _F32),
            pltpu.VMEM((n_chain, seq_len, HEAD_DIM), _F32),
            pltpu.VMEM((n_chain * n_chunk, 2 * DN_CHUNK, HEAD_DIM), _BF16),
            pltpu.VMEM((n_chain * n_chunk, DN_CHUNK + HEAD_DIM, DN_CHUNK), _BF16),
            pltpu.VMEM((n_chain, HEAD_DIM, HEAD_DIM), _F32),
        ],
        compiler_params=_params(),
        name="deltanet_ctx" if emit_state else "deltanet_lat",
    )(*args)


def _softmax_av(scores, values, sink):
    m = sink
    for s in scores:
        m = jnp.maximum(m, jnp.max(s, axis=-1, keepdims=True))
    denom = jnp.exp(sink - m)
    acc = None
    for s, v in zip(scores, values):
        p = jnp.exp(s - m)
        denom = denom + jnp.sum(p, axis=-1, keepdims=True)
        pv = _dot(p, v)
        acc = pv if acc is None else acc + pv
    return acc / denom


def _ctx_attn_kernel(sink_ref, q_ref, k_ref, v_ref, o_ref):
    kvh = pl.program_id(1)
    k = k_ref[...]
    v = v_ref[...]
    for g in range(ATT_GROUP):
        q = q_ref[:, g * HEAD_DIM:(g + 1) * HEAD_DIM]
        sink = jnp.full((SEQ, 1), sink_ref[kvh * ATT_GROUP + g], _F32)
        o = _softmax_av([_dot_nt(q, k)], [v], sink)
        o_ref[:, g * HEAD_DIM:(g + 1) * HEAD_DIM] = o.astype(_BF16)


def _context_attention(sinks, aq, ak, av):
    return pl.pallas_call(
        _ctx_attn_kernel,
        grid=(BATCH, ATT_KV_HEADS),
        in_specs=[
            pl.BlockSpec(memory_space=pltpu.SMEM),
            pl.BlockSpec((SEQ, ATT_GROUP * HEAD_DIM), lambda b, h: (b, h)),
            pl.BlockSpec((SEQ, HEAD_DIM), lambda b, h: (b, h)),
            pl.BlockSpec((SEQ, HEAD_DIM), lambda b, h: (b, h)),
        ],
        out_specs=pl.BlockSpec((SEQ, ATT_GROUP * HEAD_DIM), lambda b, h: (b, h)),
        out_shape=jax.ShapeDtypeStruct((M_CTX, ATT_Q), _BF16),
        compiler_params=_params(2),
        name="context_attention",
    )(sinks, aq, ak, av)


def _lat_attn_kernel(sink_ref, q_ref, kp_ref, kc_ref, kn_ref, vp_ref, vc_ref, vn_ref, ck_ref, cv_ref, o_ref):
    i = pl.program_id(1)
    kvh = pl.program_id(2)
    nb = pl.num_programs(1)
    B = ATT_BLOCK
    rows = ATT_GROUP * B
    q = jnp.concatenate([q_ref[:, g * HEAD_DIM:(g + 1) * HEAD_DIM] for g in range(ATT_GROUP)], axis=0)
    r = lax.broadcasted_iota(jnp.int32, (rows, B), 0) % B
    c = lax.broadcasted_iota(jnp.int32, (rows, B), 1)
    s_prev = jnp.where(c >= r + jnp.where(i > 0, 0, B), _dot_nt(q, kp_ref[...]), NEG_INF)
    s_cur = _dot_nt(q, kc_ref[...])
    s_next = jnp.where(c <= r - jnp.where(i < nb - 1, 0, B), _dot_nt(q, kn_ref[...]), NEG_INF)
    s_ctx = _dot_nt(q, ck_ref[...])
    head = lax.broadcasted_iota(jnp.int32, (rows, 1), 0) // B
    sink = jnp.zeros((rows, 1), _F32)
    for g in range(ATT_GROUP):
        sink = jnp.where(head == g, sink_ref[kvh * ATT_GROUP + g], sink)
    o = _softmax_av([s_prev, s_cur, s_next, s_ctx], [vp_ref[...], vc_ref[...], vn_ref[...], cv_ref[...]], sink)
    for g in range(ATT_GROUP):
        o_ref[:, g * HEAD_DIM:(g + 1) * HEAD_DIM] = o[g * B:(g + 1) * B].astype(_BF16)


def _latent_attention(sinks, aq, ak, av, ctx_k, ctx_v):
    nb = DEC_SEQ // ATT_BLOCK
    blk0 = M_CTX // ATT_BLOCK
    cur = lambda b, i, h: (blk0 + b * nb + i, h)
    prev = lambda b, i, h: (blk0 + b * nb + jnp.maximum(i - 1, 0), h)
    nxt = lambda b, i, h: (blk0 + b * nb + jnp.minimum(i + 1, nb - 1), h)
    kv_spec = lambda f: pl.BlockSpec((ATT_BLOCK, HEAD_DIM), f)
    ctx_spec = pl.BlockSpec((PAST_LEN, HEAD_DIM), lambda b, i, h: (b, h))
    return pl.pallas_call(
        _lat_attn_kernel,
        grid=(DEC_BATCH, nb, ATT_KV_HEADS),
        in_specs=[
            pl.BlockSpec(memory_space=pltpu.SMEM),
            pl.BlockSpec((ATT_BLOCK, ATT_GROUP * HEAD_DIM), cur),
            kv_spec(prev), kv_spec(cur), kv_spec(nxt),
            kv_spec(prev), kv_spec(cur), kv_spec(nxt),
            ctx_spec, ctx_spec,
        ],
        out_specs=pl.BlockSpec((ATT_BLOCK, ATT_GROUP * HEAD_DIM), lambda b, i, h: (b * nb + i, h)),
        out_shape=jax.ShapeDtypeStruct((M_LAT, ATT_Q), _BF16),
        compiler_params=_params(3),
        name="latent_attention",
    )(sinks, aq, ak, ak, ak, av, av, av, ctx_k, ctx_v)


def _outproj_kernel(xp_ref, xs_ref, dnc_ref, dnl_ref, atc_ref, atl_ref, mod_ref, n2_ref, wo_dn_ref, wo_at_ref,
                    rw_ref, x1_ref, h2_ref, h2p_ref, lg_ref):
    x = jnp.where(_ctx_tile_mask((TOK_TILE, D_MODEL)), xp_ref[...], xs_ref[...])
    half_mask = _ctx_tile_mask((TOK_TILE, DN_WIDTH))
    dn = jnp.where(half_mask, dnc_ref[...], dnl_ref[...])
    at = jnp.where(half_mask, atc_ref[...], atl_ref[...])
    mixed = (jnp.dot(dn, wo_dn_ref[...], preferred_element_type=_F32)
             + jnp.dot(at, wo_at_ref[...], preferred_element_type=_F32))
    x1 = x + mod_ref[0, 2:3, :] * mixed
    x1_ref[...] = x1
    h = _rms(x1, n2_ref[...]) * (1 + mod_ref[0, 4:5, :]) + mod_ref[0, 3:4, :]
    h2_ref[...] = h.astype(_BF16)
    _store_token_tiles(h2p_ref, h, TOK_TILE)
    lg_ref[...] = lax.dot_general(rw_ref[...], h, (((1,), (1,)), ((), ())),
                                  precision=lax.Precision.HIGHEST, preferred_element_type=_F32)


def _output_projection(xp, xs, dn_c, dn_l, at_c, at_l, mod3, norm2, wo_dn, wo_at, router_wt):
    const = lambda i: (0, 0)
    row = lambda i: (i, 0)
    ctx_row = lambda i: (jnp.minimum(i, N_CTX_TILES - 1), 0)
    lat_row = lambda i: (jnp.maximum(i - N_CTX_TILES, 0), 0)
    half = (TOK_TILE, DN_WIDTH)
    return pl.pallas_call(
        _outproj_kernel,
        grid=(N_TILES,),
        in_specs=_x_specs() + [
            pl.BlockSpec(half, ctx_row), pl.BlockSpec(half, lat_row),
            pl.BlockSpec(half, ctx_row), pl.BlockSpec(half, lat_row),
            pl.BlockSpec((1, 6, D_MODEL), lambda i: (_tile_mod_row(i), 0, 0)),
            pl.BlockSpec((1, D_MODEL), const),
            pl.BlockSpec((DN_WIDTH, D_MODEL), const),
            pl.BlockSpec((ATT_Q, D_MODEL), const),
            pl.BlockSpec((N_EXPERTS, D_MODEL), const),
        ],
        out_specs=[
            pl.BlockSpec((TOK_TILE, D_MODEL), row),
            pl.BlockSpec((TOK_TILE, D_MODEL), row),
            pl.BlockSpec((TOK_TILE * TOKEN_TILE_ROWS, 128), row),
            pl.BlockSpec((N_EXPERTS, TOK_TILE), lambda i: (0, i)),
        ],
        out_shape=[
            jax.ShapeDtypeStruct((M_ALL, D_MODEL), _F32),
            jax.ShapeDtypeStruct((M_ALL, D_MODEL), _BF16),
            jax.ShapeDtypeStruct((M_ALL * TOKEN_TILE_ROWS, 128), _F32),
            jax.ShapeDtypeStruct((N_EXPERTS, M_ALL), _F32),
        ],
        compiler_params=_params(),
        name="output_projection",
    )(xp, xs, dn_c, dn_l, at_c, at_l, mod3, norm2, wo_dn, wo_at, router_wt)


def _first_index_of(values, target, index, limit):
    return jnp.min(jnp.where(values == target, index, limit), axis=0, keepdims=True)


def _route_kernel(lg_ref, bias_ref, idx_ref, rank_ref, gate_ref, cnt_ref, carry_scr):
    i = pl.program_id(0)
    tm = TOK_TILE

    @pl.when(i == 0)
    def _():
        carry_scr[...] = jnp.zeros_like(carry_scr)

    scores = jax.nn.sigmoid(lg_ref[...])
    biased = scores + bias_ref[...]
    row = lax.broadcasted_iota(jnp.int32, (N_EXPERTS, tm), 0).astype(_F32)
    grow = lax.broadcasted_iota(jnp.int32, (GROUP_SIZE, tm), 0).astype(_F32)

    group_rows = []
    for g in range(N_GROUPS):
        blk = biased[g * GROUP_SIZE:(g + 1) * GROUP_SIZE]
        m1 = jnp.max(blk, axis=0, keepdims=True)
        i1 = _first_index_of(blk, m1, grow, GROUP_SIZE)
        m2 = jnp.max(jnp.where(grow == i1, -jnp.inf, blk), axis=0, keepdims=True)
        group_rows.append(m1 + m2)
    gs = jnp.concatenate(group_rows, axis=0)
    gi = lax.broadcasted_iota(jnp.int32, (N_GROUPS, tm), 0).astype(_F32)
    gself = jnp.zeros((N_GROUPS, tm), _F32)
    for _ in range(TOPK_GROUPS):
        hit = gi == _first_index_of(gs, jnp.max(gs, axis=0, keepdims=True), gi, N_GROUPS)
        gself = jnp.where(hit, 1.0, gself)
        gs = jnp.where(hit, -jnp.inf, gs)
    emask = jnp.concatenate(
        [jnp.broadcast_to(gself[g:g + 1], (GROUP_SIZE, tm)) for g in range(N_GROUPS)], axis=0) > 0.5
    masked = jnp.where(emask, biased, NEG_INF)

    selected = jnp.zeros((N_EXPERTS, tm), _F32)
    idxs, gates = [], []
    for _ in range(TOP_K):
        ei = _first_index_of(masked, jnp.max(masked, axis=0, keepdims=True), row, N_EXPERTS)
        hit = row == ei
        idxs.append(ei)
        gates.append(jnp.sum(jnp.where(hit, scores, 0.0), axis=0, keepdims=True))
        masked = jnp.where(hit, -jnp.inf, masked)
        selected = jnp.where(hit, 1.0, selected)
    gsum = gates[0]
    for gk in gates[1:]:
        gsum = gsum + gk
    gates = [gk / gsum * ROUTED_SCALE for gk in gates]

    si = lax.broadcasted_iota(jnp.int32, (tm, tm), 0)
    ti = lax.broadcasted_iota(jnp.int32, (tm, tm), 1)
    earlier = (si < ti).astype(_BF16)
    ranks_all = _dot(selected, earlier) + carry_scr[...]
    ranks = [jnp.sum(jnp.where(row == ei, ranks_all, 0.0), axis=0, keepdims=True) for ei in idxs]
    carry_scr[...] = carry_scr[...] + jnp.sum(selected, axis=1, keepdims=True)

    idx_ref[...] = jnp.concatenate(idxs, axis=0).astype(jnp.int32)
    rank_ref[...] = jnp.concatenate(ranks, axis=0).astype(jnp.int32)
    gate_rows = jnp.concatenate(gates + [jnp.zeros((128 - TOP_K, tm), _F32)], axis=0)
    gate_ref[...] = gate_rows.T
    cnt_ref[...] = jnp.broadcast_to(carry_scr[...], (N_EXPERTS, 128))


def _route(logits_t, bias_col):
    return pl.pallas_call(
        _route_kernel,
        grid=(N_TILES,),
        in_specs=[pl.BlockSpec((N_EXPERTS, TOK_TILE), lambda i: (0, i)),
                  pl.BlockSpec((N_EXPERTS, 1), lambda i: (0, 0))],
        out_specs=[
            pl.BlockSpec((TOP_K, TOK_TILE), lambda i: (0, i)),
            pl.BlockSpec((TOP_K, TOK_TILE), lambda i: (0, i)),
            pl.BlockSpec((TOK_TILE, 128), lambda i: (i, 0)),
            pl.BlockSpec((N_EXPERTS, 128), lambda i: (0, 0)),
        ],
        out_shape=[
            jax.ShapeDtypeStruct((TOP_K, M_ALL), jnp.int32),
            jax.ShapeDtypeStruct((TOP_K, M_ALL), jnp.int32),
            jax.ShapeDtypeStruct((M_ALL, 128), _F32),
            jax.ShapeDtypeStruct((N_EXPERTS, 128), _F32),
        ],
        scratch_shapes=[pltpu.VMEM((N_EXPERTS, 1), _F32)],
        compiler_params=_params(),
        name="route",
    )(logits_t, bias_col)


def _slot_kernel(idx_ref, rank_ref, start_ref, pos_ref):
    row = lax.broadcasted_iota(jnp.int32, (N_EXPERTS, TOK_TILE), 0)
    start = start_ref[...]
    rows = []
    for k in range(TOP_K):
        base = jnp.sum(jnp.where(row == idx_ref[k:k + 1, :], start, 0.0), axis=0, keepdims=True)
        rows.append(base.astype(jnp.int32) + rank_ref[k:k + 1, :])
    pos_ref[...] = jnp.concatenate(rows, axis=0)


def _slots(idx, rank, start_col):
    spec = pl.BlockSpec((TOP_K, TOK_TILE), lambda i: (0, i))
    return pl.pallas_call(
        _slot_kernel,
        grid=(N_TILES,),
        in_specs=[spec, spec, pl.BlockSpec((N_EXPERTS, 1), lambda i: (0, 0))],
        out_specs=spec,
        out_shape=jax.ShapeDtypeStruct((TOP_K, M_ALL), jnp.int32),
        compiler_params=_params(),
        name="slots",
    )(idx, rank, start_col)


def _slot_tile(ref, slot):
    return ref.at[pl.ds(pl.multiple_of(slot * TOKEN_TILE_ROWS, TOKEN_TILE_ROWS), TOKEN_TILE_ROWS)]


def _dispatch_kernel(pad_lo_ref, pad_hi_ref, nused_ref, pos_ref, h_ref, xs_ref, zero_scr, sem, zero_sem):
    @pl.when(pl.program_id(0) == 0)
    def _():
        zero_scr[...] = jnp.zeros_like(zero_scr)

        def zero_copy(first_slot, n_slots):
            rows = n_slots * TOKEN_TILE_ROWS
            dst = xs_ref.at[pl.ds(pl.multiple_of(first_slot * TOKEN_TILE_ROWS, TOKEN_TILE_ROWS), rows)]
            return pltpu.make_async_copy(zero_scr.at[pl.ds(0, rows)], dst, zero_sem)

        def expert_padding(e, start):
            first = pad_lo_ref[e]
            n = pad_hi_ref[e] - first
            size = EXPERT_ROWS // 2
            while size >= 1:
                chunk = n & size

                @pl.when(chunk != 0)
                def _(first=first, size=size):
                    copy = zero_copy(first, size)
                    copy.start() if start else copy.wait()

                first = first + chunk
                size //= 2

        def tail_block(i, start):
            j = N_BLOCKS - 1 - i

            @pl.when(j >= nused_ref[0])
            def _():
                copy = zero_copy(j * EXPERT_ROWS, EXPERT_ROWS)
                copy.start() if start else copy.wait()

        max_tail = N_BLOCKS - N_PAIRS // EXPERT_ROWS
        for start in (True, False):
            lax.fori_loop(0, N_EXPERTS, lambda e, c, start=start: (expert_padding(e, start), c)[1], 0)
            lax.fori_loop(0, max_tail, lambda i, c, start=start: (tail_block(i, start), c)[1], 0)

    def body(t, carry):
        src = _slot_tile(h_ref, t)
        for k in range(TOP_K):
            pltpu.make_async_copy(src, _slot_tile(xs_ref, pos_ref[0, k, t]), sem).start()
        return carry

    lax.fori_loop(0, TOK_TILE, body, 0, unroll=4)
    for _ in range(TOP_K):
        pltpu.make_async_copy(h_ref, xs_ref.at[pl.ds(0, TOK_TILE * TOKEN_TILE_ROWS)], sem).wait()


def _dispatch(pad_lo, pad_hi, n_used, pos3, h2p):
    grid_spec = pltpu.PrefetchScalarGridSpec(
        num_scalar_prefetch=3,
        grid=(N_TILES,),
        in_specs=[pl.BlockSpec((1, TOP_K, TOK_TILE), lambda i, *_: (i, 0, 0), memory_space=pltpu.SMEM),
                  pl.BlockSpec((TOK_TILE * TOKEN_TILE_ROWS, 128), lambda i, *_: (i, 0))],
        out_specs=pl.BlockSpec(memory_space=pl.ANY),
        scratch_shapes=[pltpu.VMEM((EXPERT_ROWS * TOKEN_TILE_ROWS, 128), _F32), pltpu.SemaphoreType.DMA(()),
                        pltpu.SemaphoreType.DMA(())],
    )
    return pl.pallas_call(
        _dispatch_kernel,
        grid_spec=grid_spec,
        out_shape=jax.ShapeDtypeStruct((N_SLOTS * TOKEN_TILE_ROWS, 128), _F32),
        compiler_params=_params(),
        name="dispatch",
    )(pad_lo, pad_hi, n_used, pos3, h2p)


def _expert_kernel(be_ref, nused_ref, first_ref, next_ref, parity_ref, xs_ref, w1_hbm, w3_hbm, w2_hbm, ys_ref,
                   w1_buf, w3_buf, w2_buf, w1_scr, w3_scr, w2_scr, sem):
    j = pl.program_id(0)

    def weight_copies(expert, slot):
        return [pltpu.make_async_copy(hbm.at[expert], buf.at[slot], sem.at[slot])
                for hbm, buf in ((w1_hbm, w1_buf), (w3_hbm, w3_buf), (w2_hbm, w2_buf))]

    @pl.when(j < nused_ref[0])
    def _():
        @pl.when(j == 0)
        def _():
            for copy in weight_copies(be_ref[0], 0):
                copy.start()

        @pl.when(first_ref[j] == 1)
        def _():
            slot = parity_ref[j]
            for copy in weight_copies(be_ref[j], slot):
                copy.wait()

            @pl.when(next_ref[j] >= 0)
            def _():
                for copy in weight_copies(next_ref[j], 1 - slot):
                    copy.start()

            w1_scr[...] = w1_buf[slot].astype(_BF16)
            w3_scr[...] = w3_buf[slot].astype(_BF16)
            w2_scr[...] = w2_buf[slot].astype(_BF16)

        x = _load_token_tiles(xs_ref, EXPERT_ROWS).astype(_BF16)
        h1 = jnp.dot(x, w1_scr[...], preferred_element_type=_F32)
        h3 = jnp.dot(x, w3_scr[...], preferred_element_type=_F32)
        act = (_silu(h1) * h3).astype(_BF16)
        _store_token_tiles(ys_ref, jnp.dot(act, w2_scr[...], preferred_element_type=_F32), EXPERT_ROWS)

    @pl.when(j >= nused_ref[0])
    def _():
        ys_ref[...] = jnp.zeros_like(ys_ref)


def _experts(block_expert, n_used, run_first, run_next, run_parity, xs, w1, w3, w2):
    def blk(j, be, nu, *_):
        return (jnp.minimum(j, nu[0] - 1), 0)

    up, down = (D_MODEL, EXPERT_FF), (EXPERT_FF, D_MODEL)
    grid_spec = pltpu.PrefetchScalarGridSpec(
        num_scalar_prefetch=5,
        grid=(N_BLOCKS,),
        in_specs=[
            pl.BlockSpec((EXPERT_ROWS * TOKEN_TILE_ROWS, 128), blk),
            pl.BlockSpec(memory_space=pl.ANY),
            pl.BlockSpec(memory_space=pl.ANY),
            pl.BlockSpec(memory_space=pl.ANY),
        ],
        out_specs=pl.BlockSpec((EXPERT_ROWS * TOKEN_TILE_ROWS, 128), lambda j, *_: (j, 0)),
        scratch_shapes=[pltpu.VMEM((2,) + up, _F32), pltpu.VMEM((2,) + up, _F32), pltpu.VMEM((2,) + down, _F32),
                        pltpu.VMEM(up, _BF16), pltpu.VMEM(up, _BF16), pltpu.VMEM(down, _BF16),
                        pltpu.SemaphoreType.DMA((2,))],
    )
    return pl.pallas_call(
        _expert_kernel,
        grid_spec=grid_spec,
        out_shape=jax.ShapeDtypeStruct((N_SLOTS * TOKEN_TILE_ROWS, 128), _F32),
        compiler_params=_params(),
        name="experts",
    )(block_expert, n_used, run_first, run_next, run_parity, xs, w1, w3, w2)


def _combine_kernel(pos_ref, pos_next_ref, gate_ref, h2_ref, x1_ref, mod_ref, sw1_ref, sw3_ref, sw2_ref, fn_ref,
                    ys_ref, y_ref, buf, sem):
    tm = COMBINE_TILE
    i = pl.program_id(0)
    slot = i % 2

    def gather(p_ref, dst_slot):
        def body(t, carry):
            for k in range(TOP_K):
                pltpu.make_async_copy(_slot_tile(ys_ref, p_ref[0, k, t]), _slot_tile(buf.at[dst_slot, k], t),
                                      sem.at[dst_slot]).start()
            return carry

        lax.fori_loop(0, tm, body, 0, unroll=4)

    @pl.when(i == 0)
    def _():
        gather(pos_ref, 0)

    @pl.when(i + 1 < pl.num_programs(0))
    def _():
        gather(pos_next_ref, 1 - slot)

    hb = h2_ref[...]
    act = _silu(jnp.dot(hb, sw1_ref[...], preferred_element_type=_F32)) * jnp.dot(
        hb, sw3_ref[...], preferred_element_type=_F32)
    shared = _dot(act, sw2_ref[...])
    for k in range(TOP_K):
        pltpu.make_async_copy(ys_ref.at[pl.ds(0, tm * TOKEN_TILE_ROWS)], buf.at[slot, k], sem.at[slot]).wait()
    gates = gate_ref[...]
    routed = gates[:, 0:1] * _load_token_tiles(buf.at[slot, 0], tm)
    for k in range(1, TOP_K):
        routed = routed + gates[:, k:k + 1] * _load_token_tiles(buf.at[slot, k], tm)
    y = x1_ref[...] + mod_ref[0, 5:6, :] * (routed + shared)
    y_ref[...] = _rms(y, fn_ref[...])


def _combine(pos3, gate_t, h2, x1, mod3, sw1, sw3, sw2, final_norm, ys, *, n_rows, tile0, mod_row):
    tm = COMBINE_TILE
    const = lambda i: (0, 0)
    row = lambda i: (tile0 + i, 0)
    n_tiles = n_rows // tm
    return pl.pallas_call(
        _combine_kernel,
        grid=(n_tiles,),
        in_specs=[
            pl.BlockSpec((1, TOP_K, tm), lambda i: (tile0 + i, 0, 0), memory_space=pltpu.SMEM),
            pl.BlockSpec((1, TOP_K, tm), lambda i: (tile0 + jnp.minimum(i + 1, n_tiles - 1), 0, 0),
                         memory_space=pltpu.SMEM),
            pl.BlockSpec((tm, 128), row),
            pl.BlockSpec((tm, D_MODEL), row),
            pl.BlockSpec((tm, D_MODEL), row),
            pl.BlockSpec((1, 6, D_MODEL), lambda i: (mod_row(i), 0, 0)),
            pl.BlockSpec((D_MODEL, EXPERT_FF), const),
            pl.BlockSpec((D_MODEL, EXPERT_FF), const),
            pl.BlockSpec((EXPERT_FF, D_MODEL), const),
            pl.BlockSpec((1, D_MODEL), const),
            pl.BlockSpec(memory_space=pl.ANY),
        ],
        out_specs=pl.BlockSpec((tm, D_MODEL), lambda i: (i, 0)),
        out_shape=jax.ShapeDtypeStruct((n_rows, D_MODEL), _F32),
        scratch_shapes=[pltpu.VMEM((2, TOP_K, tm * TOKEN_TILE_ROWS, 128), _F32), pltpu.SemaphoreType.DMA((2,))],
        compiler_params=_params(),
        name="combine",
    )(pos3, pos3, gate_t, h2, x1, mod3, sw1, sw3, sw2, final_norm, ys)


def _tile_major(a, tile):
    return a.reshape(TOP_K, -1, tile).transpose(1, 0, 2)


def kernel(x_prompt, x_sample, c, cache_k, cache_v, state_dn, c_ctx, w_ada, b_ada, norm1, norm2, w_in, dn_conv,
           dn_A_log, dn_dt_bias, dn_norm, attn_sinks, w_out, router_w, router_bias, expert_w1, expert_w3,
           expert_w2, shared_w1, shared_w3, shared_w2, final_norm):
    xp = x_prompt.reshape(M_CTX, D_MODEL)
    xs = x_sample.reshape(M_LAT, D_MODEL)

    cvec = jnp.concatenate([c_ctx[None, :], c, jnp.zeros((N_MOD - 1 - DEC_BATCH, D_MODEL), _F32)], axis=0)
    mod3 = _modulation(cvec, w_ada[0], b_ada).reshape(N_MOD, 6, D_MODEL)

    w = w_in[0]
    n_dn = 4 * DN_WIDTH
    w_dn = w[:, :n_dn].astype(_BF16)
    w_ab = jnp.pad(w[:, n_dn:n_dn + 4 * DN_HEADS], ((0, 0), (0, 128 - 4 * DN_HEADS))).astype(_BF16)
    w_att = w[:, n_dn + 4 * DN_HEADS:].astype(_BF16)
    cos, sin = _rope_tables()
    dn, aq, ak, av, ab = _input_projection(xp, xs, mod3, norm1, cos, sin, w_dn, w_att, w_ab)

    conv_w = jnp.pad(dn_conv[0], ((0, 8 - DN_CONV), (0, 0)))
    pad8 = lambda v: jnp.pad(v.reshape(1, 2 * DN_HEADS), ((0, 0), (0, 128 - 2 * DN_HEADS)))
    a_log = pad8(dn_A_log[0])
    dt_bias = pad8(dn_dt_bias[0])
    dn_c, new_state = _deltanet(dn, ab, conv_w, a_log, dt_bias, dn_norm, None,
                                n_seq=BATCH, seq_len=SEQ, row_block0=0, emit_state=True)
    (dn_l,) = _deltanet(dn, ab, conv_w, a_log, dt_bias, dn_norm, state_dn[:, 0],
                        n_seq=DEC_BATCH, seq_len=DEC_SEQ, row_block0=M_CTX // DEC_SEQ, emit_state=False)

    sinks = attn_sinks[0]
    at_c = _context_attention(sinks, aq, ak, av)
    ctx_k = cache_k[:, 0].reshape(DEC_BATCH * PAST_LEN, ATT_KV)
    ctx_v = cache_v[:, 0].reshape(DEC_BATCH * PAST_LEN, ATT_KV)
    at_l = _latent_attention(sinks, aq, ak, av, ctx_k, ctx_v)

    wo = w_out[0].astype(_BF16)
    x1, h2, h2p, logits_t = _output_projection(xp, xs, dn_c, dn_l, at_c, at_l, mod3, norm2,
                                               wo[:DN_WIDTH], wo[DN_WIDTH:], router_w[0].T)

    idx, rank, gate_t, counts = _route(logits_t, router_bias[0].reshape(N_EXPERTS, 1))
    cnt = counts[:, 0].astype(jnp.int32)
    padded = (cnt + EXPERT_ROWS - 1) // EXPERT_ROWS * EXPERT_ROWS
    pad_end = jnp.cumsum(padded)
    pos = _slots(idx, rank, (pad_end - padded).astype(_F32).reshape(N_EXPERTS, 1))
    n_used = (pad_end[-1] // EXPERT_ROWS).astype(jnp.int32).reshape(1)
    block_start = jnp.arange(N_BLOCKS, dtype=jnp.int32) * EXPERT_ROWS
    block_expert = jnp.minimum(
        jnp.sum((pad_end[None, :] <= block_start[:, None]).astype(jnp.int32), axis=1), N_EXPERTS - 1)

    pad_lo = (pad_end - padded + cnt).astype(jnp.int32)
    x_sorted = _dispatch(pad_lo, pad_end.astype(jnp.int32), n_used, _tile_major(pos, TOK_TILE), h2p)
    run_first = jnp.concatenate([jnp.ones((1,), jnp.int32),
                                 (block_expert[1:] != block_expert[:-1]).astype(jnp.int32)])
    run_parity = (jnp.cumsum(run_first) - 1) % 2
    experts = jnp.arange(N_EXPERTS, dtype=jnp.int32)
    later = (experts[None, :] > experts[:, None]) & (cnt[None, :] > 0)
    next_expert = jnp.min(jnp.where(later, experts[None, :], N_EXPERTS), axis=1)
    next_expert = jnp.where(next_expert == N_EXPERTS, -1, next_expert)
    ys = _experts(block_expert, n_used, run_first, next_expert[block_expert], run_parity.astype(jnp.int32),
                  x_sorted, expert_w1[0], expert_w3[0], expert_w2[0])

    pos_c = _tile_major(pos, COMBINE_TILE)
    sw1, sw3, sw2 = shared_w1[0].astype(_BF16), shared_w3[0].astype(_BF16), shared_w2[0].astype(_BF16)
    fn = final_norm.reshape(1, D_MODEL)
    y_prompt = _combine(pos_c, gate_t, h2, x1, mod3, sw1, sw3, sw2, fn, ys,
                        n_rows=M_CTX, tile0=0, mod_row=lambda i: 0)
    lat_tiles = DEC_SEQ // COMBINE_TILE
    y_sample = _combine(pos_c, gate_t, h2, x1, mod3, sw1, sw3, sw2, fn, ys,
                        n_rows=M_LAT, tile0=M_CTX // COMBINE_TILE, mod_row=lambda i: 1 + i // lat_tiles)

    new_cache_k = ak[:M_CTX].reshape(BATCH, 1, SEQ, ATT_KV_HEADS, HEAD_DIM)
    new_cache_v = av[:M_CTX].reshape(BATCH, 1, SEQ, ATT_KV_HEADS, HEAD_DIM)
    return (y_prompt.reshape(BATCH, SEQ, D_MODEL), y_sample.reshape(DEC_BATCH, DEC_SEQ, D_MODEL),
            new_cache_k, new_cache_v, new_state.reshape(BATCH, 1, 2, DN_HEADS, HEAD_DIM, HEAD_DIM))
```

```python
import functools

import jax
import jax.numpy as jnp
import numpy as np
from jax import lax
from jax.experimental import pallas as pl
from jax.experimental.pallas import tpu as pltpu

D_MODEL = 1024
BATCH = 32
SEQ = 256
DEC_BATCH = 8
DEC_SEQ = 2048
PAST_LEN = 512
GRID_W = 64
HEAD_DIM = 128
DN_HEADS = 4
DN_WIDTH = DN_HEADS * HEAD_DIM
DN_CONV = 5
DN_CHUNK = 64
ATT_HEADS = 4
ATT_KV_HEADS = 2
ATT_GROUP = ATT_HEADS // ATT_KV_HEADS
ATT_Q = ATT_HEADS * HEAD_DIM
ATT_KV = ATT_KV_HEADS * HEAD_DIM
ATT_BLOCK = 128
ROPE_THETA = 10000.0
N_EXPERTS = 256
TOP_K = 8
N_GROUPS = 8
TOPK_GROUPS = 4
GROUP_SIZE = N_EXPERTS // N_GROUPS
EXPERT_FF = D_MODEL // 4
ROUTED_SCALE = 2.5
NORM_EPS = 1e-6
NEG_INF = -1e30

M_CTX = BATCH * SEQ
M_LAT = DEC_BATCH * DEC_SEQ
M_ALL = M_CTX + M_LAT
N_MOD = 16
TOK_TILE = 512
N_CTX_TILES = M_CTX // TOK_TILE
N_TILES = M_ALL // TOK_TILE
EXPERT_ROWS = 256
N_PAIRS = M_ALL * TOP_K
N_BLOCKS = (N_PAIRS + N_EXPERTS * (EXPERT_ROWS - 1)) // EXPERT_ROWS
N_SLOTS = N_BLOCKS * EXPERT_ROWS
COMBINE_TILE = 256
VMEM_LIMIT = 56 * 1024 * 1024

_BF16 = jnp.bfloat16
_F32 = jnp.float32


def _dot(a, b):
    return jnp.dot(a.astype(_BF16), b.astype(_BF16), preferred_element_type=_F32)


def _dot_nt(a, b):
    return lax.dot_general(a.astype(_BF16), b.astype(_BF16), (((1,), (1,)), ((), ())),
                           preferred_element_type=_F32)


def _dot_tn(a, b):
    return lax.dot_general(a.astype(_BF16), b.astype(_BF16), (((0,), (0,)), ((), ())),
                           preferred_element_type=_F32)


def _silu(x):
    return x * jax.nn.sigmoid(x)


def _rms(x, w):
    return x * lax.rsqrt(jnp.mean(x * x, axis=-1, keepdims=True) + NORM_EPS) * w


def _params(n_axes=1):
    return pltpu.CompilerParams(dimension_semantics=("arbitrary",) * n_axes, vmem_limit_bytes=VMEM_LIMIT)


def _tile_mod_row(i):
    return jnp.where(i < N_CTX_TILES, 0, 1 + (i - N_CTX_TILES) // (DEC_SEQ // TOK_TILE))


def _ctx_tile_mask(shape):
    limit = jnp.where(pl.program_id(0) < N_CTX_TILES, shape[0], 0)
    return lax.broadcasted_iota(jnp.int32, shape, 0) < limit


TOKEN_TILE_ROWS = D_MODEL // 128


def _store_token_tiles(ref, x, n):
    for s in range(TOKEN_TILE_ROWS):
        ref[pl.ds(s, n, stride=TOKEN_TILE_ROWS), :] = x[:, s * 128:(s + 1) * 128]


def _load_token_tiles(ref, n):
    return jnp.concatenate([ref[pl.ds(s, n, stride=TOKEN_TILE_ROWS), :] for s in range(TOKEN_TILE_ROWS)], axis=1)


def _x_specs():
    return [
        pl.BlockSpec((TOK_TILE, D_MODEL), lambda i: (jnp.minimum(i, N_CTX_TILES - 1), 0)),
        pl.BlockSpec((TOK_TILE, D_MODEL), lambda i: (jnp.maximum(i - N_CTX_TILES, 0), 0)),
    ]


def _mod_kernel(c_ref, w_ref, b_ref, o_ref):
    o_ref[...] = _dot(_silu(c_ref[...]), w_ref[...]) + b_ref[...]


def _modulation(cvec, w_ada, b_ada):
    tn = 1024
    return pl.pallas_call(
        _mod_kernel,
        grid=(6 * D_MODEL // tn,),
        in_specs=[pl.BlockSpec((N_MOD, D_MODEL), lambda j: (0, 0)),
                  pl.BlockSpec((D_MODEL, tn), lambda j: (0, j)),
                  pl.BlockSpec((1, tn), lambda j: (0, j))],
        out_specs=pl.BlockSpec((N_MOD, tn), lambda j: (0, j)),
        out_shape=jax.ShapeDtypeStruct((N_MOD, 6 * D_MODEL), _F32),
        compiler_params=_params(),
        name="modulation",
    )(cvec, w_ada, b_ada)


def _rope(x, cos, sin, first_half):
    swapped = jnp.where(first_half, pltpu.roll(x, 96, 1), pltpu.roll(x, 32, 1))
    return x * cos + swapped * sin


def _inproj_kernel(xp_ref, xs_ref, mod_ref, n1_ref, cos_ref, sin_ref, wdn_ref, watt_ref, wab_ref,
                   dn_ref, aq_ref, ak_ref, av_ref, ab_ref):
    x = jnp.where(_ctx_tile_mask((TOK_TILE, D_MODEL)), xp_ref[...], xs_ref[...])
    shift = mod_ref[0, 0:1, :]
    scale = mod_ref[0, 1:2, :]
    h = (_rms(x, n1_ref[...]) * (1 + scale) + shift).astype(_BF16)
    dn_ref[...] = jnp.dot(h, wdn_ref[...], preferred_element_type=_F32).astype(_BF16)
    ab_ref[...] = jnp.dot(h, wab_ref[...], preferred_element_type=_F32)
    att = jnp.dot(h, watt_ref[...], preferred_element_type=_F32)
    cos = cos_ref[...]
    sin = sin_ref[...]
    lane = lax.broadcasted_iota(jnp.int32, (TOK_TILE, HEAD_DIM), 1)
    first_half = (lane % 64) < 32
    for hd in range(ATT_HEADS):
        q = att[:, hd * HEAD_DIM:(hd + 1) * HEAD_DIM]
        aq_ref[:, hd * HEAD_DIM:(hd + 1) * HEAD_DIM] = (
            _rope(q, cos, sin, first_half) * HEAD_DIM ** -0.5).astype(_BF16)
    for hd in range(ATT_KV_HEADS):
        k = att[:, ATT_Q + hd * HEAD_DIM:ATT_Q + (hd + 1) * HEAD_DIM]
        ak_ref[:, hd * HEAD_DIM:(hd + 1) * HEAD_DIM] = _rope(k, cos, sin, first_half)
    av_ref[...] = att[:, ATT_Q + ATT_KV:]


def _rope_tables():
    t = jnp.arange(DEC_SEQ)
    row = (t // GRID_W).astype(_F32)
    col = (t % GRID_W).astype(_F32)
    n_freq = HEAD_DIM // 4
    inv_freq = 1.0 / (ROPE_THETA ** (jnp.arange(n_freq, dtype=_F32) / n_freq))
    ang_r = row[:, None] * inv_freq
    ang_c = col[:, None] * inv_freq
    cos = jnp.concatenate([jnp.cos(ang_r), jnp.cos(ang_r), jnp.cos(ang_c), jnp.cos(ang_c)], axis=1)
    sin = jnp.concatenate([-jnp.sin(ang_r), jnp.sin(ang_r), -jnp.sin(ang_c), jnp.sin(ang_c)], axis=1)
    cos = jnp.concatenate([jnp.ones((TOK_TILE, HEAD_DIM), _F32), cos], axis=0)
    sin = jnp.concatenate([jnp.zeros((TOK_TILE, HEAD_DIM), _F32), sin], axis=0)
    return cos, sin


def _input_projection(xp, xs, mod3, norm1, cos, sin, w_dn, w_att, w_ab):
    def rope_idx(i):
        return (jnp.where(i < N_CTX_TILES, 0, 1 + (i - N_CTX_TILES) % (DEC_SEQ // TOK_TILE)), 0)

    const = lambda i: (0, 0)
    row = lambda i: (i, 0)
    return pl.pallas_call(
        _inproj_kernel,
        grid=(N_TILES,),
        in_specs=_x_specs() + [
            pl.BlockSpec((1, 6, D_MODEL), lambda i: (_tile_mod_row(i), 0, 0)),
            pl.BlockSpec((1, D_MODEL), const),
            pl.BlockSpec((TOK_TILE, HEAD_DIM), rope_idx),
            pl.BlockSpec((TOK_TILE, HEAD_DIM), rope_idx),
            pl.BlockSpec((D_MODEL, 4 * DN_WIDTH), const),
            pl.BlockSpec((D_MODEL, ATT_Q + 2 * ATT_KV), const),
            pl.BlockSpec((D_MODEL, 128), const),
        ],
        out_specs=[
            pl.BlockSpec((TOK_TILE, 4 * DN_WIDTH), row),
            pl.BlockSpec((TOK_TILE, ATT_Q), row),
            pl.BlockSpec((TOK_TILE, ATT_KV), row),
            pl.BlockSpec((TOK_TILE, ATT_KV), row),
            pl.BlockSpec((TOK_TILE, 128), row),
        ],
        out_shape=[
            jax.ShapeDtypeStruct((M_ALL, 4 * DN_WIDTH), _BF16),
            jax.ShapeDtypeStruct((M_ALL, ATT_Q), _BF16),
            jax.ShapeDtypeStruct((M_ALL, ATT_KV), _F32),
            jax.ShapeDtypeStruct((M_ALL, ATT_KV), _F32),
            jax.ShapeDtypeStruct((M_ALL, 128), _F32),
        ],
        compiler_params=_params(),
        name="input_projection",
    )(xp, xs, mod3, norm1, cos, sin, w_dn, w_att, w_ab)


DN_STACK = 16
DN_PASS_HEADS = 2


def _dn_stacking(n_chunk):
    group = min(n_chunk, DN_STACK // 2)
    return group, DN_STACK // (2 * group)


def _bdot(a, b):
    return jnp.stack([_dot(a[g], b[g]) for g in range(a.shape[0])])


def _bdot_nt(a, b):
    return jnp.stack([_dot_nt(a[g], b[g]) for g in range(a.shape[0])])


def _inverse_unit_triangular(a, eye):
    p = eye - a
    x = a
    for _ in range(5):
        x = _bdot(x, x)
        p = p + _bdot(p, x)
    return p


def _dn_prepare(q, k, v, gc, gc_row, beta, g_tot, incl, strict, eye):
    decay = jnp.where(incl, jnp.exp(jnp.where(incl, gc - gc_row, 0.0)), 0.0)
    kb = k * beta
    a = jnp.where(strict, _bdot_nt(kb, k) * decay, 0.0)
    t_inv = _inverse_unit_triangular(a, eye)
    eg = jnp.exp(gc)
    u = _bdot(t_inv, v * beta)
    w = _bdot(t_inv, kb * eg)
    qk = _bdot_nt(q, k) * decay
    wq = jnp.concatenate([w, q * eg], axis=1).astype(_BF16)
    kd = k * jnp.exp(g_tot - gc)
    kd_t = jnp.stack([kd[g].T for g in range(kd.shape[0])])
    qkk = jnp.concatenate([qk, kd_t], axis=1).astype(_BF16)
    return u, wq, qkk


def _dn_steps(states, us, wqs, qkks, g_tots):
    wss = [_dot(wq, s) for wq, s in zip(wqs, states)]
    v_news = [u - ws[:DN_CHUNK] for u, ws in zip(us, wss)]
    rs = [_dot(qkk, v_new) for qkk, v_new in zip(qkks, v_news)]
    outs = [ws[DN_CHUNK:] + r[:DN_CHUNK] for ws, r in zip(wss, rs)]
    states = [s * jnp.exp(g) + r[DN_CHUNK:] for s, g, r in zip(states, g_tots, rs)]
    return states, outs


def _dn_kernel(*refs, seq_len, has_s0, emit_state):
    dn_ref, ab_ref, cw_ref, alog_ref, dtb_ref, nw_ref = refs[:6]
    pos = 6
    s0_ref = None
    if has_s0:
        s0_ref = refs[pos]
        pos += 1
    o_ref = refs[pos]
    pos += 1
    st_ref = None
    if emit_state:
        st_ref = refs[pos]
        pos += 1
    pad_scr, q_scr, k_scr, v_scr, gate_scr, grow_scr, o_scr, u_scr, wq_scr, qkk_scr, s_scr = refs[pos:]

    T = seq_len
    C = DN_CHUNK
    n_chunk = T // C
    n_pair = n_chunk // 2
    G, stack_heads = _dn_stacking(n_chunk)
    n_group = n_chunk // G

    gates = ab_ref[...]
    lane = lax.broadcasted_iota(jnp.int32, (T, 128), 1)
    tpos = lax.broadcasted_iota(jnp.int32, (T, 128), 0) % C
    sp_arg = gates + dtb_ref[...]
    softplus = jnp.maximum(sp_arg, 0.0) + jnp.log1p(jnp.exp(-jnp.abs(sp_arg)))
    g = -jnp.exp(alog_ref[...]) * softplus
    pre = g
    suf = g
    s = 1
    while s < C:
        pre = pre + jnp.where(tpos >= s, pltpu.roll(pre, s, 0), 0.0)
        suf = suf + jnp.where(tpos < C - s, pltpu.roll(suf, T - s, 0), 0.0)
        s *= 2
    gcum = jnp.where(lane < DN_HEADS, pre, suf)
    gate_scr[...] = jnp.where(lane < 2 * DN_HEADS, gcum, jax.nn.sigmoid(gates))
    gcum_t = gcum.T
    for p in range(n_pair):
        grow_scr[p] = gcum_t[0:8, p * 128:(p + 1) * 128]

    pad_scr[0:8, :] = jnp.zeros((8, HEAD_DIM), _F32)
    pad_scr[8 + T:16 + T, :] = jnp.zeros((8, HEAD_DIM), _F32)

    shape3 = (DN_STACK, C, C)
    ri = lax.broadcasted_iota(jnp.int32, shape3, 1)
    ci = lax.broadcasted_iota(jnp.int32, shape3, 2)
    eye = (ri == ci).astype(_F32)
    is_fwd = (lax.broadcasted_iota(jnp.int32, shape3, 0) // G) % 2 == 0
    offset = jnp.where(is_fwd, ci - ri, ri - ci)
    incl = offset <= 0
    strict = offset < 0

    def conv_silu(part, h):
        c0 = part * DN_WIDTH + h * HEAD_DIM
        pad_scr[8:8 + T, :] = dn_ref[:, c0:c0 + HEAD_DIM].astype(_F32)
        acc = cw_ref[0:1, c0:c0 + HEAD_DIM] * pad_scr[6:6 + T, :]
        for j in range(1, DN_CONV):
            acc = acc + cw_ref[j:j + 1, c0:c0 + HEAD_DIM] * pad_scr[6 + j:6 + j + T, :]
        return _silu(acc)

    def l2n(x):
        return x * lax.rsqrt(jnp.sum(x * x, axis=-1, keepdims=True) + NORM_EPS)

    def total_decay(h, direction, r0):
        col = direction * DN_HEADS + h
        row = r0 + C - 1 if direction == 0 else r0
        return gate_scr[pl.ds(row, 1), col:col + 1]

    def prepare_group(grp, heads, chain0):
        r0 = pl.multiple_of(grp * G * C, G * C)
        sl = pl.ds(r0, G * C)
        stacked = lambda x: x.reshape(G, C, x.shape[-1])
        qs, ks, vs, gcs, betas, rows, g_tots = [], [], [], [], [], [], []
        for hh, h in enumerate(heads):
            for direction in range(2):
                col = direction * DN_HEADS + h
                bcol = 2 * DN_HEADS + col
                gc = stacked(gate_scr[sl, col:col + 1])
                gcs.append(gc)
                betas.append(stacked(gate_scr[sl, bcol:bcol + 1]))
                for pair in range(G // 2):
                    both = grow_scr[grp * (G // 2) + pair][col:col + 1, :]
                    rows += [both[:, :C], both[:, C:]]
                g_tots.append(gc[:, C - 1:C, :] if direction == 0 else gc[:, 0:1, :])
                qs.append(stacked(q_scr[hh, sl, :]))
                ks.append(stacked(k_scr[hh, sl, :]))
                vs.append(stacked(v_scr[hh, sl, :]))
        cat = lambda xs: jnp.concatenate(xs, axis=0)
        u, wq, qkk = _dn_prepare(cat(qs), cat(ks), cat(vs), cat(gcs), jnp.stack(rows), cat(betas), cat(g_tots),
                                 incl, strict, eye)
        for local in range(2 * len(heads)):
            chain = chain0 + local
            part = slice(local * G, (local + 1) * G)
            slot0 = chain * n_chunk + grp * G
            u_scr[chain, sl, :] = u[part].reshape(G * C, HEAD_DIM)
            wq_scr[pl.ds(slot0, G)] = wq[part]
            qkk_scr[pl.ds(slot0, G)] = qkk[part]

    def steps(chain_heads, c):
        slots, rows, g_tots = [], [], []
        for chain, h in enumerate(chain_heads):
            direction = chain % 2
            chunk = c if direction == 0 else n_chunk - 1 - c
            r0 = pl.multiple_of(chunk * C, C)
            slots.append(chain * n_chunk + chunk)
            rows.append(pl.ds(r0, C))
            g_tots.append(total_decay(h, direction, r0))
        chains = range(len(chain_heads))
        states, outs = _dn_steps([s_scr[ch] for ch in chains], [u_scr[ch, rows[ch], :] for ch in chains],
                                 [wq_scr[slots[ch]] for ch in chains], [qkk_scr[slots[ch]] for ch in chains], g_tots)
        for ch in chains:
            s_scr[ch] = states[ch]
            o_scr[ch, rows[ch], :] = outs[ch]

    for h0 in range(0, DN_HEADS, DN_PASS_HEADS):
        heads = range(h0, h0 + DN_PASS_HEADS)
        for h1 in range(h0, h0 + DN_PASS_HEADS, stack_heads):
            stack = range(h1, h1 + stack_heads)
            for hh, h in enumerate(stack):
                q_scr[hh] = l2n(conv_silu(0, h)) * HEAD_DIM ** -0.5
                k_scr[hh] = l2n(conv_silu(1, h))
                v_scr[hh] = conv_silu(2, h)

            def prepare_body(grp, carry, stack=stack, h1=h1):
                prepare_group(grp, stack, (h1 - h0) * 2)
                return carry

            lax.fori_loop(0, n_group, prepare_body, 0)
        for h in heads:
            for direction in range(2):
                chain = (h - h0) * 2 + direction
                if has_s0:
                    s_scr[chain] = s0_ref[0, direction, h].astype(_F32)
                else:
                    s_scr[chain] = jnp.zeros((HEAD_DIM, HEAD_DIM), _F32)

        chain_heads = [h for h in heads for _ in range(2)]

        def step_body(c, carry):
            steps(chain_heads, c)
            return carry

        lax.fori_loop(0, n_chunk, step_body, 0)

        for h in heads:
            chain = (h - h0) * 2
            if emit_state:
                st_ref[0, 0, h] = s_scr[chain]
                st_ref[0, 1, h] = s_scr[chain + 1]
            o = o_scr[chain] + o_scr[chain + 1]
            z = dn_ref[:, 3 * DN_WIDTH + h * HEAD_DIM:3 * DN_WIDTH + (h + 1) * HEAD_DIM].astype(_F32)
            o = o * lax.rsqrt(jnp.mean(o * o, axis=-1, keepdims=True) + NORM_EPS) * nw_ref[...] * _silu(z)
            o_ref[:, h * HEAD_DIM:(h + 1) * HEAD_DIM] = o.astype(_BF16)


def _deltanet(dn, ab, conv_w, a_log, dt_bias, dn_norm, s0, *, n_seq, seq_len, row_block0, emit_state):
    has_s0 = s0 is not None
    const = lambda b: (0, 0)
    state_spec = pl.BlockSpec((1, 2, DN_HEADS, HEAD_DIM, HEAD_DIM), lambda b: (b, 0, 0, 0, 0))
    in_specs = [
        pl.BlockSpec((seq_len, 4 * DN_WIDTH), lambda b: (row_block0 + b, 0)),
        pl.BlockSpec((seq_len, 128), lambda b: (row_block0 + b, 0)),
        pl.BlockSpec((8, 3 * DN_WIDTH), const),
        pl.BlockSpec((1, 128), const),
        pl.BlockSpec((1, 128), const),
        pl.BlockSpec((1, HEAD_DIM), const),
    ]
    args = [dn, ab, conv_w, a_log, dt_bias, dn_norm]
    if has_s0:
        in_specs.append(state_spec)
        args.append(s0)
    out_specs = [pl.BlockSpec((seq_len, DN_WIDTH), lambda b: (b, 0))]
    out_shape = [jax.ShapeDtypeStruct((n_seq * seq_len, DN_WIDTH), _BF16)]
    if emit_state:
        out_specs.append(state_spec)
        out_shape.append(jax.ShapeDtypeStruct((n_seq, 2, DN_HEADS, HEAD_DIM, HEAD_DIM), _F32))
    n_chunk = seq_len // DN_CHUNK
    pass_tile = (_dn_stacking(n_chunk)[1], seq_len, HEAD_DIM)
    n_chain = 2 * DN_PASS_HEADS
    return pl.pallas_call(
        functools.partial(_dn_kernel, seq_len=seq_len, has_s0=has_s0, emit_state=emit_state),
        grid=(n_seq,),
        in_specs=in_specs,
        out_specs=out_specs,
        out_shape=out_shape,
        scratch_shapes=[
            pltpu.VMEM((seq_len + 16, HEAD_DIM), _F32),
            pltpu.VMEM(pass_tile, _F32), pltpu.VMEM(pass_tile, _F32), pltpu.VMEM(pass_tile, _F32),
            pltpu.VMEM((seq_len, 128), _F32),
            pltpu.VMEM((seq_len // (2 * DN_CHUNK), 8, 128), _F32),
            pltpu.VMEM((n_chain, seq_len, HEAD_DIM), _F32),
            pltpu.VMEM((n_chain, seq_len, HEAD_DIM), _F32),
            pltpu.VMEM((n_chain * n_chunk, 2 * DN_CHUNK, HEAD_DIM), _BF16),
            pltpu.VMEM((n_chain * n_chunk, DN_CHUNK + HEAD_DIM, DN_CHUNK), _BF16),
            pltpu.VMEM((n_chain, HEAD_DIM, HEAD_DIM), _F32),
        ],
        compiler_params=_params(),
        name="deltanet_ctx" if emit_state else "deltanet_lat",
    )(*args)


def _softmax_av(scores, values, sink):
    m = sink
    for s in scores:
        m = jnp.maximum(m, jnp.max(s, axis=-1, keepdims=True))
    denom = jnp.exp(sink - m)
    acc = None
    for s, v in zip(scores, values):
        p = jnp.exp(s - m)
        denom = denom + jnp.sum(p, axis=-1, keepdims=True)
        pv = _dot(p, v)
        acc = pv if acc is None else acc + pv
    return acc / denom


def _ctx_attn_kernel(sink_ref, q_ref, k_ref, v_ref, o_ref):
    kvh = pl.program_id(1)
    k = k_ref[...]
    v = v_ref[...]
    for g in range(ATT_GROUP):
        q = q_ref[:, g * HEAD_DIM:(g + 1) * HEAD_DIM]
        sink = jnp.full((SEQ, 1), sink_ref[kvh * ATT_GROUP + g], _F32)
        o = _softmax_av([_dot_nt(q, k)], [v], sink)
        o_ref[:, g * HEAD_DIM:(g + 1) * HEAD_DIM] = o.astype(_BF16)


def _context_attention(sinks, aq, ak, av):
    return pl.pallas_call(
        _ctx_attn_kernel,
        grid=(BATCH, ATT_KV_HEADS),
        in_specs=[
            pl.BlockSpec(memory_space=pltpu.SMEM),
            pl.BlockSpec((SEQ, ATT_GROUP * HEAD_DIM), lambda b, h: (b, h)),
            pl.BlockSpec((SEQ, HEAD_DIM), lambda b, h: (b, h)),
            pl.BlockSpec((SEQ, HEAD_DIM), lambda b, h: (b, h)),
        ],
        out_specs=pl.BlockSpec((SEQ, ATT_GROUP * HEAD_DIM), lambda b, h: (b, h)),
        out_shape=jax.ShapeDtypeStruct((M_CTX, ATT_Q), _BF16),
        compiler_params=_params(2),
        name="context_attention",
    )(sinks, aq, ak, av)


LAT_Q_BLOCK = 2 * ATT_BLOCK


def _lat_attn_kernel(sink_ref, q_ref, kp_ref, kc_ref, kn_ref, vp_ref, vc_ref, vn_ref, ck_ref, cv_ref, o_ref):
    kvh = pl.program_id(1)
    i = pl.program_id(2)
    nb = pl.num_programs(2)
    B, Q = ATT_BLOCK, LAT_Q_BLOCK
    rows = ATT_GROUP * Q
    q = jnp.concatenate([q_ref[:, g * HEAD_DIM:(g + 1) * HEAD_DIM] for g in range(ATT_GROUP)], axis=0)
    r_b = lax.broadcasted_iota(jnp.int32, (rows, B), 0) % Q
    c_b = lax.broadcasted_iota(jnp.int32, (rows, B), 1)
    r_q = lax.broadcasted_iota(jnp.int32, (rows, Q), 0) % Q
    c_q = lax.broadcasted_iota(jnp.int32, (rows, Q), 1)
    s_prev = jnp.where(c_b >= r_b + jnp.where(i > 0, 0, Q), _dot_nt(q, kp_ref[...]), NEG_INF)
    s_cur = jnp.where(jnp.abs(r_q - c_q) <= B, _dot_nt(q, kc_ref[...]), NEG_INF)
    s_next = jnp.where(c_b <= r_b - B - jnp.where(i < nb - 1, 0, Q), _dot_nt(q, kn_ref[...]), NEG_INF)
    s_ctx = _dot_nt(q, ck_ref[...])
    head = lax.broadcasted_iota(jnp.int32, (rows, 1), 0) // Q
    sink = jnp.zeros((rows, 1), _F32)
    for g in range(ATT_GROUP):
        sink = jnp.where(head == g, sink_ref[kvh * ATT_GROUP + g], sink)
    o = _softmax_av([s_prev, s_cur, s_next, s_ctx], [vp_ref[...], vc_ref[...], vn_ref[...], cv_ref[...]], sink)
    for g in range(ATT_GROUP):
        o_ref[:, g * HEAD_DIM:(g + 1) * HEAD_DIM] = o[g * Q:(g + 1) * Q].astype(_BF16)


def _latent_attention(sinks, aq, ak, av, ctx_k, ctx_v):
    nq = DEC_SEQ // LAT_Q_BLOCK
    nb = DEC_SEQ // ATT_BLOCK
    q0 = M_CTX // LAT_Q_BLOCK
    b0 = M_CTX // ATT_BLOCK
    cur = lambda b, h, i: (q0 + b * nq + i, h)
    prev = lambda b, h, i: (b0 + b * nb + jnp.maximum(2 * i - 1, 0), h)
    nxt = lambda b, h, i: (b0 + b * nb + jnp.minimum(2 * i + 2, nb - 1), h)
    side_spec = lambda f: pl.BlockSpec((ATT_BLOCK, HEAD_DIM), f)
    cur_spec = pl.BlockSpec((LAT_Q_BLOCK, HEAD_DIM), cur)
    ctx_spec = pl.BlockSpec((PAST_LEN, HEAD_DIM), lambda b, h, i: (b, h))
    return pl.pallas_call(
        _lat_attn_kernel,
        grid=(DEC_BATCH, ATT_KV_HEADS, nq),
        in_specs=[
            pl.BlockSpec(memory_space=pltpu.SMEM),
            pl.BlockSpec((LAT_Q_BLOCK, ATT_GROUP * HEAD_DIM), cur),
            side_spec(prev), cur_spec, side_spec(nxt),
            side_spec(prev), cur_spec, side_spec(nxt),
            ctx_spec, ctx_spec,
        ],
        out_specs=pl.BlockSpec((LAT_Q_BLOCK, ATT_GROUP * HEAD_DIM), lambda b, h, i: (b * nq + i, h)),
        out_shape=jax.ShapeDtypeStruct((M_LAT, ATT_Q), _BF16),
        compiler_params=_params(3),
        name="latent_attention",
    )(sinks, aq, ak, ak, ak, av, av, av, ctx_k, ctx_v)


def _outproj_kernel(xp_ref, xs_ref, dnc_ref, dnl_ref, atc_ref, atl_ref, mod_ref, n2_ref, wo_dn_ref, wo_at_ref,
                    rw_ref, x1_ref, h2_ref, h2p_ref, lg_ref):
    x = jnp.where(_ctx_tile_mask((TOK_TILE, D_MODEL)), xp_ref[...], xs_ref[...])
    half_mask = _ctx_tile_mask((TOK_TILE, DN_WIDTH))
    dn = jnp.where(half_mask, dnc_ref[...], dnl_ref[...])
    at = jnp.where(half_mask, atc_ref[...], atl_ref[...])
    mixed = (jnp.dot(dn, wo_dn_ref[...], preferred_element_type=_F32)
             + jnp.dot(at, wo_at_ref[...], preferred_element_type=_F32))
    x1 = x + mod_ref[0, 2:3, :] * mixed
    x1_ref[...] = x1
    h = _rms(x1, n2_ref[...]) * (1 + mod_ref[0, 4:5, :]) + mod_ref[0, 3:4, :]
    h2_ref[...] = h.astype(_BF16)
    _store_token_tiles(h2p_ref, h, TOK_TILE)
    lg_ref[...] = lax.dot_general(rw_ref[...], h, (((1,), (1,)), ((), ())),
                                  precision=lax.Precision.HIGHEST, preferred_element_type=_F32)


def _output_projection(xp, xs, dn_c, dn_l, at_c, at_l, mod3, norm2, wo_dn, wo_at, router_wt):
    const = lambda i: (0, 0)
    row = lambda i: (i, 0)
    ctx_row = lambda i: (jnp.minimum(i, N_CTX_TILES - 1), 0)
    lat_row = lambda i: (jnp.maximum(i - N_CTX_TILES, 0), 0)
    half = (TOK_TILE, DN_WIDTH)
    return pl.pallas_call(
        _outproj_kernel,
        grid=(N_TILES,),
        in_specs=_x_specs() + [
            pl.BlockSpec(half, ctx_row), pl.BlockSpec(half, lat_row),
            pl.BlockSpec(half, ctx_row), pl.BlockSpec(half, lat_row),
            pl.BlockSpec((1, 6, D_MODEL), lambda i: (_tile_mod_row(i), 0, 0)),
            pl.BlockSpec((1, D_MODEL), const),
            pl.BlockSpec((DN_WIDTH, D_MODEL), const),
            pl.BlockSpec((ATT_Q, D_MODEL), const),
            pl.BlockSpec((N_EXPERTS, D_MODEL), const),
        ],
        out_specs=[
            pl.BlockSpec((TOK_TILE, D_MODEL), row),
            pl.BlockSpec((TOK_TILE, D_MODEL), row),
            pl.BlockSpec((TOK_TILE * TOKEN_TILE_ROWS, 128), row),
            pl.BlockSpec((N_EXPERTS, TOK_TILE), lambda i: (0, i)),
        ],
        out_shape=[
            jax.ShapeDtypeStruct((M_ALL, D_MODEL), _F32),
            jax.ShapeDtypeStruct((M_ALL, D_MODEL), _BF16),
            jax.ShapeDtypeStruct((M_ALL * TOKEN_TILE_ROWS, 128), _F32),
            jax.ShapeDtypeStruct((N_EXPERTS, M_ALL), _F32),
        ],
        compiler_params=_params(),
        name="output_projection",
    )(xp, xs, dn_c, dn_l, at_c, at_l, mod3, norm2, wo_dn, wo_at, router_wt)


def _first_index_of(values, target, index, limit):
    return jnp.min(jnp.where(values == target, index, limit), axis=0, keepdims=True)


def _route_kernel(lg_ref, bias_ref, idx_ref, rank_ref, gate_ref, cnt_ref, carry_scr):
    i = pl.program_id(0)
    tm = TOK_TILE

    @pl.when(i == 0)
    def _():
        carry_scr[...] = jnp.zeros_like(carry_scr)

    scores = jax.nn.sigmoid(lg_ref[...])
    biased = scores + bias_ref[...]
    row = lax.broadcasted_iota(jnp.int32, (N_EXPERTS, tm), 0).astype(_F32)
    grow = lax.broadcasted_iota(jnp.int32, (GROUP_SIZE, tm), 0).astype(_F32)

    group_rows = []
    for g in range(N_GROUPS):
        blk = biased[g * GROUP_SIZE:(g + 1) * GROUP_SIZE]
        m1 = jnp.max(blk, axis=0, keepdims=True)
        i1 = _first_index_of(blk, m1, grow, GROUP_SIZE)
        m2 = jnp.max(jnp.where(grow == i1, -jnp.inf, blk), axis=0, keepdims=True)
        group_rows.append(m1 + m2)
    gs = jnp.concatenate(group_rows, axis=0)
    gi = lax.broadcasted_iota(jnp.int32, (N_GROUPS, tm), 0).astype(_F32)
    gself = jnp.zeros((N_GROUPS, tm), _F32)
    for _ in range(TOPK_GROUPS):
        hit = gi == _first_index_of(gs, jnp.max(gs, axis=0, keepdims=True), gi, N_GROUPS)
        gself = jnp.where(hit, 1.0, gself)
        gs = jnp.where(hit, -jnp.inf, gs)
    emask = jnp.concatenate(
        [jnp.broadcast_to(gself[g:g + 1], (GROUP_SIZE, tm)) for g in range(N_GROUPS)], axis=0) > 0.5
    masked = jnp.where(emask, biased, NEG_INF)

    selected = jnp.zeros((N_EXPERTS, tm), _F32)
    idxs, gates = [], []
    for _ in range(TOP_K):
        ei = _first_index_of(masked, jnp.max(masked, axis=0, keepdims=True), row, N_EXPERTS)
        hit = row == ei
        idxs.append(ei)
        gates.append(jnp.sum(jnp.where(hit, scores, 0.0), axis=0, keepdims=True))
        masked = jnp.where(hit, -jnp.inf, masked)
        selected = jnp.where(hit, 1.0, selected)
    gsum = gates[0]
    for gk in gates[1:]:
        gsum = gsum + gk
    gates = [gk / gsum * ROUTED_SCALE for gk in gates]

    si = lax.broadcasted_iota(jnp.int32, (tm, tm), 0)
    ti = lax.broadcasted_iota(jnp.int32, (tm, tm), 1)
    earlier = (si < ti).astype(_BF16)
    ranks_all = _dot(selected, earlier) + carry_scr[...]
    ranks = [jnp.sum(jnp.where(row == ei, ranks_all, 0.0), axis=0, keepdims=True) for ei in idxs]
    carry_scr[...] = carry_scr[...] + jnp.sum(selected, axis=1, keepdims=True)

    idx_ref[...] = jnp.concatenate(idxs, axis=0).astype(jnp.int32)
    rank_ref[...] = jnp.concatenate(ranks, axis=0).astype(jnp.int32)
    gate_rows = jnp.concatenate(gates + [jnp.zeros((128 - TOP_K, tm), _F32)], axis=0)
    gate_ref[...] = gate_rows.T
    cnt_ref[...] = jnp.broadcast_to(carry_scr[...], (N_EXPERTS, 128))


def _route(logits_t, bias_col):
    return pl.pallas_call(
        _route_kernel,
        grid=(N_TILES,),
        in_specs=[pl.BlockSpec((N_EXPERTS, TOK_TILE), lambda i: (0, i)),
                  pl.BlockSpec((N_EXPERTS, 1), lambda i: (0, 0))],
        out_specs=[
            pl.BlockSpec((TOP_K, TOK_TILE), lambda i: (0, i)),
            pl.BlockSpec((TOP_K, TOK_TILE), lambda i: (0, i)),
            pl.BlockSpec((TOK_TILE, 128), lambda i: (i, 0)),
            pl.BlockSpec((N_EXPERTS, 128), lambda i: (0, 0)),
        ],
        out_shape=[
            jax.ShapeDtypeStruct((TOP_K, M_ALL), jnp.int32),
            jax.ShapeDtypeStruct((TOP_K, M_ALL), jnp.int32),
            jax.ShapeDtypeStruct((M_ALL, 128), _F32),
            jax.ShapeDtypeStruct((N_EXPERTS, 128), _F32),
        ],
        scratch_shapes=[pltpu.VMEM((N_EXPERTS, 1), _F32)],
        compiler_params=_params(),
        name="route",
    )(logits_t, bias_col)


def _slot_kernel(idx_ref, rank_ref, start_ref, pos_ref):
    row = lax.broadcasted_iota(jnp.int32, (N_EXPERTS, TOK_TILE), 0)
    start = start_ref[...]
    rows = []
    for k in range(TOP_K):
        base = jnp.sum(jnp.where(row == idx_ref[k:k + 1, :], start, 0.0), axis=0, keepdims=True)
        rows.append(base.astype(jnp.int32) + rank_ref[k:k + 1, :])
    pos_ref[...] = jnp.concatenate(rows, axis=0)


def _slots(idx, rank, start_col):
    spec = pl.BlockSpec((TOP_K, TOK_TILE), lambda i: (0, i))
    return pl.pallas_call(
        _slot_kernel,
        grid=(N_TILES,),
        in_specs=[spec, spec, pl.BlockSpec((N_EXPERTS, 1), lambda i: (0, 0))],
        out_specs=spec,
        out_shape=jax.ShapeDtypeStruct((TOP_K, M_ALL), jnp.int32),
        compiler_params=_params(),
        name="slots",
    )(idx, rank, start_col)


def _slot_tile(ref, slot):
    return ref.at[pl.ds(pl.multiple_of(slot * TOKEN_TILE_ROWS, TOKEN_TILE_ROWS), TOKEN_TILE_ROWS)]


def _dispatch_kernel(pad_lo_ref, pad_hi_ref, nused_ref, pos_ref, h_ref, xs_ref, zero_scr, sem, zero_sem):
    @pl.when(pl.program_id(0) == 0)
    def _():
        zero_scr[...] = jnp.zeros_like(zero_scr)

        def zero_copy(first_slot, n_slots):
            rows = n_slots * TOKEN_TILE_ROWS
            dst = xs_ref.at[pl.ds(pl.multiple_of(first_slot * TOKEN_TILE_ROWS, TOKEN_TILE_ROWS), rows)]
            return pltpu.make_async_copy(zero_scr.at[pl.ds(0, rows)], dst, zero_sem)

        def expert_padding(e, start):
            first = pad_lo_ref[e]
            n = pad_hi_ref[e] - first
            size = EXPERT_ROWS // 2
            while size >= 1:
                chunk = n & size

                @pl.when(chunk != 0)
                def _(first=first, size=size):
                    copy = zero_copy(first, size)
                    copy.start() if start else copy.wait()

                first = first + chunk
                size //= 2

        def tail_block(i, start):
            j = N_BLOCKS - 1 - i

            @pl.when(j >= nused_ref[0])
            def _():
                copy = zero_copy(j * EXPERT_ROWS, EXPERT_ROWS)
                copy.start() if start else copy.wait()

        max_tail = N_BLOCKS - N_PAIRS // EXPERT_ROWS
        for start in (True, False):
            lax.fori_loop(0, N_EXPERTS, lambda e, c, start=start: (expert_padding(e, start), c)[1], 0)
            lax.fori_loop(0, max_tail, lambda i, c, start=start: (tail_block(i, start), c)[1], 0)

    def body(t, carry):
        src = _slot_tile(h_ref, t)
        for k in range(TOP_K):
            pltpu.make_async_copy(src, _slot_tile(xs_ref, pos_ref[0, k, t]), sem).start()
        return carry

    lax.fori_loop(0, TOK_TILE, body, 0, unroll=4)
    for _ in range(TOP_K):
        pltpu.make_async_copy(h_ref, xs_ref.at[pl.ds(0, TOK_TILE * TOKEN_TILE_ROWS)], sem).wait()


def _dispatch(pad_lo, pad_hi, n_used, pos3, h2p):
    grid_spec = pltpu.PrefetchScalarGridSpec(
        num_scalar_prefetch=3,
        grid=(N_TILES,),
        in_specs=[pl.BlockSpec((1, TOP_K, TOK_TILE), lambda i, *_: (i, 0, 0), memory_space=pltpu.SMEM),
                  pl.BlockSpec((TOK_TILE * TOKEN_TILE_ROWS, 128), lambda i, *_: (i, 0))],
        out_specs=pl.BlockSpec(memory_space=pl.ANY),
        scratch_shapes=[pltpu.VMEM((EXPERT_ROWS * TOKEN_TILE_ROWS, 128), _F32), pltpu.SemaphoreType.DMA(()),
                        pltpu.SemaphoreType.DMA(())],
    )
    return pl.pallas_call(
        _dispatch_kernel,
        grid_spec=grid_spec,
        out_shape=jax.ShapeDtypeStruct((N_SLOTS * TOKEN_TILE_ROWS, 128), _F32),
        compiler_params=_params(),
        name="dispatch",
    )(pad_lo, pad_hi, n_used, pos3, h2p)


def _expert_kernel(be_ref, nused_ref, first_ref, next_ref, parity_ref, xs_ref, w1_hbm, w3_hbm, w2_hbm, ys_ref,
                   w1_buf, w3_buf, w2_buf, w1_scr, w3_scr, w2_scr, sem):
    j = pl.program_id(0)

    def weight_copies(expert, slot):
        return [pltpu.make_async_copy(hbm.at[expert], buf.at[slot], sem.at[slot])
                for hbm, buf in ((w1_hbm, w1_buf), (w3_hbm, w3_buf), (w2_hbm, w2_buf))]

    @pl.when(j < nused_ref[0])
    def _():
        @pl.when(j == 0)
        def _():
            for copy in weight_copies(be_ref[0], 0):
                copy.start()

        @pl.when(first_ref[j] == 1)
        def _():
            slot = parity_ref[j]
            for copy in weight_copies(be_ref[j], slot):
                copy.wait()

            @pl.when(next_ref[j] >= 0)
            def _():
                for copy in weight_copies(next_ref[j], 1 - slot):
                    copy.start()

            w1_scr[...] = w1_buf[slot].astype(_BF16)
            w3_scr[...] = w3_buf[slot].astype(_BF16)
            w2_scr[...] = w2_buf[slot].astype(_BF16)

        x = _load_token_tiles(xs_ref, EXPERT_ROWS).astype(_BF16)
        h1 = jnp.dot(x, w1_scr[...], preferred_element_type=_F32)
        h3 = jnp.dot(x, w3_scr[...], preferred_element_type=_F32)
        act = (_silu(h1) * h3).astype(_BF16)
        _store_token_tiles(ys_ref, jnp.dot(act, w2_scr[...], preferred_element_type=_F32), EXPERT_ROWS)

    @pl.when(j >= nused_ref[0])
    def _():
        ys_ref[...] = jnp.zeros_like(ys_ref)


def _experts(block_expert, n_used, run_first, run_next, run_parity, xs, w1, w3, w2):
    def blk(j, be, nu, *_):
        return (jnp.minimum(j, nu[0] - 1), 0)

    up, down = (D_MODEL, EXPERT_FF), (EXPERT_FF, D_MODEL)
    grid_spec = pltpu.PrefetchScalarGridSpec(
        num_scalar_prefetch=5,
        grid=(N_BLOCKS,),
        in_specs=[
            pl.BlockSpec((EXPERT_ROWS * TOKEN_TILE_ROWS, 128), blk),
            pl.BlockSpec(memory_space=pl.ANY),
            pl.BlockSpec(memory_space=pl.ANY),
            pl.BlockSpec(memory_space=pl.ANY),
        ],
        out_specs=pl.BlockSpec((EXPERT_ROWS * TOKEN_TILE_ROWS, 128), lambda j, *_: (j, 0)),
        scratch_shapes=[pltpu.VMEM((2,) + up, _F32), pltpu.VMEM((2,) + up, _F32), pltpu.VMEM((2,) + down, _F32),
                        pltpu.VMEM(up, _BF16), pltpu.VMEM(up, _BF16), pltpu.VMEM(down, _BF16),
                        pltpu.SemaphoreType.DMA((2,))],
    )
    return pl.pallas_call(
        _expert_kernel,
        grid_spec=grid_spec,
        out_shape=jax.ShapeDtypeStruct((N_SLOTS * TOKEN_TILE_ROWS, 128), _F32),
        compiler_params=_params(),
        name="experts",
    )(block_expert, n_used, run_first, run_next, run_parity, xs, w1, w3, w2)


def _combine_kernel(pos_ref, pos_next_ref, gate_ref, h2_ref, x1_ref, mod_ref, sw1_ref, sw3_ref, sw2_ref, fn_ref,
                    ys_ref, y_ref, buf, sem):
    tm = COMBINE_TILE
    i = pl.program_id(0)
    slot = i % 2

    def gather(p_ref, dst_slot):
        def body(t, carry):
            for k in range(TOP_K):
                pltpu.make_async_copy(_slot_tile(ys_ref, p_ref[0, k, t]), _slot_tile(buf.at[dst_slot, k], t),
                                      sem.at[dst_slot]).start()
            return carry

        lax.fori_loop(0, tm, body, 0, unroll=4)

    @pl.when(i == 0)
    def _():
        gather(pos_ref, 0)

    @pl.when(i + 1 < pl.num_programs(0))
    def _():
        gather(pos_next_ref, 1 - slot)

    hb = h2_ref[...]
    act = _silu(jnp.dot(hb, sw1_ref[...], preferred_element_type=_F32)) * jnp.dot(
        hb, sw3_ref[...], preferred_element_type=_F32)
    shared = _dot(act, sw2_ref[...])
    for k in range(TOP_K):
        pltpu.make_async_copy(ys_ref.at[pl.ds(0, tm * TOKEN_TILE_ROWS)], buf.at[slot, k], sem.at[slot]).wait()
    gates = gate_ref[...]
    routed = gates[:, 0:1] * _load_token_tiles(buf.at[slot, 0], tm)
    for k in range(1, TOP_K):
        routed = routed + gates[:, k:k + 1] * _load_token_tiles(buf.at[slot, k], tm)
    y = x1_ref[...] + mod_ref[0, 5:6, :] * (routed + shared)
    y_ref[...] = _rms(y, fn_ref[...])


def _combine(pos3, gate_t, h2, x1, mod3, sw1, sw3, sw2, final_norm, ys, *, n_rows, tile0, mod_row):
    tm = COMBINE_TILE
    const = lambda i: (0, 0)
    row = lambda i: (tile0 + i, 0)
    n_tiles = n_rows // tm
    return pl.pallas_call(
        _combine_kernel,
        grid=(n_tiles,),
        in_specs=[
            pl.BlockSpec((1, TOP_K, tm), lambda i: (tile0 + i, 0, 0), memory_space=pltpu.SMEM),
            pl.BlockSpec((1, TOP_K, tm), lambda i: (tile0 + jnp.minimum(i + 1, n_tiles - 1), 0, 0),
                         memory_space=pltpu.SMEM),
            pl.BlockSpec((tm, 128), row),
            pl.BlockSpec((tm, D_MODEL), row),
            pl.BlockSpec((tm, D_MODEL), row),
            pl.BlockSpec((1, 6, D_MODEL), lambda i: (mod_row(i), 0, 0)),
            pl.BlockSpec((D_MODEL, EXPERT_FF), const),
            pl.BlockSpec((D_MODEL, EXPERT_FF), const),
            pl.BlockSpec((EXPERT_FF, D_MODEL), const),
            pl.BlockSpec((1, D_MODEL), const),
            pl.BlockSpec(memory_space=pl.ANY),
        ],
        out_specs=pl.BlockSpec((tm, D_MODEL), lambda i: (i, 0)),
        out_shape=jax.ShapeDtypeStruct((n_rows, D_MODEL), _F32),
        scratch_shapes=[pltpu.VMEM((2, TOP_K, tm * TOKEN_TILE_ROWS, 128), _F32), pltpu.SemaphoreType.DMA((2,))],
        compiler_params=_params(),
        name="combine",
    )(pos3, pos3, gate_t, h2, x1, mod3, sw1, sw3, sw2, final_norm, ys)


def _tile_major(a, tile):
    return a.reshape(TOP_K, -1, tile).transpose(1, 0, 2)


def kernel(x_prompt, x_sample, c, cache_k, cache_v, state_dn, c_ctx, w_ada, b_ada, norm1, norm2, w_in, dn_conv,
           dn_A_log, dn_dt_bias, dn_norm, attn_sinks, w_out, router_w, router_bias, expert_w1, expert_w3,
           expert_w2, shared_w1, shared_w3, shared_w2, final_norm):
    xp = x_prompt.reshape(M_CTX, D_MODEL)
    xs = x_sample.reshape(M_LAT, D_MODEL)

    cvec = jnp.concatenate([c_ctx[None, :], c, jnp.zeros((N_MOD - 1 - DEC_BATCH, D_MODEL), _F32)], axis=0)
    mod3 = _modulation(cvec, w_ada[0], b_ada).reshape(N_MOD, 6, D_MODEL)

    w = w_in[0]
    n_dn = 4 * DN_WIDTH
    w_dn = w[:, :n_dn].astype(_BF16)
    w_ab = jnp.pad(w[:, n_dn:n_dn + 4 * DN_HEADS], ((0, 0), (0, 128 - 4 * DN_HEADS))).astype(_BF16)
    w_att = w[:, n_dn + 4 * DN_HEADS:].astype(_BF16)
    cos, sin = _rope_tables()
    dn, aq, ak, av, ab = _input_projection(xp, xs, mod3, norm1, cos, sin, w_dn, w_att, w_ab)

    conv_w = jnp.pad(dn_conv[0], ((0, 8 - DN_CONV), (0, 0)))
    pad8 = lambda v: jnp.pad(v.reshape(1, 2 * DN_HEADS), ((0, 0), (0, 128 - 2 * DN_HEADS)))
    a_log = pad8(dn_A_log[0])
    dt_bias = pad8(dn_dt_bias[0])
    dn_c, new_state = _deltanet(dn, ab, conv_w, a_log, dt_bias, dn_norm, None,
                                n_seq=BATCH, seq_len=SEQ, row_block0=0, emit_state=True)
    (dn_l,) = _deltanet(dn, ab, conv_w, a_log, dt_bias, dn_norm, state_dn[:, 0],
                        n_seq=DEC_BATCH, seq_len=DEC_SEQ, row_block0=M_CTX // DEC_SEQ, emit_state=False)

    sinks = attn_sinks[0]
    at_c = _context_attention(sinks, aq, ak, av)
    ctx_k = cache_k[:, 0].reshape(DEC_BATCH * PAST_LEN, ATT_KV)
    ctx_v = cache_v[:, 0].reshape(DEC_BATCH * PAST_LEN, ATT_KV)
    at_l = _latent_attention(sinks, aq, ak, av, ctx_k, ctx_v)

    wo = w_out[0].astype(_BF16)
    x1, h2, h2p, logits_t = _output_projection(xp, xs, dn_c, dn_l, at_c, at_l, mod3, norm2,
                                               wo[:DN_WIDTH], wo[DN_WIDTH:], router_w[0].T)

    idx, rank, gate_t, counts = _route(logits_t, router_bias[0].reshape(N_EXPERTS, 1))
    cnt = counts[:, 0].astype(jnp.int32)
    padded = (cnt + EXPERT_ROWS - 1) // EXPERT_ROWS * EXPERT_ROWS
    pad_end = jnp.cumsum(padded)
    pos = _slots(idx, rank, (pad_end - padded).astype(_F32).reshape(N_EXPERTS, 1))
    n_used = (pad_end[-1] // EXPERT_ROWS).astype(jnp.int32).reshape(1)
    block_start = jnp.arange(N_BLOCKS, dtype=jnp.int32) * EXPERT_ROWS
    block_expert = jnp.minimum(
        jnp.sum((pad_end[None, :] <= block_start[:, None]).astype(jnp.int32), axis=1), N_EXPERTS - 1)

    pad_lo = (pad_end - padded + cnt).astype(jnp.int32)
    x_sorted = _dispatch(pad_lo, pad_end.astype(jnp.int32), n_used, _tile_major(pos, TOK_TILE), h2p)
    run_first = jnp.concatenate([jnp.ones((1,), jnp.int32),
                                 (block_expert[1:] != block_expert[:-1]).astype(jnp.int32)])
    run_parity = (jnp.cumsum(run_first) - 1) % 2
    experts = jnp.arange(N_EXPERTS, dtype=jnp.int32)
    later = (experts[None, :] > experts[:, None]) & (cnt[None, :] > 0)
    next_expert = jnp.min(jnp.where(later, experts[None, :], N_EXPERTS), axis=1)
    next_expert = jnp.where(next_expert == N_EXPERTS, -1, next_expert)
    ys = _experts(block_expert, n_used, run_first, next_expert[block_expert], run_parity.astype(jnp.int32),
                  x_sorted, expert_w1[0], expert_w3[0], expert_w2[0])

    pos_c = _tile_major(pos, COMBINE_TILE)
    sw1, sw3, sw2 = shared_w1[0].astype(_BF16), shared_w3[0].astype(_BF16), shared_w2[0].astype(_BF16)
    fn = final_norm.reshape(1, D_MODEL)
    y_prompt = _combine(pos_c, gate_t, h2, x1, mod3, sw1, sw3, sw2, fn, ys,
                        n_rows=M_CTX, tile0=0, mod_row=lambda i: 0)
    lat_tiles = DEC_SEQ // COMBINE_TILE
    y_sample = _combine(pos_c, gate_t, h2, x1, mod3, sw1, sw3, sw2, fn, ys,
                        n_rows=M_LAT, tile0=M_CTX // COMBINE_TILE, mod_row=lambda i: 1 + i // lat_tiles)

    new_cache_k = ak[:M_CTX].reshape(BATCH, 1, SEQ, ATT_KV_HEADS, HEAD_DIM)
    new_cache_v = av[:M_CTX].reshape(BATCH, 1, SEQ, ATT_KV_HEADS, HEAD_DIM)
    return (y_prompt.reshape(BATCH, SEQ, D_MODEL), y_sample.reshape(DEC_BATCH, DEC_SEQ, D_MODEL),
            new_cache_k, new_cache_v, new_state.reshape(BATCH, 1, 2, DN_HEADS, HEAD_DIM, HEAD_DIM))
```

```python
import functools

import jax
import jax.numpy as jnp
import numpy as np
from jax import lax
from jax.experimental import pallas as pl
from jax.experimental.pallas import tpu as pltpu

D_MODEL = 1024
BATCH = 32
SEQ = 256
DEC_BATCH = 8
DEC_SEQ = 2048
PAST_LEN = 512
GRID_W = 64
HEAD_DIM = 128
DN_HEADS = 4
DN_WIDTH = DN_HEADS * HEAD_DIM
DN_CONV = 5
DN_CHUNK = 64
ATT_HEADS = 4
ATT_KV_HEADS = 2
ATT_GROUP = ATT_HEADS // ATT_KV_HEADS
ATT_Q = ATT_HEADS * HEAD_DIM
ATT_KV = ATT_KV_HEADS * HEAD_DIM
ATT_BLOCK = 128
ROPE_THETA = 10000.0
N_EXPERTS = 256
TOP_K = 8
N_GROUPS = 8
TOPK_GROUPS = 4
GROUP_SIZE = N_EXPERTS // N_GROUPS
EXPERT_FF = D_MODEL // 4
ROUTED_SCALE = 2.5
NORM_EPS = 1e-6
NEG_INF = -1e30

M_CTX = BATCH * SEQ
M_LAT = DEC_BATCH * DEC_SEQ
M_ALL = M_CTX + M_LAT
N_MOD = 16
TOK_TILE = 512
N_CTX_TILES = M_CTX // TOK_TILE
N_TILES = M_ALL // TOK_TILE
EXPERT_ROWS = 256
N_PAIRS = M_ALL * TOP_K
N_BLOCKS = (N_PAIRS + N_EXPERTS * (EXPERT_ROWS - 1)) // EXPERT_ROWS
N_SLOTS = N_BLOCKS * EXPERT_ROWS
COMBINE_TILE = 256
V7X_VMEM_BYTES = 64 * 1024 * 1024
VMEM_LIMIT = V7X_VMEM_BYTES - 8 * 1024 * 1024

_BF16 = jnp.bfloat16
_F32 = jnp.float32


def _dot(a, b):
    return jnp.dot(a.astype(_BF16), b.astype(_BF16), preferred_element_type=_F32)


def _dot_nt(a, b):
    return lax.dot_general(a.astype(_BF16), b.astype(_BF16), (((1,), (1,)), ((), ())),
                           preferred_element_type=_F32)


def _dot_tn(a, b):
    return lax.dot_general(a.astype(_BF16), b.astype(_BF16), (((0,), (0,)), ((), ())),
                           preferred_element_type=_F32)


def _silu(x):
    return x * jax.nn.sigmoid(x)


def _rms(x, w):
    return x * lax.rsqrt(jnp.mean(x * x, axis=-1, keepdims=True) + NORM_EPS) * w


def _params(n_axes=1):
    return pltpu.CompilerParams(dimension_semantics=("arbitrary",) * n_axes, vmem_limit_bytes=VMEM_LIMIT)


def _tile_mod_row(i):
    return jnp.where(i < N_CTX_TILES, 0, 1 + (i - N_CTX_TILES) // (DEC_SEQ // TOK_TILE))


def _ctx_tile_mask(shape):
    limit = jnp.where(pl.program_id(0) < N_CTX_TILES, shape[0], 0)
    return lax.broadcasted_iota(jnp.int32, shape, 0) < limit


TOKEN_TILE_ROWS = D_MODEL // 128


def _store_token_tiles(ref, x, n):
    for s in range(TOKEN_TILE_ROWS):
        ref[pl.ds(s, n, stride=TOKEN_TILE_ROWS), :] = x[:, s * 128:(s + 1) * 128]


def _load_token_tiles(ref, n):
    return jnp.concatenate([ref[pl.ds(s, n, stride=TOKEN_TILE_ROWS), :] for s in range(TOKEN_TILE_ROWS)], axis=1)


def _x_specs():
    return [
        pl.BlockSpec((TOK_TILE, D_MODEL), lambda i: (jnp.minimum(i, N_CTX_TILES - 1), 0)),
        pl.BlockSpec((TOK_TILE, D_MODEL), lambda i: (jnp.maximum(i - N_CTX_TILES, 0), 0)),
    ]


def _mod_kernel(c_ref, w_ref, b_ref, o_ref):
    o_ref[...] = _dot(_silu(c_ref[...]), w_ref[...]) + b_ref[...]


def _modulation(cvec, w_ada, b_ada):
    tn = 1024
    return pl.pallas_call(
        _mod_kernel,
        grid=(6 * D_MODEL // tn,),
        in_specs=[pl.BlockSpec((N_MOD, D_MODEL), lambda j: (0, 0)),
                  pl.BlockSpec((D_MODEL, tn), lambda j: (0, j)),
                  pl.BlockSpec((1, tn), lambda j: (0, j))],
        out_specs=pl.BlockSpec((N_MOD, tn), lambda j: (0, j)),
        out_shape=jax.ShapeDtypeStruct((N_MOD, 6 * D_MODEL), _F32),
        compiler_params=_params(),
        name="modulation",
    )(cvec, w_ada, b_ada)


def _rope(x, cos, sin, first_half):
    swapped = jnp.where(first_half, pltpu.roll(x, 96, 1), pltpu.roll(x, 32, 1))
    return x * cos + swapped * sin


def _inproj_kernel(xp_ref, xs_ref, mod_ref, n1_ref, cos_ref, sin_ref, wdn_ref, watt_ref, wab_ref,
                   dn_ref, aq_ref, ak_ref, av_ref, ab_ref):
    x = jnp.where(_ctx_tile_mask((TOK_TILE, D_MODEL)), xp_ref[...], xs_ref[...])
    shift = mod_ref[0, 0:1, :]
    scale = mod_ref[0, 1:2, :]
    h = (_rms(x, n1_ref[...]) * (1 + scale) + shift).astype(_BF16)
    dn_ref[...] = jnp.dot(h, wdn_ref[...], preferred_element_type=_F32).astype(_BF16)
    ab_ref[...] = jnp.dot(h, wab_ref[...], preferred_element_type=_F32)
    att = jnp.dot(h, watt_ref[...], preferred_element_type=_F32)
    cos = cos_ref[...]
    sin = sin_ref[...]
    lane = lax.broadcasted_iota(jnp.int32, (TOK_TILE, HEAD_DIM), 1)
    first_half = (lane % 64) < 32
    for hd in range(ATT_HEADS):
        q = att[:, hd * HEAD_DIM:(hd + 1) * HEAD_DIM]
        aq_ref[:, hd * HEAD_DIM:(hd + 1) * HEAD_DIM] = (
            _rope(q, cos, sin, first_half) * HEAD_DIM ** -0.5).astype(_BF16)
    for hd in range(ATT_KV_HEADS):
        k = att[:, ATT_Q + hd * HEAD_DIM:ATT_Q + (hd + 1) * HEAD_DIM]
        ak_ref[:, hd * HEAD_DIM:(hd + 1) * HEAD_DIM] = _rope(k, cos, sin, first_half)
    av_ref[...] = att[:, ATT_Q + ATT_KV:]


def _rope_tables():
    t = jnp.arange(DEC_SEQ)
    row = (t // GRID_W).astype(_F32)
    col = (t % GRID_W).astype(_F32)
    n_freq = HEAD_DIM // 4
    inv_freq = 1.0 / (ROPE_THETA ** (jnp.arange(n_freq, dtype=_F32) / n_freq))
    ang_r = row[:, None] * inv_freq
    ang_c = col[:, None] * inv_freq
    cos = jnp.concatenate([jnp.cos(ang_r), jnp.cos(ang_r), jnp.cos(ang_c), jnp.cos(ang_c)], axis=1)
    sin = jnp.concatenate([-jnp.sin(ang_r), jnp.sin(ang_r), -jnp.sin(ang_c), jnp.sin(ang_c)], axis=1)
    cos = jnp.concatenate([jnp.ones((TOK_TILE, HEAD_DIM), _F32), cos], axis=0)
    sin = jnp.concatenate([jnp.zeros((TOK_TILE, HEAD_DIM), _F32), sin], axis=0)
    return cos, sin


def _input_projection(xp, xs, mod3, norm1, cos, sin, w_dn, w_att, w_ab):
    def rope_idx(i):
        return (jnp.where(i < N_CTX_TILES, 0, 1 + (i - N_CTX_TILES) % (DEC_SEQ // TOK_TILE)), 0)

    const = lambda i: (0, 0)
    row = lambda i: (i, 0)
    return pl.pallas_call(
        _inproj_kernel,
        grid=(N_TILES,),
        in_specs=_x_specs() + [
            pl.BlockSpec((1, 6, D_MODEL), lambda i: (_tile_mod_row(i), 0, 0)),
            pl.BlockSpec((1, D_MODEL), const),
            pl.BlockSpec((TOK_TILE, HEAD_DIM), rope_idx),
            pl.BlockSpec((TOK_TILE, HEAD_DIM), rope_idx),
            pl.BlockSpec((D_MODEL, 4 * DN_WIDTH), const),
            pl.BlockSpec((D_MODEL, ATT_Q + 2 * ATT_KV), const),
            pl.BlockSpec((D_MODEL, 128), const),
        ],
        out_specs=[
            pl.BlockSpec((TOK_TILE, 4 * DN_WIDTH), row),
            pl.BlockSpec((TOK_TILE, ATT_Q), row),
            pl.BlockSpec((TOK_TILE, ATT_KV), row),
            pl.BlockSpec((TOK_TILE, ATT_KV), row),
            pl.BlockSpec((TOK_TILE, 128), row),
        ],
        out_shape=[
            jax.ShapeDtypeStruct((M_ALL, 4 * DN_WIDTH), _BF16),
            jax.ShapeDtypeStruct((M_ALL, ATT_Q), _BF16),
            jax.ShapeDtypeStruct((M_ALL, ATT_KV), _F32),
            jax.ShapeDtypeStruct((M_ALL, ATT_KV), _F32),
            jax.ShapeDtypeStruct((M_ALL, 128), _F32),
        ],
        compiler_params=_params(),
        name="input_projection",
    )(xp, xs, mod3, norm1, cos, sin, w_dn, w_att, w_ab)


DN_STACK = 16
DN_LONG_CHUNKS = 8


def _dn_pass_heads(n_chunk):
    return DN_HEADS if n_chunk <= DN_LONG_CHUNKS else DN_HEADS // 2


def _dn_stacking(n_chunk):
    group = min(n_chunk, DN_STACK // 2)
    return group, DN_STACK // (2 * group)


def _bdot(a, b):
    return jnp.stack([_dot(a[g], b[g]) for g in range(a.shape[0])])


def _bdot_nt(a, b):
    return jnp.stack([_dot_nt(a[g], b[g]) for g in range(a.shape[0])])


def _inverse_unit_triangular(a, eye):
    p = eye - a
    x = a
    for _ in range(5):
        x = _bdot(x, x)
        p = p + _bdot(p, x)
    return p


def _dn_prepare(q, k, v, gc, gc_row, beta, g_tot, incl, strict, eye):
    decay = jnp.where(incl, jnp.exp(jnp.where(incl, gc - gc_row, 0.0)), 0.0)
    kb = k * beta
    a = jnp.where(strict, _bdot_nt(kb, k) * decay, 0.0)
    t_inv = _inverse_unit_triangular(a, eye)
    eg = jnp.exp(gc)
    u = _bdot(t_inv, v * beta)
    w = _bdot(t_inv, kb * eg)
    qk = _bdot_nt(q, k) * decay
    wq = jnp.concatenate([w, q * eg], axis=1).astype(_BF16)
    kd = k * jnp.exp(g_tot - gc)
    kd_t = jnp.stack([kd[g].T for g in range(kd.shape[0])])
    qkk = jnp.concatenate([qk, kd_t], axis=1).astype(_BF16)
    return u, wq, qkk


def _dn_steps(states, us, wqs, qkks, g_tots):
    wss = [_dot(wq, s) for wq, s in zip(wqs, states)]
    v_news = [u - ws[:DN_CHUNK] for u, ws in zip(us, wss)]
    rs = [_dot(qkk, v_new) for qkk, v_new in zip(qkks, v_news)]
    outs = [ws[DN_CHUNK:] + r[:DN_CHUNK] for ws, r in zip(wss, rs)]
    states = [s * jnp.exp(g) + r[DN_CHUNK:] for s, g, r in zip(states, g_tots, rs)]
    return states, outs


def _dn_kernel(*refs, seq_len, has_s0, emit_state):
    dn_ref, ab_ref, cw_ref, alog_ref, dtb_ref, nw_ref = refs[:6]
    pos = 6
    s0_ref = None
    if has_s0:
        s0_ref = refs[pos]
        pos += 1
    o_ref = refs[pos]
    pos += 1
    st_ref = None
    if emit_state:
        st_ref = refs[pos]
        pos += 1
    pad_scr, q_scr, k_scr, v_scr, gate_scr, grow_scr, o_scr, u_scr, wq_scr, qkk_scr, s_scr = refs[pos:]

    T = seq_len
    C = DN_CHUNK
    n_chunk = T // C
    n_pair = n_chunk // 2
    G, stack_heads = _dn_stacking(n_chunk)
    n_group = n_chunk // G

    gates = ab_ref[...]
    lane = lax.broadcasted_iota(jnp.int32, (T, 128), 1)
    tpos = lax.broadcasted_iota(jnp.int32, (T, 128), 0) % C
    sp_arg = gates + dtb_ref[...]
    softplus = jnp.maximum(sp_arg, 0.0) + jnp.log1p(jnp.exp(-jnp.abs(sp_arg)))
    g = -jnp.exp(alog_ref[...]) * softplus
    pre = g
    suf = g
    s = 1
    while s < C:
        pre = pre + jnp.where(tpos >= s, pltpu.roll(pre, s, 0), 0.0)
        suf = suf + jnp.where(tpos < C - s, pltpu.roll(suf, T - s, 0), 0.0)
        s *= 2
    gcum = jnp.where(lane < DN_HEADS, pre, suf)
    gate_scr[...] = jnp.where(lane < 2 * DN_HEADS, gcum, jax.nn.sigmoid(gates))
    gcum_t = gcum.T
    for p in range(n_pair):
        grow_scr[p] = gcum_t[0:8, p * 128:(p + 1) * 128]

    pad_scr[0:8, :] = jnp.zeros((8, HEAD_DIM), _F32)
    pad_scr[8 + T:16 + T, :] = jnp.zeros((8, HEAD_DIM), _F32)

    shape3 = (DN_STACK, C, C)
    ri = lax.broadcasted_iota(jnp.int32, shape3, 1)
    ci = lax.broadcasted_iota(jnp.int32, shape3, 2)
    eye = (ri == ci).astype(_F32)
    is_fwd = (lax.broadcasted_iota(jnp.int32, shape3, 0) // G) % 2 == 0
    offset = jnp.where(is_fwd, ci - ri, ri - ci)
    incl = offset <= 0
    strict = offset < 0

    def conv_silu(part, h):
        c0 = part * DN_WIDTH + h * HEAD_DIM
        pad_scr[8:8 + T, :] = dn_ref[:, c0:c0 + HEAD_DIM].astype(_F32)
        acc = cw_ref[0:1, c0:c0 + HEAD_DIM] * pad_scr[6:6 + T, :]
        for j in range(1, DN_CONV):
            acc = acc + cw_ref[j:j + 1, c0:c0 + HEAD_DIM] * pad_scr[6 + j:6 + j + T, :]
        return _silu(acc)

    def l2n(x):
        return x * lax.rsqrt(jnp.sum(x * x, axis=-1, keepdims=True) + NORM_EPS)

    def total_decay(h, direction, r0):
        col = direction * DN_HEADS + h
        row = r0 + C - 1 if direction == 0 else r0
        return gate_scr[pl.ds(row, 1), col:col + 1]

    def prepare_group(grp, heads, chain0):
        r0 = pl.multiple_of(grp * G * C, G * C)
        sl = pl.ds(r0, G * C)
        stacked = lambda x: x.reshape(G, C, x.shape[-1])
        qs, ks, vs, gcs, betas, rows, g_tots = [], [], [], [], [], [], []
        for hh, h in enumerate(heads):
            for direction in range(2):
                col = direction * DN_HEADS + h
                bcol = 2 * DN_HEADS + col
                gc = stacked(gate_scr[sl, col:col + 1])
                gcs.append(gc)
                betas.append(stacked(gate_scr[sl, bcol:bcol + 1]))
                for pair in range(G // 2):
                    both = grow_scr[grp * (G // 2) + pair][col:col + 1, :]
                    rows += [both[:, :C], both[:, C:]]
                g_tots.append(gc[:, C - 1:C, :] if direction == 0 else gc[:, 0:1, :])
                qs.append(stacked(q_scr[hh, sl, :]))
                ks.append(stacked(k_scr[hh, sl, :]))
                vs.append(stacked(v_scr[hh, sl, :]))
        cat = lambda xs: jnp.concatenate(xs, axis=0)
        u, wq, qkk = _dn_prepare(cat(qs), cat(ks), cat(vs), cat(gcs), jnp.stack(rows), cat(betas), cat(g_tots),
                                 incl, strict, eye)
        for local in range(2 * len(heads)):
            chain = chain0 + local
            part = slice(local * G, (local + 1) * G)
            slot0 = chain * n_chunk + grp * G
            u_scr[chain, sl, :] = u[part].reshape(G * C, HEAD_DIM)
            wq_scr[pl.ds(slot0, G)] = wq[part]
            qkk_scr[pl.ds(slot0, G)] = qkk[part]

    def steps(chain_heads, c):
        slots, rows, g_tots = [], [], []
        for chain, h in enumerate(chain_heads):
            direction = chain % 2
            chunk = c if direction == 0 else n_chunk - 1 - c
            r0 = pl.multiple_of(chunk * C, C)
            slots.append(chain * n_chunk + chunk)
            rows.append(pl.ds(r0, C))
            g_tots.append(total_decay(h, direction, r0))
        chains = range(len(chain_heads))
        states, outs = _dn_steps([s_scr[ch] for ch in chains], [u_scr[ch, rows[ch], :] for ch in chains],
                                 [wq_scr[slots[ch]] for ch in chains], [qkk_scr[slots[ch]] for ch in chains], g_tots)
        for ch in chains:
            s_scr[ch] = states[ch]
            o_scr[ch, rows[ch], :] = outs[ch]

    pass_heads = _dn_pass_heads(n_chunk)
    for h0 in range(0, DN_HEADS, pass_heads):
        heads = range(h0, h0 + pass_heads)
        for h1 in range(h0, h0 + pass_heads, stack_heads):
            stack = range(h1, h1 + stack_heads)
            for hh, h in enumerate(stack):
                q_scr[hh] = l2n(conv_silu(0, h)) * HEAD_DIM ** -0.5
                k_scr[hh] = l2n(conv_silu(1, h))
                v_scr[hh] = conv_silu(2, h)

            def prepare_body(grp, carry, stack=stack, h1=h1):
                prepare_group(grp, stack, (h1 - h0) * 2)
                return carry

            lax.fori_loop(0, n_group, prepare_body, 0)
        for h in heads:
            for direction in range(2):
                chain = (h - h0) * 2 + direction
                if has_s0:
                    s_scr[chain] = s0_ref[0, direction, h].astype(_F32)
                else:
                    s_scr[chain] = jnp.zeros((HEAD_DIM, HEAD_DIM), _F32)

        chain_heads = [h for h in heads for _ in range(2)]

        def step_body(c, carry):
            steps(chain_heads, c)
            return carry

        lax.fori_loop(0, n_chunk, step_body, 0)

        for h in heads:
            chain = (h - h0) * 2
            if emit_state:
                st_ref[0, 0, h] = s_scr[chain]
                st_ref[0, 1, h] = s_scr[chain + 1]
            o = o_scr[chain] + o_scr[chain + 1]
            z = dn_ref[:, 3 * DN_WIDTH + h * HEAD_DIM:3 * DN_WIDTH + (h + 1) * HEAD_DIM].astype(_F32)
            o = o * lax.rsqrt(jnp.mean(o * o, axis=-1, keepdims=True) + NORM_EPS) * nw_ref[...] * _silu(z)
            o_ref[:, h * HEAD_DIM:(h + 1) * HEAD_DIM] = o.astype(_BF16)


def _deltanet(dn, ab, conv_w, a_log, dt_bias, dn_norm, s0, *, n_seq, seq_len, row_block0, emit_state):
    has_s0 = s0 is not None
    const = lambda b: (0, 0)
    state_spec = pl.BlockSpec((1, 2, DN_HEADS, HEAD_DIM, HEAD_DIM), lambda b: (b, 0, 0, 0, 0))
    in_specs = [
        pl.BlockSpec((seq_len, 4 * DN_WIDTH), lambda b: (row_block0 + b, 0)),
        pl.BlockSpec((seq_len, 128), lambda b: (row_block0 + b, 0)),
        pl.BlockSpec((8, 3 * DN_WIDTH), const),
        pl.BlockSpec((1, 128), const),
        pl.BlockSpec((1, 128), const),
        pl.BlockSpec((1, HEAD_DIM), const),
    ]
    args = [dn, ab, conv_w, a_log, dt_bias, dn_norm]
    if has_s0:
        in_specs.append(state_spec)
        args.append(s0)
    out_specs = [pl.BlockSpec((seq_len, DN_WIDTH), lambda b: (b, 0))]
    out_shape = [jax.ShapeDtypeStruct((n_seq * seq_len, DN_WIDTH), _BF16)]
    if emit_state:
        out_specs.append(state_spec)
        out_shape.append(jax.ShapeDtypeStruct((n_seq, 2, DN_HEADS, HEAD_DIM, HEAD_DIM), _F32))
    n_chunk = seq_len // DN_CHUNK
    pass_tile = (_dn_stacking(n_chunk)[1], seq_len, HEAD_DIM)
    n_chain = 2 * _dn_pass_heads(n_chunk)
    return pl.pallas_call(
        functools.partial(_dn_kernel, seq_len=seq_len, has_s0=has_s0, emit_state=emit_state),
        grid=(n_seq,),
        in_specs=in_specs,
        out_specs=out_specs,
        out_shape=out_shape,
        scratch_shapes=[
            pltpu.VMEM((seq_len + 16, HEAD_DIM), _F32),
            pltpu.VMEM(pass_tile, _F32), pltpu.VMEM(pass_tile, _F32), pltpu.VMEM(pass_tile, _F32),
            pltpu.VMEM((seq_len, 128), _F32),
            pltpu.VMEM((seq_len // (2 * DN_CHUNK), 8, 128), _F32),
            pltpu.VMEM((n_chain, seq_len, HEAD_DIM), _F32),
            pltpu.VMEM((n_chain, seq_len, HEAD_DIM), _F32),
            pltpu.VMEM((n_chain * n_chunk, 2 * DN_CHUNK, HEAD_DIM), _BF16),
            pltpu.VMEM((n_chain * n_chunk, DN_CHUNK + HEAD_DIM, DN_CHUNK), _BF16),
            pltpu.VMEM((n_chain, HEAD_DIM, HEAD_DIM), _F32),
        ],
        compiler_params=_params(),
        name="deltanet_ctx" if emit_state else "deltanet_lat",
    )(*args)


def _softmax_av(scores, values, sink):
    m = sink
    for s in scores:
        m = jnp.maximum(m, jnp.max(s, axis=-1, keepdims=True))
    denom = jnp.exp(sink - m)
    acc = None
    for s, v in zip(scores, values):
        p = jnp.exp(s - m)
        denom = denom + jnp.sum(p, axis=-1, keepdims=True)
        pv = _dot(p, v)
        acc = pv if acc is None else acc + pv
    return acc / denom


def _ctx_attn_kernel(sink_ref, q_ref, k_ref, v_ref, o_ref):
    kvh = pl.program_id(1)
    k = k_ref[...]
    v = v_ref[...]
    for g in range(ATT_GROUP):
        q = q_ref[:, g * HEAD_DIM:(g + 1) * HEAD_DIM]
        sink = jnp.full((SEQ, 1), sink_ref[kvh * ATT_GROUP + g], _F32)
        o = _softmax_av([_dot_nt(q, k)], [v], sink)
        o_ref[:, g * HEAD_DIM:(g + 1) * HEAD_DIM] = o.astype(_BF16)


def _context_attention(sinks, aq, ak, av):
    return pl.pallas_call(
        _ctx_attn_kernel,
        grid=(BATCH, ATT_KV_HEADS),
        in_specs=[
            pl.BlockSpec(memory_space=pltpu.SMEM),
            pl.BlockSpec((SEQ, ATT_GROUP * HEAD_DIM), lambda b, h: (b, h)),
            pl.BlockSpec((SEQ, HEAD_DIM), lambda b, h: (b, h)),
            pl.BlockSpec((SEQ, HEAD_DIM), lambda b, h: (b, h)),
        ],
        out_specs=pl.BlockSpec((SEQ, ATT_GROUP * HEAD_DIM), lambda b, h: (b, h)),
        out_shape=jax.ShapeDtypeStruct((M_CTX, ATT_Q), _BF16),
        compiler_params=_params(2),
        name="context_attention",
    )(sinks, aq, ak, av)


LAT_Q_BLOCK = 2 * ATT_BLOCK


def _lat_attn_kernel(sink_ref, q_ref, kp_ref, kc_ref, kn_ref, vp_ref, vc_ref, vn_ref, ck_ref, cv_ref, o_ref):
    kvh = pl.program_id(1)
    i = pl.program_id(2)
    nb = pl.num_programs(2)
    B, Q = ATT_BLOCK, LAT_Q_BLOCK
    rows = ATT_GROUP * Q
    q = jnp.concatenate([q_ref[:, g * HEAD_DIM:(g + 1) * HEAD_DIM] for g in range(ATT_GROUP)], axis=0)
    r_b = lax.broadcasted_iota(jnp.int32, (rows, B), 0) % Q
    c_b = lax.broadcasted_iota(jnp.int32, (rows, B), 1)
    r_q = lax.broadcasted_iota(jnp.int32, (rows, Q), 0) % Q
    c_q = lax.broadcasted_iota(jnp.int32, (rows, Q), 1)
    s_prev = jnp.where(c_b >= r_b + jnp.where(i > 0, 0, Q), _dot_nt(q, kp_ref[...]), NEG_INF)
    s_cur = jnp.where(jnp.abs(r_q - c_q) <= B, _dot_nt(q, kc_ref[...]), NEG_INF)
    s_next = jnp.where(c_b <= r_b - B - jnp.where(i < nb - 1, 0, Q), _dot_nt(q, kn_ref[...]), NEG_INF)
    s_ctx = _dot_nt(q, ck_ref[...])
    head = lax.broadcasted_iota(jnp.int32, (rows, 1), 0) // Q
    sink = jnp.zeros((rows, 1), _F32)
    for g in range(ATT_GROUP):
        sink = jnp.where(head == g, sink_ref[kvh * ATT_GROUP + g], sink)
    o = _softmax_av([s_prev, s_cur, s_next, s_ctx], [vp_ref[...], vc_ref[...], vn_ref[...], cv_ref[...]], sink)
    for g in range(ATT_GROUP):
        o_ref[:, g * HEAD_DIM:(g + 1) * HEAD_DIM] = o[g * Q:(g + 1) * Q].astype(_BF16)


def _latent_attention(sinks, aq, ak, av, ctx_k, ctx_v):
    nq = DEC_SEQ // LAT_Q_BLOCK
    nb = DEC_SEQ // ATT_BLOCK
    q0 = M_CTX // LAT_Q_BLOCK
    b0 = M_CTX // ATT_BLOCK
    cur = lambda b, h, i: (q0 + b * nq + i, h)
    prev = lambda b, h, i: (b0 + b * nb + jnp.maximum(2 * i - 1, 0), h)
    nxt = lambda b, h, i: (b0 + b * nb + jnp.minimum(2 * i + 2, nb - 1), h)
    side_spec = lambda f: pl.BlockSpec((ATT_BLOCK, HEAD_DIM), f)
    cur_spec = pl.BlockSpec((LAT_Q_BLOCK, HEAD_DIM), cur)
    ctx_spec = pl.BlockSpec((PAST_LEN, HEAD_DIM), lambda b, h, i: (b, h))
    return pl.pallas_call(
        _lat_attn_kernel,
        grid=(DEC_BATCH, ATT_KV_HEADS, nq),
        in_specs=[
            pl.BlockSpec(memory_space=pltpu.SMEM),
            pl.BlockSpec((LAT_Q_BLOCK, ATT_GROUP * HEAD_DIM), cur),
            side_spec(prev), cur_spec, side_spec(nxt),
            side_spec(prev), cur_spec, side_spec(nxt),
            ctx_spec, ctx_spec,
        ],
        out_specs=pl.BlockSpec((LAT_Q_BLOCK, ATT_GROUP * HEAD_DIM), lambda b, h, i: (b * nq + i, h)),
        out_shape=jax.ShapeDtypeStruct((M_LAT, ATT_Q), _BF16),
        compiler_params=_params(3),
        name="latent_attention",
    )(sinks, aq, ak, ak, ak, av, av, av, ctx_k, ctx_v)


def _outproj_kernel(xp_ref, xs_ref, dnc_ref, dnl_ref, atc_ref, atl_ref, mod_ref, n2_ref, wo_dn_ref, wo_at_ref,
                    rw_ref, x1_ref, h2_ref, h2p_ref, lg_ref):
    x = jnp.where(_ctx_tile_mask((TOK_TILE, D_MODEL)), xp_ref[...], xs_ref[...])
    half_mask = _ctx_tile_mask((TOK_TILE, DN_WIDTH))
    dn = jnp.where(half_mask, dnc_ref[...], dnl_ref[...])
    at = jnp.where(half_mask, atc_ref[...], atl_ref[...])
    mixed = (jnp.dot(dn, wo_dn_ref[...], preferred_element_type=_F32)
             + jnp.dot(at, wo_at_ref[...], preferred_element_type=_F32))
    x1 = x + mod_ref[0, 2:3, :] * mixed
    x1_ref[...] = x1
    h = _rms(x1, n2_ref[...]) * (1 + mod_ref[0, 4:5, :]) + mod_ref[0, 3:4, :]
    h2_ref[...] = h.astype(_BF16)
    _store_token_tiles(h2p_ref, h, TOK_TILE)
    w = rw_ref[...]
    w_hi = w.astype(_BF16)
    w_lo = (w - w_hi.astype(_F32)).astype(_BF16)
    h_hi = h.astype(_BF16)
    h_lo = (h - h_hi.astype(_F32)).astype(_BF16)
    lg_ref[...] = (_dot_nt(w_hi, h_hi) + _dot_nt(w_hi, h_lo)) + _dot_nt(w_lo, h_hi)


def _output_projection(xp, xs, dn_c, dn_l, at_c, at_l, mod3, norm2, wo_dn, wo_at, router_wt):
    const = lambda i: (0, 0)
    row = lambda i: (i, 0)
    ctx_row = lambda i: (jnp.minimum(i, N_CTX_TILES - 1), 0)
    lat_row = lambda i: (jnp.maximum(i - N_CTX_TILES, 0), 0)
    half = (TOK_TILE, DN_WIDTH)
    return pl.pallas_call(
        _outproj_kernel,
        grid=(N_TILES,),
        in_specs=_x_specs() + [
            pl.BlockSpec(half, ctx_row), pl.BlockSpec(half, lat_row),
            pl.BlockSpec(half, ctx_row), pl.BlockSpec(half, lat_row),
            pl.BlockSpec((1, 6, D_MODEL), lambda i: (_tile_mod_row(i), 0, 0)),
            pl.BlockSpec((1, D_MODEL), const),
            pl.BlockSpec((DN_WIDTH, D_MODEL), const),
            pl.BlockSpec((ATT_Q, D_MODEL), const),
            pl.BlockSpec((N_EXPERTS, D_MODEL), const),
        ],
        out_specs=[
            pl.BlockSpec((TOK_TILE, D_MODEL), row),
            pl.BlockSpec((TOK_TILE, D_MODEL), row),
            pl.BlockSpec((TOK_TILE * TOKEN_TILE_ROWS, 128), row),
            pl.BlockSpec((N_EXPERTS, TOK_TILE), lambda i: (0, i)),
        ],
        out_shape=[
            jax.ShapeDtypeStruct((M_ALL, D_MODEL), _F32),
            jax.ShapeDtypeStruct((M_ALL, D_MODEL), _BF16),
            jax.ShapeDtypeStruct((M_ALL * TOKEN_TILE_ROWS, 128), _F32),
            jax.ShapeDtypeStruct((N_EXPERTS, M_ALL), _F32),
        ],
        compiler_params=_params(),
        name="output_projection",
    )(xp, xs, dn_c, dn_l, at_c, at_l, mod3, norm2, wo_dn, wo_at, router_wt)


def _first_index_of(values, target, index, limit):
    return jnp.min(jnp.where(values == target, index, limit), axis=0, keepdims=True)


def _route_kernel(lg_ref, bias_ref, idx_ref, rank_ref, gate_ref, cnt_ref, carry_scr):
    i = pl.program_id(0)
    tm = TOK_TILE

    @pl.when(i == 0)
    def _():
        carry_scr[...] = jnp.zeros_like(carry_scr)

    scores = jax.nn.sigmoid(lg_ref[...])
    biased = scores + bias_ref[...]
    row = lax.broadcasted_iota(jnp.int32, (N_EXPERTS, tm), 0).astype(_F32)
    grow = lax.broadcasted_iota(jnp.int32, (GROUP_SIZE, tm), 0).astype(_F32)

    group_rows = []
    for g in range(N_GROUPS):
        blk = biased[g * GROUP_SIZE:(g + 1) * GROUP_SIZE]
        m1 = jnp.max(blk, axis=0, keepdims=True)
        i1 = _first_index_of(blk, m1, grow, GROUP_SIZE)
        m2 = jnp.max(jnp.where(grow == i1, -jnp.inf, blk), axis=0, keepdims=True)
        group_rows.append(m1 + m2)
    gs = jnp.concatenate(group_rows, axis=0)
    gi = lax.broadcasted_iota(jnp.int32, (N_GROUPS, tm), 0).astype(_F32)
    gself = jnp.zeros((N_GROUPS, tm), _F32)
    for _ in range(TOPK_GROUPS):
        hit = gi == _first_index_of(gs, jnp.max(gs, axis=0, keepdims=True), gi, N_GROUPS)
        gself = jnp.where(hit, 1.0, gself)
        gs = jnp.where(hit, -jnp.inf, gs)
    emask = jnp.concatenate(
        [jnp.broadcast_to(gself[g:g + 1], (GROUP_SIZE, tm)) for g in range(N_GROUPS)], axis=0) > 0.5
    masked = jnp.where(emask, biased, NEG_INF)

    selected = jnp.zeros((N_EXPERTS, tm), _F32)
    idxs, gates = [], []
    for _ in range(TOP_K):
        ei = _first_index_of(masked, jnp.max(masked, axis=0, keepdims=True), row, N_EXPERTS)
        hit = row == ei
        idxs.append(ei)
        gates.append(jnp.sum(jnp.where(hit, scores, 0.0), axis=0, keepdims=True))
        masked = jnp.where(hit, -jnp.inf, masked)
        selected = jnp.where(hit, 1.0, selected)
    gsum = gates[0]
    for gk in gates[1:]:
        gsum = gsum + gk
    gates = [gk / gsum * ROUTED_SCALE for gk in gates]

    si = lax.broadcasted_iota(jnp.int32, (tm, tm), 0)
    ti = lax.broadcasted_iota(jnp.int32, (tm, tm), 1)
    earlier = (si < ti).astype(_BF16)
    ranks_all = _dot(selected, earlier) + carry_scr[...]
    ranks = [jnp.sum(jnp.where(row == ei, ranks_all, 0.0), axis=0, keepdims=True) for ei in idxs]
    carry_scr[...] = carry_scr[...] + jnp.sum(selected, axis=1, keepdims=True)

    idx_ref[...] = jnp.concatenate(idxs, axis=0).astype(jnp.int32)
    rank_ref[...] = jnp.concatenate(ranks, axis=0).astype(jnp.int32)
    gate_rows = jnp.concatenate(gates + [jnp.zeros((128 - TOP_K, tm), _F32)], axis=0)
    gate_ref[...] = gate_rows.T
    cnt_ref[...] = jnp.broadcast_to(carry_scr[...], (N_EXPERTS, 128))


def _route(logits_t, bias_col):
    return pl.pallas_call(
        _route_kernel,
        grid=(N_TILES,),
        in_specs=[pl.BlockSpec((N_EXPERTS, TOK_TILE), lambda i: (0, i)),
                  pl.BlockSpec((N_EXPERTS, 1), lambda i: (0, 0))],
        out_specs=[
            pl.BlockSpec((TOP_K, TOK_TILE), lambda i: (0, i)),
            pl.BlockSpec((TOP_K, TOK_TILE), lambda i: (0, i)),
            pl.BlockSpec((TOK_TILE, 128), lambda i: (i, 0)),
            pl.BlockSpec((N_EXPERTS, 128), lambda i: (0, 0)),
        ],
        out_shape=[
            jax.ShapeDtypeStruct((TOP_K, M_ALL), jnp.int32),
            jax.ShapeDtypeStruct((TOP_K, M_ALL), jnp.int32),
            jax.ShapeDtypeStruct((M_ALL, 128), _F32),
            jax.ShapeDtypeStruct((N_EXPERTS, 128), _F32),
        ],
        scratch_shapes=[pltpu.VMEM((N_EXPERTS, 1), _F32)],
        compiler_params=_params(),
        name="route",
    )(logits_t, bias_col)


def _slot_kernel(idx_ref, rank_ref, start_ref, pos_ref):
    row = lax.broadcasted_iota(jnp.int32, (N_EXPERTS, TOK_TILE), 0)
    start = start_ref[...]
    rows = []
    for k in range(TOP_K):
        base = jnp.sum(jnp.where(row == idx_ref[k:k + 1, :], start, 0.0), axis=0, keepdims=True)
        rows.append(base.astype(jnp.int32) + rank_ref[k:k + 1, :])
    pos_ref[...] = jnp.concatenate(rows, axis=0)


def _slots(idx, rank, start_col):
    spec = pl.BlockSpec((TOP_K, TOK_TILE), lambda i: (0, i))
    return pl.pallas_call(
        _slot_kernel,
        grid=(N_TILES,),
        in_specs=[spec, spec, pl.BlockSpec((N_EXPERTS, 1), lambda i: (0, 0))],
        out_specs=spec,
        out_shape=jax.ShapeDtypeStruct((TOP_K, M_ALL), jnp.int32),
        compiler_params=_params(),
        name="slots",
    )(idx, rank, start_col)


def _slot_tile(ref, slot):
    return ref.at[pl.ds(pl.multiple_of(slot * TOKEN_TILE_ROWS, TOKEN_TILE_ROWS), TOKEN_TILE_ROWS)]


def _dispatch_kernel(pad_lo_ref, pad_hi_ref, nused_ref, pos_ref, h_hbm, xs_ref, zero_scr, h_buf, in_sem, out_sem,
                     zero_sem):
    i = pl.program_id(0)
    n_steps = pl.num_programs(0)
    block_rows = TOK_TILE * TOKEN_TILE_ROWS

    def fetch(step):
        src = h_hbm.at[pl.ds(pl.multiple_of(step * block_rows, block_rows), block_rows)]
        return pltpu.make_async_copy(src, h_buf.at[step % 3], in_sem.at[step % 3])

    def wait_copies(step):
        for _ in range(TOP_K):
            pltpu.make_async_copy(h_buf.at[0], xs_ref.at[pl.ds(0, block_rows)], out_sem.at[step % 2]).wait()

    @pl.when(i == 0)
    def _():
        fetch(i).start()

    @pl.when(i + 1 < n_steps)
    def _():
        fetch(i + 1).start()

    @pl.when(i == 0)
    def _():
        zero_scr[...] = jnp.zeros_like(zero_scr)

        def zero_copy(first_slot, n_slots):
            rows = n_slots * TOKEN_TILE_ROWS
            dst = xs_ref.at[pl.ds(pl.multiple_of(first_slot * TOKEN_TILE_ROWS, TOKEN_TILE_ROWS), rows)]
            return pltpu.make_async_copy(zero_scr.at[pl.ds(0, rows)], dst, zero_sem)

        def expert_padding(e, start):
            first = pad_lo_ref[e]
            n = pad_hi_ref[e] - first
            size = EXPERT_ROWS // 2
            while size >= 1:
                chunk = n & size

                @pl.when(chunk != 0)
                def _(first=first, size=size):
                    copy = zero_copy(first, size)
                    copy.start() if start else copy.wait()

                first = first + chunk
                size //= 2

        def tail_block(i, start):
            j = N_BLOCKS - 1 - i

            @pl.when(j >= nused_ref[0])
            def _():
                copy = zero_copy(j * EXPERT_ROWS, EXPERT_ROWS)
                copy.start() if start else copy.wait()

        max_tail = N_BLOCKS - N_PAIRS // EXPERT_ROWS
        for start in (True, False):
            lax.fori_loop(0, N_EXPERTS, lambda e, c, start=start: (expert_padding(e, start), c)[1], 0)
            lax.fori_loop(0, max_tail, lambda i, c, start=start: (tail_block(i, start), c)[1], 0)

    fetch(i).wait()
    h_tile = h_buf.at[i % 3]

    def body(t, carry):
        src = _slot_tile(h_tile, t)
        for k in range(TOP_K):
            pltpu.make_async_copy(src, _slot_tile(xs_ref, pos_ref[0, k, t]), out_sem.at[i % 2]).start()
        return carry

    lax.fori_loop(0, TOK_TILE, body, 0, unroll=4)

    @pl.when(i >= 1)
    def _():
        wait_copies(i - 1)

    @pl.when(i == n_steps - 1)
    def _():
        wait_copies(i)


def _dispatch(pad_lo, pad_hi, n_used, pos3, h2p):
    grid_spec = pltpu.PrefetchScalarGridSpec(
        num_scalar_prefetch=3,
        grid=(N_TILES,),
        in_specs=[pl.BlockSpec((1, TOP_K, TOK_TILE), lambda i, *_: (i, 0, 0), memory_space=pltpu.SMEM),
                  pl.BlockSpec(memory_space=pl.ANY)],
        out_specs=pl.BlockSpec(memory_space=pl.ANY),
        scratch_shapes=[pltpu.VMEM((EXPERT_ROWS * TOKEN_TILE_ROWS, 128), _F32),
                        pltpu.VMEM((3, TOK_TILE * TOKEN_TILE_ROWS, 128), _F32),
                        pltpu.SemaphoreType.DMA((3,)), pltpu.SemaphoreType.DMA((2,)),
                        pltpu.SemaphoreType.DMA(())],
    )
    return pl.pallas_call(
        _dispatch_kernel,
        grid_spec=grid_spec,
        out_shape=jax.ShapeDtypeStruct((N_SLOTS * TOKEN_TILE_ROWS, 128), _F32),
        compiler_params=_params(),
        name="dispatch",
    )(pad_lo, pad_hi, n_used, pos3, h2p)


def _expert_kernel(be_ref, nused_ref, first_ref, next_ref, parity_ref, xs_ref, w1_hbm, w3_hbm, w2_hbm, ys_ref,
                   w1_buf, w3_buf, w2_buf, w1_scr, w3_scr, w2_scr, sem):
    j = pl.program_id(0)

    def weight_copies(expert, slot):
        return [pltpu.make_async_copy(hbm.at[expert], buf.at[slot], sem.at[slot])
                for hbm, buf in ((w1_hbm, w1_buf), (w3_hbm, w3_buf), (w2_hbm, w2_buf))]

    @pl.when(j < nused_ref[0])
    def _():
        @pl.when(j == 0)
        def _():
            for copy in weight_copies(be_ref[0], 0):
                copy.start()

        @pl.when(first_ref[j] == 1)
        def _():
            slot = parity_ref[j]
            for copy in weight_copies(be_ref[j], slot):
                copy.wait()

            @pl.when(next_ref[j] >= 0)
            def _():
                for copy in weight_copies(next_ref[j], 1 - slot):
                    copy.start()

            w1_scr[...] = w1_buf[slot].astype(_BF16)
            w3_scr[...] = w3_buf[slot].astype(_BF16)
            w2_scr[...] = w2_buf[slot].astype(_BF16)

        x = _load_token_tiles(xs_ref, EXPERT_ROWS).astype(_BF16)
        h1 = jnp.dot(x, w1_scr[...], preferred_element_type=_F32)
        h3 = jnp.dot(x, w3_scr[...], preferred_element_type=_F32)
        act = (_silu(h1) * h3).astype(_BF16)
        _store_token_tiles(ys_ref, jnp.dot(act, w2_scr[...], preferred_element_type=_F32), EXPERT_ROWS)

    @pl.when(j >= nused_ref[0])
    def _():
        ys_ref[...] = jnp.zeros_like(ys_ref)


def _experts(block_expert, n_used, run_first, run_next, run_parity, xs, w1, w3, w2):
    def blk(j, be, nu, *_):
        return (jnp.minimum(j, nu[0] - 1), 0)

    up, down = (D_MODEL, EXPERT_FF), (EXPERT_FF, D_MODEL)
    grid_spec = pltpu.PrefetchScalarGridSpec(
        num_scalar_prefetch=5,
        grid=(N_BLOCKS,),
        in_specs=[
            pl.BlockSpec((EXPERT_ROWS * TOKEN_TILE_ROWS, 128), blk),
            pl.BlockSpec(memory_space=pl.ANY),
            pl.BlockSpec(memory_space=pl.ANY),
            pl.BlockSpec(memory_space=pl.ANY),
        ],
        out_specs=pl.BlockSpec((EXPERT_ROWS * TOKEN_TILE_ROWS, 128), lambda j, *_: (j, 0)),
        scratch_shapes=[pltpu.VMEM((2,) + up, _F32), pltpu.VMEM((2,) + up, _F32), pltpu.VMEM((2,) + down, _F32),
                        pltpu.VMEM(up, _BF16), pltpu.VMEM(up, _BF16), pltpu.VMEM(down, _BF16),
                        pltpu.SemaphoreType.DMA((2,))],
    )
    return pl.pallas_call(
        _expert_kernel,
        grid_spec=grid_spec,
        out_shape=jax.ShapeDtypeStruct((N_SLOTS * TOKEN_TILE_ROWS, 128), _F32),
        compiler_params=_params(),
        name="experts",
    )(block_expert, n_used, run_first, run_next, run_parity, xs, w1, w3, w2)


def _combine_kernel(pos_ref, pos_next_ref, gate_ref, h2_ref, x1_ref, mod_ref, sw1_ref, sw3_ref, sw2_ref, fn_ref,
                    ys_ref, y_ref, buf, sem):
    tm = COMBINE_TILE
    i = pl.program_id(0)
    slot = i % 2

    def gather(p_ref, dst_slot):
        def body(t, carry):
            for k in range(TOP_K):
                pltpu.make_async_copy(_slot_tile(ys_ref, p_ref[0, k, t]), _slot_tile(buf.at[dst_slot, k], t),
                                      sem.at[dst_slot]).start()
            return carry

        lax.fori_loop(0, tm, body, 0, unroll=4)

    @pl.when(i == 0)
    def _():
        gather(pos_ref, 0)

    @pl.when(i + 1 < pl.num_programs(0))
    def _():
        gather(pos_next_ref, 1 - slot)

    hb = h2_ref[...]
    act = _silu(jnp.dot(hb, sw1_ref[...], preferred_element_type=_F32)) * jnp.dot(
        hb, sw3_ref[...], preferred_element_type=_F32)
    shared = _dot(act, sw2_ref[...])
    for k in range(TOP_K):
        pltpu.make_async_copy(ys_ref.at[pl.ds(0, tm * TOKEN_TILE_ROWS)], buf.at[slot, k], sem.at[slot]).wait()
    gates = gate_ref[...]
    routed = gates[:, 0:1] * _load_token_tiles(buf.at[slot, 0], tm)
    for k in range(1, TOP_K):
        routed = routed + gates[:, k:k + 1] * _load_token_tiles(buf.at[slot, k], tm)
    y = x1_ref[...] + mod_ref[0, 5:6, :] * (routed + shared)
    y_ref[...] = _rms(y, fn_ref[...])


def _combine(pos3, gate_t, h2, x1, mod3, sw1, sw3, sw2, final_norm, ys, *, n_rows, tile0, mod_row):
    tm = COMBINE_TILE
    const = lambda i: (0, 0)
    row = lambda i: (tile0 + i, 0)
    n_tiles = n_rows // tm
    return pl.pallas_call(
        _combine_kernel,
        grid=(n_tiles,),
        in_specs=[
            pl.BlockSpec((1, TOP_K, tm), lambda i: (tile0 + i, 0, 0), memory_space=pltpu.SMEM),
            pl.BlockSpec((1, TOP_K, tm), lambda i: (tile0 + jnp.minimum(i + 1, n_tiles - 1), 0, 0),
                         memory_space=pltpu.SMEM),
            pl.BlockSpec((tm, 128), row),
            pl.BlockSpec((tm, D_MODEL), row),
            pl.BlockSpec((tm, D_MODEL), row),
            pl.BlockSpec((1, 6, D_MODEL), lambda i: (mod_row(i), 0, 0)),
            pl.BlockSpec((D_MODEL, EXPERT_FF), const),
            pl.BlockSpec((D_MODEL, EXPERT_FF), const),
            pl.BlockSpec((EXPERT_FF, D_MODEL), const),
            pl.BlockSpec((1, D_MODEL), const),
            pl.BlockSpec(memory_space=pl.ANY),
        ],
        out_specs=pl.BlockSpec((tm, D_MODEL), lambda i: (i, 0)),
        out_shape=jax.ShapeDtypeStruct((n_rows, D_MODEL), _F32),
        scratch_shapes=[pltpu.VMEM((2, TOP_K, tm * TOKEN_TILE_ROWS, 128), _F32), pltpu.SemaphoreType.DMA((2,))],
        compiler_params=_params(),
        name="combine",
    )(pos3, pos3, gate_t, h2, x1, mod3, sw1, sw3, sw2, final_norm, ys)


def _tile_major(a, tile):
    return a.reshape(TOP_K, -1, tile).transpose(1, 0, 2)


def kernel(x_prompt, x_sample, c, cache_k, cache_v, state_dn, c_ctx, w_ada, b_ada, norm1, norm2, w_in, dn_conv,
           dn_A_log, dn_dt_bias, dn_norm, attn_sinks, w_out, router_w, router_bias, expert_w1, expert_w3,
           expert_w2, shared_w1, shared_w3, shared_w2, final_norm):
    xp = x_prompt.reshape(M_CTX, D_MODEL)
    xs = x_sample.reshape(M_LAT, D_MODEL)

    cvec = jnp.concatenate([c_ctx[None, :], c, jnp.zeros((N_MOD - 1 - DEC_BATCH, D_MODEL), _F32)], axis=0)
    mod3 = _modulation(cvec, w_ada[0], b_ada).reshape(N_MOD, 6, D_MODEL)

    w = w_in[0]
    n_dn = 4 * DN_WIDTH
    w_dn = w[:, :n_dn].astype(_BF16)
    w_ab = jnp.pad(w[:, n_dn:n_dn + 4 * DN_HEADS], ((0, 0), (0, 128 - 4 * DN_HEADS))).astype(_BF16)
    w_att = w[:, n_dn + 4 * DN_HEADS:].astype(_BF16)
    cos, sin = _rope_tables()
    dn, aq, ak, av, ab = _input_projection(xp, xs, mod3, norm1, cos, sin, w_dn, w_att, w_ab)

    conv_w = jnp.pad(dn_conv[0], ((0, 8 - DN_CONV), (0, 0)))
    pad8 = lambda v: jnp.pad(v.reshape(1, 2 * DN_HEADS), ((0, 0), (0, 128 - 2 * DN_HEADS)))
    a_log = pad8(dn_A_log[0])
    dt_bias = pad8(dn_dt_bias[0])
    dn_c, new_state = _deltanet(dn, ab, conv_w, a_log, dt_bias, dn_norm, None,
                                n_seq=BATCH, seq_len=SEQ, row_block0=0, emit_state=True)
    (dn_l,) = _deltanet(dn, ab, conv_w, a_log, dt_bias, dn_norm, state_dn[:, 0],
                        n_seq=DEC_BATCH, seq_len=DEC_SEQ, row_block0=M_CTX // DEC_SEQ, emit_state=False)

    sinks = attn_sinks[0]
    at_c = _context_attention(sinks, aq, ak, av)
    ctx_k = cache_k[:, 0].reshape(DEC_BATCH * PAST_LEN, ATT_KV)
    ctx_v = cache_v[:, 0].reshape(DEC_BATCH * PAST_LEN, ATT_KV)
    at_l = _latent_attention(sinks, aq, ak, av, ctx_k, ctx_v)

    wo = w_out[0].astype(_BF16)
    x1, h2, h2p, logits_t = _output_projection(xp, xs, dn_c, dn_l, at_c, at_l, mod3, norm2,
                                               wo[:DN_WIDTH], wo[DN_WIDTH:], router_w[0].T)

    idx, rank, gate_t, counts = _route(logits_t, router_bias[0].reshape(N_EXPERTS, 1))
    cnt = counts[:, 0].astype(jnp.int32)
    padded = (cnt + EXPERT_ROWS - 1) // EXPERT_ROWS * EXPERT_ROWS
    pad_end = jnp.cumsum(padded)
    pos = _slots(idx, rank, (pad_end - padded).astype(_F32).reshape(N_EXPERTS, 1))
    n_used = (pad_end[-1] // EXPERT_ROWS).astype(jnp.int32).reshape(1)
    block_start = jnp.arange(N_BLOCKS, dtype=jnp.int32) * EXPERT_ROWS
    block_expert = jnp.minimum(
        jnp.sum((pad_end[None, :] <= block_start[:, None]).astype(jnp.int32), axis=1), N_EXPERTS - 1)

    pad_lo = (pad_end - padded + cnt).astype(jnp.int32)
    x_sorted = _dispatch(pad_lo, pad_end.astype(jnp.int32), n_used, _tile_major(pos, TOK_TILE), h2p)
    run_first = jnp.concatenate([jnp.ones((1,), jnp.int32),
                                 (block_expert[1:] != block_expert[:-1]).astype(jnp.int32)])
    run_parity = (jnp.cumsum(run_first) - 1) % 2
    experts = jnp.arange(N_EXPERTS, dtype=jnp.int32)
    later = (experts[None, :] > experts[:, None]) & (cnt[None, :] > 0)
    next_expert = jnp.min(jnp.where(later, experts[None, :], N_EXPERTS), axis=1)
    next_expert = jnp.where(next_expert == N_EXPERTS, -1, next_expert)
    ys = _experts(block_expert, n_used, run_first, next_expert[block_expert], run_parity.astype(jnp.int32),
                  x_sorted, expert_w1[0], expert_w3[0], expert_w2[0])

    pos_c = _tile_major(pos, COMBINE_TILE)
    sw1, sw3, sw2 = shared_w1[0].astype(_BF16), shared_w3[0].astype(_BF16), shared_w2[0].astype(_BF16)
    fn = final_norm.reshape(1, D_MODEL)
    y_prompt = _combine(pos_c, gate_t, h2, x1, mod3, sw1, sw3, sw2, fn, ys,
                        n_rows=M_CTX, tile0=0, mod_row=lambda i: 0)
    lat_tiles = DEC_SEQ // COMBINE_TILE
    y_sample = _combine(pos_c, gate_t, h2, x1, mod3, sw1, sw3, sw2, fn, ys,
                        n_rows=M_LAT, tile0=M_CTX // COMBINE_TILE, mod_row=lambda i: 1 + i // lat_tiles)

    new_cache_k = ak[:M_CTX].reshape(BATCH, 1, SEQ, ATT_KV_HEADS, HEAD_DIM)
    new_cache_v = av[:M_CTX].reshape(BATCH, 1, SEQ, ATT_KV_HEADS, HEAD_DIM)
    return (y_prompt.reshape(BATCH, SEQ, D_MODEL), y_sample.reshape(DEC_BATCH, DEC_SEQ, D_MODEL),
            new_cache_k, new_cache_v, new_state.reshape(BATCH, 1, 2, DN_HEADS, HEAD_DIM, HEAD_DIM))
```

```python
import functools

import jax
import jax.numpy as jnp
import numpy as np
from jax import lax
from jax.experimental import pallas as pl
from jax.experimental.pallas import tpu as pltpu

D_MODEL = 1024
BATCH = 32
SEQ = 256
DEC_BATCH = 8
DEC_SEQ = 2048
PAST_LEN = 512
GRID_W = 64
HEAD_DIM = 128
DN_HEADS = 4
DN_WIDTH = DN_HEADS * HEAD_DIM
DN_CONV = 5
DN_CHUNK = 64
ATT_HEADS = 4
ATT_KV_HEADS = 2
ATT_GROUP = ATT_HEADS // ATT_KV_HEADS
ATT_Q = ATT_HEADS * HEAD_DIM
ATT_KV = ATT_KV_HEADS * HEAD_DIM
ATT_BLOCK = 128
ROPE_THETA = 10000.0
N_EXPERTS = 256
TOP_K = 8
N_GROUPS = 8
TOPK_GROUPS = 4
GROUP_SIZE = N_EXPERTS // N_GROUPS
EXPERT_FF = D_MODEL // 4
ROUTED_SCALE = 2.5
NORM_EPS = 1e-6
NEG_INF = -1e30

M_CTX = BATCH * SEQ
M_LAT = DEC_BATCH * DEC_SEQ
M_ALL = M_CTX + M_LAT
N_MOD = 16
TOK_TILE = 512
N_CTX_TILES = M_CTX // TOK_TILE
N_TILES = M_ALL // TOK_TILE
EXPERT_ROWS = 256
N_PAIRS = M_ALL * TOP_K
N_BLOCKS = (N_PAIRS + N_EXPERTS * (EXPERT_ROWS - 1)) // EXPERT_ROWS
N_SLOTS = N_BLOCKS * EXPERT_ROWS
COMBINE_TILE = 256
V7X_VMEM_BYTES = 64 * 1024 * 1024
VMEM_LIMIT = V7X_VMEM_BYTES - 8 * 1024 * 1024

_BF16 = jnp.bfloat16
_F32 = jnp.float32


def _dot(a, b):
    return jnp.dot(a.astype(_BF16), b.astype(_BF16), preferred_element_type=_F32)


def _dot_nt(a, b):
    return lax.dot_general(a.astype(_BF16), b.astype(_BF16), (((1,), (1,)), ((), ())),
                           preferred_element_type=_F32)


def _dot_tn(a, b):
    return lax.dot_general(a.astype(_BF16), b.astype(_BF16), (((0,), (0,)), ((), ())),
                           preferred_element_type=_F32)


def _silu(x):
    return x * jax.nn.sigmoid(x)


def _rms(x, w):
    return x * lax.rsqrt(jnp.mean(x * x, axis=-1, keepdims=True) + NORM_EPS) * w


def _params(n_axes=1):
    return pltpu.CompilerParams(dimension_semantics=("arbitrary",) * n_axes, vmem_limit_bytes=VMEM_LIMIT)


def _tile_mod_row(i):
    return jnp.where(i < N_CTX_TILES, 0, 1 + (i - N_CTX_TILES) // (DEC_SEQ // TOK_TILE))


def _ctx_tile_mask(shape):
    limit = jnp.where(pl.program_id(0) < N_CTX_TILES, shape[0], 0)
    return lax.broadcasted_iota(jnp.int32, shape, 0) < limit


TOKEN_TILE_ROWS = D_MODEL // 128


def _store_token_tiles(ref, x, n):
    for s in range(TOKEN_TILE_ROWS):
        ref[pl.ds(s, n, stride=TOKEN_TILE_ROWS), :] = x[:, s * 128:(s + 1) * 128]


def _load_token_tiles(ref, n):
    return jnp.concatenate([ref[pl.ds(s, n, stride=TOKEN_TILE_ROWS), :] for s in range(TOKEN_TILE_ROWS)], axis=1)


def _x_specs():
    return [
        pl.BlockSpec((TOK_TILE, D_MODEL), lambda i: (jnp.minimum(i, N_CTX_TILES - 1), 0)),
        pl.BlockSpec((TOK_TILE, D_MODEL), lambda i: (jnp.maximum(i - N_CTX_TILES, 0), 0)),
    ]


def _mod_kernel(c_ref, w_ref, b_ref, o_ref):
    o_ref[...] = _dot(_silu(c_ref[...]), w_ref[...]) + b_ref[...]


def _modulation(cvec, w_ada, b_ada):
    tn = 1024
    return pl.pallas_call(
        _mod_kernel,
        grid=(6 * D_MODEL // tn,),
        in_specs=[pl.BlockSpec((N_MOD, D_MODEL), lambda j: (0, 0)),
                  pl.BlockSpec((D_MODEL, tn), lambda j: (0, j)),
                  pl.BlockSpec((1, tn), lambda j: (0, j))],
        out_specs=pl.BlockSpec((N_MOD, tn), lambda j: (0, j)),
        out_shape=jax.ShapeDtypeStruct((N_MOD, 6 * D_MODEL), _F32),
        compiler_params=_params(),
        name="modulation",
    )(cvec, w_ada, b_ada)


def _rope(x, cos, sin, first_half):
    swapped = jnp.where(first_half, pltpu.roll(x, 96, 1), pltpu.roll(x, 32, 1))
    return x * cos + swapped * sin


def _inproj_kernel(xp_ref, xs_ref, mod_ref, n1_ref, cos_ref, sin_ref, wdn_ref, watt_ref, wab_ref,
                   dn_ref, aq_ref, ak_ref, av_ref, ab_ref):
    x = jnp.where(_ctx_tile_mask((TOK_TILE, D_MODEL)), xp_ref[...], xs_ref[...])
    shift = mod_ref[0, 0:1, :]
    scale = mod_ref[0, 1:2, :]
    h = (_rms(x, n1_ref[...]) * (1 + scale) + shift).astype(_BF16)
    dn_ref[...] = jnp.dot(h, wdn_ref[...], preferred_element_type=_F32).astype(_BF16)
    ab_ref[...] = jnp.dot(h, wab_ref[...], preferred_element_type=_F32)
    att = jnp.dot(h, watt_ref[...], preferred_element_type=_F32)
    cos = cos_ref[...]
    sin = sin_ref[...]
    lane = lax.broadcasted_iota(jnp.int32, (TOK_TILE, HEAD_DIM), 1)
    first_half = (lane % 64) < 32
    for hd in range(ATT_HEADS):
        q = att[:, hd * HEAD_DIM:(hd + 1) * HEAD_DIM]
        aq_ref[:, hd * HEAD_DIM:(hd + 1) * HEAD_DIM] = (
            _rope(q, cos, sin, first_half) * HEAD_DIM ** -0.5).astype(_BF16)
    for hd in range(ATT_KV_HEADS):
        k = att[:, ATT_Q + hd * HEAD_DIM:ATT_Q + (hd + 1) * HEAD_DIM]
        ak_ref[:, hd * HEAD_DIM:(hd + 1) * HEAD_DIM] = _rope(k, cos, sin, first_half)
    av_ref[...] = att[:, ATT_Q + ATT_KV:]


def _rope_tables():
    t = jnp.arange(DEC_SEQ)
    row = (t // GRID_W).astype(_F32)
    col = (t % GRID_W).astype(_F32)
    n_freq = HEAD_DIM // 4
    inv_freq = 1.0 / (ROPE_THETA ** (jnp.arange(n_freq, dtype=_F32) / n_freq))
    ang_r = row[:, None] * inv_freq
    ang_c = col[:, None] * inv_freq
    cos = jnp.concatenate([jnp.cos(ang_r), jnp.cos(ang_r), jnp.cos(ang_c), jnp.cos(ang_c)], axis=1)
    sin = jnp.concatenate([-jnp.sin(ang_r), jnp.sin(ang_r), -jnp.sin(ang_c), jnp.sin(ang_c)], axis=1)
    cos = jnp.concatenate([jnp.ones((TOK_TILE, HEAD_DIM), _F32), cos], axis=0)
    sin = jnp.concatenate([jnp.zeros((TOK_TILE, HEAD_DIM), _F32), sin], axis=0)
    return cos, sin


def _input_projection(xp, xs, mod3, norm1, cos, sin, w_dn, w_att, w_ab):
    def rope_idx(i):
        return (jnp.where(i < N_CTX_TILES, 0, 1 + (i - N_CTX_TILES) % (DEC_SEQ // TOK_TILE)), 0)

    const = lambda i: (0, 0)
    row = lambda i: (i, 0)
    return pl.pallas_call(
        _inproj_kernel,
        grid=(N_TILES,),
        in_specs=_x_specs() + [
            pl.BlockSpec((1, 6, D_MODEL), lambda i: (_tile_mod_row(i), 0, 0)),
            pl.BlockSpec((1, D_MODEL), const),
            pl.BlockSpec((TOK_TILE, HEAD_DIM), rope_idx),
            pl.BlockSpec((TOK_TILE, HEAD_DIM), rope_idx),
            pl.BlockSpec((D_MODEL, 4 * DN_WIDTH), const),
            pl.BlockSpec((D_MODEL, ATT_Q + 2 * ATT_KV), const),
            pl.BlockSpec((D_MODEL, 128), const),
        ],
        out_specs=[
            pl.BlockSpec((TOK_TILE, 4 * DN_WIDTH), row),
            pl.BlockSpec((TOK_TILE, ATT_Q), row),
            pl.BlockSpec((TOK_TILE, ATT_KV), row),
            pl.BlockSpec((TOK_TILE, ATT_KV), row),
            pl.BlockSpec((TOK_TILE, 128), row),
        ],
        out_shape=[
            jax.ShapeDtypeStruct((M_ALL, 4 * DN_WIDTH), _BF16),
            jax.ShapeDtypeStruct((M_ALL, ATT_Q), _BF16),
            jax.ShapeDtypeStruct((M_ALL, ATT_KV), _F32),
            jax.ShapeDtypeStruct((M_ALL, ATT_KV), _F32),
            jax.ShapeDtypeStruct((M_ALL, 128), _F32),
        ],
        compiler_params=_params(),
        name="input_projection",
    )(xp, xs, mod3, norm1, cos, sin, w_dn, w_att, w_ab)


DN_STACK = 16
DN_LONG_CHUNKS = 8


def _dn_pass_heads(n_chunk):
    return DN_HEADS if n_chunk <= DN_LONG_CHUNKS else DN_HEADS // 2


def _dn_stacking(n_chunk):
    group = min(n_chunk, DN_STACK // 2)
    return group, DN_STACK // (2 * group)


def _bdot(a, b):
    return jnp.stack([_dot(a[g], b[g]) for g in range(a.shape[0])])


def _bdot_nt(a, b):
    return jnp.stack([_dot_nt(a[g], b[g]) for g in range(a.shape[0])])


def _inverse_unit_triangular(a, eye):
    p = eye - a
    x = a
    for _ in range(5):
        x = _bdot(x, x)
        p = p + _bdot(p, x)
    return p


def _dn_prepare(q, k, v, gc, gc_row, beta, g_tot, incl, strict, eye):
    decay = jnp.where(incl, jnp.exp(jnp.where(incl, gc - gc_row, 0.0)), 0.0)
    kb = k * beta
    a = jnp.where(strict, _bdot_nt(kb, k) * decay, 0.0)
    t_inv = _inverse_unit_triangular(a, eye)
    eg = jnp.exp(gc)
    u = _bdot(t_inv, v * beta)
    w = _bdot(t_inv, kb * eg)
    qk = _bdot_nt(q, k) * decay
    wq = jnp.concatenate([w, q * eg], axis=1).astype(_BF16)
    kd = k * jnp.exp(g_tot - gc)
    kd_t = jnp.stack([kd[g].T for g in range(kd.shape[0])])
    qkk = jnp.concatenate([qk, kd_t], axis=1).astype(_BF16)
    return u, wq, qkk


def _dn_steps(states, us, wqs, qkks, g_tots):
    wss = [_dot(wq, s) for wq, s in zip(wqs, states)]
    v_news = [u - ws[:DN_CHUNK] for u, ws in zip(us, wss)]
    rs = [_dot(qkk, v_new) for qkk, v_new in zip(qkks, v_news)]
    outs = [ws[DN_CHUNK:] + r[:DN_CHUNK] for ws, r in zip(wss, rs)]
    states = [s * jnp.exp(g) + r[DN_CHUNK:] for s, g, r in zip(states, g_tots, rs)]
    return states, outs


def _dn_kernel(*refs, seq_len, has_s0, emit_state):
    dn_ref, ab_ref, cw_ref, alog_ref, dtb_ref, nw_ref = refs[:6]
    pos = 6
    s0_ref = None
    if has_s0:
        s0_ref = refs[pos]
        pos += 1
    o_ref = refs[pos]
    pos += 1
    st_ref = None
    if emit_state:
        st_ref = refs[pos]
        pos += 1
    pad_scr, q_scr, k_scr, v_scr, gate_scr, grow_scr, o_scr, u_scr, wq_scr, qkk_scr, s_scr = refs[pos:]

    T = seq_len
    C = DN_CHUNK
    n_chunk = T // C
    n_pair = n_chunk // 2
    G, stack_heads = _dn_stacking(n_chunk)
    n_group = n_chunk // G

    gates = ab_ref[...]
    lane = lax.broadcasted_iota(jnp.int32, (T, 128), 1)
    tpos = lax.broadcasted_iota(jnp.int32, (T, 128), 0) % C
    sp_arg = gates + dtb_ref[...]
    softplus = jnp.maximum(sp_arg, 0.0) + jnp.log1p(jnp.exp(-jnp.abs(sp_arg)))
    g = -jnp.exp(alog_ref[...]) * softplus
    pre = g
    suf = g
    s = 1
    while s < C:
        pre = pre + jnp.where(tpos >= s, pltpu.roll(pre, s, 0), 0.0)
        suf = suf + jnp.where(tpos < C - s, pltpu.roll(suf, T - s, 0), 0.0)
        s *= 2
    gcum = jnp.where(lane < DN_HEADS, pre, suf)
    gate_scr[...] = jnp.where(lane < 2 * DN_HEADS, gcum, jax.nn.sigmoid(gates))
    gcum_t = gcum.T
    for p in range(n_pair):
        grow_scr[p] = gcum_t[0:8, p * 128:(p + 1) * 128]

    pad_scr[0:8, :] = jnp.zeros((8, HEAD_DIM), _F32)
    pad_scr[8 + T:16 + T, :] = jnp.zeros((8, HEAD_DIM), _F32)

    shape3 = (DN_STACK, C, C)
    ri = lax.broadcasted_iota(jnp.int32, shape3, 1)
    ci = lax.broadcasted_iota(jnp.int32, shape3, 2)
    eye = (ri == ci).astype(_F32)
    is_fwd = (lax.broadcasted_iota(jnp.int32, shape3, 0) // G) % 2 == 0
    offset = jnp.where(is_fwd, ci - ri, ri - ci)
    incl = offset <= 0
    strict = offset < 0

    def conv_silu(part, h):
        c0 = part * DN_WIDTH + h * HEAD_DIM
        pad_scr[8:8 + T, :] = dn_ref[:, c0:c0 + HEAD_DIM].astype(_F32)
        acc = cw_ref[0:1, c0:c0 + HEAD_DIM] * pad_scr[6:6 + T, :]
        for j in range(1, DN_CONV):
            acc = acc + cw_ref[j:j + 1, c0:c0 + HEAD_DIM] * pad_scr[6 + j:6 + j + T, :]
        return _silu(acc)

    def l2n(x):
        return x * lax.rsqrt(jnp.sum(x * x, axis=-1, keepdims=True) + NORM_EPS)

    def total_decay(h, direction, r0):
        col = direction * DN_HEADS + h
        row = r0 + C - 1 if direction == 0 else r0
        return gate_scr[pl.ds(row, 1), col:col + 1]

    def prepare_group(grp, heads, chain0):
        r0 = pl.multiple_of(grp * G * C, G * C)
        sl = pl.ds(r0, G * C)
        stacked = lambda x: x.reshape(G, C, x.shape[-1])
        qs, ks, vs, gcs, betas, rows, g_tots = [], [], [], [], [], [], []
        for hh, h in enumerate(heads):
            for direction in range(2):
                col = direction * DN_HEADS + h
                bcol = 2 * DN_HEADS + col
                gc = stacked(gate_scr[sl, col:col + 1])
                gcs.append(gc)
                betas.append(stacked(gate_scr[sl, bcol:bcol + 1]))
                for pair in range(G // 2):
                    both = grow_scr[grp * (G // 2) + pair][col:col + 1, :]
                    rows += [both[:, :C], both[:, C:]]
                g_tots.append(gc[:, C - 1:C, :] if direction == 0 else gc[:, 0:1, :])
                qs.append(stacked(q_scr[hh, sl, :]))
                ks.append(stacked(k_scr[hh, sl, :]))
                vs.append(stacked(v_scr[hh, sl, :]))
        cat = lambda xs: jnp.concatenate(xs, axis=0)
        u, wq, qkk = _dn_prepare(cat(qs), cat(ks), cat(vs), cat(gcs), jnp.stack(rows), cat(betas), cat(g_tots),
                                 incl, strict, eye)
        for local in range(2 * len(heads)):
            chain = chain0 + local
            part = slice(local * G, (local + 1) * G)
            slot0 = chain * n_chunk + grp * G
            u_scr[chain, sl, :] = u[part].reshape(G * C, HEAD_DIM)
            wq_scr[pl.ds(slot0, G)] = wq[part]
            qkk_scr[pl.ds(slot0, G)] = qkk[part]

    def steps(chain_heads, c):
        slots, rows, g_tots = [], [], []
        for chain, h in enumerate(chain_heads):
            direction = chain % 2
            chunk = c if direction == 0 else n_chunk - 1 - c
            r0 = pl.multiple_of(chunk * C, C)
            slots.append(chain * n_chunk + chunk)
            rows.append(pl.ds(r0, C))
            g_tots.append(total_decay(h, direction, r0))
        chains = range(len(chain_heads))
        states, outs = _dn_steps([s_scr[ch] for ch in chains], [u_scr[ch, rows[ch], :] for ch in chains],
                                 [wq_scr[slots[ch]] for ch in chains], [qkk_scr[slots[ch]] for ch in chains], g_tots)
        for ch in chains:
            s_scr[ch] = states[ch]
            o_scr[ch, rows[ch], :] = outs[ch]

    pass_heads = _dn_pass_heads(n_chunk)
    for h0 in range(0, DN_HEADS, pass_heads):
        heads = range(h0, h0 + pass_heads)
        for h1 in range(h0, h0 + pass_heads, stack_heads):
            stack = range(h1, h1 + stack_heads)
            for hh, h in enumerate(stack):
                q_scr[hh] = l2n(conv_silu(0, h)) * HEAD_DIM ** -0.5
                k_scr[hh] = l2n(conv_silu(1, h))
                v_scr[hh] = conv_silu(2, h)

            def prepare_body(grp, carry, stack=stack, h1=h1):
                prepare_group(grp, stack, (h1 - h0) * 2)
                return carry

            lax.fori_loop(0, n_group, prepare_body, 0)
        for h in heads:
            for direction in range(2):
                chain = (h - h0) * 2 + direction
                if has_s0:
                    s_scr[chain] = s0_ref[0, direction, h].astype(_F32)
                else:
                    s_scr[chain] = jnp.zeros((HEAD_DIM, HEAD_DIM), _F32)

        chain_heads = [h for h in heads for _ in range(2)]

        def step_body(c, carry):
            steps(chain_heads, c)
            return carry

        lax.fori_loop(0, n_chunk, step_body, 0)

        for h in heads:
            chain = (h - h0) * 2
            if emit_state:
                st_ref[0, 0, h] = s_scr[chain]
                st_ref[0, 1, h] = s_scr[chain + 1]
            o = o_scr[chain] + o_scr[chain + 1]
            z = dn_ref[:, 3 * DN_WIDTH + h * HEAD_DIM:3 * DN_WIDTH + (h + 1) * HEAD_DIM].astype(_F32)
            o = o * lax.rsqrt(jnp.mean(o * o, axis=-1, keepdims=True) + NORM_EPS) * nw_ref[...] * _silu(z)
            o_ref[:, h * HEAD_DIM:(h + 1) * HEAD_DIM] = o.astype(_BF16)


def _deltanet(dn, ab, conv_w, a_log, dt_bias, dn_norm, s0, *, n_seq, seq_len, row_block0, emit_state):
    has_s0 = s0 is not None
    const = lambda b: (0, 0)
    state_spec = pl.BlockSpec((1, 2, DN_HEADS, HEAD_DIM, HEAD_DIM), lambda b: (b, 0, 0, 0, 0))
    in_specs = [
        pl.BlockSpec((seq_len, 4 * DN_WIDTH), lambda b: (row_block0 + b, 0)),
        pl.BlockSpec((seq_len, 128), lambda b: (row_block0 + b, 0)),
        pl.BlockSpec((8, 3 * DN_WIDTH), const),
        pl.BlockSpec((1, 128), const),
        pl.BlockSpec((1, 128), const),
        pl.BlockSpec((1, HEAD_DIM), const),
    ]
    args = [dn, ab, conv_w, a_log, dt_bias, dn_norm]
    if has_s0:
        in_specs.append(state_spec)
        args.append(s0)
    out_specs = [pl.BlockSpec((seq_len, DN_WIDTH), lambda b: (b, 0))]
    out_shape = [jax.ShapeDtypeStruct((n_seq * seq_len, DN_WIDTH), _BF16)]
    if emit_state:
        out_specs.append(state_spec)
        out_shape.append(jax.ShapeDtypeStruct((n_seq, 2, DN_HEADS, HEAD_DIM, HEAD_DIM), _F32))
    n_chunk = seq_len // DN_CHUNK
    pass_tile = (_dn_stacking(n_chunk)[1], seq_len, HEAD_DIM)
    n_chain = 2 * _dn_pass_heads(n_chunk)
    return pl.pallas_call(
        functools.partial(_dn_kernel, seq_len=seq_len, has_s0=has_s0, emit_state=emit_state),
        grid=(n_seq,),
        in_specs=in_specs,
        out_specs=out_specs,
        out_shape=out_shape,
        scratch_shapes=[
            pltpu.VMEM((seq_len + 16, HEAD_DIM), _F32),
            pltpu.VMEM(pass_tile, _F32), pltpu.VMEM(pass_tile, _F32), pltpu.VMEM(pass_tile, _F32),
            pltpu.VMEM((seq_len, 128), _F32),
            pltpu.VMEM((seq_len // (2 * DN_CHUNK), 8, 128), _F32),
            pltpu.VMEM((n_chain, seq_len, HEAD_DIM), _F32),
            pltpu.VMEM((n_chain, seq_len, HEAD_DIM), _F32),
            pltpu.VMEM((n_chain * n_chunk, 2 * DN_CHUNK, HEAD_DIM), _BF16),
            pltpu.VMEM((n_chain * n_chunk, DN_CHUNK + HEAD_DIM, DN_CHUNK), _BF16),
            pltpu.VMEM((n_chain, HEAD_DIM, HEAD_DIM), _F32),
        ],
        compiler_params=_params(),
        name="deltanet_ctx" if emit_state else "deltanet_lat",
    )(*args)


def _softmax_av(scores, values, sink):
    m = sink
    for s in scores:
        m = jnp.maximum(m, jnp.max(s, axis=-1, keepdims=True))
    denom = jnp.exp(sink - m)
    acc = None
    for s, v in zip(scores, values):
        p = jnp.exp(s - m)
        denom = denom + jnp.sum(p, axis=-1, keepdims=True)
        pv = _dot(p, v)
        acc = pv if acc is None else acc + pv
    return acc / denom


def _ctx_attn_kernel(sink_ref, q_ref, k_ref, v_ref, o_ref):
    kvh = pl.program_id(1)
    k = k_ref[...]
    v = v_ref[...]
    for g in range(ATT_GROUP):
        q = q_ref[:, g * HEAD_DIM:(g + 1) * HEAD_DIM]
        sink = jnp.full((SEQ, 1), sink_ref[kvh * ATT_GROUP + g], _F32)
        o = _softmax_av([_dot_nt(q, k)], [v], sink)
        o_ref[:, g * HEAD_DIM:(g + 1) * HEAD_DIM] = o.astype(_BF16)


def _context_attention(sinks, aq, ak, av):
    return pl.pallas_call(
        _ctx_attn_kernel,
        grid=(BATCH, ATT_KV_HEADS),
        in_specs=[
            pl.BlockSpec(memory_space=pltpu.SMEM),
            pl.BlockSpec((SEQ, ATT_GROUP * HEAD_DIM), lambda b, h: (b, h)),
            pl.BlockSpec((SEQ, HEAD_DIM), lambda b, h: (b, h)),
            pl.BlockSpec((SEQ, HEAD_DIM), lambda b, h: (b, h)),
        ],
        out_specs=pl.BlockSpec((SEQ, ATT_GROUP * HEAD_DIM), lambda b, h: (b, h)),
        out_shape=jax.ShapeDtypeStruct((M_CTX, ATT_Q), _BF16),
        compiler_params=_params(2),
        name="context_attention",
    )(sinks, aq, ak, av)


LAT_Q_BLOCK = 2 * ATT_BLOCK


def _lat_attn_kernel(sink_ref, q_ref, kp_ref, kc_ref, kn_ref, vp_ref, vc_ref, vn_ref, ck_ref, cv_ref, o_ref):
    kvh = pl.program_id(1)
    i = pl.program_id(2)
    nb = pl.num_programs(2)
    B, Q = ATT_BLOCK, LAT_Q_BLOCK
    rows = ATT_GROUP * Q
    q = jnp.concatenate([q_ref[:, g * HEAD_DIM:(g + 1) * HEAD_DIM] for g in range(ATT_GROUP)], axis=0)
    r_b = lax.broadcasted_iota(jnp.int32, (rows, B), 0) % Q
    c_b = lax.broadcasted_iota(jnp.int32, (rows, B), 1)
    r_q = lax.broadcasted_iota(jnp.int32, (rows, Q), 0) % Q
    c_q = lax.broadcasted_iota(jnp.int32, (rows, Q), 1)
    s_prev = jnp.where(c_b >= r_b + jnp.where(i > 0, 0, Q), _dot_nt(q, kp_ref[...]), NEG_INF)
    s_cur = jnp.where(jnp.abs(r_q - c_q) <= B, _dot_nt(q, kc_ref[...]), NEG_INF)
    s_next = jnp.where(c_b <= r_b - B - jnp.where(i < nb - 1, 0, Q), _dot_nt(q, kn_ref[...]), NEG_INF)
    s_ctx = _dot_nt(q, ck_ref[...])
    head = lax.broadcasted_iota(jnp.int32, (rows, 1), 0) // Q
    sink = jnp.zeros((rows, 1), _F32)
    for g in range(ATT_GROUP):
        sink = jnp.where(head == g, sink_ref[kvh * ATT_GROUP + g], sink)
    o = _softmax_av([s_prev, s_cur, s_next, s_ctx], [vp_ref[...], vc_ref[...], vn_ref[...], cv_ref[...]], sink)
    for g in range(ATT_GROUP):
        o_ref[:, g * HEAD_DIM:(g + 1) * HEAD_DIM] = o[g * Q:(g + 1) * Q].astype(_BF16)


def _latent_attention(sinks, aq, ak, av, ctx_k, ctx_v):
    nq = DEC_SEQ // LAT_Q_BLOCK
    nb = DEC_SEQ // ATT_BLOCK
    q0 = M_CTX // LAT_Q_BLOCK
    b0 = M_CTX // ATT_BLOCK
    cur = lambda b, h, i: (q0 + b * nq + i, h)
    prev = lambda b, h, i: (b0 + b * nb + jnp.maximum(2 * i - 1, 0), h)
    nxt = lambda b, h, i: (b0 + b * nb + jnp.minimum(2 * i + 2, nb - 1), h)
    side_spec = lambda f: pl.BlockSpec((ATT_BLOCK, HEAD_DIM), f)
    cur_spec = pl.BlockSpec((LAT_Q_BLOCK, HEAD_DIM), cur)
    ctx_spec = pl.BlockSpec((PAST_LEN, HEAD_DIM), lambda b, h, i: (b, h))
    return pl.pallas_call(
        _lat_attn_kernel,
        grid=(DEC_BATCH, ATT_KV_HEADS, nq),
        in_specs=[
            pl.BlockSpec(memory_space=pltpu.SMEM),
            pl.BlockSpec((LAT_Q_BLOCK, ATT_GROUP * HEAD_DIM), cur),
            side_spec(prev), cur_spec, side_spec(nxt),
            side_spec(prev), cur_spec, side_spec(nxt),
            ctx_spec, ctx_spec,
        ],
        out_specs=pl.BlockSpec((LAT_Q_BLOCK, ATT_GROUP * HEAD_DIM), lambda b, h, i: (b * nq + i, h)),
        out_shape=jax.ShapeDtypeStruct((M_LAT, ATT_Q), _BF16),
        compiler_params=_params(3),
        name="latent_attention",
    )(sinks, aq, ak, ak, ak, av, av, av, ctx_k, ctx_v)


def _outproj_kernel(xp_ref, xs_ref, dnc_ref, dnl_ref, atc_ref, atl_ref, mod_ref, n2_ref, wo_dn_ref, wo_at_ref,
                    rw_ref, x1_ref, h2_ref, h2p_ref, lg_ref):
    x = jnp.where(_ctx_tile_mask((TOK_TILE, D_MODEL)), xp_ref[...], xs_ref[...])
    half_mask = _ctx_tile_mask((TOK_TILE, DN_WIDTH))
    dn = jnp.where(half_mask, dnc_ref[...], dnl_ref[...])
    at = jnp.where(half_mask, atc_ref[...], atl_ref[...])
    mixed = (jnp.dot(dn, wo_dn_ref[...], preferred_element_type=_F32)
             + jnp.dot(at, wo_at_ref[...], preferred_element_type=_F32))
    x1 = x + mod_ref[0, 2:3, :] * mixed
    x1_ref[...] = x1
    h = _rms(x1, n2_ref[...]) * (1 + mod_ref[0, 4:5, :]) + mod_ref[0, 3:4, :]
    h2_ref[...] = h.astype(_BF16)
    _store_token_tiles(h2p_ref, h, TOK_TILE)
    w = rw_ref[...]
    w_hi = w.astype(_BF16)
    w_lo = (w - w_hi.astype(_F32)).astype(_BF16)
    h_hi = h.astype(_BF16)
    h_lo = (h - h_hi.astype(_F32)).astype(_BF16)
    lg_ref[...] = (_dot_nt(w_hi, h_hi) + _dot_nt(w_hi, h_lo)) + _dot_nt(w_lo, h_hi)


def _output_projection(xp, xs, dn_c, dn_l, at_c, at_l, mod3, norm2, wo_dn, wo_at, router_wt):
    const = lambda i: (0, 0)
    row = lambda i: (i, 0)
    ctx_row = lambda i: (jnp.minimum(i, N_CTX_TILES - 1), 0)
    lat_row = lambda i: (jnp.maximum(i - N_CTX_TILES, 0), 0)
    half = (TOK_TILE, DN_WIDTH)
    return pl.pallas_call(
        _outproj_kernel,
        grid=(N_TILES,),
        in_specs=_x_specs() + [
            pl.BlockSpec(half, ctx_row), pl.BlockSpec(half, lat_row),
            pl.BlockSpec(half, ctx_row), pl.BlockSpec(half, lat_row),
            pl.BlockSpec((1, 6, D_MODEL), lambda i: (_tile_mod_row(i), 0, 0)),
            pl.BlockSpec((1, D_MODEL), const),
            pl.BlockSpec((DN_WIDTH, D_MODEL), const),
            pl.BlockSpec((ATT_Q, D_MODEL), const),
            pl.BlockSpec((N_EXPERTS, D_MODEL), const),
        ],
        out_specs=[
            pl.BlockSpec((TOK_TILE, D_MODEL), row),
            pl.BlockSpec((TOK_TILE, D_MODEL), row),
            pl.BlockSpec((TOK_TILE * TOKEN_TILE_ROWS, 128), row),
            pl.BlockSpec((N_EXPERTS, TOK_TILE), lambda i: (0, i)),
        ],
        out_shape=[
            jax.ShapeDtypeStruct((M_ALL, D_MODEL), _F32),
            jax.ShapeDtypeStruct((M_ALL, D_MODEL), _BF16),
            jax.ShapeDtypeStruct((M_ALL * TOKEN_TILE_ROWS, 128), _F32),
            jax.ShapeDtypeStruct((N_EXPERTS, M_ALL), _F32),
        ],
        compiler_params=_params(),
        name="output_projection",
    )(xp, xs, dn_c, dn_l, at_c, at_l, mod3, norm2, wo_dn, wo_at, router_wt)


def _first_index_of(values, target, index, limit):
    return jnp.min(jnp.where(values == target, index, limit), axis=0, keepdims=True)


def _route_kernel(lg_ref, bias_ref, idx_ref, rank_ref, gate_ref, cnt_ref, carry_scr):
    i = pl.program_id(0)
    tm = TOK_TILE

    @pl.when(i == 0)
    def _():
        carry_scr[...] = jnp.zeros_like(carry_scr)

    scores = jax.nn.sigmoid(lg_ref[...])
    biased = scores + bias_ref[...]
    row = lax.broadcasted_iota(jnp.int32, (N_EXPERTS, tm), 0).astype(_F32)
    grow = lax.broadcasted_iota(jnp.int32, (GROUP_SIZE, tm), 0).astype(_F32)

    group_rows = []
    for g in range(N_GROUPS):
        blk = biased[g * GROUP_SIZE:(g + 1) * GROUP_SIZE]
        m1 = jnp.max(blk, axis=0, keepdims=True)
        i1 = _first_index_of(blk, m1, grow, GROUP_SIZE)
        m2 = jnp.max(jnp.where(grow == i1, -jnp.inf, blk), axis=0, keepdims=True)
        group_rows.append(m1 + m2)
    gs = jnp.concatenate(group_rows, axis=0)
    gi = lax.broadcasted_iota(jnp.int32, (N_GROUPS, tm), 0).astype(_F32)
    gself = jnp.zeros((N_GROUPS, tm), _F32)
    for _ in range(TOPK_GROUPS):
        hit = gi == _first_index_of(gs, jnp.max(gs, axis=0, keepdims=True), gi, N_GROUPS)
        gself = jnp.where(hit, 1.0, gself)
        gs = jnp.where(hit, -jnp.inf, gs)
    emask = jnp.concatenate(
        [jnp.broadcast_to(gself[g:g + 1], (GROUP_SIZE, tm)) for g in range(N_GROUPS)], axis=0) > 0.5
    masked = jnp.where(emask, biased, NEG_INF)

    selected = jnp.zeros((N_EXPERTS, tm), _F32)
    idxs, gates = [], []
    for _ in range(TOP_K):
        ei = _first_index_of(masked, jnp.max(masked, axis=0, keepdims=True), row, N_EXPERTS)
        hit = row == ei
        idxs.append(ei)
        gates.append(jnp.sum(jnp.where(hit, scores, 0.0), axis=0, keepdims=True))
        masked = jnp.where(hit, -jnp.inf, masked)
        selected = jnp.where(hit, 1.0, selected)
    gsum = gates[0]
    for gk in gates[1:]:
        gsum = gsum + gk
    gates = [gk / gsum * ROUTED_SCALE for gk in gates]

    si = lax.broadcasted_iota(jnp.int32, (tm, tm), 0)
    ti = lax.broadcasted_iota(jnp.int32, (tm, tm), 1)
    earlier = (si < ti).astype(_BF16)
    ranks_all = _dot(selected, earlier) + carry_scr[...]
    ranks = [jnp.sum(jnp.where(row == ei, ranks_all, 0.0), axis=0, keepdims=True) for ei in idxs]
    carry_scr[...] = carry_scr[...] + jnp.sum(selected, axis=1, keepdims=True)

    idx_ref[...] = jnp.concatenate(idxs, axis=0).astype(jnp.int32)
    rank_ref[...] = jnp.concatenate(ranks, axis=0).astype(jnp.int32)
    gate_rows = jnp.concatenate(gates + [jnp.zeros((128 - TOP_K, tm), _F32)], axis=0)
    gate_ref[...] = gate_rows.T
    cnt_ref[...] = jnp.broadcast_to(carry_scr[...], (N_EXPERTS, 128))


def _route(logits_t, bias_col):
    return pl.pallas_call(
        _route_kernel,
        grid=(N_TILES,),
        in_specs=[pl.BlockSpec((N_EXPERTS, TOK_TILE), lambda i: (0, i)),
                  pl.BlockSpec((N_EXPERTS, 1), lambda i: (0, 0))],
        out_specs=[
            pl.BlockSpec((TOP_K, TOK_TILE), lambda i: (0, i)),
            pl.BlockSpec((TOP_K, TOK_TILE), lambda i: (0, i)),
            pl.BlockSpec((TOK_TILE, 128), lambda i: (i, 0)),
            pl.BlockSpec((N_EXPERTS, 128), lambda i: (0, 0)),
        ],
        out_shape=[
            jax.ShapeDtypeStruct((TOP_K, M_ALL), jnp.int32),
            jax.ShapeDtypeStruct((TOP_K, M_ALL), jnp.int32),
            jax.ShapeDtypeStruct((M_ALL, 128), _F32),
            jax.ShapeDtypeStruct((N_EXPERTS, 128), _F32),
        ],
        scratch_shapes=[pltpu.VMEM((N_EXPERTS, 1), _F32)],
        compiler_params=_params(),
        name="route",
    )(logits_t, bias_col)


def _slot_kernel(idx_ref, rank_ref, start_ref, pos_ref):
    row = lax.broadcasted_iota(jnp.int32, (N_EXPERTS, TOK_TILE), 0)
    start = start_ref[...]
    rows = []
    for k in range(TOP_K):
        base = jnp.sum(jnp.where(row == idx_ref[k:k + 1, :], start, 0.0), axis=0, keepdims=True)
        rows.append(base.astype(jnp.int32) + rank_ref[k:k + 1, :])
    pos_ref[...] = jnp.concatenate(rows, axis=0)


def _slots(idx, rank, start_col):
    spec = pl.BlockSpec((TOP_K, TOK_TILE), lambda i: (0, i))
    return pl.pallas_call(
        _slot_kernel,
        grid=(N_TILES,),
        in_specs=[spec, spec, pl.BlockSpec((N_EXPERTS, 1), lambda i: (0, 0))],
        out_specs=spec,
        out_shape=jax.ShapeDtypeStruct((TOP_K, M_ALL), jnp.int32),
        compiler_params=_params(),
        name="slots",
    )(idx, rank, start_col)


def _slot_tile(ref, slot):
    return ref.at[pl.ds(pl.multiple_of(slot * TOKEN_TILE_ROWS, TOKEN_TILE_ROWS), TOKEN_TILE_ROWS)]


def _dispatch_kernel(pad_lo_ref, pad_hi_ref, nused_ref, pos_ref, h_hbm, xs_ref, zero_scr, h_buf, in_sem, out_sem,
                     zero_sem):
    i = pl.program_id(0)
    n_steps = pl.num_programs(0)
    block_rows = TOK_TILE * TOKEN_TILE_ROWS

    def fetch(step):
        src = h_hbm.at[pl.ds(pl.multiple_of(step * block_rows, block_rows), block_rows)]
        return pltpu.make_async_copy(src, h_buf.at[step % 3], in_sem.at[step % 3])

    def wait_copies(step):
        for _ in range(TOP_K):
            pltpu.make_async_copy(h_buf.at[0], xs_ref.at[pl.ds(0, block_rows)], out_sem.at[step % 2]).wait()

    @pl.when(i == 0)
    def _():
        fetch(i).start()

    @pl.when(i + 1 < n_steps)
    def _():
        fetch(i + 1).start()

    def zero_fill(start):
        def zero_copy(first_slot, n_slots):
            rows = n_slots * TOKEN_TILE_ROWS
            dst = xs_ref.at[pl.ds(pl.multiple_of(first_slot * TOKEN_TILE_ROWS, TOKEN_TILE_ROWS), rows)]
            return pltpu.make_async_copy(zero_scr.at[pl.ds(0, rows)], dst, zero_sem)

        def expert_padding(e, carry):
            first = pad_lo_ref[e]
            n = pad_hi_ref[e] - first
            size = EXPERT_ROWS // 2
            while size >= 1:
                chunk = n & size

                @pl.when(chunk != 0)
                def _(first=first, size=size):
                    copy = zero_copy(first, size)
                    copy.start() if start else copy.wait()

                first = first + chunk
                size //= 2
            return carry

        def tail_block(t, carry):
            j = N_BLOCKS - 1 - t

            @pl.when(j >= nused_ref[0])
            def _():
                copy = zero_copy(j * EXPERT_ROWS, EXPERT_ROWS)
                copy.start() if start else copy.wait()

            return carry

        lax.fori_loop(0, N_EXPERTS, expert_padding, 0)
        lax.fori_loop(0, N_BLOCKS - N_PAIRS // EXPERT_ROWS, tail_block, 0)

    @pl.when(i == 0)
    def _():
        zero_scr[...] = jnp.zeros_like(zero_scr)
        zero_fill(start=True)

    fetch(i).wait()
    h_tile = h_buf.at[i % 3]

    def body(t, carry):
        src = _slot_tile(h_tile, t)
        for k in range(TOP_K):
            pltpu.make_async_copy(src, _slot_tile(xs_ref, pos_ref[0, k, t]), out_sem.at[i % 2]).start()
        return carry

    lax.fori_loop(0, TOK_TILE, body, 0, unroll=4)

    @pl.when(i >= 1)
    def _():
        wait_copies(i - 1)

    @pl.when(i == n_steps - 1)
    def _():
        wait_copies(i)
        zero_fill(start=False)


def _dispatch(pad_lo, pad_hi, n_used, pos3, h2p):
    grid_spec = pltpu.PrefetchScalarGridSpec(
        num_scalar_prefetch=3,
        grid=(N_TILES,),
        in_specs=[pl.BlockSpec((1, TOP_K, TOK_TILE), lambda i, *_: (i, 0, 0), memory_space=pltpu.SMEM),
                  pl.BlockSpec(memory_space=pl.ANY)],
        out_specs=pl.BlockSpec(memory_space=pl.ANY),
        scratch_shapes=[pltpu.VMEM((EXPERT_ROWS * TOKEN_TILE_ROWS, 128), _F32),
                        pltpu.VMEM((3, TOK_TILE * TOKEN_TILE_ROWS, 128), _F32),
                        pltpu.SemaphoreType.DMA((3,)), pltpu.SemaphoreType.DMA((2,)),
                        pltpu.SemaphoreType.DMA(())],
    )
    return pl.pallas_call(
        _dispatch_kernel,
        grid_spec=grid_spec,
        out_shape=jax.ShapeDtypeStruct((N_SLOTS * TOKEN_TILE_ROWS, 128), _F32),
        compiler_params=_params(),
        name="dispatch",
    )(pad_lo, pad_hi, n_used, pos3, h2p)


def _expert_kernel(be_ref, nused_ref, first_ref, next_ref, parity_ref, xs_ref, w1_hbm, w3_hbm, w2_hbm, ys_ref,
                   w1_buf, w3_buf, w2_buf, w1_scr, w3_scr, w2_scr, sem):
    j = pl.program_id(0)

    def weight_copies(expert, slot):
        return [pltpu.make_async_copy(hbm.at[expert], buf.at[slot], sem.at[slot])
                for hbm, buf in ((w1_hbm, w1_buf), (w3_hbm, w3_buf), (w2_hbm, w2_buf))]

    @pl.when(j < nused_ref[0])
    def _():
        @pl.when(j == 0)
        def _():
            for copy in weight_copies(be_ref[0], 0):
                copy.start()

        @pl.when(first_ref[j] == 1)
        def _():
            slot = parity_ref[j]
            for copy in weight_copies(be_ref[j], slot):
                copy.wait()

            @pl.when(next_ref[j] >= 0)
            def _():
                for copy in weight_copies(next_ref[j], 1 - slot):
                    copy.start()

            w1_scr[...] = w1_buf[slot].astype(_BF16)
            w3_scr[...] = w3_buf[slot].astype(_BF16)
            w2_scr[...] = w2_buf[slot].astype(_BF16)

        x = _load_token_tiles(xs_ref, EXPERT_ROWS).astype(_BF16)
        h1 = jnp.dot(x, w1_scr[...], preferred_element_type=_F32)
        h3 = jnp.dot(x, w3_scr[...], preferred_element_type=_F32)
        act = (_silu(h1) * h3).astype(_BF16)
        _store_token_tiles(ys_ref, jnp.dot(act, w2_scr[...], preferred_element_type=_F32), EXPERT_ROWS)

    @pl.when(j >= nused_ref[0])
    def _():
        ys_ref[...] = jnp.zeros_like(ys_ref)


def _experts(block_expert, n_used, run_first, run_next, run_parity, xs, w1, w3, w2):
    def blk(j, be, nu, *_):
        return (jnp.minimum(j, nu[0] - 1), 0)

    up, down = (D_MODEL, EXPERT_FF), (EXPERT_FF, D_MODEL)
    grid_spec = pltpu.PrefetchScalarGridSpec(
        num_scalar_prefetch=5,
        grid=(N_BLOCKS,),
        in_specs=[
            pl.BlockSpec((EXPERT_ROWS * TOKEN_TILE_ROWS, 128), blk),
            pl.BlockSpec(memory_space=pl.ANY),
            pl.BlockSpec(memory_space=pl.ANY),
            pl.BlockSpec(memory_space=pl.ANY),
        ],
        out_specs=pl.BlockSpec((EXPERT_ROWS * TOKEN_TILE_ROWS, 128), lambda j, *_: (j, 0)),
        scratch_shapes=[pltpu.VMEM((2,) + up, _F32), pltpu.VMEM((2,) + up, _F32), pltpu.VMEM((2,) + down, _F32),
                        pltpu.VMEM(up, _BF16), pltpu.VMEM(up, _BF16), pltpu.VMEM(down, _BF16),
                        pltpu.SemaphoreType.DMA((2,))],
    )
    return pl.pallas_call(
        _expert_kernel,
        grid_spec=grid_spec,
        out_shape=jax.ShapeDtypeStruct((N_SLOTS * TOKEN_TILE_ROWS, 128), _F32),
        compiler_params=_params(),
        name="experts",
    )(block_expert, n_used, run_first, run_next, run_parity, xs, w1, w3, w2)


def _combine_kernel(pos_ref, pos_next_ref, gate_ref, h2_ref, x1_ref, mod_ref, sw1_ref, sw3_ref, sw2_ref, fn_ref,
                    ys_ref, y_ref, buf, sem):
    tm = COMBINE_TILE
    i = pl.program_id(0)
    slot = i % 2

    def gather(p_ref, dst_slot):
        def body(t, carry):
            for k in range(TOP_K):
                pltpu.make_async_copy(_slot_tile(ys_ref, p_ref[0, k, t]), _slot_tile(buf.at[dst_slot, k], t),
                                      sem.at[dst_slot]).start()
            return carry

        lax.fori_loop(0, tm, body, 0, unroll=4)

    @pl.when(i == 0)
    def _():
        gather(pos_ref, 0)

    @pl.when(i + 1 < pl.num_programs(0))
    def _():
        gather(pos_next_ref, 1 - slot)

    hb = h2_ref[...]
    act = _silu(jnp.dot(hb, sw1_ref[...], preferred_element_type=_F32)) * jnp.dot(
        hb, sw3_ref[...], preferred_element_type=_F32)
    shared = _dot(act, sw2_ref[...])
    for k in range(TOP_K):
        pltpu.make_async_copy(ys_ref.at[pl.ds(0, tm * TOKEN_TILE_ROWS)], buf.at[slot, k], sem.at[slot]).wait()
    gates = gate_ref[...]
    routed = gates[:, 0:1] * _load_token_tiles(buf.at[slot, 0], tm)
    for k in range(1, TOP_K):
        routed = routed + gates[:, k:k + 1] * _load_token_tiles(buf.at[slot, k], tm)
    y = x1_ref[...] + mod_ref[0, 5:6, :] * (routed + shared)
    y_ref[...] = _rms(y, fn_ref[...])


def _combine(pos3, gate_t, h2, x1, mod3, sw1, sw3, sw2, final_norm, ys, *, n_rows, tile0, mod_row):
    tm = COMBINE_TILE
    const = lambda i: (0, 0)
    row = lambda i: (tile0 + i, 0)
    n_tiles = n_rows // tm
    return pl.pallas_call(
        _combine_kernel,
        grid=(n_tiles,),
        in_specs=[
            pl.BlockSpec((1, TOP_K, tm), lambda i: (tile0 + i, 0, 0), memory_space=pltpu.SMEM),
            pl.BlockSpec((1, TOP_K, tm), lambda i: (tile0 + jnp.minimum(i + 1, n_tiles - 1), 0, 0),
                         memory_space=pltpu.SMEM),
            pl.BlockSpec((tm, 128), row),
            pl.BlockSpec((tm, D_MODEL), row),
            pl.BlockSpec((tm, D_MODEL), row),
            pl.BlockSpec((1, 6, D_MODEL), lambda i: (mod_row(i), 0, 0)),
            pl.BlockSpec((D_MODEL, EXPERT_FF), const),
            pl.BlockSpec((D_MODEL, EXPERT_FF), const),
            pl.BlockSpec((EXPERT_FF, D_MODEL), const),
            pl.BlockSpec((1, D_MODEL), const),
            pl.BlockSpec(memory_space=pl.ANY),
        ],
        out_specs=pl.BlockSpec((tm, D_MODEL), lambda i: (i, 0)),
        out_shape=jax.ShapeDtypeStruct((n_rows, D_MODEL), _F32),
        scratch_shapes=[pltpu.VMEM((2, TOP_K, tm * TOKEN_TILE_ROWS, 128), _F32), pltpu.SemaphoreType.DMA((2,))],
        compiler_params=_params(),
        name="combine",
    )(pos3, pos3, gate_t, h2, x1, mod3, sw1, sw3, sw2, final_norm, ys)


def _tile_major(a, tile):
    return a.reshape(TOP_K, -1, tile).transpose(1, 0, 2)


def kernel(x_prompt, x_sample, c, cache_k, cache_v, state_dn, c_ctx, w_ada, b_ada, norm1, norm2, w_in, dn_conv,
           dn_A_log, dn_dt_bias, dn_norm, attn_sinks, w_out, router_w, router_bias, expert_w1, expert_w3,
           expert_w2, shared_w1, shared_w3, shared_w2, final_norm):
    xp = x_prompt.reshape(M_CTX, D_MODEL)
    xs = x_sample.reshape(M_LAT, D_MODEL)

    cvec = jnp.concatenate([c_ctx[None, :], c, jnp.zeros((N_MOD - 1 - DEC_BATCH, D_MODEL), _F32)], axis=0)
    mod3 = _modulation(cvec, w_ada[0], b_ada).reshape(N_MOD, 6, D_MODEL)

    w = w_in[0]
    n_dn = 4 * DN_WIDTH
    w_dn = w[:, :n_dn].astype(_BF16)
    w_ab = jnp.pad(w[:, n_dn:n_dn + 4 * DN_HEADS], ((0, 0), (0, 128 - 4 * DN_HEADS))).astype(_BF16)
    w_att = w[:, n_dn + 4 * DN_HEADS:].astype(_BF16)
    cos, sin = _rope_tables()
    dn, aq, ak, av, ab = _input_projection(xp, xs, mod3, norm1, cos, sin, w_dn, w_att, w_ab)

    conv_w = jnp.pad(dn_conv[0], ((0, 8 - DN_CONV), (0, 0)))
    pad8 = lambda v: jnp.pad(v.reshape(1, 2 * DN_HEADS), ((0, 0), (0, 128 - 2 * DN_HEADS)))
    a_log = pad8(dn_A_log[0])
    dt_bias = pad8(dn_dt_bias[0])
    dn_c, new_state = _deltanet(dn, ab, conv_w, a_log, dt_bias, dn_norm, None,
                                n_seq=BATCH, seq_len=SEQ, row_block0=0, emit_state=True)
    (dn_l,) = _deltanet(dn, ab, conv_w, a_log, dt_bias, dn_norm, state_dn[:, 0],
                        n_seq=DEC_BATCH, seq_len=DEC_SEQ, row_block0=M_CTX // DEC_SEQ, emit_state=False)

    sinks = attn_sinks[0]
    at_c = _context_attention(sinks, aq, ak, av)
    ctx_k = cache_k[:, 0].reshape(DEC_BATCH * PAST_LEN, ATT_KV)
    ctx_v = cache_v[:, 0].reshape(DEC_BATCH * PAST_LEN, ATT_KV)
    at_l = _latent_attention(sinks, aq, ak, av, ctx_k, ctx_v)

    wo = w_out[0].astype(_BF16)
    x1, h2, h2p, logits_t = _output_projection(xp, xs, dn_c, dn_l, at_c, at_l, mod3, norm2,
                                               wo[:DN_WIDTH], wo[DN_WIDTH:], router_w[0].T)

    idx, rank, gate_t, counts = _route(logits_t, router_bias[0].reshape(N_EXPERTS, 1))
    cnt = counts[:, 0].astype(jnp.int32)
    padded = (cnt + EXPERT_ROWS - 1) // EXPERT_ROWS * EXPERT_ROWS
    pad_end = jnp.cumsum(padded)
    pos = _slots(idx, rank, (pad_end - padded).astype(_F32).reshape(N_EXPERTS, 1))
    n_used = (pad_end[-1] // EXPERT_ROWS).astype(jnp.int32).reshape(1)
    block_start = jnp.arange(N_BLOCKS, dtype=jnp.int32) * EXPERT_ROWS
    block_expert = jnp.minimum(
        jnp.sum((pad_end[None, :] <= block_start[:, None]).astype(jnp.int32), axis=1), N_EXPERTS - 1)

    pad_lo = (pad_end - padded + cnt).astype(jnp.int32)
    x_sorted = _dispatch(pad_lo, pad_end.astype(jnp.int32), n_used, _tile_major(pos, TOK_TILE), h2p)
    run_first = jnp.concatenate([jnp.ones((1,), jnp.int32),
                                 (block_expert[1:] != block_expert[:-1]).astype(jnp.int32)])
    run_parity = (jnp.cumsum(run_first) - 1) % 2
    experts = jnp.arange(N_EXPERTS, dtype=jnp.int32)
    later = (experts[None, :] > experts[:, None]) & (cnt[None, :] > 0)
    next_expert = jnp.min(jnp.where(later, experts[None, :], N_EXPERTS), axis=1)
    next_expert = jnp.where(next_expert == N_EXPERTS, -1, next_expert)
    run_next = jnp.sum(jnp.where(block_expert[:, None] == experts[None, :], next_expert[None, :], 0), axis=1)
    ys = _experts(block_expert, n_used, run_first, run_next.astype(jnp.int32), run_parity.astype(jnp.int32),
                  x_sorted, expert_w1[0], expert_w3[0], expert_w2[0])

    pos_c = _tile_major(pos, COMBINE_TILE)
    sw1, sw3, sw2 = shared_w1[0].astype(_BF16), shared_w3[0].astype(_BF16), shared_w2[0].astype(_BF16)
    fn = final_norm.reshape(1, D_MODEL)
    y_prompt = _combine(pos_c, gate_t, h2, x1, mod3, sw1, sw3, sw2, fn, ys,
                        n_rows=M_CTX, tile0=0, mod_row=lambda i: 0)
    lat_tiles = DEC_SEQ // COMBINE_TILE
    y_sample = _combine(pos_c, gate_t, h2, x1, mod3, sw1, sw3, sw2, fn, ys,
                        n_rows=M_LAT, tile0=M_CTX // COMBINE_TILE, mod_row=lambda i: 1 + i // lat_tiles)

    new_cache_k = ak[:M_CTX].reshape(BATCH, 1, SEQ, ATT_KV_HEADS, HEAD_DIM)
    new_cache_v = av[:M_CTX].reshape(BATCH, 1, SEQ, ATT_KV_HEADS, HEAD_DIM)
    return (y_prompt.reshape(BATCH, SEQ, D_MODEL), y_sample.reshape(DEC_BATCH, DEC_SEQ, D_MODEL),
            new_cache_k, new_cache_v, new_state.reshape(BATCH, 1, 2, DN_HEADS, HEAD_DIM, HEAD_DIM))
```

```python
import functools

import jax
import jax.numpy as jnp
import numpy as np
from jax import lax
from jax.experimental import pallas as pl
from jax.experimental.pallas import tpu as pltpu

D_MODEL = 1024
BATCH = 32
SEQ = 256
DEC_BATCH = 8
DEC_SEQ = 2048
PAST_LEN = 512
GRID_W = 64
HEAD_DIM = 128
DN_HEADS = 4
DN_WIDTH = DN_HEADS * HEAD_DIM
DN_CONV = 5
DN_CHUNK = 64
ATT_HEADS = 4
ATT_KV_HEADS = 2
ATT_GROUP = ATT_HEADS // ATT_KV_HEADS
ATT_Q = ATT_HEADS * HEAD_DIM
ATT_KV = ATT_KV_HEADS * HEAD_DIM
ATT_BLOCK = 128
ROPE_THETA = 10000.0
N_EXPERTS = 256
TOP_K = 8
N_GROUPS = 8
TOPK_GROUPS = 4
GROUP_SIZE = N_EXPERTS // N_GROUPS
EXPERT_FF = D_MODEL // 4
ROUTED_SCALE = 2.5
NORM_EPS = 1e-6
NEG_INF = -1e30

M_CTX = BATCH * SEQ
M_LAT = DEC_BATCH * DEC_SEQ
M_ALL = M_CTX + M_LAT
N_MOD = 16
TOK_TILE = 512
N_CTX_TILES = M_CTX // TOK_TILE
N_TILES = M_ALL // TOK_TILE
EXPERT_ROWS = 256
N_PAIRS = M_ALL * TOP_K
N_BLOCKS = (N_PAIRS + N_EXPERTS * (EXPERT_ROWS - 1)) // EXPERT_ROWS
N_SLOTS = N_BLOCKS * EXPERT_ROWS
COMBINE_TILE = 256
V7X_VMEM_BYTES = 64 * 1024 * 1024
VMEM_LIMIT = V7X_VMEM_BYTES - 8 * 1024 * 1024

_BF16 = jnp.bfloat16
_F32 = jnp.float32


def _dot(a, b):
    return jnp.dot(a.astype(_BF16), b.astype(_BF16), preferred_element_type=_F32)


def _dot_nt(a, b):
    return lax.dot_general(a.astype(_BF16), b.astype(_BF16), (((1,), (1,)), ((), ())),
                           preferred_element_type=_F32)


def _dot_tn(a, b):
    return lax.dot_general(a.astype(_BF16), b.astype(_BF16), (((0,), (0,)), ((), ())),
                           preferred_element_type=_F32)


def _silu(x):
    return x * jax.nn.sigmoid(x)


def _rms(x, w):
    return x * lax.rsqrt(jnp.mean(x * x, axis=-1, keepdims=True) + NORM_EPS) * w


def _params(n_axes=1):
    return pltpu.CompilerParams(dimension_semantics=("arbitrary",) * n_axes, vmem_limit_bytes=VMEM_LIMIT)


def _tile_mod_row(i):
    return jnp.where(i < N_CTX_TILES, 0, 1 + (i - N_CTX_TILES) // (DEC_SEQ // TOK_TILE))


def _ctx_tile_mask(shape):
    limit = jnp.where(pl.program_id(0) < N_CTX_TILES, shape[0], 0)
    return lax.broadcasted_iota(jnp.int32, shape, 0) < limit


TOKEN_TILE_ROWS = D_MODEL // 128


def _store_token_tiles(ref, x, n):
    for s in range(TOKEN_TILE_ROWS):
        ref[pl.ds(s, n, stride=TOKEN_TILE_ROWS), :] = x[:, s * 128:(s + 1) * 128]


def _load_token_tiles(ref, n):
    return jnp.concatenate([ref[pl.ds(s, n, stride=TOKEN_TILE_ROWS), :] for s in range(TOKEN_TILE_ROWS)], axis=1)


def _x_specs():
    return [
        pl.BlockSpec((TOK_TILE, D_MODEL), lambda i: (jnp.minimum(i, N_CTX_TILES - 1), 0)),
        pl.BlockSpec((TOK_TILE, D_MODEL), lambda i: (jnp.maximum(i - N_CTX_TILES, 0), 0)),
    ]


def _mod_kernel(c_ref, w_ref, b_ref, o_ref):
    o_ref[...] = _dot(_silu(c_ref[...]), w_ref[...]) + b_ref[...]


def _modulation(cvec, w_ada, b_ada):
    tn = 1024
    return pl.pallas_call(
        _mod_kernel,
        grid=(6 * D_MODEL // tn,),
        in_specs=[pl.BlockSpec((N_MOD, D_MODEL), lambda j: (0, 0)),
                  pl.BlockSpec((D_MODEL, tn), lambda j: (0, j)),
                  pl.BlockSpec((1, tn), lambda j: (0, j))],
        out_specs=pl.BlockSpec((N_MOD, tn), lambda j: (0, j)),
        out_shape=jax.ShapeDtypeStruct((N_MOD, 6 * D_MODEL), _F32),
        compiler_params=_params(),
        name="modulation",
    )(cvec, w_ada, b_ada)


def _rope(x, cos, sin, first_half):
    swapped = jnp.where(first_half, pltpu.roll(x, 96, 1), pltpu.roll(x, 32, 1))
    return x * cos + swapped * sin


def _inproj_kernel(xp_ref, xs_ref, mod_ref, n1_ref, cos_ref, sin_ref, wdn_ref, watt_ref, wab_ref,
                   dn_ref, aq_ref, ak_ref, av_ref, ab_ref):
    x = jnp.where(_ctx_tile_mask((TOK_TILE, D_MODEL)), xp_ref[...], xs_ref[...])
    shift = mod_ref[0, 0:1, :]
    scale = mod_ref[0, 1:2, :]
    h = (_rms(x, n1_ref[...]) * (1 + scale) + shift).astype(_BF16)
    dn_ref[...] = jnp.dot(h, wdn_ref[...], preferred_element_type=_F32).astype(_BF16)
    ab_ref[...] = jnp.dot(h, wab_ref[...], preferred_element_type=_F32)
    att = jnp.dot(h, watt_ref[...], preferred_element_type=_F32)
    cos = cos_ref[...]
    sin = sin_ref[...]
    lane = lax.broadcasted_iota(jnp.int32, (TOK_TILE, HEAD_DIM), 1)
    first_half = (lane % 64) < 32
    for hd in range(ATT_HEADS):
        q = att[:, hd * HEAD_DIM:(hd + 1) * HEAD_DIM]
        aq_ref[:, hd * HEAD_DIM:(hd + 1) * HEAD_DIM] = (
            _rope(q, cos, sin, first_half) * HEAD_DIM ** -0.5).astype(_BF16)
    for hd in range(ATT_KV_HEADS):
        k = att[:, ATT_Q + hd * HEAD_DIM:ATT_Q + (hd + 1) * HEAD_DIM]
        ak_ref[:, hd * HEAD_DIM:(hd + 1) * HEAD_DIM] = _rope(k, cos, sin, first_half)
    av_ref[...] = att[:, ATT_Q + ATT_KV:]


def _rope_tables():
    t = jnp.arange(DEC_SEQ)
    row = (t // GRID_W).astype(_F32)
    col = (t % GRID_W).astype(_F32)
    n_freq = HEAD_DIM // 4
    inv_freq = 1.0 / (ROPE_THETA ** (jnp.arange(n_freq, dtype=_F32) / n_freq))
    ang_r = row[:, None] * inv_freq
    ang_c = col[:, None] * inv_freq
    cos = jnp.concatenate([jnp.cos(ang_r), jnp.cos(ang_r), jnp.cos(ang_c), jnp.cos(ang_c)], axis=1)
    sin = jnp.concatenate([-jnp.sin(ang_r), jnp.sin(ang_r), -jnp.sin(ang_c), jnp.sin(ang_c)], axis=1)
    cos = jnp.concatenate([jnp.ones((TOK_TILE, HEAD_DIM), _F32), cos], axis=0)
    sin = jnp.concatenate([jnp.zeros((TOK_TILE, HEAD_DIM), _F32), sin], axis=0)
    return cos, sin


def _input_projection(xp, xs, mod3, norm1, cos, sin, w_dn, w_att, w_ab):
    def rope_idx(i):
        return (jnp.where(i < N_CTX_TILES, 0, 1 + (i - N_CTX_TILES) % (DEC_SEQ // TOK_TILE)), 0)

    const = lambda i: (0, 0)
    row = lambda i: (i, 0)
    return pl.pallas_call(
        _inproj_kernel,
        grid=(N_TILES,),
        in_specs=_x_specs() + [
            pl.BlockSpec((1, 6, D_MODEL), lambda i: (_tile_mod_row(i), 0, 0)),
            pl.BlockSpec((1, D_MODEL), const),
            pl.BlockSpec((TOK_TILE, HEAD_DIM), rope_idx),
            pl.BlockSpec((TOK_TILE, HEAD_DIM), rope_idx),
            pl.BlockSpec((D_MODEL, 4 * DN_WIDTH), const),
            pl.BlockSpec((D_MODEL, ATT_Q + 2 * ATT_KV), const),
            pl.BlockSpec((D_MODEL, 128), const),
        ],
        out_specs=[
            pl.BlockSpec((TOK_TILE, 4 * DN_WIDTH), row),
            pl.BlockSpec((TOK_TILE, ATT_Q), row),
            pl.BlockSpec((TOK_TILE, ATT_KV), row),
            pl.BlockSpec((TOK_TILE, ATT_KV), row),
            pl.BlockSpec((TOK_TILE, 128), row),
        ],
        out_shape=[
            jax.ShapeDtypeStruct((M_ALL, 4 * DN_WIDTH), _BF16),
            jax.ShapeDtypeStruct((M_ALL, ATT_Q), _BF16),
            jax.ShapeDtypeStruct((M_ALL, ATT_KV), _F32),
            jax.ShapeDtypeStruct((M_ALL, ATT_KV), _F32),
            jax.ShapeDtypeStruct((M_ALL, 128), _F32),
        ],
        compiler_params=_params(),
        name="input_projection",
    )(xp, xs, mod3, norm1, cos, sin, w_dn, w_att, w_ab)


DN_STACK = 16
DN_LONG_CHUNKS = 8


def _dn_pass_heads(n_chunk):
    return DN_HEADS if n_chunk <= DN_LONG_CHUNKS else DN_HEADS // 2


def _dn_stacking(n_chunk):
    group = min(n_chunk, DN_STACK // 2)
    return group, DN_STACK // (2 * group)


def _bdot(a, b):
    return jnp.stack([_dot(a[g], b[g]) for g in range(a.shape[0])])


def _bdot_nt(a, b):
    return jnp.stack([_dot_nt(a[g], b[g]) for g in range(a.shape[0])])


def _inverse_unit_triangular(a, eye):
    p = eye - a
    x = a
    for _ in range(5):
        x = _bdot(x, x)
        p = p + _bdot(p, x)
    return p


def _dn_prepare(q, k, v, gc, gc_row, beta, g_tot, incl, strict, eye):
    decay = jnp.where(incl, jnp.exp(jnp.where(incl, gc - gc_row, 0.0)), 0.0)
    kb = k * beta
    a = jnp.where(strict, _bdot_nt(kb, k) * decay, 0.0)
    t_inv = _inverse_unit_triangular(a, eye)
    eg = jnp.exp(gc)
    u = _bdot(t_inv, v * beta)
    w = _bdot(t_inv, kb * eg)
    qk = _bdot_nt(q, k) * decay
    wq = jnp.concatenate([w, q * eg], axis=1).astype(_BF16)
    kd = k * jnp.exp(g_tot - gc)
    kd_t = jnp.stack([kd[g].T for g in range(kd.shape[0])])
    qkk = jnp.concatenate([qk, kd_t], axis=1).astype(_BF16)
    return u, wq, qkk


def _dn_steps(states, us, wqs, qkks, g_tots):
    wss = [_dot(wq, s) for wq, s in zip(wqs, states)]
    v_news = [u - ws[:DN_CHUNK] for u, ws in zip(us, wss)]
    rs = [_dot(qkk, v_new) for qkk, v_new in zip(qkks, v_news)]
    outs = [ws[DN_CHUNK:] + r[:DN_CHUNK] for ws, r in zip(wss, rs)]
    states = [s * jnp.exp(g) + r[DN_CHUNK:] for s, g, r in zip(states, g_tots, rs)]
    return states, outs


def _dn_kernel(*refs, seq_len, has_s0, emit_state):
    dn_ref, ab_ref, cw_ref, alog_ref, dtb_ref, nw_ref = refs[:6]
    pos = 6
    s0_ref = None
    if has_s0:
        s0_ref = refs[pos]
        pos += 1
    o_ref = refs[pos]
    pos += 1
    st_ref = None
    if emit_state:
        st_ref = refs[pos]
        pos += 1
    pad_scr, q_scr, k_scr, v_scr, gate_scr, grow_scr, o_scr, u_scr, wq_scr, qkk_scr, s_scr = refs[pos:]

    T = seq_len
    C = DN_CHUNK
    n_chunk = T // C
    n_pair = n_chunk // 2
    G, stack_heads = _dn_stacking(n_chunk)
    n_group = n_chunk // G

    gates = ab_ref[...]
    lane = lax.broadcasted_iota(jnp.int32, (T, 128), 1)
    tpos = lax.broadcasted_iota(jnp.int32, (T, 128), 0) % C
    sp_arg = gates + dtb_ref[...]
    softplus = jnp.maximum(sp_arg, 0.0) + jnp.log1p(jnp.exp(-jnp.abs(sp_arg)))
    g = -jnp.exp(alog_ref[...]) * softplus
    pre = g
    suf = g
    s = 1
    while s < C:
        pre = pre + jnp.where(tpos >= s, pltpu.roll(pre, s, 0), 0.0)
        suf = suf + jnp.where(tpos < C - s, pltpu.roll(suf, T - s, 0), 0.0)
        s *= 2
    gcum = jnp.where(lane < DN_HEADS, pre, suf)
    gate_scr[...] = jnp.where(lane < 2 * DN_HEADS, gcum, jax.nn.sigmoid(gates))
    gcum_t = gcum.T
    for p in range(n_pair):
        grow_scr[p] = gcum_t[0:8, p * 128:(p + 1) * 128]

    pad_scr[0:8, :] = jnp.zeros((8, HEAD_DIM), _F32)
    pad_scr[8 + T:16 + T, :] = jnp.zeros((8, HEAD_DIM), _F32)

    shape3 = (DN_STACK, C, C)
    ri = lax.broadcasted_iota(jnp.int32, shape3, 1)
    ci = lax.broadcasted_iota(jnp.int32, shape3, 2)
    eye = (ri == ci).astype(_F32)
    is_fwd = (lax.broadcasted_iota(jnp.int32, shape3, 0) // G) % 2 == 0
    offset = jnp.where(is_fwd, ci - ri, ri - ci)
    incl = offset <= 0
    strict = offset < 0

    def conv_silu(part, h):
        c0 = part * DN_WIDTH + h * HEAD_DIM
        pad_scr[8:8 + T, :] = dn_ref[:, c0:c0 + HEAD_DIM].astype(_F32)
        acc = cw_ref[0:1, c0:c0 + HEAD_DIM] * pad_scr[6:6 + T, :]
        for j in range(1, DN_CONV):
            acc = acc + cw_ref[j:j + 1, c0:c0 + HEAD_DIM] * pad_scr[6 + j:6 + j + T, :]
        return _silu(acc)

    def l2n(x):
        return x * lax.rsqrt(jnp.sum(x * x, axis=-1, keepdims=True) + NORM_EPS)

    def total_decay(h, direction, r0):
        col = direction * DN_HEADS + h
        row = r0 + C - 1 if direction == 0 else r0
        return gate_scr[pl.ds(row, 1), col:col + 1]

    def prepare_group(grp, heads, chain0):
        r0 = pl.multiple_of(grp * G * C, G * C)
        sl = pl.ds(r0, G * C)
        stacked = lambda x: x.reshape(G, C, x.shape[-1])
        qs, ks, vs, gcs, betas, rows, g_tots = [], [], [], [], [], [], []
        for hh, h in enumerate(heads):
            for direction in range(2):
                col = direction * DN_HEADS + h
                bcol = 2 * DN_HEADS + col
                gc = stacked(gate_scr[sl, col:col + 1])
                gcs.append(gc)
                betas.append(stacked(gate_scr[sl, bcol:bcol + 1]))
                for pair in range(G // 2):
                    both = grow_scr[grp * (G // 2) + pair][col:col + 1, :]
                    rows += [both[:, :C], both[:, C:]]
                g_tots.append(gc[:, C - 1:C, :] if direction == 0 else gc[:, 0:1, :])
                qs.append(stacked(q_scr[hh, sl, :]))
                ks.append(stacked(k_scr[hh, sl, :]))
                vs.append(stacked(v_scr[hh, sl, :]))
        cat = lambda xs: jnp.concatenate(xs, axis=0)
        u, wq, qkk = _dn_prepare(cat(qs), cat(ks), cat(vs), cat(gcs), jnp.stack(rows), cat(betas), cat(g_tots),
                                 incl, strict, eye)
        for local in range(2 * len(heads)):
            chain = chain0 + local
            part = slice(local * G, (local + 1) * G)
            slot0 = chain * n_chunk + grp * G
            u_scr[chain, sl, :] = u[part].reshape(G * C, HEAD_DIM)
            wq_scr[pl.ds(slot0, G)] = wq[part]
            qkk_scr[pl.ds(slot0, G)] = qkk[part]

    def steps(chain_heads, c):
        slots, rows, g_tots = [], [], []
        for chain, h in enumerate(chain_heads):
            direction = chain % 2
            chunk = c if direction == 0 else n_chunk - 1 - c
            r0 = pl.multiple_of(chunk * C, C)
            slots.append(chain * n_chunk + chunk)
            rows.append(pl.ds(r0, C))
            g_tots.append(total_decay(h, direction, r0))
        chains = range(len(chain_heads))
        states, outs = _dn_steps([s_scr[ch] for ch in chains], [u_scr[ch, rows[ch], :] for ch in chains],
                                 [wq_scr[slots[ch]] for ch in chains], [qkk_scr[slots[ch]] for ch in chains], g_tots)
        for ch in chains:
            s_scr[ch] = states[ch]
            o_scr[ch, rows[ch], :] = outs[ch]

    pass_heads = _dn_pass_heads(n_chunk)
    for h0 in range(0, DN_HEADS, pass_heads):
        heads = range(h0, h0 + pass_heads)
        for h1 in range(h0, h0 + pass_heads, stack_heads):
            stack = range(h1, h1 + stack_heads)
            for hh, h in enumerate(stack):
                q_scr[hh] = l2n(conv_silu(0, h)) * HEAD_DIM ** -0.5
                k_scr[hh] = l2n(conv_silu(1, h))
                v_scr[hh] = conv_silu(2, h)

            def prepare_body(grp, carry, stack=stack, h1=h1):
                prepare_group(grp, stack, (h1 - h0) * 2)
                return carry

            lax.fori_loop(0, n_group, prepare_body, 0)
        for h in heads:
            for direction in range(2):
                chain = (h - h0) * 2 + direction
                if has_s0:
                    s_scr[chain] = s0_ref[0, direction, h].astype(_F32)
                else:
                    s_scr[chain] = jnp.zeros((HEAD_DIM, HEAD_DIM), _F32)

        chain_heads = [h for h in heads for _ in range(2)]

        def step_body(c, carry):
            steps(chain_heads, c)
            return carry

        lax.fori_loop(0, n_chunk, step_body, 0)

        for h in heads:
            chain = (h - h0) * 2
            if emit_state:
                st_ref[0, 0, h] = s_scr[chain]
                st_ref[0, 1, h] = s_scr[chain + 1]
            o = o_scr[chain] + o_scr[chain + 1]
            z = dn_ref[:, 3 * DN_WIDTH + h * HEAD_DIM:3 * DN_WIDTH + (h + 1) * HEAD_DIM].astype(_F32)
            o = o * lax.rsqrt(jnp.mean(o * o, axis=-1, keepdims=True) + NORM_EPS) * nw_ref[...] * _silu(z)
            o_ref[:, h * HEAD_DIM:(h + 1) * HEAD_DIM] = o.astype(_BF16)


def _deltanet(dn, ab, conv_w, a_log, dt_bias, dn_norm, s0, *, n_seq, seq_len, row_block0, emit_state):
    has_s0 = s0 is not None
    const = lambda b: (0, 0)
    state_spec = pl.BlockSpec((1, 2, DN_HEADS, HEAD_DIM, HEAD_DIM), lambda b: (b, 0, 0, 0, 0))
    in_specs = [
        pl.BlockSpec((seq_len, 4 * DN_WIDTH), lambda b: (row_block0 + b, 0)),
        pl.BlockSpec((seq_len, 128), lambda b: (row_block0 + b, 0)),
        pl.BlockSpec((8, 3 * DN_WIDTH), const),
        pl.BlockSpec((1, 128), const),
        pl.BlockSpec((1, 128), const),
        pl.BlockSpec((1, HEAD_DIM), const),
    ]
    args = [dn, ab, conv_w, a_log, dt_bias, dn_norm]
    if has_s0:
        in_specs.append(state_spec)
        args.append(s0)
    out_specs = [pl.BlockSpec((seq_len, DN_WIDTH), lambda b: (b, 0))]
    out_shape = [jax.ShapeDtypeStruct((n_seq * seq_len, DN_WIDTH), _BF16)]
    if emit_state:
        out_specs.append(state_spec)
        out_shape.append(jax.ShapeDtypeStruct((n_seq, 2, DN_HEADS, HEAD_DIM, HEAD_DIM), _F32))
    n_chunk = seq_len // DN_CHUNK
    pass_tile = (_dn_stacking(n_chunk)[1], seq_len, HEAD_DIM)
    n_chain = 2 * _dn_pass_heads(n_chunk)
    return pl.pallas_call(
        functools.partial(_dn_kernel, seq_len=seq_len, has_s0=has_s0, emit_state=emit_state),
        grid=(n_seq,),
        in_specs=in_specs,
        out_specs=out_specs,
        out_shape=out_shape,
        scratch_shapes=[
            pltpu.VMEM((seq_len + 16, HEAD_DIM), _F32),
            pltpu.VMEM(pass_tile, _F32), pltpu.VMEM(pass_tile, _F32), pltpu.VMEM(pass_tile, _F32),
            pltpu.VMEM((seq_len, 128), _F32),
            pltpu.VMEM((seq_len // (2 * DN_CHUNK), 8, 128), _F32),
            pltpu.VMEM((n_chain, seq_len, HEAD_DIM), _F32),
            pltpu.VMEM((n_chain, seq_len, HEAD_DIM), _F32),
            pltpu.VMEM((n_chain * n_chunk, 2 * DN_CHUNK, HEAD_DIM), _BF16),
            pltpu.VMEM((n_chain * n_chunk, DN_CHUNK + HEAD_DIM, DN_CHUNK), _BF16),
            pltpu.VMEM((n_chain, HEAD_DIM, HEAD_DIM), _F32),
        ],
        compiler_params=_params(),
        name="deltanet_ctx" if emit_state else "deltanet_lat",
    )(*args)


def _softmax_av(scores, values, sink):
    m = sink
    for s in scores:
        m = jnp.maximum(m, jnp.max(s, axis=-1, keepdims=True))
    denom = jnp.exp(sink - m)
    acc = None
    for s, v in zip(scores, values):
        p = jnp.exp(s - m)
        denom = denom + jnp.sum(p, axis=-1, keepdims=True)
        pv = _dot(p, v)
        acc = pv if acc is None else acc + pv
    return acc / denom


def _ctx_attn_kernel(sink_ref, q_ref, k_ref, v_ref, o_ref):
    kvh = pl.program_id(1)
    k = k_ref[...]
    v = v_ref[...]
    for g in range(ATT_GROUP):
        q = q_ref[:, g * HEAD_DIM:(g + 1) * HEAD_DIM]
        sink = jnp.full((SEQ, 1), sink_ref[kvh * ATT_GROUP + g], _F32)
        o = _softmax_av([_dot_nt(q, k)], [v], sink)
        o_ref[:, g * HEAD_DIM:(g + 1) * HEAD_DIM] = o.astype(_BF16)


def _context_attention(sinks, aq, ak, av):
    return pl.pallas_call(
        _ctx_attn_kernel,
        grid=(BATCH, ATT_KV_HEADS),
        in_specs=[
            pl.BlockSpec(memory_space=pltpu.SMEM),
            pl.BlockSpec((SEQ, ATT_GROUP * HEAD_DIM), lambda b, h: (b, h)),
            pl.BlockSpec((SEQ, HEAD_DIM), lambda b, h: (b, h)),
            pl.BlockSpec((SEQ, HEAD_DIM), lambda b, h: (b, h)),
        ],
        out_specs=pl.BlockSpec((SEQ, ATT_GROUP * HEAD_DIM), lambda b, h: (b, h)),
        out_shape=jax.ShapeDtypeStruct((M_CTX, ATT_Q), _BF16),
        compiler_params=_params(2),
        name="context_attention",
    )(sinks, aq, ak, av)


LAT_Q_BLOCK = 2 * ATT_BLOCK


def _lat_attn_kernel(sink_ref, q_ref, kp_ref, kc_ref, kn_ref, vp_ref, vc_ref, vn_ref, ck_ref, cv_ref, o_ref):
    kvh = pl.program_id(1)
    i = pl.program_id(2)
    nb = pl.num_programs(2)
    B, Q = ATT_BLOCK, LAT_Q_BLOCK
    rows = ATT_GROUP * Q
    q = jnp.concatenate([q_ref[:, g * HEAD_DIM:(g + 1) * HEAD_DIM] for g in range(ATT_GROUP)], axis=0)
    r_b = lax.broadcasted_iota(jnp.int32, (rows, B), 0) % Q
    c_b = lax.broadcasted_iota(jnp.int32, (rows, B), 1)
    r_q = lax.broadcasted_iota(jnp.int32, (rows, Q), 0) % Q
    c_q = lax.broadcasted_iota(jnp.int32, (rows, Q), 1)
    s_prev = jnp.where(c_b >= r_b + jnp.where(i > 0, 0, Q), _dot_nt(q, kp_ref[...]), NEG_INF)
    s_cur = jnp.where(jnp.abs(r_q - c_q) <= B, _dot_nt(q, kc_ref[...]), NEG_INF)
    s_next = jnp.where(c_b <= r_b - B - jnp.where(i < nb - 1, 0, Q), _dot_nt(q, kn_ref[...]), NEG_INF)
    s_ctx = _dot_nt(q, ck_ref[...])
    head = lax.broadcasted_iota(jnp.int32, (rows, 1), 0) // Q
    sink = jnp.zeros((rows, 1), _F32)
    for g in range(ATT_GROUP):
        sink = jnp.where(head == g, sink_ref[kvh * ATT_GROUP + g], sink)
    o = _softmax_av([s_prev, s_cur, s_next, s_ctx], [vp_ref[...], vc_ref[...], vn_ref[...], cv_ref[...]], sink)
    for g in range(ATT_GROUP):
        o_ref[:, g * HEAD_DIM:(g + 1) * HEAD_DIM] = o[g * Q:(g + 1) * Q].astype(_BF16)


def _latent_attention(sinks, aq, ak, av, ctx_k, ctx_v):
    nq = DEC_SEQ // LAT_Q_BLOCK
    nb = DEC_SEQ // ATT_BLOCK
    q0 = M_CTX // LAT_Q_BLOCK
    b0 = M_CTX // ATT_BLOCK
    cur = lambda b, h, i: (q0 + b * nq + i, h)
    prev = lambda b, h, i: (b0 + b * nb + jnp.maximum(2 * i - 1, 0), h)
    nxt = lambda b, h, i: (b0 + b * nb + jnp.minimum(2 * i + 2, nb - 1), h)
    side_spec = lambda f: pl.BlockSpec((ATT_BLOCK, HEAD_DIM), f)
    cur_spec = pl.BlockSpec((LAT_Q_BLOCK, HEAD_DIM), cur)
    ctx_spec = pl.BlockSpec((PAST_LEN, HEAD_DIM), lambda b, h, i: (b, h))
    return pl.pallas_call(
        _lat_attn_kernel,
        grid=(DEC_BATCH, ATT_KV_HEADS, nq),
        in_specs=[
            pl.BlockSpec(memory_space=pltpu.SMEM),
            pl.BlockSpec((LAT_Q_BLOCK, ATT_GROUP * HEAD_DIM), cur),
            side_spec(prev), cur_spec, side_spec(nxt),
            side_spec(prev), cur_spec, side_spec(nxt),
            ctx_spec, ctx_spec,
        ],
        out_specs=pl.BlockSpec((LAT_Q_BLOCK, ATT_GROUP * HEAD_DIM), lambda b, h, i: (b * nq + i, h)),
        out_shape=jax.ShapeDtypeStruct((M_LAT, ATT_Q), _BF16),
        compiler_params=_params(3),
        name="latent_attention",
    )(sinks, aq, ak, ak, ak, av, av, av, ctx_k, ctx_v)


def _outproj_kernel(xp_ref, xs_ref, dnc_ref, dnl_ref, atc_ref, atl_ref, mod_ref, n2_ref, wo_dn_ref, wo_at_ref,
                    rw_ref, x1_ref, h2_ref, h2p_ref, lg_ref):
    x = jnp.where(_ctx_tile_mask((TOK_TILE, D_MODEL)), xp_ref[...], xs_ref[...])
    half_mask = _ctx_tile_mask((TOK_TILE, DN_WIDTH))
    dn = jnp.where(half_mask, dnc_ref[...], dnl_ref[...])
    at = jnp.where(half_mask, atc_ref[...], atl_ref[...])
    mixed = (jnp.dot(dn, wo_dn_ref[...], preferred_element_type=_F32)
             + jnp.dot(at, wo_at_ref[...], preferred_element_type=_F32))
    x1 = x + mod_ref[0, 2:3, :] * mixed
    x1_ref[...] = x1
    h = _rms(x1, n2_ref[...]) * (1 + mod_ref[0, 4:5, :]) + mod_ref[0, 3:4, :]
    h2_ref[...] = h.astype(_BF16)
    _store_token_tiles(h2p_ref, h, TOK_TILE)
    w = rw_ref[...]
    w_hi = w.astype(_BF16)
    w_lo = (w - w_hi.astype(_F32)).astype(_BF16)
    h_hi = h.astype(_BF16)
    h_lo = (h - h_hi.astype(_F32)).astype(_BF16)
    lg_ref[...] = (_dot_nt(w_hi, h_hi) + _dot_nt(w_hi, h_lo)) + _dot_nt(w_lo, h_hi)


def _output_projection(xp, xs, dn_c, dn_l, at_c, at_l, mod3, norm2, wo_dn, wo_at, router_wt):
    const = lambda i: (0, 0)
    row = lambda i: (i, 0)
    ctx_row = lambda i: (jnp.minimum(i, N_CTX_TILES - 1), 0)
    lat_row = lambda i: (jnp.maximum(i - N_CTX_TILES, 0), 0)
    half = (TOK_TILE, DN_WIDTH)
    return pl.pallas_call(
        _outproj_kernel,
        grid=(N_TILES,),
        in_specs=_x_specs() + [
            pl.BlockSpec(half, ctx_row), pl.BlockSpec(half, lat_row),
            pl.BlockSpec(half, ctx_row), pl.BlockSpec(half, lat_row),
            pl.BlockSpec((1, 6, D_MODEL), lambda i: (_tile_mod_row(i), 0, 0)),
            pl.BlockSpec((1, D_MODEL), const),
            pl.BlockSpec((DN_WIDTH, D_MODEL), const),
            pl.BlockSpec((ATT_Q, D_MODEL), const),
            pl.BlockSpec((N_EXPERTS, D_MODEL), const),
        ],
        out_specs=[
            pl.BlockSpec((TOK_TILE, D_MODEL), row),
            pl.BlockSpec((TOK_TILE, D_MODEL), row),
            pl.BlockSpec((TOK_TILE * TOKEN_TILE_ROWS, 128), row),
            pl.BlockSpec((N_EXPERTS, TOK_TILE), lambda i: (0, i)),
        ],
        out_shape=[
            jax.ShapeDtypeStruct((M_ALL, D_MODEL), _F32),
            jax.ShapeDtypeStruct((M_ALL, D_MODEL), _BF16),
            jax.ShapeDtypeStruct((M_ALL * TOKEN_TILE_ROWS, 128), _F32),
            jax.ShapeDtypeStruct((N_EXPERTS, M_ALL), _F32),
        ],
        compiler_params=_params(),
        name="output_projection",
    )(xp, xs, dn_c, dn_l, at_c, at_l, mod3, norm2, wo_dn, wo_at, router_wt)


def _first_index_of(values, target, index, limit):
    return jnp.min(jnp.where(values == target, index, limit), axis=0, keepdims=True)


def _route_kernel(lg_ref, bias_ref, idx_ref, rank_ref, gate_ref, cnt_ref, carry_scr):
    i = pl.program_id(0)
    tm = TOK_TILE

    @pl.when(i == 0)
    def _():
        carry_scr[...] = jnp.zeros_like(carry_scr)

    scores = jax.nn.sigmoid(lg_ref[...])
    biased = scores + bias_ref[...]
    row = lax.broadcasted_iota(jnp.int32, (N_EXPERTS, tm), 0).astype(_F32)
    grow = lax.broadcasted_iota(jnp.int32, (GROUP_SIZE, tm), 0).astype(_F32)

    group_rows = []
    for g in range(N_GROUPS):
        blk = biased[g * GROUP_SIZE:(g + 1) * GROUP_SIZE]
        m1 = jnp.max(blk, axis=0, keepdims=True)
        i1 = _first_index_of(blk, m1, grow, GROUP_SIZE)
        m2 = jnp.max(jnp.where(grow == i1, -jnp.inf, blk), axis=0, keepdims=True)
        group_rows.append(m1 + m2)
    gs = jnp.concatenate(group_rows, axis=0)
    gi = lax.broadcasted_iota(jnp.int32, (N_GROUPS, tm), 0).astype(_F32)
    gself = jnp.zeros((N_GROUPS, tm), _F32)
    for _ in range(TOPK_GROUPS):
        hit = gi == _first_index_of(gs, jnp.max(gs, axis=0, keepdims=True), gi, N_GROUPS)
        gself = jnp.where(hit, 1.0, gself)
        gs = jnp.where(hit, -jnp.inf, gs)
    emask = jnp.concatenate(
        [jnp.broadcast_to(gself[g:g + 1], (GROUP_SIZE, tm)) for g in range(N_GROUPS)], axis=0) > 0.5
    masked = jnp.where(emask, biased, NEG_INF)

    selected = jnp.zeros((N_EXPERTS, tm), _F32)
    idxs, gates = [], []
    for _ in range(TOP_K):
        ei = _first_index_of(masked, jnp.max(masked, axis=0, keepdims=True), row, N_EXPERTS)
        hit = row == ei
        idxs.append(ei)
        gates.append(jnp.sum(jnp.where(hit, scores, 0.0), axis=0, keepdims=True))
        masked = jnp.where(hit, -jnp.inf, masked)
        selected = jnp.where(hit, 1.0, selected)
    gsum = gates[0]
    for gk in gates[1:]:
        gsum = gsum + gk
    gates = [gk / gsum * ROUTED_SCALE for gk in gates]

    si = lax.broadcasted_iota(jnp.int32, (tm, tm), 0)
    ti = lax.broadcasted_iota(jnp.int32, (tm, tm), 1)
    earlier = (si < ti).astype(_BF16)
    ranks_all = _dot(selected, earlier) + carry_scr[...]
    ranks = [jnp.sum(jnp.where(row == ei, ranks_all, 0.0), axis=0, keepdims=True) for ei in idxs]
    carry_scr[...] = carry_scr[...] + jnp.sum(selected, axis=1, keepdims=True)

    idx_ref[...] = jnp.concatenate(idxs, axis=0).astype(jnp.int32)
    rank_ref[...] = jnp.concatenate(ranks, axis=0).astype(jnp.int32)
    gate_rows = jnp.concatenate(gates + [jnp.zeros((128 - TOP_K, tm), _F32)], axis=0)
    gate_ref[...] = gate_rows.T
    cnt_ref[...] = jnp.broadcast_to(carry_scr[...], (N_EXPERTS, 128))


def _route(logits_t, bias_col):
    return pl.pallas_call(
        _route_kernel,
        grid=(N_TILES,),
        in_specs=[pl.BlockSpec((N_EXPERTS, TOK_TILE), lambda i: (0, i)),
                  pl.BlockSpec((N_EXPERTS, 1), lambda i: (0, 0))],
        out_specs=[
            pl.BlockSpec((TOP_K, TOK_TILE), lambda i: (0, i)),
            pl.BlockSpec((TOP_K, TOK_TILE), lambda i: (0, i)),
            pl.BlockSpec((TOK_TILE, 128), lambda i: (i, 0)),
            pl.BlockSpec((N_EXPERTS, 128), lambda i: (0, 0)),
        ],
        out_shape=[
            jax.ShapeDtypeStruct((TOP_K, M_ALL), jnp.int32),
            jax.ShapeDtypeStruct((TOP_K, M_ALL), jnp.int32),
            jax.ShapeDtypeStruct((M_ALL, 128), _F32),
            jax.ShapeDtypeStruct((N_EXPERTS, 128), _F32),
        ],
        scratch_shapes=[pltpu.VMEM((N_EXPERTS, 1), _F32)],
        compiler_params=_params(),
        name="route",
    )(logits_t, bias_col)


def _slot_kernel(idx_ref, rank_ref, start_ref, pos_ref):
    row = lax.broadcasted_iota(jnp.int32, (N_EXPERTS, TOK_TILE), 0)
    start = start_ref[...]
    rows = []
    for k in range(TOP_K):
        base = jnp.sum(jnp.where(row == idx_ref[k:k + 1, :], start, 0.0), axis=0, keepdims=True)
        rows.append(base.astype(jnp.int32) + rank_ref[k:k + 1, :])
    pos_ref[...] = jnp.concatenate(rows, axis=0)


def _slots(idx, rank, start_col):
    spec = pl.BlockSpec((TOP_K, TOK_TILE), lambda i: (0, i))
    return pl.pallas_call(
        _slot_kernel,
        grid=(N_TILES,),
        in_specs=[spec, spec, pl.BlockSpec((N_EXPERTS, 1), lambda i: (0, 0))],
        out_specs=spec,
        out_shape=jax.ShapeDtypeStruct((TOP_K, M_ALL), jnp.int32),
        compiler_params=_params(),
        name="slots",
    )(idx, rank, start_col)


def _slot_tile(ref, slot):
    return ref.at[pl.ds(pl.multiple_of(slot * TOKEN_TILE_ROWS, TOKEN_TILE_ROWS), TOKEN_TILE_ROWS)]


def _dispatch_kernel(pad_lo_ref, pad_hi_ref, nused_ref, pos_ref, h_hbm, xs_ref, zero_scr, h_buf, in_sem, out_sem,
                     zero_sem):
    i = pl.program_id(0)
    n_steps = pl.num_programs(0)
    block_rows = TOK_TILE * TOKEN_TILE_ROWS

    def fetch(step):
        src = h_hbm.at[pl.ds(pl.multiple_of(step * block_rows, block_rows), block_rows)]
        return pltpu.make_async_copy(src, h_buf.at[step % 3], in_sem.at[step % 3])

    def wait_copies(step):
        for _ in range(TOP_K):
            pltpu.make_async_copy(h_buf.at[0], xs_ref.at[pl.ds(0, block_rows)], out_sem.at[step % 2]).wait()

    @pl.when(i == 0)
    def _():
        fetch(i).start()

    @pl.when(i + 1 < n_steps)
    def _():
        fetch(i + 1).start()

    def zero_fill(start):
        def zero_copy(first_slot, n_slots):
            rows = n_slots * TOKEN_TILE_ROWS
            dst = xs_ref.at[pl.ds(pl.multiple_of(first_slot * TOKEN_TILE_ROWS, TOKEN_TILE_ROWS), rows)]
            return pltpu.make_async_copy(zero_scr.at[pl.ds(0, rows)], dst, zero_sem)

        def expert_padding(e, carry):
            first = pad_lo_ref[e]
            n = pad_hi_ref[e] - first
            size = EXPERT_ROWS // 2
            while size >= 1:
                chunk = n & size

                @pl.when(chunk != 0)
                def _(first=first, size=size):
                    copy = zero_copy(first, size)
                    copy.start() if start else copy.wait()

                first = first + chunk
                size //= 2
            return carry

        def tail_block(t, carry):
            j = N_BLOCKS - 1 - t

            @pl.when(j >= nused_ref[0])
            def _():
                copy = zero_copy(j * EXPERT_ROWS, EXPERT_ROWS)
                copy.start() if start else copy.wait()

            return carry

        lax.fori_loop(0, N_EXPERTS, expert_padding, 0)
        lax.fori_loop(0, N_BLOCKS - N_PAIRS // EXPERT_ROWS, tail_block, 0)

    @pl.when(i == 0)
    def _():
        zero_scr[...] = jnp.zeros_like(zero_scr)
        zero_fill(start=True)

    fetch(i).wait()
    h_tile = h_buf.at[i % 3]

    def body(t, carry):
        src = _slot_tile(h_tile, t)
        for k in range(TOP_K):
            pltpu.make_async_copy(src, _slot_tile(xs_ref, pos_ref[0, k, t]), out_sem.at[i % 2]).start(
                priority=k % 2)
        return carry

    lax.fori_loop(0, TOK_TILE, body, 0, unroll=4)

    @pl.when(i >= 1)
    def _():
        wait_copies(i - 1)

    @pl.when(i == n_steps - 1)
    def _():
        wait_copies(i)
        zero_fill(start=False)


def _dispatch(pad_lo, pad_hi, n_used, pos3, h2p):
    grid_spec = pltpu.PrefetchScalarGridSpec(
        num_scalar_prefetch=3,
        grid=(N_TILES,),
        in_specs=[pl.BlockSpec((1, TOP_K, TOK_TILE), lambda i, *_: (i, 0, 0), memory_space=pltpu.SMEM),
                  pl.BlockSpec(memory_space=pl.ANY)],
        out_specs=pl.BlockSpec(memory_space=pl.ANY),
        scratch_shapes=[pltpu.VMEM((EXPERT_ROWS * TOKEN_TILE_ROWS, 128), _F32),
                        pltpu.VMEM((3, TOK_TILE * TOKEN_TILE_ROWS, 128), _F32),
                        pltpu.SemaphoreType.DMA((3,)), pltpu.SemaphoreType.DMA((2,)),
                        pltpu.SemaphoreType.DMA(())],
    )
    return pl.pallas_call(
        _dispatch_kernel,
        grid_spec=grid_spec,
        out_shape=jax.ShapeDtypeStruct((N_SLOTS * TOKEN_TILE_ROWS, 128), _F32),
        compiler_params=_params(),
        name="dispatch",
    )(pad_lo, pad_hi, n_used, pos3, h2p)


def _expert_kernel(be_ref, nused_ref, first_ref, next_ref, parity_ref, xs_ref, w1_hbm, w3_hbm, w2_hbm, ys_ref,
                   w1_buf, w3_buf, w2_buf, w1_scr, w3_scr, w2_scr, sem):
    j = pl.program_id(0)

    def weight_copies(expert, slot):
        return [pltpu.make_async_copy(hbm.at[expert], buf.at[slot], sem.at[slot])
                for hbm, buf in ((w1_hbm, w1_buf), (w3_hbm, w3_buf), (w2_hbm, w2_buf))]

    @pl.when(j < nused_ref[0])
    def _():
        @pl.when(j == 0)
        def _():
            for copy in weight_copies(be_ref[0], 0):
                copy.start()

        @pl.when(first_ref[j] == 1)
        def _():
            slot = parity_ref[j]
            for copy in weight_copies(be_ref[j], slot):
                copy.wait()

            @pl.when(next_ref[j] >= 0)
            def _():
                for copy in weight_copies(next_ref[j], 1 - slot):
                    copy.start()

            w1_scr[...] = w1_buf[slot].astype(_BF16)
            w3_scr[...] = w3_buf[slot].astype(_BF16)
            w2_scr[...] = w2_buf[slot].astype(_BF16)

        x = _load_token_tiles(xs_ref, EXPERT_ROWS).astype(_BF16)
        h1 = jnp.dot(x, w1_scr[...], preferred_element_type=_F32)
        h3 = jnp.dot(x, w3_scr[...], preferred_element_type=_F32)
        act = (_silu(h1) * h3).astype(_BF16)
        _store_token_tiles(ys_ref, jnp.dot(act, w2_scr[...], preferred_element_type=_F32), EXPERT_ROWS)

    @pl.when(j >= nused_ref[0])
    def _():
        ys_ref[...] = jnp.zeros_like(ys_ref)


def _experts(block_expert, n_used, run_first, run_next, run_parity, xs, w1, w3, w2):
    def blk(j, be, nu, *_):
        return (jnp.minimum(j, nu[0] - 1), 0)

    up, down = (D_MODEL, EXPERT_FF), (EXPERT_FF, D_MODEL)
    grid_spec = pltpu.PrefetchScalarGridSpec(
        num_scalar_prefetch=5,
        grid=(N_BLOCKS,),
        in_specs=[
            pl.BlockSpec((EXPERT_ROWS * TOKEN_TILE_ROWS, 128), blk),
            pl.BlockSpec(memory_space=pl.ANY),
            pl.BlockSpec(memory_space=pl.ANY),
            pl.BlockSpec(memory_space=pl.ANY),
        ],
        out_specs=pl.BlockSpec((EXPERT_ROWS * TOKEN_TILE_ROWS, 128), lambda j, *_: (j, 0)),
        scratch_shapes=[pltpu.VMEM((2,) + up, _F32), pltpu.VMEM((2,) + up, _F32), pltpu.VMEM((2,) + down, _F32),
                        pltpu.VMEM(up, _BF16), pltpu.VMEM(up, _BF16), pltpu.VMEM(down, _BF16),
                        pltpu.SemaphoreType.DMA((2,))],
    )
    return pl.pallas_call(
        _expert_kernel,
        grid_spec=grid_spec,
        out_shape=jax.ShapeDtypeStruct((N_SLOTS * TOKEN_TILE_ROWS, 128), _F32),
        compiler_params=_params(),
        name="experts",
    )(block_expert, n_used, run_first, run_next, run_parity, xs, w1, w3, w2)


def _combine_kernel(pos_ref, pos_next_ref, gate_ref, h2_ref, x1_ref, mod_ref, sw1_ref, sw3_ref, sw2_ref, fn_ref,
                    ys_ref, y_ref, buf, sem):
    tm = COMBINE_TILE
    i = pl.program_id(0)
    slot = i % 2

    def gather(p_ref, dst_slot):
        def body(t, carry):
            for k in range(TOP_K):
                pltpu.make_async_copy(_slot_tile(ys_ref, p_ref[0, k, t]), _slot_tile(buf.at[dst_slot, k], t),
                                      sem.at[dst_slot]).start(priority=k % 2)
            return carry

        lax.fori_loop(0, tm, body, 0, unroll=4)

    @pl.when(i == 0)
    def _():
        gather(pos_ref, 0)

    @pl.when(i + 1 < pl.num_programs(0))
    def _():
        gather(pos_next_ref, 1 - slot)

    hb = h2_ref[...]
    act = _silu(jnp.dot(hb, sw1_ref[...], preferred_element_type=_F32)) * jnp.dot(
        hb, sw3_ref[...], preferred_element_type=_F32)
    shared = _dot(act, sw2_ref[...])
    for k in range(TOP_K):
        pltpu.make_async_copy(ys_ref.at[pl.ds(0, tm * TOKEN_TILE_ROWS)], buf.at[slot, k], sem.at[slot]).wait()
    gates = gate_ref[...]
    routed = gates[:, 0:1] * _load_token_tiles(buf.at[slot, 0], tm)
    for k in range(1, TOP_K):
        routed = routed + gates[:, k:k + 1] * _load_token_tiles(buf.at[slot, k], tm)
    y = x1_ref[...] + mod_ref[0, 5:6, :] * (routed + shared)
    y_ref[...] = _rms(y, fn_ref[...])


def _combine(pos3, gate_t, h2, x1, mod3, sw1, sw3, sw2, final_norm, ys, *, n_rows, tile0, mod_row):
    tm = COMBINE_TILE
    const = lambda i: (0, 0)
    row = lambda i: (tile0 + i, 0)
    n_tiles = n_rows // tm
    return pl.pallas_call(
        _combine_kernel,
        grid=(n_tiles,),
        in_specs=[
            pl.BlockSpec((1, TOP_K, tm), lambda i: (tile0 + i, 0, 0), memory_space=pltpu.SMEM),
            pl.BlockSpec((1, TOP_K, tm), lambda i: (tile0 + jnp.minimum(i + 1, n_tiles - 1), 0, 0),
                         memory_space=pltpu.SMEM),
            pl.BlockSpec((tm, 128), row),
            pl.BlockSpec((tm, D_MODEL), row),
            pl.BlockSpec((tm, D_MODEL), row),
            pl.BlockSpec((1, 6, D_MODEL), lambda i: (mod_row(i), 0, 0)),
            pl.BlockSpec((D_MODEL, EXPERT_FF), const),
            pl.BlockSpec((D_MODEL, EXPERT_FF), const),
            pl.BlockSpec((EXPERT_FF, D_MODEL), const),
            pl.BlockSpec((1, D_MODEL), const),
            pl.BlockSpec(memory_space=pl.ANY),
        ],
        out_specs=pl.BlockSpec((tm, D_MODEL), lambda i: (i, 0)),
        out_shape=jax.ShapeDtypeStruct((n_rows, D_MODEL), _F32),
        scratch_shapes=[pltpu.VMEM((2, TOP_K, tm * TOKEN_TILE_ROWS, 128), _F32), pltpu.SemaphoreType.DMA((2,))],
        compiler_params=_params(),
        name="combine",
    )(pos3, pos3, gate_t, h2, x1, mod3, sw1, sw3, sw2, final_norm, ys)


def _tile_major(a, tile):
    return a.reshape(TOP_K, -1, tile).transpose(1, 0, 2)


def kernel(x_prompt, x_sample, c, cache_k, cache_v, state_dn, c_ctx, w_ada, b_ada, norm1, norm2, w_in, dn_conv,
           dn_A_log, dn_dt_bias, dn_norm, attn_sinks, w_out, router_w, router_bias, expert_w1, expert_w3,
           expert_w2, shared_w1, shared_w3, shared_w2, final_norm):
    xp = x_prompt.reshape(M_CTX, D_MODEL)
    xs = x_sample.reshape(M_LAT, D_MODEL)

    cvec = jnp.concatenate([c_ctx[None, :], c, jnp.zeros((N_MOD - 1 - DEC_BATCH, D_MODEL), _F32)], axis=0)
    mod3 = _modulation(cvec, w_ada[0], b_ada).reshape(N_MOD, 6, D_MODEL)

    w = w_in[0]
    n_dn = 4 * DN_WIDTH
    w_dn = w[:, :n_dn].astype(_BF16)
    w_ab = jnp.pad(w[:, n_dn:n_dn + 4 * DN_HEADS], ((0, 0), (0, 128 - 4 * DN_HEADS))).astype(_BF16)
    w_att = w[:, n_dn + 4 * DN_HEADS:].astype(_BF16)
    cos, sin = _rope_tables()
    dn, aq, ak, av, ab = _input_projection(xp, xs, mod3, norm1, cos, sin, w_dn, w_att, w_ab)

    conv_w = jnp.pad(dn_conv[0], ((0, 8 - DN_CONV), (0, 0)))
    pad8 = lambda v: jnp.pad(v.reshape(1, 2 * DN_HEADS), ((0, 0), (0, 128 - 2 * DN_HEADS)))
    a_log = pad8(dn_A_log[0])
    dt_bias = pad8(dn_dt_bias[0])
    dn_c, new_state = _deltanet(dn, ab, conv_w, a_log, dt_bias, dn_norm, None,
                                n_seq=BATCH, seq_len=SEQ, row_block0=0, emit_state=True)
    (dn_l,) = _deltanet(dn, ab, conv_w, a_log, dt_bias, dn_norm, state_dn[:, 0],
                        n_seq=DEC_BATCH, seq_len=DEC_SEQ, row_block0=M_CTX // DEC_SEQ, emit_state=False)

    sinks = attn_sinks[0]
    at_c = _context_attention(sinks, aq, ak, av)
    ctx_k = cache_k[:, 0].reshape(DEC_BATCH * PAST_LEN, ATT_KV)
    ctx_v = cache_v[:, 0].reshape(DEC_BATCH * PAST_LEN, ATT_KV)
    at_l = _latent_attention(sinks, aq, ak, av, ctx_k, ctx_v)

    wo = w_out[0].astype(_BF16)
    x1, h2, h2p, logits_t = _output_projection(xp, xs, dn_c, dn_l, at_c, at_l, mod3, norm2,
                                               wo[:DN_WIDTH], wo[DN_WIDTH:], router_w[0].T)

    idx, rank, gate_t, counts = _route(logits_t, router_bias[0].reshape(N_EXPERTS, 1))
    cnt = counts[:, 0].astype(jnp.int32)
    padded = (cnt + EXPERT_ROWS - 1) // EXPERT_ROWS * EXPERT_ROWS
    pad_end = jnp.cumsum(padded)
    pos = _slots(idx, rank, (pad_end - padded).astype(_F32).reshape(N_EXPERTS, 1))
    n_used = (pad_end[-1] // EXPERT_ROWS).astype(jnp.int32).reshape(1)
    block_start = jnp.arange(N_BLOCKS, dtype=jnp.int32) * EXPERT_ROWS
    block_expert = jnp.minimum(
        jnp.sum((pad_end[None, :] <= block_start[:, None]).astype(jnp.int32), axis=1), N_EXPERTS - 1)

    pad_lo = (pad_end - padded + cnt).astype(jnp.int32)
    x_sorted = _dispatch(pad_lo, pad_end.astype(jnp.int32), n_used, _tile_major(pos, TOK_TILE), h2p)
    run_first = jnp.concatenate([jnp.ones((1,), jnp.int32),
                                 (block_expert[1:] != block_expert[:-1]).astype(jnp.int32)])
    run_parity = (jnp.cumsum(run_first) - 1) % 2
    experts = jnp.arange(N_EXPERTS, dtype=jnp.int32)
    later = (experts[None, :] > experts[:, None]) & (cnt[None, :] > 0)
    next_expert = jnp.min(jnp.where(later, experts[None, :], N_EXPERTS), axis=1)
    next_expert = jnp.where(next_expert == N_EXPERTS, -1, next_expert)
    run_next = jnp.sum(jnp.where(block_expert[:, None] == experts[None, :], next_expert[None, :], 0), axis=1)
    ys = _experts(block_expert, n_used, run_first, run_next.astype(jnp.int32), run_parity.astype(jnp.int32),
                  x_sorted, expert_w1[0], expert_w3[0], expert_w2[0])

    pos_c = _tile_major(pos, COMBINE_TILE)
    sw1, sw3, sw2 = shared_w1[0].astype(_BF16), shared_w3[0].astype(_BF16), shared_w2[0].astype(_BF16)
    fn = final_norm.reshape(1, D_MODEL)
    y_prompt = _combine(pos_c, gate_t, h2, x1, mod3, sw1, sw3, sw2, fn, ys,
                        n_rows=M_CTX, tile0=0, mod_row=lambda i: 0)
    lat_tiles = DEC_SEQ // COMBINE_TILE
    y_sample = _combine(pos_c, gate_t, h2, x1, mod3, sw1, sw3, sw2, fn, ys,
                        n_rows=M_LAT, tile0=M_CTX // COMBINE_TILE, mod_row=lambda i: 1 + i // lat_tiles)

    new_cache_k = ak[:M_CTX].reshape(BATCH, 1, SEQ, ATT_KV_HEADS, HEAD_DIM)
    new_cache_v = av[:M_CTX].reshape(BATCH, 1, SEQ, ATT_KV_HEADS, HEAD_DIM)
    return (y_prompt.reshape(BATCH, SEQ, D_MODEL), y_sample.reshape(DEC_BATCH, DEC_SEQ, D_MODEL),
            new_cache_k, new_cache_v, new_state.reshape(BATCH, 1, 2, DN_HEADS, HEAD_DIM, HEAD_DIM))
```

```python
import functools

import jax
import jax.numpy as jnp
import numpy as np
from jax import lax
from jax.experimental import pallas as pl
from jax.experimental.pallas import tpu as pltpu

D_MODEL = 1024
BATCH = 32
SEQ = 256
DEC_BATCH = 8
DEC_SEQ = 2048
PAST_LEN = 512
GRID_W = 64
HEAD_DIM = 128
DN_HEADS = 4
DN_WIDTH = DN_HEADS * HEAD_DIM
DN_CONV = 5
DN_CHUNK = 64
ATT_HEADS = 4
ATT_KV_HEADS = 2
ATT_GROUP = ATT_HEADS // ATT_KV_HEADS
ATT_Q = ATT_HEADS * HEAD_DIM
ATT_KV = ATT_KV_HEADS * HEAD_DIM
ATT_BLOCK = 128
ROPE_THETA = 10000.0
N_EXPERTS = 256
TOP_K = 8
N_GROUPS = 8
TOPK_GROUPS = 4
GROUP_SIZE = N_EXPERTS // N_GROUPS
EXPERT_FF = D_MODEL // 4
ROUTED_SCALE = 2.5
NORM_EPS = 1e-6
NEG_INF = -1e30

M_CTX = BATCH * SEQ
M_LAT = DEC_BATCH * DEC_SEQ
M_ALL = M_CTX + M_LAT
N_MOD = 16
TOK_TILE = 512
N_CTX_TILES = M_CTX // TOK_TILE
N_TILES = M_ALL // TOK_TILE
EXPERT_ROWS = 256
N_PAIRS = M_ALL * TOP_K
N_BLOCKS = (N_PAIRS + N_EXPERTS * (EXPERT_ROWS - 1)) // EXPERT_ROWS
N_SLOTS = N_BLOCKS * EXPERT_ROWS
COMBINE_TILE = 256
V7X_VMEM_BYTES = 64 * 1024 * 1024
VMEM_LIMIT = V7X_VMEM_BYTES - 8 * 1024 * 1024

_BF16 = jnp.bfloat16
_F32 = jnp.float32


def _dot(a, b):
    return jnp.dot(a.astype(_BF16), b.astype(_BF16), preferred_element_type=_F32)


def _dot_nt(a, b):
    return lax.dot_general(a.astype(_BF16), b.astype(_BF16), (((1,), (1,)), ((), ())),
                           preferred_element_type=_F32)


def _dot_tn(a, b):
    return lax.dot_general(a.astype(_BF16), b.astype(_BF16), (((0,), (0,)), ((), ())),
                           preferred_element_type=_F32)


def _silu(x):
    return x * jax.nn.sigmoid(x)


def _rms(x, w):
    return x * lax.rsqrt(jnp.mean(x * x, axis=-1, keepdims=True) + NORM_EPS) * w


def _params(n_axes=1):
    return pltpu.CompilerParams(dimension_semantics=("arbitrary",) * n_axes, vmem_limit_bytes=VMEM_LIMIT)


def _tile_mod_row(i):
    return jnp.where(i < N_CTX_TILES, 0, 1 + (i - N_CTX_TILES) // (DEC_SEQ // TOK_TILE))


def _ctx_tile_mask(shape):
    limit = jnp.where(pl.program_id(0) < N_CTX_TILES, shape[0], 0)
    return lax.broadcasted_iota(jnp.int32, shape, 0) < limit


TOKEN_TILE_ROWS = D_MODEL // 128


def _store_token_tiles(ref, x, n):
    for s in range(TOKEN_TILE_ROWS):
        ref[pl.ds(s, n, stride=TOKEN_TILE_ROWS), :] = x[:, s * 128:(s + 1) * 128]


def _load_token_tiles(ref, n):
    return jnp.concatenate([ref[pl.ds(s, n, stride=TOKEN_TILE_ROWS), :] for s in range(TOKEN_TILE_ROWS)], axis=1)


def _x_specs():
    return [
        pl.BlockSpec((TOK_TILE, D_MODEL), lambda i: (jnp.minimum(i, N_CTX_TILES - 1), 0)),
        pl.BlockSpec((TOK_TILE, D_MODEL), lambda i: (jnp.maximum(i - N_CTX_TILES, 0), 0)),
    ]


def _mod_kernel(c_ref, w_ref, b_ref, o_ref):
    o_ref[...] = _dot(_silu(c_ref[...]), w_ref[...]) + b_ref[...]


def _modulation(cvec, w_ada, b_ada):
    tn = 1024
    return pl.pallas_call(
        _mod_kernel,
        grid=(6 * D_MODEL // tn,),
        in_specs=[pl.BlockSpec((N_MOD, D_MODEL), lambda j: (0, 0)),
                  pl.BlockSpec((D_MODEL, tn), lambda j: (0, j)),
                  pl.BlockSpec((1, tn), lambda j: (0, j))],
        out_specs=pl.BlockSpec((N_MOD, tn), lambda j: (0, j)),
        out_shape=jax.ShapeDtypeStruct((N_MOD, 6 * D_MODEL), _F32),
        compiler_params=_params(),
        name="modulation",
    )(cvec, w_ada, b_ada)


def _rope(x, cos, sin, first_half):
    swapped = jnp.where(first_half, pltpu.roll(x, 96, 1), pltpu.roll(x, 32, 1))
    return x * cos + swapped * sin


def _inproj_kernel(xp_ref, xs_ref, mod_ref, n1_ref, cos_ref, sin_ref, wdn_ref, watt_ref, wab_ref,
                   dn_ref, aq_ref, ak_ref, av_ref, ab_ref):
    x = jnp.where(_ctx_tile_mask((TOK_TILE, D_MODEL)), xp_ref[...], xs_ref[...])
    shift = mod_ref[0, 0:1, :]
    scale = mod_ref[0, 1:2, :]
    h = (_rms(x, n1_ref[...]) * (1 + scale) + shift).astype(_BF16)
    dn_ref[...] = jnp.dot(h, wdn_ref[...], preferred_element_type=_F32).astype(_BF16)
    ab_ref[...] = jnp.dot(h, wab_ref[...], preferred_element_type=_F32)
    att = jnp.dot(h, watt_ref[...], preferred_element_type=_F32)
    cos = cos_ref[...]
    sin = sin_ref[...]
    lane = lax.broadcasted_iota(jnp.int32, (TOK_TILE, HEAD_DIM), 1)
    first_half = (lane % 64) < 32
    for hd in range(ATT_HEADS):
        q = att[:, hd * HEAD_DIM:(hd + 1) * HEAD_DIM]
        aq_ref[:, hd * HEAD_DIM:(hd + 1) * HEAD_DIM] = (
            _rope(q, cos, sin, first_half) * HEAD_DIM ** -0.5).astype(_BF16)
    for hd in range(ATT_KV_HEADS):
        k = att[:, ATT_Q + hd * HEAD_DIM:ATT_Q + (hd + 1) * HEAD_DIM]
        ak_ref[:, hd * HEAD_DIM:(hd + 1) * HEAD_DIM] = _rope(k, cos, sin, first_half)
    av_ref[...] = att[:, ATT_Q + ATT_KV:]


def _rope_tables():
    t = jnp.arange(DEC_SEQ)
    row = (t // GRID_W).astype(_F32)
    col = (t % GRID_W).astype(_F32)
    n_freq = HEAD_DIM // 4
    inv_freq = 1.0 / (ROPE_THETA ** (jnp.arange(n_freq, dtype=_F32) / n_freq))
    ang_r = row[:, None] * inv_freq
    ang_c = col[:, None] * inv_freq
    cos = jnp.concatenate([jnp.cos(ang_r), jnp.cos(ang_r), jnp.cos(ang_c), jnp.cos(ang_c)], axis=1)
    sin = jnp.concatenate([-jnp.sin(ang_r), jnp.sin(ang_r), -jnp.sin(ang_c), jnp.sin(ang_c)], axis=1)
    cos = jnp.concatenate([jnp.ones((TOK_TILE, HEAD_DIM), _F32), cos], axis=0)
    sin = jnp.concatenate([jnp.zeros((TOK_TILE, HEAD_DIM), _F32), sin], axis=0)
    return cos, sin


def _input_projection(xp, xs, mod3, norm1, cos, sin, w_dn, w_att, w_ab):
    def rope_idx(i):
        return (jnp.where(i < N_CTX_TILES, 0, 1 + (i - N_CTX_TILES) % (DEC_SEQ // TOK_TILE)), 0)

    const = lambda i: (0, 0)
    row = lambda i: (i, 0)
    return pl.pallas_call(
        _inproj_kernel,
        grid=(N_TILES,),
        in_specs=_x_specs() + [
            pl.BlockSpec((1, 6, D_MODEL), lambda i: (_tile_mod_row(i), 0, 0)),
            pl.BlockSpec((1, D_MODEL), const),
            pl.BlockSpec((TOK_TILE, HEAD_DIM), rope_idx),
            pl.BlockSpec((TOK_TILE, HEAD_DIM), rope_idx),
            pl.BlockSpec((D_MODEL, 4 * DN_WIDTH), const),
            pl.BlockSpec((D_MODEL, ATT_Q + 2 * ATT_KV), const),
            pl.BlockSpec((D_MODEL, 128), const),
        ],
        out_specs=[
            pl.BlockSpec((TOK_TILE, 4 * DN_WIDTH), row),
            pl.BlockSpec((TOK_TILE, ATT_Q), row),
            pl.BlockSpec((TOK_TILE, ATT_KV), row),
            pl.BlockSpec((TOK_TILE, ATT_KV), row),
            pl.BlockSpec((TOK_TILE, 128), row),
        ],
        out_shape=[
            jax.ShapeDtypeStruct((M_ALL, 4 * DN_WIDTH), _BF16),
            jax.ShapeDtypeStruct((M_ALL, ATT_Q), _BF16),
            jax.ShapeDtypeStruct((M_ALL, ATT_KV), _F32),
            jax.ShapeDtypeStruct((M_ALL, ATT_KV), _F32),
            jax.ShapeDtypeStruct((M_ALL, 128), _F32),
        ],
        compiler_params=_params(),
        name="input_projection",
    )(xp, xs, mod3, norm1, cos, sin, w_dn, w_att, w_ab)


DN_STACK = 16
DN_LONG_CHUNKS = 8


def _dn_pass_heads(n_chunk):
    return DN_HEADS if n_chunk <= DN_LONG_CHUNKS else DN_HEADS // 2


def _dn_stacking(n_chunk):
    group = min(n_chunk, DN_STACK // 2)
    return group, DN_STACK // (2 * group)


def _bdot(a, b):
    return jnp.stack([_dot(a[g], b[g]) for g in range(a.shape[0])])


def _bdot_nt(a, b):
    return jnp.stack([_dot_nt(a[g], b[g]) for g in range(a.shape[0])])


def _inverse_unit_triangular(a, eye):
    p = eye - a
    x = a
    for _ in range(5):
        x = _bdot(x, x)
        p = p + _bdot(p, x)
    return p


def _dn_prepare(q, k, v, gc, gc_row, beta, g_tot, incl, strict, eye):
    decay = jnp.where(incl, jnp.exp(jnp.where(incl, gc - gc_row, 0.0)), 0.0)
    kb = k * beta
    a = jnp.where(strict, _bdot_nt(kb, k) * decay, 0.0)
    t_inv = _inverse_unit_triangular(a, eye)
    eg = jnp.exp(gc)
    u = _bdot(t_inv, v * beta)
    w = _bdot(t_inv, kb * eg)
    qk = _bdot_nt(q, k) * decay
    wq = jnp.concatenate([w, q * eg], axis=1).astype(_BF16)
    kd = k * jnp.exp(g_tot - gc)
    kd_t = jnp.stack([kd[g].T for g in range(kd.shape[0])])
    qkk = jnp.concatenate([qk, kd_t], axis=1).astype(_BF16)
    return u, wq, qkk


def _dn_steps(states, us, wqs, qkks, g_tots):
    wss = [_dot(wq, s) for wq, s in zip(wqs, states)]
    v_news = [u - ws[:DN_CHUNK] for u, ws in zip(us, wss)]
    rs = [_dot(qkk, v_new) for qkk, v_new in zip(qkks, v_news)]
    outs = [ws[DN_CHUNK:] + r[:DN_CHUNK] for ws, r in zip(wss, rs)]
    states = [s * jnp.exp(g) + r[DN_CHUNK:] for s, g, r in zip(states, g_tots, rs)]
    return states, outs


def _dn_kernel(*refs, seq_len, has_s0, emit_state):
    dn_ref, ab_ref, cw_ref, alog_ref, dtb_ref, nw_ref = refs[:6]
    pos = 6
    s0_ref = None
    if has_s0:
        s0_ref = refs[pos]
        pos += 1
    o_ref = refs[pos]
    pos += 1
    st_ref = None
    if emit_state:
        st_ref = refs[pos]
        pos += 1
    pad_scr, q_scr, k_scr, v_scr, gate_scr, grow_scr, o_scr, u_scr, wq_scr, qkk_scr, s_scr = refs[pos:]

    T = seq_len
    C = DN_CHUNK
    n_chunk = T // C
    n_pair = n_chunk // 2
    G, stack_heads = _dn_stacking(n_chunk)
    n_group = n_chunk // G

    gates = ab_ref[...]
    lane = lax.broadcasted_iota(jnp.int32, (T, 128), 1)
    tpos = lax.broadcasted_iota(jnp.int32, (T, 128), 0) % C
    sp_arg = gates + dtb_ref[...]
    softplus = jnp.maximum(sp_arg, 0.0) + jnp.log1p(jnp.exp(-jnp.abs(sp_arg)))
    g = -jnp.exp(alog_ref[...]) * softplus
    pre = g
    suf = g
    s = 1
    while s < C:
        pre = pre + jnp.where(tpos >= s, pltpu.roll(pre, s, 0), 0.0)
        suf = suf + jnp.where(tpos < C - s, pltpu.roll(suf, T - s, 0), 0.0)
        s *= 2
    gcum = jnp.where(lane < DN_HEADS, pre, suf)
    gate_scr[...] = jnp.where(lane < 2 * DN_HEADS, gcum, jax.nn.sigmoid(gates))
    gcum_t = gcum.T
    for p in range(n_pair):
        grow_scr[p] = gcum_t[0:8, p * 128:(p + 1) * 128]

    pad_scr[0:8, :] = jnp.zeros((8, HEAD_DIM), _F32)
    pad_scr[8 + T:16 + T, :] = jnp.zeros((8, HEAD_DIM), _F32)

    shape3 = (DN_STACK, C, C)
    ri = lax.broadcasted_iota(jnp.int32, shape3, 1)
    ci = lax.broadcasted_iota(jnp.int32, shape3, 2)
    eye = (ri == ci).astype(_F32)
    is_fwd = (lax.broadcasted_iota(jnp.int32, shape3, 0) // G) % 2 == 0
    offset = jnp.where(is_fwd, ci - ri, ri - ci)
    incl = offset <= 0
    strict = offset < 0

    def conv_silu(part, h):
        c0 = part * DN_WIDTH + h * HEAD_DIM
        pad_scr[8:8 + T, :] = dn_ref[:, c0:c0 + HEAD_DIM].astype(_F32)
        acc = cw_ref[0:1, c0:c0 + HEAD_DIM] * pad_scr[6:6 + T, :]
        for j in range(1, DN_CONV):
            acc = acc + cw_ref[j:j + 1, c0:c0 + HEAD_DIM] * pad_scr[6 + j:6 + j + T, :]
        return _silu(acc)

    def l2n(x):
        return x * lax.rsqrt(jnp.sum(x * x, axis=-1, keepdims=True) + NORM_EPS)

    def total_decay(h, direction, r0):
        col = direction * DN_HEADS + h
        row = r0 + C - 1 if direction == 0 else r0
        return gate_scr[pl.ds(row, 1), col:col + 1]

    def prepare_group(grp, heads, chain0):
        r0 = pl.multiple_of(grp * G * C, G * C)
        sl = pl.ds(r0, G * C)
        stacked = lambda x: x.reshape(G, C, x.shape[-1])
        qs, ks, vs, gcs, betas, rows, g_tots = [], [], [], [], [], [], []
        for hh, h in enumerate(heads):
            for direction in range(2):
                col = direction * DN_HEADS + h
                bcol = 2 * DN_HEADS + col
                gc = stacked(gate_scr[sl, col:col + 1])
                gcs.append(gc)
                betas.append(stacked(gate_scr[sl, bcol:bcol + 1]))
                for pair in range(G // 2):
                    both = grow_scr[grp * (G // 2) + pair][col:col + 1, :]
                    rows += [both[:, :C], both[:, C:]]
                g_tots.append(gc[:, C - 1:C, :] if direction == 0 else gc[:, 0:1, :])
                qs.append(stacked(q_scr[hh, sl, :]))
                ks.append(stacked(k_scr[hh, sl, :]))
                vs.append(stacked(v_scr[hh, sl, :]))
        cat = lambda xs: jnp.concatenate(xs, axis=0)
        u, wq, qkk = _dn_prepare(cat(qs), cat(ks), cat(vs), cat(gcs), jnp.stack(rows), cat(betas), cat(g_tots),
                                 incl, strict, eye)
        for local in range(2 * len(heads)):
            chain = chain0 + local
            part = slice(local * G, (local + 1) * G)
            slot0 = chain * n_chunk + grp * G
            u_scr[chain, sl, :] = u[part].reshape(G * C, HEAD_DIM)
            wq_scr[pl.ds(slot0, G)] = wq[part]
            qkk_scr[pl.ds(slot0, G)] = qkk[part]

    def steps(chain_heads, c):
        slots, rows, g_tots = [], [], []
        for chain, h in enumerate(chain_heads):
            direction = chain % 2
            chunk = c if direction == 0 else n_chunk - 1 - c
            r0 = pl.multiple_of(chunk * C, C)
            slots.append(chain * n_chunk + chunk)
            rows.append(pl.ds(r0, C))
            g_tots.append(total_decay(h, direction, r0))
        chains = range(len(chain_heads))
        states, outs = _dn_steps([s_scr[ch] for ch in chains], [u_scr[ch, rows[ch], :] for ch in chains],
                                 [wq_scr[slots[ch]] for ch in chains], [qkk_scr[slots[ch]] for ch in chains], g_tots)
        for ch in chains:
            s_scr[ch] = states[ch]
            o_scr[ch, rows[ch], :] = outs[ch]

    pass_heads = _dn_pass_heads(n_chunk)
    for h0 in range(0, DN_HEADS, pass_heads):
        heads = range(h0, h0 + pass_heads)
        for h1 in range(h0, h0 + pass_heads, stack_heads):
            stack = range(h1, h1 + stack_heads)
            for hh, h in enumerate(stack):
                q_scr[hh] = l2n(conv_silu(0, h)) * HEAD_DIM ** -0.5
                k_scr[hh] = l2n(conv_silu(1, h))
                v_scr[hh] = conv_silu(2, h)

            def prepare_body(grp, carry, stack=stack, h1=h1):
                prepare_group(grp, stack, (h1 - h0) * 2)
                return carry

            lax.fori_loop(0, n_group, prepare_body, 0)
        for h in heads:
            for direction in range(2):
                chain = (h - h0) * 2 + direction
                if has_s0:
                    s_scr[chain] = s0_ref[0, direction, h].astype(_F32)
                else:
                    s_scr[chain] = jnp.zeros((HEAD_DIM, HEAD_DIM), _F32)

        chain_heads = [h for h in heads for _ in range(2)]

        def step_body(c, carry):
            steps(chain_heads, c)
            return carry

        lax.fori_loop(0, n_chunk, step_body, 0)

        for h in heads:
            chain = (h - h0) * 2
            if emit_state:
                st_ref[0, 0, h] = s_scr[chain]
                st_ref[0, 1, h] = s_scr[chain + 1]
            o = o_scr[chain] + o_scr[chain + 1]
            z = dn_ref[:, 3 * DN_WIDTH + h * HEAD_DIM:3 * DN_WIDTH + (h + 1) * HEAD_DIM].astype(_F32)
            o = o * lax.rsqrt(jnp.mean(o * o, axis=-1, keepdims=True) + NORM_EPS) * nw_ref[...] * _silu(z)
            o_ref[:, h * HEAD_DIM:(h + 1) * HEAD_DIM] = o.astype(_BF16)


def _deltanet(dn, ab, conv_w, a_log, dt_bias, dn_norm, s0, *, n_seq, seq_len, row_block0, emit_state):
    has_s0 = s0 is not None
    const = lambda b: (0, 0)
    state_spec = pl.BlockSpec((1, 2, DN_HEADS, HEAD_DIM, HEAD_DIM), lambda b: (b, 0, 0, 0, 0))
    in_specs = [
        pl.BlockSpec((seq_len, 4 * DN_WIDTH), lambda b: (row_block0 + b, 0)),
        pl.BlockSpec((seq_len, 128), lambda b: (row_block0 + b, 0)),
        pl.BlockSpec((8, 3 * DN_WIDTH), const),
        pl.BlockSpec((1, 128), const),
        pl.BlockSpec((1, 128), const),
        pl.BlockSpec((1, HEAD_DIM), const),
    ]
    args = [dn, ab, conv_w, a_log, dt_bias, dn_norm]
    if has_s0:
        in_specs.append(state_spec)
        args.append(s0)
    out_specs = [pl.BlockSpec((seq_len, DN_WIDTH), lambda b: (b, 0))]
    out_shape = [jax.ShapeDtypeStruct((n_seq * seq_len, DN_WIDTH), _BF16)]
    if emit_state:
        out_specs.append(state_spec)
        out_shape.append(jax.ShapeDtypeStruct((n_seq, 2, DN_HEADS, HEAD_DIM, HEAD_DIM), _F32))
    n_chunk = seq_len // DN_CHUNK
    pass_tile = (_dn_stacking(n_chunk)[1], seq_len, HEAD_DIM)
    n_chain = 2 * _dn_pass_heads(n_chunk)
    return pl.pallas_call(
        functools.partial(_dn_kernel, seq_len=seq_len, has_s0=has_s0, emit_state=emit_state),
        grid=(n_seq,),
        in_specs=in_specs,
        out_specs=out_specs,
        out_shape=out_shape,
        scratch_shapes=[
            pltpu.VMEM((seq_len + 16, HEAD_DIM), _F32),
            pltpu.VMEM(pass_tile, _F32), pltpu.VMEM(pass_tile, _F32), pltpu.VMEM(pass_tile, _F32),
            pltpu.VMEM((seq_len, 128), _F32),
            pltpu.VMEM((seq_len // (2 * DN_CHUNK), 8, 128), _F32),
            pltpu.VMEM((n_chain, seq_len, HEAD_DIM), _F32),
            pltpu.VMEM((n_chain, seq_len, HEAD_DIM), _F32),
            pltpu.VMEM((n_chain * n_chunk, 2 * DN_CHUNK, HEAD_DIM), _BF16),
            pltpu.VMEM((n_chain * n_chunk, DN_CHUNK + HEAD_DIM, DN_CHUNK), _BF16),
            pltpu.VMEM((n_chain, HEAD_DIM, HEAD_DIM), _F32),
        ],
        compiler_params=_params(),
        name="deltanet_ctx" if emit_state else "deltanet_lat",
    )(*args)


def _softmax_av(scores, values, sink):
    m = sink
    for s in scores:
        m = jnp.maximum(m, jnp.max(s, axis=-1, keepdims=True))
    denom = jnp.exp(sink - m)
    acc = None
    for s, v in zip(scores, values):
        p = jnp.exp(s - m)
        denom = denom + jnp.sum(p, axis=-1, keepdims=True)
        pv = _dot(p, v)
        acc = pv if acc is None else acc + pv
    return acc / denom


def _ctx_attn_kernel(sink_ref, q_ref, k_ref, v_ref, o_ref):
    kvh = pl.program_id(1)
    k = k_ref[...]
    v = v_ref[...]
    for g in range(ATT_GROUP):
        q = q_ref[:, g * HEAD_DIM:(g + 1) * HEAD_DIM]
        sink = jnp.full((SEQ, 1), sink_ref[kvh * ATT_GROUP + g], _F32)
        o = _softmax_av([_dot_nt(q, k)], [v], sink)
        o_ref[:, g * HEAD_DIM:(g + 1) * HEAD_DIM] = o.astype(_BF16)


def _context_attention(sinks, aq, ak, av):
    return pl.pallas_call(
        _ctx_attn_kernel,
        grid=(BATCH, ATT_KV_HEADS),
        in_specs=[
            pl.BlockSpec(memory_space=pltpu.SMEM),
            pl.BlockSpec((SEQ, ATT_GROUP * HEAD_DIM), lambda b, h: (b, h)),
            pl.BlockSpec((SEQ, HEAD_DIM), lambda b, h: (b, h)),
            pl.BlockSpec((SEQ, HEAD_DIM), lambda b, h: (b, h)),
        ],
        out_specs=pl.BlockSpec((SEQ, ATT_GROUP * HEAD_DIM), lambda b, h: (b, h)),
        out_shape=jax.ShapeDtypeStruct((M_CTX, ATT_Q), _BF16),
        compiler_params=_params(2),
        name="context_attention",
    )(sinks, aq, ak, av)


LAT_Q_BLOCK = 2 * ATT_BLOCK


def _lat_attn_kernel(sink_ref, q_ref, kp_ref, kc_ref, kn_ref, vp_ref, vc_ref, vn_ref, ck_ref, cv_ref, o_ref):
    kvh = pl.program_id(1)
    i = pl.program_id(2)
    nb = pl.num_programs(2)
    B, Q = ATT_BLOCK, LAT_Q_BLOCK
    rows = ATT_GROUP * Q
    q = jnp.concatenate([q_ref[:, g * HEAD_DIM:(g + 1) * HEAD_DIM] for g in range(ATT_GROUP)], axis=0)
    r_b = lax.broadcasted_iota(jnp.int32, (rows, B), 0) % Q
    c_b = lax.broadcasted_iota(jnp.int32, (rows, B), 1)
    r_q = lax.broadcasted_iota(jnp.int32, (rows, Q), 0) % Q
    c_q = lax.broadcasted_iota(jnp.int32, (rows, Q), 1)
    s_prev = jnp.where(c_b >= r_b + jnp.where(i > 0, 0, Q), _dot_nt(q, kp_ref[...]), NEG_INF)
    s_cur = jnp.where(jnp.abs(r_q - c_q) <= B, _dot_nt(q, kc_ref[...]), NEG_INF)
    s_next = jnp.where(c_b <= r_b - B - jnp.where(i < nb - 1, 0, Q), _dot_nt(q, kn_ref[...]), NEG_INF)
    s_ctx = _dot_nt(q, ck_ref[...])
    head = lax.broadcasted_iota(jnp.int32, (rows, 1), 0) // Q
    sink = jnp.zeros((rows, 1), _F32)
    for g in range(ATT_GROUP):
        sink = jnp.where(head == g, sink_ref[kvh * ATT_GROUP + g], sink)
    o = _softmax_av([s_prev, s_cur, s_next, s_ctx], [vp_ref[...], vc_ref[...], vn_ref[...], cv_ref[...]], sink)
    for g in range(ATT_GROUP):
        o_ref[:, g * HEAD_DIM:(g + 1) * HEAD_DIM] = o[g * Q:(g + 1) * Q].astype(_BF16)


def _latent_attention(sinks, aq, ak, av, ctx_k, ctx_v):
    nq = DEC_SEQ // LAT_Q_BLOCK
    nb = DEC_SEQ // ATT_BLOCK
    q0 = M_CTX // LAT_Q_BLOCK
    b0 = M_CTX // ATT_BLOCK
    cur = lambda b, h, i: (q0 + b * nq + i, h)
    prev = lambda b, h, i: (b0 + b * nb + jnp.maximum(2 * i - 1, 0), h)
    nxt = lambda b, h, i: (b0 + b * nb + jnp.minimum(2 * i + 2, nb - 1), h)
    side_spec = lambda f: pl.BlockSpec((ATT_BLOCK, HEAD_DIM), f)
    cur_spec = pl.BlockSpec((LAT_Q_BLOCK, HEAD_DIM), cur)
    ctx_spec = pl.BlockSpec((PAST_LEN, HEAD_DIM), lambda b, h, i: (b, h))
    return pl.pallas_call(
        _lat_attn_kernel,
        grid=(DEC_BATCH, ATT_KV_HEADS, nq),
        in_specs=[
            pl.BlockSpec(memory_space=pltpu.SMEM),
            pl.BlockSpec((LAT_Q_BLOCK, ATT_GROUP * HEAD_DIM), cur),
            side_spec(prev), cur_spec, side_spec(nxt),
            side_spec(prev), cur_spec, side_spec(nxt),
            ctx_spec, ctx_spec,
        ],
        out_specs=pl.BlockSpec((LAT_Q_BLOCK, ATT_GROUP * HEAD_DIM), lambda b, h, i: (b * nq + i, h)),
        out_shape=jax.ShapeDtypeStruct((M_LAT, ATT_Q), _BF16),
        compiler_params=_params(3),
        name="latent_attention",
    )(sinks, aq, ak, ak, ak, av, av, av, ctx_k, ctx_v)


def _outproj_kernel(xp_ref, xs_ref, dnc_ref, dnl_ref, atc_ref, atl_ref, mod_ref, n2_ref, wo_dn_ref, wo_at_ref,
                    rw_ref, x1_ref, h2_ref, h2p_ref, lg_ref):
    x = jnp.where(_ctx_tile_mask((TOK_TILE, D_MODEL)), xp_ref[...], xs_ref[...])
    half_mask = _ctx_tile_mask((TOK_TILE, DN_WIDTH))
    dn = jnp.where(half_mask, dnc_ref[...], dnl_ref[...])
    at = jnp.where(half_mask, atc_ref[...], atl_ref[...])
    mixed = (jnp.dot(dn, wo_dn_ref[...], preferred_element_type=_F32)
             + jnp.dot(at, wo_at_ref[...], preferred_element_type=_F32))
    x1 = x + mod_ref[0, 2:3, :] * mixed
    x1_ref[...] = x1
    h = _rms(x1, n2_ref[...]) * (1 + mod_ref[0, 4:5, :]) + mod_ref[0, 3:4, :]
    h2_ref[...] = h.astype(_BF16)
    _store_token_tiles(h2p_ref, h, TOK_TILE)
    w = rw_ref[...]
    w_hi = w.astype(_BF16)
    w_lo = (w - w_hi.astype(_F32)).astype(_BF16)
    h_hi = h.astype(_BF16)
    h_lo = (h - h_hi.astype(_F32)).astype(_BF16)
    lg_ref[...] = (_dot_nt(w_hi, h_hi) + _dot_nt(w_hi, h_lo)) + _dot_nt(w_lo, h_hi)


def _output_projection(xp, xs, dn_c, dn_l, at_c, at_l, mod3, norm2, wo_dn, wo_at, router_wt):
    const = lambda i: (0, 0)
    row = lambda i: (i, 0)
    ctx_row = lambda i: (jnp.minimum(i, N_CTX_TILES - 1), 0)
    lat_row = lambda i: (jnp.maximum(i - N_CTX_TILES, 0), 0)
    half = (TOK_TILE, DN_WIDTH)
    return pl.pallas_call(
        _outproj_kernel,
        grid=(N_TILES,),
        in_specs=_x_specs() + [
            pl.BlockSpec(half, ctx_row), pl.BlockSpec(half, lat_row),
            pl.BlockSpec(half, ctx_row), pl.BlockSpec(half, lat_row),
            pl.BlockSpec((1, 6, D_MODEL), lambda i: (_tile_mod_row(i), 0, 0)),
            pl.BlockSpec((1, D_MODEL), const),
            pl.BlockSpec((DN_WIDTH, D_MODEL), const),
            pl.BlockSpec((ATT_Q, D_MODEL), const),
            pl.BlockSpec((N_EXPERTS, D_MODEL), const),
        ],
        out_specs=[
            pl.BlockSpec((TOK_TILE, D_MODEL), row),
            pl.BlockSpec((TOK_TILE, D_MODEL), row),
            pl.BlockSpec((TOK_TILE * TOKEN_TILE_ROWS, 128), row),
            pl.BlockSpec((N_EXPERTS, TOK_TILE), lambda i: (0, i)),
        ],
        out_shape=[
            jax.ShapeDtypeStruct((M_ALL, D_MODEL), _F32),
            jax.ShapeDtypeStruct((M_ALL, D_MODEL), _BF16),
            jax.ShapeDtypeStruct((M_ALL * TOKEN_TILE_ROWS, 128), _F32),
            jax.ShapeDtypeStruct((N_EXPERTS, M_ALL), _F32),
        ],
        compiler_params=_params(),
        name="output_projection",
    )(xp, xs, dn_c, dn_l, at_c, at_l, mod3, norm2, wo_dn, wo_at, router_wt)


def _first_index_of(values, target, index, limit):
    return jnp.min(jnp.where(values == target, index, limit), axis=0, keepdims=True)


def _route_kernel(lg_ref, bias_ref, idx_ref, rank_ref, gate_ref, cnt_ref, carry_scr):
    i = pl.program_id(0)
    tm = TOK_TILE

    @pl.when(i == 0)
    def _():
        carry_scr[...] = jnp.zeros_like(carry_scr)

    scores = jax.nn.sigmoid(lg_ref[...])
    biased = scores + bias_ref[...]
    row = lax.broadcasted_iota(jnp.int32, (N_EXPERTS, tm), 0).astype(_F32)
    grow = lax.broadcasted_iota(jnp.int32, (GROUP_SIZE, tm), 0).astype(_F32)

    group_rows = []
    for g in range(N_GROUPS):
        blk = biased[g * GROUP_SIZE:(g + 1) * GROUP_SIZE]
        m1 = jnp.max(blk, axis=0, keepdims=True)
        i1 = _first_index_of(blk, m1, grow, GROUP_SIZE)
        m2 = jnp.max(jnp.where(grow == i1, -jnp.inf, blk), axis=0, keepdims=True)
        group_rows.append(m1 + m2)
    gs = jnp.concatenate(group_rows, axis=0)
    gi = lax.broadcasted_iota(jnp.int32, (N_GROUPS, tm), 0).astype(_F32)
    gself = jnp.zeros((N_GROUPS, tm), _F32)
    for _ in range(TOPK_GROUPS):
        hit = gi == _first_index_of(gs, jnp.max(gs, axis=0, keepdims=True), gi, N_GROUPS)
        gself = jnp.where(hit, 1.0, gself)
        gs = jnp.where(hit, -jnp.inf, gs)
    emask = jnp.concatenate(
        [jnp.broadcast_to(gself[g:g + 1], (GROUP_SIZE, tm)) for g in range(N_GROUPS)], axis=0) > 0.5
    masked = jnp.where(emask, biased, NEG_INF)

    selected = jnp.zeros((N_EXPERTS, tm), _F32)
    idxs, gates = [], []
    for _ in range(TOP_K):
        ei = _first_index_of(masked, jnp.max(masked, axis=0, keepdims=True), row, N_EXPERTS)
        hit = row == ei
        idxs.append(ei)
        gates.append(jnp.sum(jnp.where(hit, scores, 0.0), axis=0, keepdims=True))
        masked = jnp.where(hit, -jnp.inf, masked)
        selected = jnp.where(hit, 1.0, selected)
    gsum = gates[0]
    for gk in gates[1:]:
        gsum = gsum + gk
    gates = [gk / gsum * ROUTED_SCALE for gk in gates]

    si = lax.broadcasted_iota(jnp.int32, (tm, tm), 0)
    ti = lax.broadcasted_iota(jnp.int32, (tm, tm), 1)
    earlier = (si < ti).astype(_BF16)
    ranks_all = _dot(selected, earlier) + carry_scr[...]
    ranks = [jnp.sum(jnp.where(row == ei, ranks_all, 0.0), axis=0, keepdims=True) for ei in idxs]
    carry_scr[...] = carry_scr[...] + jnp.sum(selected, axis=1, keepdims=True)

    idx_ref[...] = jnp.concatenate(idxs, axis=0).astype(jnp.int32)
    rank_ref[...] = jnp.concatenate(ranks, axis=0).astype(jnp.int32)
    gate_rows = jnp.concatenate(gates + [jnp.zeros((128 - TOP_K, tm), _F32)], axis=0)
    gate_ref[...] = gate_rows.T
    cnt_ref[...] = jnp.broadcast_to(carry_scr[...], (N_EXPERTS, 128))


def _route(logits_t, bias_col):
    return pl.pallas_call(
        _route_kernel,
        grid=(N_TILES,),
        in_specs=[pl.BlockSpec((N_EXPERTS, TOK_TILE), lambda i: (0, i)),
                  pl.BlockSpec((N_EXPERTS, 1), lambda i: (0, 0))],
        out_specs=[
            pl.BlockSpec((TOP_K, TOK_TILE), lambda i: (0, i)),
            pl.BlockSpec((TOP_K, TOK_TILE), lambda i: (0, i)),
            pl.BlockSpec((TOK_TILE, 128), lambda i: (i, 0)),
            pl.BlockSpec((N_EXPERTS, 128), lambda i: (0, 0)),
        ],
        out_shape=[
            jax.ShapeDtypeStruct((TOP_K, M_ALL), jnp.int32),
            jax.ShapeDtypeStruct((TOP_K, M_ALL), jnp.int32),
            jax.ShapeDtypeStruct((M_ALL, 128), _F32),
            jax.ShapeDtypeStruct((N_EXPERTS, 128), _F32),
        ],
        scratch_shapes=[pltpu.VMEM((N_EXPERTS, 1), _F32)],
        compiler_params=_params(),
        name="route",
    )(logits_t, bias_col)


def _slot_kernel(idx_ref, rank_ref, start_ref, pos_ref):
    row = lax.broadcasted_iota(jnp.int32, (N_EXPERTS, TOK_TILE), 0)
    start = start_ref[...]
    rows = []
    for k in range(TOP_K):
        base = jnp.sum(jnp.where(row == idx_ref[k:k + 1, :], start, 0.0), axis=0, keepdims=True)
        rows.append(base.astype(jnp.int32) + rank_ref[k:k + 1, :])
    pos_ref[...] = jnp.concatenate(rows, axis=0)


def _slots(idx, rank, start_col):
    spec = pl.BlockSpec((TOP_K, TOK_TILE), lambda i: (0, i))
    return pl.pallas_call(
        _slot_kernel,
        grid=(N_TILES,),
        in_specs=[spec, spec, pl.BlockSpec((N_EXPERTS, 1), lambda i: (0, 0))],
        out_specs=spec,
        out_shape=jax.ShapeDtypeStruct((TOP_K, M_ALL), jnp.int32),
        compiler_params=_params(),
        name="slots",
    )(idx, rank, start_col)


def _slot_tile(ref, slot):
    return ref.at[pl.ds(pl.multiple_of(slot * TOKEN_TILE_ROWS, TOKEN_TILE_ROWS), TOKEN_TILE_ROWS)]


def _dispatch_kernel(pad_lo_ref, pad_hi_ref, nused_ref, pos_ref, h_hbm, xs_ref, zero_scr, h_buf, in_sem, out_sem,
                     zero_sem):
    i = pl.program_id(0)
    n_steps = pl.num_programs(0)
    block_rows = TOK_TILE * TOKEN_TILE_ROWS

    def fetch(step):
        src = h_hbm.at[pl.ds(pl.multiple_of(step * block_rows, block_rows), block_rows)]
        return pltpu.make_async_copy(src, h_buf.at[step % 3], in_sem.at[step % 3])

    def wait_copies(step):
        for _ in range(TOP_K):
            pltpu.make_async_copy(h_buf.at[0], xs_ref.at[pl.ds(0, block_rows)], out_sem.at[step % 2]).wait()

    @pl.when(i == 0)
    def _():
        fetch(i).start()

    @pl.when(i + 1 < n_steps)
    def _():
        fetch(i + 1).start()

    def zero_fill(start):
        def zero_copy(first_slot, n_slots):
            rows = n_slots * TOKEN_TILE_ROWS
            dst = xs_ref.at[pl.ds(pl.multiple_of(first_slot * TOKEN_TILE_ROWS, TOKEN_TILE_ROWS), rows)]
            return pltpu.make_async_copy(zero_scr.at[pl.ds(0, rows)], dst, zero_sem)

        def expert_padding(e, carry):
            first = pad_lo_ref[e]
            n = pad_hi_ref[e] - first
            size = EXPERT_ROWS // 2
            while size >= 1:
                chunk = n & size

                @pl.when(chunk != 0)
                def _(first=first, size=size):
                    copy = zero_copy(first, size)
                    copy.start() if start else copy.wait()

                first = first + chunk
                size //= 2
            return carry

        def tail_block(t, carry):
            j = N_BLOCKS - 1 - t

            @pl.when(j >= nused_ref[0])
            def _():
                copy = zero_copy(j * EXPERT_ROWS, EXPERT_ROWS)
                copy.start() if start else copy.wait()

            return carry

        lax.fori_loop(0, N_EXPERTS, expert_padding, 0)
        lax.fori_loop(0, N_BLOCKS - N_PAIRS // EXPERT_ROWS, tail_block, 0)

    @pl.when(i == 0)
    def _():
        zero_scr[...] = jnp.zeros_like(zero_scr)
        zero_fill(start=True)

    fetch(i).wait()
    h_tile = h_buf.at[i % 3]

    def body(t, carry):
        src = _slot_tile(h_tile, t)
        for k in range(TOP_K):
            pltpu.make_async_copy(src, _slot_tile(xs_ref, pos_ref[0, k, t]), out_sem.at[i % 2]).start(
                priority=k % 2)
        return carry

    lax.fori_loop(0, TOK_TILE, body, 0, unroll=4)

    @pl.when(i >= 1)
    def _():
        wait_copies(i - 1)

    @pl.when(i == n_steps - 1)
    def _():
        wait_copies(i)
        zero_fill(start=False)


def _dispatch(pad_lo, pad_hi, n_used, pos3, h2p):
    grid_spec = pltpu.PrefetchScalarGridSpec(
        num_scalar_prefetch=3,
        grid=(N_TILES,),
        in_specs=[pl.BlockSpec((1, TOP_K, TOK_TILE), lambda i, *_: (i, 0, 0), memory_space=pltpu.SMEM),
                  pl.BlockSpec(memory_space=pl.ANY)],
        out_specs=pl.BlockSpec(memory_space=pl.ANY),
        scratch_shapes=[pltpu.VMEM((EXPERT_ROWS * TOKEN_TILE_ROWS, 128), _F32),
                        pltpu.VMEM((3, TOK_TILE * TOKEN_TILE_ROWS, 128), _F32),
                        pltpu.SemaphoreType.DMA((3,)), pltpu.SemaphoreType.DMA((2,)),
                        pltpu.SemaphoreType.DMA(())],
    )
    return pl.pallas_call(
        _dispatch_kernel,
        grid_spec=grid_spec,
        out_shape=jax.ShapeDtypeStruct((N_SLOTS * TOKEN_TILE_ROWS, 128), _F32),
        compiler_params=_params(),
        name="dispatch",
    )(pad_lo, pad_hi, n_used, pos3, h2p)


def _expert_kernel(be_ref, nused_ref, first_ref, next_ref, parity_ref, xs_ref, w1_hbm, w3_hbm, w2_hbm, ys_ref,
                   w1_buf, w3_buf, w2_buf, w1_scr, w3_scr, w2_scr, sem):
    j = pl.program_id(0)

    def weight_copies(expert, slot):
        return [pltpu.make_async_copy(hbm.at[expert], buf.at[slot], sem.at[slot])
                for hbm, buf in ((w1_hbm, w1_buf), (w3_hbm, w3_buf), (w2_hbm, w2_buf))]

    @pl.when(j < nused_ref[0])
    def _():
        @pl.when(j == 0)
        def _():
            for copy in weight_copies(be_ref[0], 0):
                copy.start()

        @pl.when(first_ref[j] == 1)
        def _():
            slot = parity_ref[j]
            for copy in weight_copies(be_ref[j], slot):
                copy.wait()

            @pl.when(next_ref[j] >= 0)
            def _():
                for copy in weight_copies(next_ref[j], 1 - slot):
                    copy.start(priority=1)

            w1_scr[...] = w1_buf[slot].astype(_BF16)
            w3_scr[...] = w3_buf[slot].astype(_BF16)
            w2_scr[...] = w2_buf[slot].astype(_BF16)

        x = _load_token_tiles(xs_ref, EXPERT_ROWS).astype(_BF16)
        h1 = jnp.dot(x, w1_scr[...], preferred_element_type=_F32)
        h3 = jnp.dot(x, w3_scr[...], preferred_element_type=_F32)
        act = (_silu(h1) * h3).astype(_BF16)
        _store_token_tiles(ys_ref, jnp.dot(act, w2_scr[...], preferred_element_type=_F32), EXPERT_ROWS)

    @pl.when(j >= nused_ref[0])
    def _():
        ys_ref[...] = jnp.zeros_like(ys_ref)


def _experts(block_expert, n_used, run_first, run_next, run_parity, xs, w1, w3, w2):
    def blk(j, be, nu, *_):
        return (jnp.minimum(j, nu[0] - 1), 0)

    up, down = (D_MODEL, EXPERT_FF), (EXPERT_FF, D_MODEL)
    grid_spec = pltpu.PrefetchScalarGridSpec(
        num_scalar_prefetch=5,
        grid=(N_BLOCKS,),
        in_specs=[
            pl.BlockSpec((EXPERT_ROWS * TOKEN_TILE_ROWS, 128), blk),
            pl.BlockSpec(memory_space=pl.ANY),
            pl.BlockSpec(memory_space=pl.ANY),
            pl.BlockSpec(memory_space=pl.ANY),
        ],
        out_specs=pl.BlockSpec((EXPERT_ROWS * TOKEN_TILE_ROWS, 128), lambda j, *_: (j, 0)),
        scratch_shapes=[pltpu.VMEM((2,) + up, _F32), pltpu.VMEM((2,) + up, _F32), pltpu.VMEM((2,) + down, _F32),
                        pltpu.VMEM(up, _BF16), pltpu.VMEM(up, _BF16), pltpu.VMEM(down, _BF16),
                        pltpu.SemaphoreType.DMA((2,))],
    )
    return pl.pallas_call(
        _expert_kernel,
        grid_spec=grid_spec,
        out_shape=jax.ShapeDtypeStruct((N_SLOTS * TOKEN_TILE_ROWS, 128), _F32),
        compiler_params=_params(),
        name="experts",
    )(block_expert, n_used, run_first, run_next, run_parity, xs, w1, w3, w2)


COMBINE_CHUNK = 8


def _combine_kernel(pos_ref, pos_next_ref, gate_ref, h2_ref, x1_ref, mod_ref, sw1_ref, sw3_ref, sw2_ref, fn_ref,
                    ys_ref, y_ref, buf, routed_scr, sem):
    tm = COMBINE_TILE
    tc = COMBINE_CHUNK
    i = pl.program_id(0)
    slot = i % 2

    def issue(p_ref, dst_slot, t):
        for k in range(TOP_K):
            pltpu.make_async_copy(_slot_tile(ys_ref, p_ref[0, k, t]), _slot_tile(buf.at[dst_slot, k], t),
                                  sem.at[dst_slot]).start(priority=k % 2)

    @pl.when(i == 0)
    def _():
        def body(t, carry):
            issue(pos_ref, 0, t)
            return carry

        lax.fori_loop(0, tm, body, 0, unroll=4)

    for k in range(TOP_K):
        pltpu.make_async_copy(ys_ref.at[pl.ds(0, tm * TOKEN_TILE_ROWS)], buf.at[slot, k], sem.at[slot]).wait()

    def weighted_sum_loop(issue_next):
        def body(c, carry):
            t0 = pl.multiple_of(c * tc, tc)
            if issue_next:
                for tt in range(tc):
                    issue(pos_next_ref, 1 - slot, t0 + tt)
            gates = gate_ref[pl.ds(t0, tc), :]
            weights = [gates[:, k:k + 1] for k in range(TOP_K)]
            for s in range(TOKEN_TILE_ROWS):
                rows = pl.ds(t0 * TOKEN_TILE_ROWS + s, tc, stride=TOKEN_TILE_ROWS)
                acc = weights[0] * buf[slot, 0, rows, :]
                for k in range(1, TOP_K):
                    acc = acc + weights[k] * buf[slot, k, rows, :]
                routed_scr[pl.ds(t0, tc), s * 128:(s + 1) * 128] = acc
            return carry

        lax.fori_loop(0, tm // tc, body, 0)

    has_next = i + 1 < pl.num_programs(0)

    @pl.when(has_next)
    def _():
        weighted_sum_loop(True)

    @pl.when(jnp.logical_not(has_next))
    def _():
        weighted_sum_loop(False)

    hb = h2_ref[...]
    act = _silu(jnp.dot(hb, sw1_ref[...], preferred_element_type=_F32)) * jnp.dot(
        hb, sw3_ref[...], preferred_element_type=_F32)
    shared = _dot(act, sw2_ref[...])
    y = x1_ref[...] + mod_ref[0, 5:6, :] * (routed_scr[...] + shared)
    y_ref[...] = _rms(y, fn_ref[...])


def _combine(pos3, gate_t, h2, x1, mod3, sw1, sw3, sw2, final_norm, ys, *, n_rows, tile0, mod_row):
    tm = COMBINE_TILE
    const = lambda i: (0, 0)
    row = lambda i: (tile0 + i, 0)
    n_tiles = n_rows // tm
    return pl.pallas_call(
        _combine_kernel,
        grid=(n_tiles,),
        in_specs=[
            pl.BlockSpec((1, TOP_K, tm), lambda i: (tile0 + i, 0, 0), memory_space=pltpu.SMEM),
            pl.BlockSpec((1, TOP_K, tm), lambda i: (tile0 + jnp.minimum(i + 1, n_tiles - 1), 0, 0),
                         memory_space=pltpu.SMEM),
            pl.BlockSpec((tm, 128), row),
            pl.BlockSpec((tm, D_MODEL), row),
            pl.BlockSpec((tm, D_MODEL), row),
            pl.BlockSpec((1, 6, D_MODEL), lambda i: (mod_row(i), 0, 0)),
            pl.BlockSpec((D_MODEL, EXPERT_FF), const),
            pl.BlockSpec((D_MODEL, EXPERT_FF), const),
            pl.BlockSpec((EXPERT_FF, D_MODEL), const),
            pl.BlockSpec((1, D_MODEL), const),
            pl.BlockSpec(memory_space=pl.ANY),
        ],
        out_specs=pl.BlockSpec((tm, D_MODEL), lambda i: (i, 0)),
        out_shape=jax.ShapeDtypeStruct((n_rows, D_MODEL), _F32),
        scratch_shapes=[pltpu.VMEM((2, TOP_K, tm * TOKEN_TILE_ROWS, 128), _F32), pltpu.VMEM((tm, D_MODEL), _F32),
                        pltpu.SemaphoreType.DMA((2,))],
        compiler_params=_params(),
        name="combine",
    )(pos3, pos3, gate_t, h2, x1, mod3, sw1, sw3, sw2, final_norm, ys)


def _tile_major(a, tile):
    return a.reshape(TOP_K, -1, tile).transpose(1, 0, 2)


def kernel(x_prompt, x_sample, c, cache_k, cache_v, state_dn, c_ctx, w_ada, b_ada, norm1, norm2, w_in, dn_conv,
           dn_A_log, dn_dt_bias, dn_norm, attn_sinks, w_out, router_w, router_bias, expert_w1, expert_w3,
           expert_w2, shared_w1, shared_w3, shared_w2, final_norm):
    xp = x_prompt.reshape(M_CTX, D_MODEL)
    xs = x_sample.reshape(M_LAT, D_MODEL)

    cvec = jnp.concatenate([c_ctx[None, :], c, jnp.zeros((N_MOD - 1 - DEC_BATCH, D_MODEL), _F32)], axis=0)
    mod3 = _modulation(cvec, w_ada[0], b_ada).reshape(N_MOD, 6, D_MODEL)

    w = w_in[0]
    n_dn = 4 * DN_WIDTH
    w_dn = w[:, :n_dn].astype(_BF16)
    w_ab = jnp.pad(w[:, n_dn:n_dn + 4 * DN_HEADS], ((0, 0), (0, 128 - 4 * DN_HEADS))).astype(_BF16)
    w_att = w[:, n_dn + 4 * DN_HEADS:].astype(_BF16)
    cos, sin = _rope_tables()
    dn, aq, ak, av, ab = _input_projection(xp, xs, mod3, norm1, cos, sin, w_dn, w_att, w_ab)

    conv_w = jnp.pad(dn_conv[0], ((0, 8 - DN_CONV), (0, 0)))
    pad8 = lambda v: jnp.pad(v.reshape(1, 2 * DN_HEADS), ((0, 0), (0, 128 - 2 * DN_HEADS)))
    a_log = pad8(dn_A_log[0])
    dt_bias = pad8(dn_dt_bias[0])
    dn_c, new_state = _deltanet(dn, ab, conv_w, a_log, dt_bias, dn_norm, None,
                                n_seq=BATCH, seq_len=SEQ, row_block0=0, emit_state=True)
    (dn_l,) = _deltanet(dn, ab, conv_w, a_log, dt_bias, dn_norm, state_dn[:, 0],
                        n_seq=DEC_BATCH, seq_len=DEC_SEQ, row_block0=M_CTX // DEC_SEQ, emit_state=False)

    sinks = attn_sinks[0]
    at_c = _context_attention(sinks, aq, ak, av)
    ctx_k = cache_k[:, 0].reshape(DEC_BATCH * PAST_LEN, ATT_KV)
    ctx_v = cache_v[:, 0].reshape(DEC_BATCH * PAST_LEN, ATT_KV)
    at_l = _latent_attention(sinks, aq, ak, av, ctx_k, ctx_v)

    wo = w_out[0].astype(_BF16)
    x1, h2, h2p, logits_t = _output_projection(xp, xs, dn_c, dn_l, at_c, at_l, mod3, norm2,
                                               wo[:DN_WIDTH], wo[DN_WIDTH:], router_w[0].T)

    idx, rank, gate_t, counts = _route(logits_t, router_bias[0].reshape(N_EXPERTS, 1))
    cnt = counts[:, 0].astype(jnp.int32)
    padded = (cnt + EXPERT_ROWS - 1) // EXPERT_ROWS * EXPERT_ROWS
    pad_end = jnp.cumsum(padded)
    pos = _slots(idx, rank, (pad_end - padded).astype(_F32).reshape(N_EXPERTS, 1))
    n_used = (pad_end[-1] // EXPERT_ROWS).astype(jnp.int32).reshape(1)
    block_start = jnp.arange(N_BLOCKS, dtype=jnp.int32) * EXPERT_ROWS
    block_expert = jnp.minimum(
        jnp.sum((pad_end[None, :] <= block_start[:, None]).astype(jnp.int32), axis=1), N_EXPERTS - 1)

    pad_lo = (pad_end - padded + cnt).astype(jnp.int32)
    x_sorted = _dispatch(pad_lo, pad_end.astype(jnp.int32), n_used, _tile_major(pos, TOK_TILE), h2p)
    run_first = jnp.concatenate([jnp.ones((1,), jnp.int32),
                                 (block_expert[1:] != block_expert[:-1]).astype(jnp.int32)])
    run_parity = (jnp.cumsum(run_first) - 1) % 2
    experts = jnp.arange(N_EXPERTS, dtype=jnp.int32)
    later = (experts[None, :] > experts[:, None]) & (cnt[None, :] > 0)
    next_expert = jnp.min(jnp.where(later, experts[None, :], N_EXPERTS), axis=1)
    next_expert = jnp.where(next_expert == N_EXPERTS, -1, next_expert)
    run_next = jnp.sum(jnp.where(block_expert[:, None] == experts[None, :], next_expert[None, :], 0), axis=1)
    ys = _experts(block_expert, n_used, run_first, run_next.astype(jnp.int32), run_parity.astype(jnp.int32),
                  x_sorted, expert_w1[0], expert_w3[0], expert_w2[0])

    pos_c = _tile_major(pos, COMBINE_TILE)
    sw1, sw3, sw2 = shared_w1[0].astype(_BF16), shared_w3[0].astype(_BF16), shared_w2[0].astype(_BF16)
    fn = final_norm.reshape(1, D_MODEL)
    y_prompt = _combine(pos_c, gate_t, h2, x1, mod3, sw1, sw3, sw2, fn, ys,
                        n_rows=M_CTX, tile0=0, mod_row=lambda i: 0)
    lat_tiles = DEC_SEQ // COMBINE_TILE
    y_sample = _combine(pos_c, gate_t, h2, x1, mod3, sw1, sw3, sw2, fn, ys,
                        n_rows=M_LAT, tile0=M_CTX // COMBINE_TILE, mod_row=lambda i: 1 + i // lat_tiles)

    new_cache_k = ak[:M_CTX].reshape(BATCH, 1, SEQ, ATT_KV_HEADS, HEAD_DIM)
    new_cache_v = av[:M_CTX].reshape(BATCH, 1, SEQ, ATT_KV_HEADS, HEAD_DIM)
    return (y_prompt.reshape(BATCH, SEQ, D_MODEL), y_sample.reshape(DEC_BATCH, DEC_SEQ, D_MODEL),
            new_cache_k, new_cache_v, new_state.reshape(BATCH, 1, 2, DN_HEADS, HEAD_DIM, HEAD_DIM))
```

```python
import functools

import jax
import jax.numpy as jnp
import numpy as np
from jax import lax
from jax.experimental import pallas as pl
from jax.experimental.pallas import tpu as pltpu

D_MODEL = 1024
BATCH = 32
SEQ = 256
DEC_BATCH = 8
DEC_SEQ = 2048
PAST_LEN = 512
GRID_W = 64
HEAD_DIM = 128
DN_HEADS = 4
DN_WIDTH = DN_HEADS * HEAD_DIM
DN_CONV = 5
DN_CHUNK = 64
ATT_HEADS = 4
ATT_KV_HEADS = 2
ATT_GROUP = ATT_HEADS // ATT_KV_HEADS
ATT_Q = ATT_HEADS * HEAD_DIM
ATT_KV = ATT_KV_HEADS * HEAD_DIM
ATT_BLOCK = 128
ROPE_THETA = 10000.0
N_EXPERTS = 256
TOP_K = 8
N_GROUPS = 8
TOPK_GROUPS = 4
GROUP_SIZE = N_EXPERTS // N_GROUPS
EXPERT_FF = D_MODEL // 4
ROUTED_SCALE = 2.5
NORM_EPS = 1e-6
NEG_INF = -1e30

M_CTX = BATCH * SEQ
M_LAT = DEC_BATCH * DEC_SEQ
M_ALL = M_CTX + M_LAT
N_MOD = 16
TOK_TILE = 512
N_CTX_TILES = M_CTX // TOK_TILE
N_TILES = M_ALL // TOK_TILE
EXPERT_ROWS = 256
N_PAIRS = M_ALL * TOP_K
N_BLOCKS = (N_PAIRS + N_EXPERTS * (EXPERT_ROWS - 1)) // EXPERT_ROWS
N_SLOTS = N_BLOCKS * EXPERT_ROWS
COMBINE_TILE = 256
V7X_VMEM_BYTES = 64 * 1024 * 1024
VMEM_LIMIT = V7X_VMEM_BYTES - 8 * 1024 * 1024

_BF16 = jnp.bfloat16
_F32 = jnp.float32


def _dot(a, b):
    return jnp.dot(a.astype(_BF16), b.astype(_BF16), preferred_element_type=_F32)


def _dot_nt(a, b):
    return lax.dot_general(a.astype(_BF16), b.astype(_BF16), (((1,), (1,)), ((), ())),
                           preferred_element_type=_F32)


def _dot_tn(a, b):
    return lax.dot_general(a.astype(_BF16), b.astype(_BF16), (((0,), (0,)), ((), ())),
                           preferred_element_type=_F32)


def _silu(x):
    return x * jax.nn.sigmoid(x)


def _rms(x, w):
    return x * lax.rsqrt(jnp.mean(x * x, axis=-1, keepdims=True) + NORM_EPS) * w


def _params(n_axes=1):
    return pltpu.CompilerParams(dimension_semantics=("arbitrary",) * n_axes, vmem_limit_bytes=VMEM_LIMIT)


def _tile_mod_row(i):
    return jnp.where(i < N_CTX_TILES, 0, 1 + (i - N_CTX_TILES) // (DEC_SEQ // TOK_TILE))


def _ctx_tile_mask(shape):
    limit = jnp.where(pl.program_id(0) < N_CTX_TILES, shape[0], 0)
    return lax.broadcasted_iota(jnp.int32, shape, 0) < limit


TOKEN_TILE_ROWS = D_MODEL // 128


def _store_token_tiles(ref, x, n):
    for s in range(TOKEN_TILE_ROWS):
        ref[pl.ds(s, n, stride=TOKEN_TILE_ROWS), :] = x[:, s * 128:(s + 1) * 128]


def _load_token_tiles(ref, n):
    return jnp.concatenate([ref[pl.ds(s, n, stride=TOKEN_TILE_ROWS), :] for s in range(TOKEN_TILE_ROWS)], axis=1)


def _x_specs():
    return [
        pl.BlockSpec((TOK_TILE, D_MODEL), lambda i: (jnp.minimum(i, N_CTX_TILES - 1), 0)),
        pl.BlockSpec((TOK_TILE, D_MODEL), lambda i: (jnp.maximum(i - N_CTX_TILES, 0), 0)),
    ]


def _mod_kernel(c_ref, w_ref, b_ref, o_ref):
    o_ref[...] = _dot(_silu(c_ref[...]), w_ref[...]) + b_ref[...]


def _modulation(cvec, w_ada, b_ada):
    tn = 1024
    return pl.pallas_call(
        _mod_kernel,
        grid=(6 * D_MODEL // tn,),
        in_specs=[pl.BlockSpec((N_MOD, D_MODEL), lambda j: (0, 0)),
                  pl.BlockSpec((D_MODEL, tn), lambda j: (0, j)),
                  pl.BlockSpec((1, tn), lambda j: (0, j))],
        out_specs=pl.BlockSpec((N_MOD, tn), lambda j: (0, j)),
        out_shape=jax.ShapeDtypeStruct((N_MOD, 6 * D_MODEL), _F32),
        compiler_params=_params(),
        name="modulation",
    )(cvec, w_ada, b_ada)


def _rope(x, cos, sin, first_half):
    swapped = jnp.where(first_half, pltpu.roll(x, 96, 1), pltpu.roll(x, 32, 1))
    return x * cos + swapped * sin


def _inproj_kernel(xp_ref, xs_ref, mod_ref, n1_ref, cos_ref, sin_ref, wdn_ref, watt_ref, wab_ref,
                   dn_ref, aq_ref, ak_ref, av_ref, ab_ref):
    x = jnp.where(_ctx_tile_mask((TOK_TILE, D_MODEL)), xp_ref[...], xs_ref[...])
    shift = mod_ref[0, 0:1, :]
    scale = mod_ref[0, 1:2, :]
    h = (_rms(x, n1_ref[...]) * (1 + scale) + shift).astype(_BF16)
    dn_ref[...] = jnp.dot(h, wdn_ref[...], preferred_element_type=_F32).astype(_BF16)
    ab_ref[...] = jnp.dot(h, wab_ref[...], preferred_element_type=_F32)
    att = jnp.dot(h, watt_ref[...], preferred_element_type=_F32)
    cos = cos_ref[...]
    sin = sin_ref[...]
    lane = lax.broadcasted_iota(jnp.int32, (TOK_TILE, HEAD_DIM), 1)
    first_half = (lane % 64) < 32
    for hd in range(ATT_HEADS):
        q = att[:, hd * HEAD_DIM:(hd + 1) * HEAD_DIM]
        aq_ref[:, hd * HEAD_DIM:(hd + 1) * HEAD_DIM] = (
            _rope(q, cos, sin, first_half) * HEAD_DIM ** -0.5).astype(_BF16)
    for hd in range(ATT_KV_HEADS):
        k = att[:, ATT_Q + hd * HEAD_DIM:ATT_Q + (hd + 1) * HEAD_DIM]
        ak_ref[:, hd * HEAD_DIM:(hd + 1) * HEAD_DIM] = _rope(k, cos, sin, first_half)
    av_ref[...] = att[:, ATT_Q + ATT_KV:]


def _rope_tables():
    t = jnp.arange(DEC_SEQ)
    row = (t // GRID_W).astype(_F32)
    col = (t % GRID_W).astype(_F32)
    n_freq = HEAD_DIM // 4
    inv_freq = 1.0 / (ROPE_THETA ** (jnp.arange(n_freq, dtype=_F32) / n_freq))
    ang_r = row[:, None] * inv_freq
    ang_c = col[:, None] * inv_freq
    cos = jnp.concatenate([jnp.cos(ang_r), jnp.cos(ang_r), jnp.cos(ang_c), jnp.cos(ang_c)], axis=1)
    sin = jnp.concatenate([-jnp.sin(ang_r), jnp.sin(ang_r), -jnp.sin(ang_c), jnp.sin(ang_c)], axis=1)
    cos = jnp.concatenate([jnp.ones((TOK_TILE, HEAD_DIM), _F32), cos], axis=0)
    sin = jnp.concatenate([jnp.zeros((TOK_TILE, HEAD_DIM), _F32), sin], axis=0)
    return cos, sin


def _input_projection(xp, xs, mod3, norm1, cos, sin, w_dn, w_att, w_ab):
    def rope_idx(i):
        return (jnp.where(i < N_CTX_TILES, 0, 1 + (i - N_CTX_TILES) % (DEC_SEQ // TOK_TILE)), 0)

    const = lambda i: (0, 0)
    row = lambda i: (i, 0)
    return pl.pallas_call(
        _inproj_kernel,
        grid=(N_TILES,),
        in_specs=_x_specs() + [
            pl.BlockSpec((1, 6, D_MODEL), lambda i: (_tile_mod_row(i), 0, 0)),
            pl.BlockSpec((1, D_MODEL), const),
            pl.BlockSpec((TOK_TILE, HEAD_DIM), rope_idx),
            pl.BlockSpec((TOK_TILE, HEAD_DIM), rope_idx),
            pl.BlockSpec((D_MODEL, 4 * DN_WIDTH), const),
            pl.BlockSpec((D_MODEL, ATT_Q + 2 * ATT_KV), const),
            pl.BlockSpec((D_MODEL, 128), const),
        ],
        out_specs=[
            pl.BlockSpec((TOK_TILE, 4 * DN_WIDTH), row),
            pl.BlockSpec((TOK_TILE, ATT_Q), row),
            pl.BlockSpec((TOK_TILE, ATT_KV), row),
            pl.BlockSpec((TOK_TILE, ATT_KV), row),
            pl.BlockSpec((TOK_TILE, 128), row),
        ],
        out_shape=[
            jax.ShapeDtypeStruct((M_ALL, 4 * DN_WIDTH), _BF16),
            jax.ShapeDtypeStruct((M_ALL, ATT_Q), _BF16),
            jax.ShapeDtypeStruct((M_ALL, ATT_KV), _F32),
            jax.ShapeDtypeStruct((M_ALL, ATT_KV), _F32),
            jax.ShapeDtypeStruct((M_ALL, 128), _F32),
        ],
        compiler_params=_params(),
        name="input_projection",
    )(xp, xs, mod3, norm1, cos, sin, w_dn, w_att, w_ab)


DN_STACK = 16
DN_LONG_CHUNKS = 8


def _dn_pass_heads(n_chunk):
    return DN_HEADS if n_chunk <= DN_LONG_CHUNKS else DN_HEADS // 2


def _dn_stacking(n_chunk):
    group = min(n_chunk, DN_STACK // 2)
    return group, DN_STACK // (2 * group)


def _bdot(a, b):
    return jnp.stack([_dot(a[g], b[g]) for g in range(a.shape[0])])


def _bdot_nt(a, b):
    return jnp.stack([_dot_nt(a[g], b[g]) for g in range(a.shape[0])])


def _inverse_unit_triangular(a, eye):
    p = eye - a
    x = a
    for _ in range(5):
        x = _bdot(x, x)
        p = p + _bdot(p, x)
    return p


def _dn_prepare(q, k, v, gc, gc_row, beta, g_tot, incl, strict, eye):
    decay = jnp.where(incl, jnp.exp(jnp.where(incl, gc - gc_row, 0.0)), 0.0)
    kb = k * beta
    a = jnp.where(strict, _bdot_nt(kb, k) * decay, 0.0)
    t_inv = _inverse_unit_triangular(a, eye)
    eg = jnp.exp(gc)
    u = _bdot(t_inv, v * beta)
    w = _bdot(t_inv, kb * eg)
    qk = _bdot_nt(q, k) * decay
    wq = jnp.concatenate([w, q * eg], axis=1).astype(_BF16)
    kd = k * jnp.exp(g_tot - gc)
    kd_t = jnp.stack([kd[g].T for g in range(kd.shape[0])])
    qkk = jnp.concatenate([qk, kd_t], axis=1).astype(_BF16)
    return u, wq, qkk


def _dn_steps(states, us, wqs, qkks, g_tots):
    wss = [_dot(wq, s) for wq, s in zip(wqs, states)]
    v_news = [u - ws[:DN_CHUNK] for u, ws in zip(us, wss)]
    rs = [_dot(qkk, v_new) for qkk, v_new in zip(qkks, v_news)]
    outs = [ws[DN_CHUNK:] + r[:DN_CHUNK] for ws, r in zip(wss, rs)]
    states = [s * jnp.exp(g) + r[DN_CHUNK:] for s, g, r in zip(states, g_tots, rs)]
    return states, outs


def _dn_kernel(*refs, seq_len, has_s0, emit_state):
    dn_ref, ab_ref, cw_ref, alog_ref, dtb_ref, nw_ref = refs[:6]
    pos = 6
    s0_ref = None
    if has_s0:
        s0_ref = refs[pos]
        pos += 1
    o_ref = refs[pos]
    pos += 1
    st_ref = None
    if emit_state:
        st_ref = refs[pos]
        pos += 1
    pad_scr, q_scr, k_scr, v_scr, gate_scr, grow_scr, o_scr, u_scr, wq_scr, qkk_scr, s_scr = refs[pos:]

    T = seq_len
    C = DN_CHUNK
    n_chunk = T // C
    n_pair = n_chunk // 2
    G, stack_heads = _dn_stacking(n_chunk)
    n_group = n_chunk // G

    gates = ab_ref[...]
    lane = lax.broadcasted_iota(jnp.int32, (T, 128), 1)
    tpos = lax.broadcasted_iota(jnp.int32, (T, 128), 0) % C
    sp_arg = gates + dtb_ref[...]
    softplus = jnp.maximum(sp_arg, 0.0) + jnp.log1p(jnp.exp(-jnp.abs(sp_arg)))
    g = -jnp.exp(alog_ref[...]) * softplus
    pre = g
    suf = g
    s = 1
    while s < C:
        pre = pre + jnp.where(tpos >= s, pltpu.roll(pre, s, 0), 0.0)
        suf = suf + jnp.where(tpos < C - s, pltpu.roll(suf, T - s, 0), 0.0)
        s *= 2
    gcum = jnp.where(lane < DN_HEADS, pre, suf)
    gate_scr[...] = jnp.where(lane < 2 * DN_HEADS, gcum, jax.nn.sigmoid(gates))
    gcum_t = gcum.T
    for p in range(n_pair):
        grow_scr[p] = gcum_t[0:8, p * 128:(p + 1) * 128]

    pad_scr[0:8, :] = jnp.zeros((8, HEAD_DIM), _F32)
    pad_scr[8 + T:16 + T, :] = jnp.zeros((8, HEAD_DIM), _F32)

    shape3 = (DN_STACK, C, C)
    ri = lax.broadcasted_iota(jnp.int32, shape3, 1)
    ci = lax.broadcasted_iota(jnp.int32, shape3, 2)
    eye = (ri == ci).astype(_F32)
    is_fwd = (lax.broadcasted_iota(jnp.int32, shape3, 0) // G) % 2 == 0
    offset = jnp.where(is_fwd, ci - ri, ri - ci)
    incl = offset <= 0
    strict = offset < 0

    def conv_silu(part, h):
        c0 = part * DN_WIDTH + h * HEAD_DIM
        pad_scr[8:8 + T, :] = dn_ref[:, c0:c0 + HEAD_DIM].astype(_F32)
        acc = cw_ref[0:1, c0:c0 + HEAD_DIM] * pad_scr[6:6 + T, :]
        for j in range(1, DN_CONV):
            acc = acc + cw_ref[j:j + 1, c0:c0 + HEAD_DIM] * pad_scr[6 + j:6 + j + T, :]
        return _silu(acc)

    def l2n(x):
        return x * lax.rsqrt(jnp.sum(x * x, axis=-1, keepdims=True) + NORM_EPS)

    def total_decay(h, direction, r0):
        col = direction * DN_HEADS + h
        row = r0 + C - 1 if direction == 0 else r0
        return gate_scr[pl.ds(row, 1), col:col + 1]

    def prepare_group(grp, heads, chain0):
        r0 = pl.multiple_of(grp * G * C, G * C)
        sl = pl.ds(r0, G * C)
        stacked = lambda x: x.reshape(G, C, x.shape[-1])
        qs, ks, vs, gcs, betas, rows, g_tots = [], [], [], [], [], [], []
        for hh, h in enumerate(heads):
            for direction in range(2):
                col = direction * DN_HEADS + h
                bcol = 2 * DN_HEADS + col
                gc = stacked(gate_scr[sl, col:col + 1])
                gcs.append(gc)
                betas.append(stacked(gate_scr[sl, bcol:bcol + 1]))
                for pair in range(G // 2):
                    both = grow_scr[grp * (G // 2) + pair][col:col + 1, :]
                    rows += [both[:, :C], both[:, C:]]
                g_tots.append(gc[:, C - 1:C, :] if direction == 0 else gc[:, 0:1, :])
                qs.append(stacked(q_scr[hh, sl, :]))
                ks.append(stacked(k_scr[hh, sl, :]))
                vs.append(stacked(v_scr[hh, sl, :]))
        cat = lambda xs: jnp.concatenate(xs, axis=0)
        u, wq, qkk = _dn_prepare(cat(qs), cat(ks), cat(vs), cat(gcs), jnp.stack(rows), cat(betas), cat(g_tots),
                                 incl, strict, eye)
        for local in range(2 * len(heads)):
            chain = chain0 + local
            part = slice(local * G, (local + 1) * G)
            slot0 = chain * n_chunk + grp * G
            u_scr[chain, sl, :] = u[part].reshape(G * C, HEAD_DIM)
            wq_scr[pl.ds(slot0, G)] = wq[part]
            qkk_scr[pl.ds(slot0, G)] = qkk[part]

    def steps(chain_heads, c):
        slots, rows, g_tots = [], [], []
        for chain, h in enumerate(chain_heads):
            direction = chain % 2
            chunk = c if direction == 0 else n_chunk - 1 - c
            r0 = pl.multiple_of(chunk * C, C)
            slots.append(chain * n_chunk + chunk)
            rows.append(pl.ds(r0, C))
            g_tots.append(total_decay(h, direction, r0))
        chains = range(len(chain_heads))
        states, outs = _dn_steps([s_scr[ch] for ch in chains], [u_scr[ch, rows[ch], :] for ch in chains],
                                 [wq_scr[slots[ch]] for ch in chains], [qkk_scr[slots[ch]] for ch in chains], g_tots)
        for ch in chains:
            s_scr[ch] = states[ch]
            o_scr[ch, rows[ch], :] = outs[ch]

    pass_heads = _dn_pass_heads(n_chunk)
    for h0 in range(0, DN_HEADS, pass_heads):
        heads = range(h0, h0 + pass_heads)
        for h1 in range(h0, h0 + pass_heads, stack_heads):
            stack = range(h1, h1 + stack_heads)
            for hh, h in enumerate(stack):
                q_scr[hh] = l2n(conv_silu(0, h)) * HEAD_DIM ** -0.5
                k_scr[hh] = l2n(conv_silu(1, h))
                v_scr[hh] = conv_silu(2, h)

            def prepare_body(grp, carry, stack=stack, h1=h1):
                prepare_group(grp, stack, (h1 - h0) * 2)
                return carry

            lax.fori_loop(0, n_group, prepare_body, 0)
        for h in heads:
            for direction in range(2):
                chain = (h - h0) * 2 + direction
                if has_s0:
                    s_scr[chain] = s0_ref[0, direction, h].astype(_F32)
                else:
                    s_scr[chain] = jnp.zeros((HEAD_DIM, HEAD_DIM), _F32)

        chain_heads = [h for h in heads for _ in range(2)]

        def step_body(c, carry):
            steps(chain_heads, c)
            return carry

        lax.fori_loop(0, n_chunk, step_body, 0)

        for h in heads:
            chain = (h - h0) * 2
            if emit_state:
                st_ref[0, 0, h] = s_scr[chain]
                st_ref[0, 1, h] = s_scr[chain + 1]
            o = o_scr[chain] + o_scr[chain + 1]
            z = dn_ref[:, 3 * DN_WIDTH + h * HEAD_DIM:3 * DN_WIDTH + (h + 1) * HEAD_DIM].astype(_F32)
            o = o * lax.rsqrt(jnp.mean(o * o, axis=-1, keepdims=True) + NORM_EPS) * nw_ref[...] * _silu(z)
            o_ref[:, h * HEAD_DIM:(h + 1) * HEAD_DIM] = o.astype(_BF16)


def _deltanet(dn, ab, conv_w, a_log, dt_bias, dn_norm, s0, *, n_seq, seq_len, row_block0, emit_state):
    has_s0 = s0 is not None
    const = lambda b: (0, 0)
    state_spec = pl.BlockSpec((1, 2, DN_HEADS, HEAD_DIM, HEAD_DIM), lambda b: (b, 0, 0, 0, 0))
    in_specs = [
        pl.BlockSpec((seq_len, 4 * DN_WIDTH), lambda b: (row_block0 + b, 0)),
        pl.BlockSpec((seq_len, 128), lambda b: (row_block0 + b, 0)),
        pl.BlockSpec((8, 3 * DN_WIDTH), const),
        pl.BlockSpec((1, 128), const),
        pl.BlockSpec((1, 128), const),
        pl.BlockSpec((1, HEAD_DIM), const),
    ]
    args = [dn, ab, conv_w, a_log, dt_bias, dn_norm]
    if has_s0:
        in_specs.append(state_spec)
        args.append(s0)
    out_specs = [pl.BlockSpec((seq_len, DN_WIDTH), lambda b: (b, 0))]
    out_shape = [jax.ShapeDtypeStruct((n_seq * seq_len, DN_WIDTH), _BF16)]
    if emit_state:
        out_specs.append(state_spec)
        out_shape.append(jax.ShapeDtypeStruct((n_seq, 2, DN_HEADS, HEAD_DIM, HEAD_DIM), _F32))
    n_chunk = seq_len // DN_CHUNK
    pass_tile = (_dn_stacking(n_chunk)[1], seq_len, HEAD_DIM)
    n_chain = 2 * _dn_pass_heads(n_chunk)
    return pl.pallas_call(
        functools.partial(_dn_kernel, seq_len=seq_len, has_s0=has_s0, emit_state=emit_state),
        grid=(n_seq,),
        in_specs=in_specs,
        out_specs=out_specs,
        out_shape=out_shape,
        scratch_shapes=[
            pltpu.VMEM((seq_len + 16, HEAD_DIM), _F32),
            pltpu.VMEM(pass_tile, _F32), pltpu.VMEM(pass_tile, _F32), pltpu.VMEM(pass_tile, _F32),
            pltpu.VMEM((seq_len, 128), _F32),
            pltpu.VMEM((seq_len // (2 * DN_CHUNK), 8, 128), _F32),
            pltpu.VMEM((n_chain, seq_len, HEAD_DIM), _F32),
            pltpu.VMEM((n_chain, seq_len, HEAD_DIM), _F32),
            pltpu.VMEM((n_chain * n_chunk, 2 * DN_CHUNK, HEAD_DIM), _BF16),
            pltpu.VMEM((n_chain * n_chunk, DN_CHUNK + HEAD_DIM, DN_CHUNK), _BF16),
            pltpu.VMEM((n_chain, HEAD_DIM, HEAD_DIM), _F32),
        ],
        compiler_params=_params(),
        name="deltanet_ctx" if emit_state else "deltanet_lat",
    )(*args)


def _softmax_av(scores, values, sink):
    m = sink
    for s in scores:
        m = jnp.maximum(m, jnp.max(s, axis=-1, keepdims=True))
    denom = jnp.exp(sink - m)
    acc = None
    for s, v in zip(scores, values):
        p = jnp.exp(s - m)
        denom = denom + jnp.sum(p, axis=-1, keepdims=True)
        pv = _dot(p, v)
        acc = pv if acc is None else acc + pv
    return acc / denom


def _ctx_attn_kernel(sink_ref, q_ref, k_ref, v_ref, o_ref):
    kvh = pl.program_id(1)
    k = k_ref[...]
    v = v_ref[...]
    for g in range(ATT_GROUP):
        q = q_ref[:, g * HEAD_DIM:(g + 1) * HEAD_DIM]
        sink = jnp.full((SEQ, 1), sink_ref[kvh * ATT_GROUP + g], _F32)
        o = _softmax_av([_dot_nt(q, k)], [v], sink)
        o_ref[:, g * HEAD_DIM:(g + 1) * HEAD_DIM] = o.astype(_BF16)


def _context_attention(sinks, aq, ak, av):
    return pl.pallas_call(
        _ctx_attn_kernel,
        grid=(BATCH, ATT_KV_HEADS),
        in_specs=[
            pl.BlockSpec(memory_space=pltpu.SMEM),
            pl.BlockSpec((SEQ, ATT_GROUP * HEAD_DIM), lambda b, h: (b, h)),
            pl.BlockSpec((SEQ, HEAD_DIM), lambda b, h: (b, h)),
            pl.BlockSpec((SEQ, HEAD_DIM), lambda b, h: (b, h)),
        ],
        out_specs=pl.BlockSpec((SEQ, ATT_GROUP * HEAD_DIM), lambda b, h: (b, h)),
        out_shape=jax.ShapeDtypeStruct((M_CTX, ATT_Q), _BF16),
        compiler_params=_params(2),
        name="context_attention",
    )(sinks, aq, ak, av)


LAT_Q_BLOCK = 2 * ATT_BLOCK


def _lat_attn_kernel(sink_ref, q_ref, kp_ref, kc_ref, kn_ref, vp_ref, vc_ref, vn_ref, ck_ref, cv_ref, o_ref):
    kvh = pl.program_id(1)
    i = pl.program_id(2)
    nb = pl.num_programs(2)
    B, Q = ATT_BLOCK, LAT_Q_BLOCK
    rows = ATT_GROUP * Q
    q = jnp.concatenate([q_ref[:, g * HEAD_DIM:(g + 1) * HEAD_DIM] for g in range(ATT_GROUP)], axis=0)
    r_b = lax.broadcasted_iota(jnp.int32, (rows, B), 0) % Q
    c_b = lax.broadcasted_iota(jnp.int32, (rows, B), 1)
    r_q = lax.broadcasted_iota(jnp.int32, (rows, Q), 0) % Q
    c_q = lax.broadcasted_iota(jnp.int32, (rows, Q), 1)
    s_prev = jnp.where(c_b >= r_b + jnp.where(i > 0, 0, Q), _dot_nt(q, kp_ref[...]), NEG_INF)
    s_cur = jnp.where(jnp.abs(r_q - c_q) <= B, _dot_nt(q, kc_ref[...]), NEG_INF)
    s_next = jnp.where(c_b <= r_b - B - jnp.where(i < nb - 1, 0, Q), _dot_nt(q, kn_ref[...]), NEG_INF)
    s_ctx = _dot_nt(q, ck_ref[...])
    head = lax.broadcasted_iota(jnp.int32, (rows, 1), 0) // Q
    sink = jnp.zeros((rows, 1), _F32)
    for g in range(ATT_GROUP):
        sink = jnp.where(head == g, sink_ref[kvh * ATT_GROUP + g], sink)
    o = _softmax_av([s_prev, s_cur, s_next, s_ctx], [vp_ref[...], vc_ref[...], vn_ref[...], cv_ref[...]], sink)
    for g in range(ATT_GROUP):
        o_ref[:, g * HEAD_DIM:(g + 1) * HEAD_DIM] = o[g * Q:(g + 1) * Q].astype(_BF16)


def _latent_attention(sinks, aq, ak, av, ctx_k, ctx_v):
    nq = DEC_SEQ // LAT_Q_BLOCK
    nb = DEC_SEQ // ATT_BLOCK
    q0 = M_CTX // LAT_Q_BLOCK
    b0 = M_CTX // ATT_BLOCK
    cur = lambda b, h, i: (q0 + b * nq + i, h)
    prev = lambda b, h, i: (b0 + b * nb + jnp.maximum(2 * i - 1, 0), h)
    nxt = lambda b, h, i: (b0 + b * nb + jnp.minimum(2 * i + 2, nb - 1), h)
    side_spec = lambda f: pl.BlockSpec((ATT_BLOCK, HEAD_DIM), f)
    cur_spec = pl.BlockSpec((LAT_Q_BLOCK, HEAD_DIM), cur)
    ctx_spec = pl.BlockSpec((PAST_LEN, HEAD_DIM), lambda b, h, i: (b, h))
    return pl.pallas_call(
        _lat_attn_kernel,
        grid=(DEC_BATCH, ATT_KV_HEADS, nq),
        in_specs=[
            pl.BlockSpec(memory_space=pltpu.SMEM),
            pl.BlockSpec((LAT_Q_BLOCK, ATT_GROUP * HEAD_DIM), cur),
            side_spec(prev), cur_spec, side_spec(nxt),
            side_spec(prev), cur_spec, side_spec(nxt),
            ctx_spec, ctx_spec,
        ],
        out_specs=pl.BlockSpec((LAT_Q_BLOCK, ATT_GROUP * HEAD_DIM), lambda b, h, i: (b * nq + i, h)),
        out_shape=jax.ShapeDtypeStruct((M_LAT, ATT_Q), _BF16),
        compiler_params=_params(3),
        name="latent_attention",
    )(sinks, aq, ak, ak, ak, av, av, av, ctx_k, ctx_v)


def _outproj_kernel(xp_ref, xs_ref, dnc_ref, dnl_ref, atc_ref, atl_ref, mod_ref, n2_ref, wo_dn_ref, wo_at_ref,
                    rw_ref, x1_ref, h2_ref, h2p_ref, lg_ref):
    x = jnp.where(_ctx_tile_mask((TOK_TILE, D_MODEL)), xp_ref[...], xs_ref[...])
    half_mask = _ctx_tile_mask((TOK_TILE, DN_WIDTH))
    dn = jnp.where(half_mask, dnc_ref[...], dnl_ref[...])
    at = jnp.where(half_mask, atc_ref[...], atl_ref[...])
    mixed = (jnp.dot(dn, wo_dn_ref[...], preferred_element_type=_F32)
             + jnp.dot(at, wo_at_ref[...], preferred_element_type=_F32))
    x1 = x + mod_ref[0, 2:3, :] * mixed
    x1_ref[...] = x1
    h = _rms(x1, n2_ref[...]) * (1 + mod_ref[0, 4:5, :]) + mod_ref[0, 3:4, :]
    h2_ref[...] = h.astype(_BF16)
    _store_token_tiles(h2p_ref, h, TOK_TILE)
    w = rw_ref[...]
    w_hi = w.astype(_BF16)
    w_lo = (w - w_hi.astype(_F32)).astype(_BF16)
    h_hi = h.astype(_BF16)
    h_lo = (h - h_hi.astype(_F32)).astype(_BF16)
    lg_ref[...] = (_dot_nt(w_hi, h_hi) + _dot_nt(w_hi, h_lo)) + _dot_nt(w_lo, h_hi)


def _output_projection(xp, xs, dn_c, dn_l, at_c, at_l, mod3, norm2, wo_dn, wo_at, router_wt):
    const = lambda i: (0, 0)
    row = lambda i: (i, 0)
    ctx_row = lambda i: (jnp.minimum(i, N_CTX_TILES - 1), 0)
    lat_row = lambda i: (jnp.maximum(i - N_CTX_TILES, 0), 0)
    half = (TOK_TILE, DN_WIDTH)
    return pl.pallas_call(
        _outproj_kernel,
        grid=(N_TILES,),
        in_specs=_x_specs() + [
            pl.BlockSpec(half, ctx_row), pl.BlockSpec(half, lat_row),
            pl.BlockSpec(half, ctx_row), pl.BlockSpec(half, lat_row),
            pl.BlockSpec((1, 6, D_MODEL), lambda i: (_tile_mod_row(i), 0, 0)),
            pl.BlockSpec((1, D_MODEL), const),
            pl.BlockSpec((DN_WIDTH, D_MODEL), const),
            pl.BlockSpec((ATT_Q, D_MODEL), const),
            pl.BlockSpec((N_EXPERTS, D_MODEL), const),
        ],
        out_specs=[
            pl.BlockSpec((TOK_TILE, D_MODEL), row),
            pl.BlockSpec((TOK_TILE, D_MODEL), row),
            pl.BlockSpec((TOK_TILE * TOKEN_TILE_ROWS, 128), row),
            pl.BlockSpec((N_EXPERTS, TOK_TILE), lambda i: (0, i)),
        ],
        out_shape=[
            jax.ShapeDtypeStruct((M_ALL, D_MODEL), _F32),
            jax.ShapeDtypeStruct((M_ALL, D_MODEL), _BF16),
            jax.ShapeDtypeStruct((M_ALL * TOKEN_TILE_ROWS, 128), _F32),
            jax.ShapeDtypeStruct((N_EXPERTS, M_ALL), _F32),
        ],
        compiler_params=_params(),
        name="output_projection",
    )(xp, xs, dn_c, dn_l, at_c, at_l, mod3, norm2, wo_dn, wo_at, router_wt)


def _first_index_of(values, target, index, limit):
    return jnp.min(jnp.where(values == target, index, limit), axis=0, keepdims=True)


def _route_kernel(lg_ref, bias_ref, idx_ref, rank_ref, gate_ref, cnt_ref, carry_scr):
    i = pl.program_id(0)
    tm = TOK_TILE

    @pl.when(i == 0)
    def _():
        carry_scr[...] = jnp.zeros_like(carry_scr)

    scores = jax.nn.sigmoid(lg_ref[...])
    biased = scores + bias_ref[...]
    row = lax.broadcasted_iota(jnp.int32, (N_EXPERTS, tm), 0).astype(_F32)
    grow = lax.broadcasted_iota(jnp.int32, (GROUP_SIZE, tm), 0).astype(_F32)

    group_rows = []
    for g in range(N_GROUPS):
        blk = biased[g * GROUP_SIZE:(g + 1) * GROUP_SIZE]
        m1 = jnp.max(blk, axis=0, keepdims=True)
        i1 = _first_index_of(blk, m1, grow, GROUP_SIZE)
        m2 = jnp.max(jnp.where(grow == i1, -jnp.inf, blk), axis=0, keepdims=True)
        group_rows.append(m1 + m2)
    gs = jnp.concatenate(group_rows, axis=0)
    gi = lax.broadcasted_iota(jnp.int32, (N_GROUPS, tm), 0).astype(_F32)
    gself = jnp.zeros((N_GROUPS, tm), _F32)
    for _ in range(TOPK_GROUPS):
        hit = gi == _first_index_of(gs, jnp.max(gs, axis=0, keepdims=True), gi, N_GROUPS)
        gself = jnp.where(hit, 1.0, gself)
        gs = jnp.where(hit, -jnp.inf, gs)
    emask = jnp.concatenate(
        [jnp.broadcast_to(gself[g:g + 1], (GROUP_SIZE, tm)) for g in range(N_GROUPS)], axis=0) > 0.5
    masked = jnp.where(emask, biased, NEG_INF)

    selected = jnp.zeros((N_EXPERTS, tm), _F32)
    idxs, gates = [], []
    for _ in range(TOP_K):
        ei = _first_index_of(masked, jnp.max(masked, axis=0, keepdims=True), row, N_EXPERTS)
        hit = row == ei
        idxs.append(ei)
        gates.append(jnp.sum(jnp.where(hit, scores, 0.0), axis=0, keepdims=True))
        masked = jnp.where(hit, -jnp.inf, masked)
        selected = jnp.where(hit, 1.0, selected)
    gsum = gates[0]
    for gk in gates[1:]:
        gsum = gsum + gk
    gates = [gk / gsum * ROUTED_SCALE for gk in gates]

    si = lax.broadcasted_iota(jnp.int32, (tm, tm), 0)
    ti = lax.broadcasted_iota(jnp.int32, (tm, tm), 1)
    earlier = (si < ti).astype(_BF16)
    ranks_all = _dot(selected, earlier) + carry_scr[...]
    ranks = [jnp.sum(jnp.where(row == ei, ranks_all, 0.0), axis=0, keepdims=True) for ei in idxs]
    carry_scr[...] = carry_scr[...] + jnp.sum(selected, axis=1, keepdims=True)

    idx_ref[...] = jnp.concatenate(idxs, axis=0).astype(jnp.int32)
    rank_ref[...] = jnp.concatenate(ranks, axis=0).astype(jnp.int32)
    gate_rows = jnp.concatenate(gates + [jnp.zeros((128 - TOP_K, tm), _F32)], axis=0)
    gate_ref[...] = gate_rows.T
    cnt_ref[...] = jnp.broadcast_to(carry_scr[...], (N_EXPERTS, 128))


def _route(logits_t, bias_col):
    return pl.pallas_call(
        _route_kernel,
        grid=(N_TILES,),
        in_specs=[pl.BlockSpec((N_EXPERTS, TOK_TILE), lambda i: (0, i)),
                  pl.BlockSpec((N_EXPERTS, 1), lambda i: (0, 0))],
        out_specs=[
            pl.BlockSpec((TOP_K, TOK_TILE), lambda i: (0, i)),
            pl.BlockSpec((TOP_K, TOK_TILE), lambda i: (0, i)),
            pl.BlockSpec((TOK_TILE, 128), lambda i: (i, 0)),
            pl.BlockSpec((N_EXPERTS, 128), lambda i: (0, 0)),
        ],
        out_shape=[
            jax.ShapeDtypeStruct((TOP_K, M_ALL), jnp.int32),
            jax.ShapeDtypeStruct((TOP_K, M_ALL), jnp.int32),
            jax.ShapeDtypeStruct((M_ALL, 128), _F32),
            jax.ShapeDtypeStruct((N_EXPERTS, 128), _F32),
        ],
        scratch_shapes=[pltpu.VMEM((N_EXPERTS, 1), _F32)],
        compiler_params=_params(),
        name="route",
    )(logits_t, bias_col)


def _slot_kernel(idx_ref, rank_ref, start_ref, pos_ref):
    row = lax.broadcasted_iota(jnp.int32, (N_EXPERTS, TOK_TILE), 0)
    start = start_ref[...]
    rows = []
    for k in range(TOP_K):
        base = jnp.sum(jnp.where(row == idx_ref[k:k + 1, :], start, 0.0), axis=0, keepdims=True)
        rows.append(base.astype(jnp.int32) + rank_ref[k:k + 1, :])
    pos_ref[...] = jnp.concatenate(rows, axis=0)


def _slots(idx, rank, start_col):
    spec = pl.BlockSpec((TOP_K, TOK_TILE), lambda i: (0, i))
    return pl.pallas_call(
        _slot_kernel,
        grid=(N_TILES,),
        in_specs=[spec, spec, pl.BlockSpec((N_EXPERTS, 1), lambda i: (0, 0))],
        out_specs=spec,
        out_shape=jax.ShapeDtypeStruct((TOP_K, M_ALL), jnp.int32),
        compiler_params=_params(),
        name="slots",
    )(idx, rank, start_col)


def _slot_tile(ref, slot):
    return ref.at[pl.ds(pl.multiple_of(slot * TOKEN_TILE_ROWS, TOKEN_TILE_ROWS), TOKEN_TILE_ROWS)]


def _dispatch_kernel(pad_lo_ref, pad_hi_ref, nused_ref, pos_ref, h_hbm, xs_ref, zero_scr, h_buf, in_sem, out_sem,
                     zero_sem):
    i = pl.program_id(0)
    n_steps = pl.num_programs(0)
    block_rows = TOK_TILE * TOKEN_TILE_ROWS

    def fetch(step):
        src = h_hbm.at[pl.ds(pl.multiple_of(step * block_rows, block_rows), block_rows)]
        return pltpu.make_async_copy(src, h_buf.at[step % 3], in_sem.at[step % 3])

    def wait_copies(step):
        for _ in range(TOP_K):
            pltpu.make_async_copy(h_buf.at[0], xs_ref.at[pl.ds(0, block_rows)], out_sem.at[step % 2]).wait()

    @pl.when(i == 0)
    def _():
        fetch(i).start()

    @pl.when(i + 1 < n_steps)
    def _():
        fetch(i + 1).start()

    def zero_fill(start):
        def zero_copy(first_slot, n_slots):
            rows = n_slots * TOKEN_TILE_ROWS
            dst = xs_ref.at[pl.ds(pl.multiple_of(first_slot * TOKEN_TILE_ROWS, TOKEN_TILE_ROWS), rows)]
            return pltpu.make_async_copy(zero_scr.at[pl.ds(0, rows)], dst, zero_sem)

        def expert_padding(e, carry):
            first = pad_lo_ref[e]
            n = pad_hi_ref[e] - first
            size = EXPERT_ROWS // 2
            while size >= 1:
                chunk = n & size

                @pl.when(chunk != 0)
                def _(first=first, size=size):
                    copy = zero_copy(first, size)
                    copy.start() if start else copy.wait()

                first = first + chunk
                size //= 2
            return carry

        def tail_block(t, carry):
            j = N_BLOCKS - 1 - t

            @pl.when(j >= nused_ref[0])
            def _():
                copy = zero_copy(j * EXPERT_ROWS, EXPERT_ROWS)
                copy.start() if start else copy.wait()

            return carry

        lax.fori_loop(0, N_EXPERTS, expert_padding, 0)
        lax.fori_loop(0, N_BLOCKS - N_PAIRS // EXPERT_ROWS, tail_block, 0)

    @pl.when(i == 0)
    def _():
        zero_scr[...] = jnp.zeros_like(zero_scr)
        zero_fill(start=True)

    fetch(i).wait()
    h_tile = h_buf.at[i % 3]

    def body(t, carry):
        src = _slot_tile(h_tile, t)
        for k in range(TOP_K):
            pltpu.make_async_copy(src, _slot_tile(xs_ref, pos_ref[0, k, t]), out_sem.at[i % 2]).start(
                priority=k % 2)
        return carry

    lax.fori_loop(0, TOK_TILE, body, 0, unroll=4)

    @pl.when(i >= 1)
    def _():
        wait_copies(i - 1)

    @pl.when(i == n_steps - 1)
    def _():
        wait_copies(i)
        zero_fill(start=False)


def _dispatch(pad_lo, pad_hi, n_used, pos3, h2p):
    grid_spec = pltpu.PrefetchScalarGridSpec(
        num_scalar_prefetch=3,
        grid=(N_TILES,),
        in_specs=[pl.BlockSpec((1, TOP_K, TOK_TILE), lambda i, *_: (i, 0, 0), memory_space=pltpu.SMEM),
                  pl.BlockSpec(memory_space=pl.ANY)],
        out_specs=pl.BlockSpec(memory_space=pl.ANY),
        scratch_shapes=[pltpu.VMEM((EXPERT_ROWS * TOKEN_TILE_ROWS, 128), _F32),
                        pltpu.VMEM((3, TOK_TILE * TOKEN_TILE_ROWS, 128), _F32),
                        pltpu.SemaphoreType.DMA((3,)), pltpu.SemaphoreType.DMA((2,)),
                        pltpu.SemaphoreType.DMA(())],
    )
    return pl.pallas_call(
        _dispatch_kernel,
        grid_spec=grid_spec,
        out_shape=jax.ShapeDtypeStruct((N_SLOTS * TOKEN_TILE_ROWS, 128), _F32),
        compiler_params=_params(),
        name="dispatch",
    )(pad_lo, pad_hi, n_used, pos3, h2p)


def _expert_kernel(be_ref, nused_ref, first_ref, next_ref, parity_ref, xs_hbm, w1_hbm, w3_hbm, w2_hbm, ys_ref,
                   x_buf, w1_buf, w3_buf, w2_buf, w1_scr, w3_scr, w2_scr, x_sem, sem):
    j = pl.program_id(0)
    n_used = nused_ref[0]
    block_rows = EXPERT_ROWS * TOKEN_TILE_ROWS

    def weight_copies(expert, slot):
        return [pltpu.make_async_copy(hbm.at[expert], buf.at[slot], sem.at[slot])
                for hbm, buf in ((w1_hbm, w1_buf), (w3_hbm, w3_buf), (w2_hbm, w2_buf))]

    def row_copy(step):
        src = xs_hbm.at[pl.ds(pl.multiple_of(step * block_rows, block_rows), block_rows)]
        return pltpu.make_async_copy(src, x_buf.at[step % 3], x_sem.at[step % 3])

    @pl.when(j < n_used)
    def _():
        @pl.when(j == 0)
        def _():
            row_copy(0).start()

            @pl.when(n_used > 1)
            def _():
                row_copy(1).start()

            for copy in weight_copies(be_ref[0], 0):
                copy.start()

        @pl.when(j + 2 < n_used)
        def _():
            row_copy(j + 2).start()

        row_copy(j).wait()

        @pl.when(first_ref[j] == 1)
        def _():
            slot = parity_ref[j]
            for copy in weight_copies(be_ref[j], slot):
                copy.wait()

            @pl.when(next_ref[j] >= 0)
            def _():
                for copy in weight_copies(next_ref[j], 1 - slot):
                    copy.start(priority=1)

            w1_scr[...] = w1_buf[slot].astype(_BF16)
            w3_scr[...] = w3_buf[slot].astype(_BF16)
            w2_scr[...] = w2_buf[slot].astype(_BF16)

        x = _load_token_tiles(x_buf.at[j % 3], EXPERT_ROWS).astype(_BF16)
        h1 = jnp.dot(x, w1_scr[...], preferred_element_type=_F32)
        h3 = jnp.dot(x, w3_scr[...], preferred_element_type=_F32)
        act = (_silu(h1) * h3).astype(_BF16)
        _store_token_tiles(ys_ref, jnp.dot(act, w2_scr[...], preferred_element_type=_F32), EXPERT_ROWS)

    @pl.when(j >= nused_ref[0])
    def _():
        ys_ref[...] = jnp.zeros_like(ys_ref)


def _experts(block_expert, n_used, run_first, run_next, run_parity, xs, w1, w3, w2):
    up, down = (D_MODEL, EXPERT_FF), (EXPERT_FF, D_MODEL)
    block = (EXPERT_ROWS * TOKEN_TILE_ROWS, 128)
    grid_spec = pltpu.PrefetchScalarGridSpec(
        num_scalar_prefetch=5,
        grid=(N_BLOCKS,),
        in_specs=[pl.BlockSpec(memory_space=pl.ANY)] * 4,
        out_specs=pl.BlockSpec(block, lambda j, *_: (j, 0)),
        scratch_shapes=[pltpu.VMEM((3,) + block, _F32),
                        pltpu.VMEM((2,) + up, _F32), pltpu.VMEM((2,) + up, _F32), pltpu.VMEM((2,) + down, _F32),
                        pltpu.VMEM(up, _BF16), pltpu.VMEM(up, _BF16), pltpu.VMEM(down, _BF16),
                        pltpu.SemaphoreType.DMA((3,)), pltpu.SemaphoreType.DMA((2,))],
    )
    return pl.pallas_call(
        _expert_kernel,
        grid_spec=grid_spec,
        out_shape=jax.ShapeDtypeStruct((N_SLOTS * TOKEN_TILE_ROWS, 128), _F32),
        compiler_params=_params(),
        name="experts",
    )(block_expert, n_used, run_first, run_next, run_parity, xs, w1, w3, w2)


COMBINE_CHUNK = 8


def _combine_kernel(pos_ref, pos_next_ref, gate_ref, h2_ref, x1_ref, mod_ref, sw1_ref, sw3_ref, sw2_ref, fn_ref,
                    ys_ref, y_ref, buf, routed_scr, sem):
    tm = COMBINE_TILE
    tc = COMBINE_CHUNK
    i = pl.program_id(0)
    slot = i % 2

    def issue(p_ref, dst_slot, t):
        for k in range(TOP_K):
            pltpu.make_async_copy(_slot_tile(ys_ref, p_ref[0, k, t]), _slot_tile(buf.at[dst_slot, k], t),
                                  sem.at[dst_slot]).start(priority=k % 2)

    @pl.when(i == 0)
    def _():
        def body(t, carry):
            issue(pos_ref, 0, t)
            return carry

        lax.fori_loop(0, tm, body, 0, unroll=4)

    for k in range(TOP_K):
        pltpu.make_async_copy(ys_ref.at[pl.ds(0, tm * TOKEN_TILE_ROWS)], buf.at[slot, k], sem.at[slot]).wait()

    def weighted_sum_loop(issue_next):
        def body(c, carry):
            t0 = pl.multiple_of(c * tc, tc)
            if issue_next:
                for tt in range(tc):
                    issue(pos_next_ref, 1 - slot, t0 + tt)
            gates = gate_ref[pl.ds(t0, tc), :]
            weights = [gates[:, k:k + 1] for k in range(TOP_K)]
            for s in range(TOKEN_TILE_ROWS):
                rows = pl.ds(t0 * TOKEN_TILE_ROWS + s, tc, stride=TOKEN_TILE_ROWS)
                acc = weights[0] * buf[slot, 0, rows, :]
                for k in range(1, TOP_K):
                    acc = acc + weights[k] * buf[slot, k, rows, :]
                routed_scr[pl.ds(t0, tc), s * 128:(s + 1) * 128] = acc
            return carry

        lax.fori_loop(0, tm // tc, body, 0)

    has_next = i + 1 < pl.num_programs(0)

    @pl.when(has_next)
    def _():
        weighted_sum_loop(True)

    @pl.when(jnp.logical_not(has_next))
    def _():
        weighted_sum_loop(False)

    hb = h2_ref[...]
    act = _silu(jnp.dot(hb, sw1_ref[...], preferred_element_type=_F32)) * jnp.dot(
        hb, sw3_ref[...], preferred_element_type=_F32)
    shared = _dot(act, sw2_ref[...])
    y = x1_ref[...] + mod_ref[0, 5:6, :] * (routed_scr[...] + shared)
    y_ref[...] = _rms(y, fn_ref[...])


def _combine(pos3, gate_t, h2, x1, mod3, sw1, sw3, sw2, final_norm, ys, *, n_rows, tile0, mod_row):
    tm = COMBINE_TILE
    const = lambda i: (0, 0)
    row = lambda i: (tile0 + i, 0)
    n_tiles = n_rows // tm
    return pl.pallas_call(
        _combine_kernel,
        grid=(n_tiles,),
        in_specs=[
            pl.BlockSpec((1, TOP_K, tm), lambda i: (tile0 + i, 0, 0), memory_space=pltpu.SMEM),
            pl.BlockSpec((1, TOP_K, tm), lambda i: (tile0 + jnp.minimum(i + 1, n_tiles - 1), 0, 0),
                         memory_space=pltpu.SMEM),
            pl.BlockSpec((tm, 128), row),
            pl.BlockSpec((tm, D_MODEL), row),
            pl.BlockSpec((tm, D_MODEL), row),
            pl.BlockSpec((1, 6, D_MODEL), lambda i: (mod_row(i), 0, 0)),
            pl.BlockSpec((D_MODEL, EXPERT_FF), const),
            pl.BlockSpec((D_MODEL, EXPERT_FF), const),
            pl.BlockSpec((EXPERT_FF, D_MODEL), const),
            pl.BlockSpec((1, D_MODEL), const),
            pl.BlockSpec(memory_space=pl.ANY),
        ],
        out_specs=pl.BlockSpec((tm, D_MODEL), lambda i: (i, 0)),
        out_shape=jax.ShapeDtypeStruct((n_rows, D_MODEL), _F32),
        scratch_shapes=[pltpu.VMEM((2, TOP_K, tm * TOKEN_TILE_ROWS, 128), _F32), pltpu.VMEM((tm, D_MODEL), _F32),
                        pltpu.SemaphoreType.DMA((2,))],
        compiler_params=_params(),
        name="combine",
    )(pos3, pos3, gate_t, h2, x1, mod3, sw1, sw3, sw2, final_norm, ys)


def _tile_major(a, tile):
    return a.reshape(TOP_K, -1, tile).transpose(1, 0, 2)


def kernel(x_prompt, x_sample, c, cache_k, cache_v, state_dn, c_ctx, w_ada, b_ada, norm1, norm2, w_in, dn_conv,
           dn_A_log, dn_dt_bias, dn_norm, attn_sinks, w_out, router_w, router_bias, expert_w1, expert_w3,
           expert_w2, shared_w1, shared_w3, shared_w2, final_norm):
    xp = x_prompt.reshape(M_CTX, D_MODEL)
    xs = x_sample.reshape(M_LAT, D_MODEL)

    cvec = jnp.concatenate([c_ctx[None, :], c, jnp.zeros((N_MOD - 1 - DEC_BATCH, D_MODEL), _F32)], axis=0)
    mod3 = _modulation(cvec, w_ada[0], b_ada).reshape(N_MOD, 6, D_MODEL)

    w = w_in[0]
    n_dn = 4 * DN_WIDTH
    w_dn = w[:, :n_dn].astype(_BF16)
    w_ab = jnp.pad(w[:, n_dn:n_dn + 4 * DN_HEADS], ((0, 0), (0, 128 - 4 * DN_HEADS))).astype(_BF16)
    w_att = w[:, n_dn + 4 * DN_HEADS:].astype(_BF16)
    cos, sin = _rope_tables()
    dn, aq, ak, av, ab = _input_projection(xp, xs, mod3, norm1, cos, sin, w_dn, w_att, w_ab)

    conv_w = jnp.pad(dn_conv[0], ((0, 8 - DN_CONV), (0, 0)))
    pad8 = lambda v: jnp.pad(v.reshape(1, 2 * DN_HEADS), ((0, 0), (0, 128 - 2 * DN_HEADS)))
    a_log = pad8(dn_A_log[0])
    dt_bias = pad8(dn_dt_bias[0])
    dn_c, new_state = _deltanet(dn, ab, conv_w, a_log, dt_bias, dn_norm, None,
                                n_seq=BATCH, seq_len=SEQ, row_block0=0, emit_state=True)
    (dn_l,) = _deltanet(dn, ab, conv_w, a_log, dt_bias, dn_norm, state_dn[:, 0],
                        n_seq=DEC_BATCH, seq_len=DEC_SEQ, row_block0=M_CTX // DEC_SEQ, emit_state=False)

    sinks = attn_sinks[0]
    at_c = _context_attention(sinks, aq, ak, av)
    ctx_k = cache_k[:, 0].reshape(DEC_BATCH * PAST_LEN, ATT_KV)
    ctx_v = cache_v[:, 0].reshape(DEC_BATCH * PAST_LEN, ATT_KV)
    at_l = _latent_attention(sinks, aq, ak, av, ctx_k, ctx_v)

    wo = w_out[0].astype(_BF16)
    x1, h2, h2p, logits_t = _output_projection(xp, xs, dn_c, dn_l, at_c, at_l, mod3, norm2,
                                               wo[:DN_WIDTH], wo[DN_WIDTH:], router_w[0].T)

    idx, rank, gate_t, counts = _route(logits_t, router_bias[0].reshape(N_EXPERTS, 1))
    cnt = counts[:, 0].astype(jnp.int32)
    padded = (cnt + EXPERT_ROWS - 1) // EXPERT_ROWS * EXPERT_ROWS
    pad_end = jnp.cumsum(padded)
    pos = _slots(idx, rank, (pad_end - padded).astype(_F32).reshape(N_EXPERTS, 1))
    n_used = (pad_end[-1] // EXPERT_ROWS).astype(jnp.int32).reshape(1)
    block_start = jnp.arange(N_BLOCKS, dtype=jnp.int32) * EXPERT_ROWS
    block_expert = jnp.minimum(
        jnp.sum((pad_end[None, :] <= block_start[:, None]).astype(jnp.int32), axis=1), N_EXPERTS - 1)

    pad_lo = (pad_end - padded + cnt).astype(jnp.int32)
    x_sorted = _dispatch(pad_lo, pad_end.astype(jnp.int32), n_used, _tile_major(pos, TOK_TILE), h2p)
    run_first = jnp.concatenate([jnp.ones((1,), jnp.int32),
                                 (block_expert[1:] != block_expert[:-1]).astype(jnp.int32)])
    run_parity = (jnp.cumsum(run_first) - 1) % 2
    experts = jnp.arange(N_EXPERTS, dtype=jnp.int32)
    later = (experts[None, :] > experts[:, None]) & (cnt[None, :] > 0)
    next_expert = jnp.min(jnp.where(later, experts[None, :], N_EXPERTS), axis=1)
    next_expert = jnp.where(next_expert == N_EXPERTS, -1, next_expert)
    run_next = jnp.sum(jnp.where(block_expert[:, None] == experts[None, :], next_expert[None, :], 0), axis=1)
    ys = _experts(block_expert, n_used, run_first, run_next.astype(jnp.int32), run_parity.astype(jnp.int32),
                  x_sorted, expert_w1[0], expert_w3[0], expert_w2[0])

    pos_c = _tile_major(pos, COMBINE_TILE)
    sw1, sw3, sw2 = shared_w1[0].astype(_BF16), shared_w3[0].astype(_BF16), shared_w2[0].astype(_BF16)
    fn = final_norm.reshape(1, D_MODEL)
    y_prompt = _combine(pos_c, gate_t, h2, x1, mod3, sw1, sw3, sw2, fn, ys,
                        n_rows=M_CTX, tile0=0, mod_row=lambda i: 0)
    lat_tiles = DEC_SEQ // COMBINE_TILE
    y_sample = _combine(pos_c, gate_t, h2, x1, mod3, sw1, sw3, sw2, fn, ys,
                        n_rows=M_LAT, tile0=M_CTX // COMBINE_TILE, mod_row=lambda i: 1 + i // lat_tiles)

    new_cache_k = ak[:M_CTX].reshape(BATCH, 1, SEQ, ATT_KV_HEADS, HEAD_DIM)
    new_cache_v = av[:M_CTX].reshape(BATCH, 1, SEQ, ATT_KV_HEADS, HEAD_DIM)
    return (y_prompt.reshape(BATCH, SEQ, D_MODEL), y_sample.reshape(DEC_BATCH, DEC_SEQ, D_MODEL),
            new_cache_k, new_cache_v, new_state.reshape(BATCH, 1, 2, DN_HEADS, HEAD_DIM, HEAD_DIM))
```

```python
import functools

import jax
import jax.numpy as jnp
import numpy as np
from jax import lax
from jax.experimental import pallas as pl
from jax.experimental.pallas import tpu as pltpu

D_MODEL = 1024
BATCH = 32
SEQ = 256
DEC_BATCH = 8
DEC_SEQ = 2048
PAST_LEN = 512
GRID_W = 64
HEAD_DIM = 128
DN_HEADS = 4
DN_WIDTH = DN_HEADS * HEAD_DIM
DN_CONV = 5
DN_CHUNK = 64
ATT_HEADS = 4
ATT_KV_HEADS = 2
ATT_GROUP = ATT_HEADS // ATT_KV_HEADS
ATT_Q = ATT_HEADS * HEAD_DIM
ATT_KV = ATT_KV_HEADS * HEAD_DIM
ATT_BLOCK = 128
ROPE_THETA = 10000.0
N_EXPERTS = 256
TOP_K = 8
N_GROUPS = 8
TOPK_GROUPS = 4
GROUP_SIZE = N_EXPERTS // N_GROUPS
EXPERT_FF = D_MODEL // 4
ROUTED_SCALE = 2.5
NORM_EPS = 1e-6
NEG_INF = -1e30

M_CTX = BATCH * SEQ
M_LAT = DEC_BATCH * DEC_SEQ
M_ALL = M_CTX + M_LAT
N_MOD = 16
TOK_TILE = 512
N_CTX_TILES = M_CTX // TOK_TILE
N_TILES = M_ALL // TOK_TILE
EXPERT_ROWS = 256
N_PAIRS = M_ALL * TOP_K
N_BLOCKS = (N_PAIRS + N_EXPERTS * (EXPERT_ROWS - 1)) // EXPERT_ROWS
N_SLOTS = N_BLOCKS * EXPERT_ROWS
COMBINE_TILE = 256
V7X_VMEM_BYTES = 64 * 1024 * 1024
VMEM_LIMIT = V7X_VMEM_BYTES - 8 * 1024 * 1024

_BF16 = jnp.bfloat16
_F32 = jnp.float32


def _dot(a, b):
    return jnp.dot(a.astype(_BF16), b.astype(_BF16), preferred_element_type=_F32)


def _dot_nt(a, b):
    return lax.dot_general(a.astype(_BF16), b.astype(_BF16), (((1,), (1,)), ((), ())),
                           preferred_element_type=_F32)


def _dot_tn(a, b):
    return lax.dot_general(a.astype(_BF16), b.astype(_BF16), (((0,), (0,)), ((), ())),
                           preferred_element_type=_F32)


def _silu(x):
    return x * jax.nn.sigmoid(x)


def _rms(x, w):
    return x * lax.rsqrt(jnp.mean(x * x, axis=-1, keepdims=True) + NORM_EPS) * w


def _params(n_axes=1):
    return pltpu.CompilerParams(dimension_semantics=("arbitrary",) * n_axes, vmem_limit_bytes=VMEM_LIMIT)


def _tile_mod_row(i):
    return jnp.where(i < N_CTX_TILES, 0, 1 + (i - N_CTX_TILES) // (DEC_SEQ // TOK_TILE))


def _ctx_tile_mask(shape):
    limit = jnp.where(pl.program_id(0) < N_CTX_TILES, shape[0], 0)
    return lax.broadcasted_iota(jnp.int32, shape, 0) < limit


TOKEN_TILE_ROWS = D_MODEL // 128


def _store_token_tiles(ref, x, n):
    for s in range(TOKEN_TILE_ROWS):
        ref[pl.ds(s, n, stride=TOKEN_TILE_ROWS), :] = x[:, s * 128:(s + 1) * 128]


def _load_token_tiles(ref, n):
    return jnp.concatenate([ref[pl.ds(s, n, stride=TOKEN_TILE_ROWS), :] for s in range(TOKEN_TILE_ROWS)], axis=1)


def _x_specs():
    return [
        pl.BlockSpec((TOK_TILE, D_MODEL), lambda i: (jnp.minimum(i, N_CTX_TILES - 1), 0)),
        pl.BlockSpec((TOK_TILE, D_MODEL), lambda i: (jnp.maximum(i - N_CTX_TILES, 0), 0)),
    ]


def _mod_kernel(c_ref, w_ref, b_ref, o_ref):
    o_ref[...] = _dot(_silu(c_ref[...]), w_ref[...]) + b_ref[...]


def _modulation(cvec, w_ada, b_ada):
    tn = 1024
    return pl.pallas_call(
        _mod_kernel,
        grid=(6 * D_MODEL // tn,),
        in_specs=[pl.BlockSpec((N_MOD, D_MODEL), lambda j: (0, 0)),
                  pl.BlockSpec((D_MODEL, tn), lambda j: (0, j)),
                  pl.BlockSpec((1, tn), lambda j: (0, j))],
        out_specs=pl.BlockSpec((N_MOD, tn), lambda j: (0, j)),
        out_shape=jax.ShapeDtypeStruct((N_MOD, 6 * D_MODEL), _F32),
        compiler_params=_params(),
        name="modulation",
    )(cvec, w_ada, b_ada)


def _rope(x, cos, sin, first_half):
    swapped = jnp.where(first_half, pltpu.roll(x, 96, 1), pltpu.roll(x, 32, 1))
    return x * cos + swapped * sin


def _inproj_kernel(xp_ref, xs_ref, mod_ref, n1_ref, cos_ref, sin_ref, wdn_ref, watt_ref, wab_ref,
                   dn_ref, aq_ref, ak_ref, av_ref, ab_ref):
    x = jnp.where(_ctx_tile_mask((TOK_TILE, D_MODEL)), xp_ref[...], xs_ref[...])
    shift = mod_ref[0, 0:1, :]
    scale = mod_ref[0, 1:2, :]
    h = (_rms(x, n1_ref[...]) * (1 + scale) + shift).astype(_BF16)
    dn_ref[...] = jnp.dot(h, wdn_ref[...], preferred_element_type=_F32).astype(_BF16)
    ab_ref[...] = jnp.dot(h, wab_ref[...], preferred_element_type=_F32)
    att = jnp.dot(h, watt_ref[...], preferred_element_type=_F32)
    cos = cos_ref[...]
    sin = sin_ref[...]
    lane = lax.broadcasted_iota(jnp.int32, (TOK_TILE, HEAD_DIM), 1)
    first_half = (lane % 64) < 32
    for hd in range(ATT_HEADS):
        q = att[:, hd * HEAD_DIM:(hd + 1) * HEAD_DIM]
        aq_ref[:, hd * HEAD_DIM:(hd + 1) * HEAD_DIM] = (
            _rope(q, cos, sin, first_half) * HEAD_DIM ** -0.5).astype(_BF16)
    for hd in range(ATT_KV_HEADS):
        k = att[:, ATT_Q + hd * HEAD_DIM:ATT_Q + (hd + 1) * HEAD_DIM]
        ak_ref[:, hd * HEAD_DIM:(hd + 1) * HEAD_DIM] = _rope(k, cos, sin, first_half)
    av_ref[...] = att[:, ATT_Q + ATT_KV:]


def _rope_tables():
    t = jnp.arange(DEC_SEQ)
    row = (t // GRID_W).astype(_F32)
    col = (t % GRID_W).astype(_F32)
    n_freq = HEAD_DIM // 4
    inv_freq = 1.0 / (ROPE_THETA ** (jnp.arange(n_freq, dtype=_F32) / n_freq))
    ang_r = row[:, None] * inv_freq
    ang_c = col[:, None] * inv_freq
    cos = jnp.concatenate([jnp.cos(ang_r), jnp.cos(ang_r), jnp.cos(ang_c), jnp.cos(ang_c)], axis=1)
    sin = jnp.concatenate([-jnp.sin(ang_r), jnp.sin(ang_r), -jnp.sin(ang_c), jnp.sin(ang_c)], axis=1)
    cos = jnp.concatenate([jnp.ones((TOK_TILE, HEAD_DIM), _F32), cos], axis=0)
    sin = jnp.concatenate([jnp.zeros((TOK_TILE, HEAD_DIM), _F32), sin], axis=0)
    return cos, sin


def _input_projection(xp, xs, mod3, norm1, cos, sin, w_dn, w_att, w_ab):
    def rope_idx(i):
        return (jnp.where(i < N_CTX_TILES, 0, 1 + (i - N_CTX_TILES) % (DEC_SEQ // TOK_TILE)), 0)

    const = lambda i: (0, 0)
    row = lambda i: (i, 0)
    return pl.pallas_call(
        _inproj_kernel,
        grid=(N_TILES,),
        in_specs=_x_specs() + [
            pl.BlockSpec((1, 6, D_MODEL), lambda i: (_tile_mod_row(i), 0, 0)),
            pl.BlockSpec((1, D_MODEL), const),
            pl.BlockSpec((TOK_TILE, HEAD_DIM), rope_idx),
            pl.BlockSpec((TOK_TILE, HEAD_DIM), rope_idx),
            pl.BlockSpec((D_MODEL, 4 * DN_WIDTH), const),
            pl.BlockSpec((D_MODEL, ATT_Q + 2 * ATT_KV), const),
            pl.BlockSpec((D_MODEL, 128), const),
        ],
        out_specs=[
            pl.BlockSpec((TOK_TILE, 4 * DN_WIDTH), row),
            pl.BlockSpec((TOK_TILE, ATT_Q), row),
            pl.BlockSpec((TOK_TILE, ATT_KV), row),
            pl.BlockSpec((TOK_TILE, ATT_KV), row),
            pl.BlockSpec((TOK_TILE, 128), row),
        ],
        out_shape=[
            jax.ShapeDtypeStruct((M_ALL, 4 * DN_WIDTH), _BF16),
            jax.ShapeDtypeStruct((M_ALL, ATT_Q), _BF16),
            jax.ShapeDtypeStruct((M_ALL, ATT_KV), _F32),
            jax.ShapeDtypeStruct((M_ALL, ATT_KV), _F32),
            jax.ShapeDtypeStruct((M_ALL, 128), _F32),
        ],
        compiler_params=_params(),
        name="input_projection",
    )(xp, xs, mod3, norm1, cos, sin, w_dn, w_att, w_ab)


DN_STACK = 16
DN_LONG_CHUNKS = 8


def _dn_pass_heads(n_chunk):
    return DN_HEADS if n_chunk <= DN_LONG_CHUNKS else DN_HEADS // 2


def _dn_stacking(n_chunk):
    group = min(n_chunk, DN_STACK // 2)
    return group, DN_STACK // (2 * group)


def _bdot(a, b):
    return jnp.stack([_dot(a[g], b[g]) for g in range(a.shape[0])])


def _bdot_nt(a, b):
    return jnp.stack([_dot_nt(a[g], b[g]) for g in range(a.shape[0])])


def _inverse_unit_triangular(a, eye):
    p = eye - a
    x = a
    for _ in range(5):
        x = _bdot(x, x)
        p = p + _bdot(p, x)
    return p


def _dn_prepare(q, k, v, gc, gc_row, beta, g_tot, incl, strict, eye):
    decay = jnp.where(incl, jnp.exp(jnp.where(incl, gc - gc_row, 0.0)), 0.0)
    kb = k * beta
    a = jnp.where(strict, _bdot_nt(kb, k) * decay, 0.0)
    t_inv = _inverse_unit_triangular(a, eye)
    eg = jnp.exp(gc)
    u = _bdot(t_inv, v * beta)
    w = _bdot(t_inv, kb * eg)
    qk = _bdot_nt(q, k) * decay
    wq = jnp.concatenate([w, q * eg], axis=1).astype(_BF16)
    kd = k * jnp.exp(g_tot - gc)
    kd_t = jnp.stack([kd[g].T for g in range(kd.shape[0])])
    qkk = jnp.concatenate([qk, kd_t], axis=1).astype(_BF16)
    return u, wq, qkk


def _dn_steps(states, us, wqs, qkks, g_tots):
    wss = [_dot(wq, s) for wq, s in zip(wqs, states)]
    v_news = [u - ws[:DN_CHUNK] for u, ws in zip(us, wss)]
    rs = [_dot(qkk, v_new) for qkk, v_new in zip(qkks, v_news)]
    outs = [ws[DN_CHUNK:] + r[:DN_CHUNK] for ws, r in zip(wss, rs)]
    states = [s * jnp.exp(g) + r[DN_CHUNK:] for s, g, r in zip(states, g_tots, rs)]
    return states, outs


def _dn_kernel(*refs, seq_len, has_s0, emit_state):
    dn_ref, ab_ref, cw_ref, alog_ref, dtb_ref, nw_ref = refs[:6]
    pos = 6
    s0_ref = None
    if has_s0:
        s0_ref = refs[pos]
        pos += 1
    o_ref = refs[pos]
    pos += 1
    st_ref = None
    if emit_state:
        st_ref = refs[pos]
        pos += 1
    pad_scr, q_scr, k_scr, v_scr, gate_scr, grow_scr, o_scr, u_scr, wq_scr, qkk_scr, s_scr = refs[pos:]

    T = seq_len
    C = DN_CHUNK
    n_chunk = T // C
    n_pair = n_chunk // 2
    G, stack_heads = _dn_stacking(n_chunk)
    n_group = n_chunk // G

    gates = ab_ref[...]
    lane = lax.broadcasted_iota(jnp.int32, (T, 128), 1)
    tpos = lax.broadcasted_iota(jnp.int32, (T, 128), 0) % C
    sp_arg = gates + dtb_ref[...]
    softplus = jnp.maximum(sp_arg, 0.0) + jnp.log1p(jnp.exp(-jnp.abs(sp_arg)))
    g = -jnp.exp(alog_ref[...]) * softplus
    pre = g
    suf = g
    s = 1
    while s < C:
        pre = pre + jnp.where(tpos >= s, pltpu.roll(pre, s, 0), 0.0)
        suf = suf + jnp.where(tpos < C - s, pltpu.roll(suf, T - s, 0), 0.0)
        s *= 2
    gcum = jnp.where(lane < DN_HEADS, pre, suf)
    gate_scr[...] = jnp.where(lane < 2 * DN_HEADS, gcum, jax.nn.sigmoid(gates))
    gcum_t = gcum.T
    for p in range(n_pair):
        grow_scr[p] = gcum_t[0:8, p * 128:(p + 1) * 128]

    pad_scr[0:8, :] = jnp.zeros((8, HEAD_DIM), _F32)
    pad_scr[8 + T:16 + T, :] = jnp.zeros((8, HEAD_DIM), _F32)

    shape3 = (DN_STACK, C, C)
    ri = lax.broadcasted_iota(jnp.int32, shape3, 1)
    ci = lax.broadcasted_iota(jnp.int32, shape3, 2)
    eye = (ri == ci).astype(_F32)
    is_fwd = (lax.broadcasted_iota(jnp.int32, shape3, 0) // G) % 2 == 0
    offset = jnp.where(is_fwd, ci - ri, ri - ci)
    incl = offset <= 0
    strict = offset < 0

    def conv_silu(part, h):
        c0 = part * DN_WIDTH + h * HEAD_DIM
        pad_scr[8:8 + T, :] = dn_ref[:, c0:c0 + HEAD_DIM].astype(_F32)
        acc = cw_ref[0:1, c0:c0 + HEAD_DIM] * pad_scr[6:6 + T, :]
        for j in range(1, DN_CONV):
            acc = acc + cw_ref[j:j + 1, c0:c0 + HEAD_DIM] * pad_scr[6 + j:6 + j + T, :]
        return _silu(acc)

    def l2n(x):
        return x * lax.rsqrt(jnp.sum(x * x, axis=-1, keepdims=True) + NORM_EPS)

    def total_decay(h, direction, r0):
        col = direction * DN_HEADS + h
        row = r0 + C - 1 if direction == 0 else r0
        return gate_scr[pl.ds(row, 1), col:col + 1]

    def prepare_group(grp, heads, chain0):
        r0 = pl.multiple_of(grp * G * C, G * C)
        sl = pl.ds(r0, G * C)
        stacked = lambda x: x.reshape(G, C, x.shape[-1])
        qs, ks, vs, gcs, betas, rows, g_tots = [], [], [], [], [], [], []
        for hh, h in enumerate(heads):
            for direction in range(2):
                col = direction * DN_HEADS + h
                bcol = 2 * DN_HEADS + col
                gc = stacked(gate_scr[sl, col:col + 1])
                gcs.append(gc)
                betas.append(stacked(gate_scr[sl, bcol:bcol + 1]))
                for pair in range(G // 2):
                    both = grow_scr[grp * (G // 2) + pair][col:col + 1, :]
                    rows += [both[:, :C], both[:, C:]]
                g_tots.append(gc[:, C - 1:C, :] if direction == 0 else gc[:, 0:1, :])
                qs.append(stacked(q_scr[hh, sl, :]))
                ks.append(stacked(k_scr[hh, sl, :]))
                vs.append(stacked(v_scr[hh, sl, :]))
        cat = lambda xs: jnp.concatenate(xs, axis=0)
        u, wq, qkk = _dn_prepare(cat(qs), cat(ks), cat(vs), cat(gcs), jnp.stack(rows), cat(betas), cat(g_tots),
                                 incl, strict, eye)
        for local in range(2 * len(heads)):
            chain = chain0 + local
            part = slice(local * G, (local + 1) * G)
            slot0 = chain * n_chunk + grp * G
            u_scr[chain, sl, :] = u[part].reshape(G * C, HEAD_DIM)
            wq_scr[pl.ds(slot0, G)] = wq[part]
            qkk_scr[pl.ds(slot0, G)] = qkk[part]

    def steps(chain_heads, c):
        slots, rows, g_tots = [], [], []
        for chain, h in enumerate(chain_heads):
            direction = chain % 2
            chunk = c if direction == 0 else n_chunk - 1 - c
            r0 = pl.multiple_of(chunk * C, C)
            slots.append(chain * n_chunk + chunk)
            rows.append(pl.ds(r0, C))
            g_tots.append(total_decay(h, direction, r0))
        chains = range(len(chain_heads))
        states, outs = _dn_steps([s_scr[ch] for ch in chains], [u_scr[ch, rows[ch], :] for ch in chains],
                                 [wq_scr[slots[ch]] for ch in chains], [qkk_scr[slots[ch]] for ch in chains], g_tots)
        for ch in chains:
            s_scr[ch] = states[ch]
            o_scr[ch, rows[ch], :] = outs[ch]

    pass_heads = _dn_pass_heads(n_chunk)
    for h0 in range(0, DN_HEADS, pass_heads):
        heads = range(h0, h0 + pass_heads)
        for h1 in range(h0, h0 + pass_heads, stack_heads):
            stack = range(h1, h1 + stack_heads)
            for hh, h in enumerate(stack):
                q_scr[hh] = l2n(conv_silu(0, h)) * HEAD_DIM ** -0.5
                k_scr[hh] = l2n(conv_silu(1, h))
                v_scr[hh] = conv_silu(2, h)

            def prepare_body(grp, carry, stack=stack, h1=h1):
                prepare_group(grp, stack, (h1 - h0) * 2)
                return carry

            lax.fori_loop(0, n_group, prepare_body, 0)
        for h in heads:
            for direction in range(2):
                chain = (h - h0) * 2 + direction
                if has_s0:
                    s_scr[chain] = s0_ref[0, direction, h].astype(_F32)
                else:
                    s_scr[chain] = jnp.zeros((HEAD_DIM, HEAD_DIM), _F32)

        chain_heads = [h for h in heads for _ in range(2)]

        def step_body(c, carry):
            steps(chain_heads, c)
            return carry

        lax.fori_loop(0, n_chunk, step_body, 0)

        for h in heads:
            chain = (h - h0) * 2
            if emit_state:
                st_ref[0, 0, h] = s_scr[chain]
                st_ref[0, 1, h] = s_scr[chain + 1]
            o = o_scr[chain] + o_scr[chain + 1]
            z = dn_ref[:, 3 * DN_WIDTH + h * HEAD_DIM:3 * DN_WIDTH + (h + 1) * HEAD_DIM].astype(_F32)
            o = o * lax.rsqrt(jnp.mean(o * o, axis=-1, keepdims=True) + NORM_EPS) * nw_ref[...] * _silu(z)
            o_ref[:, h * HEAD_DIM:(h + 1) * HEAD_DIM] = o.astype(_BF16)


def _deltanet(dn, ab, conv_w, a_log, dt_bias, dn_norm, s0, *, n_seq, seq_len, row_block0, emit_state):
    has_s0 = s0 is not None
    const = lambda b: (0, 0)
    state_spec = pl.BlockSpec((1, 2, DN_HEADS, HEAD_DIM, HEAD_DIM), lambda b: (b, 0, 0, 0, 0))
    in_specs = [
        pl.BlockSpec((seq_len, 4 * DN_WIDTH), lambda b: (row_block0 + b, 0)),
        pl.BlockSpec((seq_len, 128), lambda b: (row_block0 + b, 0)),
        pl.BlockSpec((8, 3 * DN_WIDTH), const),
        pl.BlockSpec((1, 128), const),
        pl.BlockSpec((1, 128), const),
        pl.BlockSpec((1, HEAD_DIM), const),
    ]
    args = [dn, ab, conv_w, a_log, dt_bias, dn_norm]
    if has_s0:
        in_specs.append(state_spec)
        args.append(s0)
    out_specs = [pl.BlockSpec((seq_len, DN_WIDTH), lambda b: (b, 0))]
    out_shape = [jax.ShapeDtypeStruct((n_seq * seq_len, DN_WIDTH), _BF16)]
    if emit_state:
        out_specs.append(state_spec)
        out_shape.append(jax.ShapeDtypeStruct((n_seq, 2, DN_HEADS, HEAD_DIM, HEAD_DIM), _F32))
    n_chunk = seq_len // DN_CHUNK
    pass_tile = (_dn_stacking(n_chunk)[1], seq_len, HEAD_DIM)
    n_chain = 2 * _dn_pass_heads(n_chunk)
    return pl.pallas_call(
        functools.partial(_dn_kernel, seq_len=seq_len, has_s0=has_s0, emit_state=emit_state),
        grid=(n_seq,),
        in_specs=in_specs,
        out_specs=out_specs,
        out_shape=out_shape,
        scratch_shapes=[
            pltpu.VMEM((seq_len + 16, HEAD_DIM), _F32),
            pltpu.VMEM(pass_tile, _F32), pltpu.VMEM(pass_tile, _F32), pltpu.VMEM(pass_tile, _F32),
            pltpu.VMEM((seq_len, 128), _F32),
            pltpu.VMEM((seq_len // (2 * DN_CHUNK), 8, 128), _F32),
            pltpu.VMEM((n_chain, seq_len, HEAD_DIM), _F32),
            pltpu.VMEM((n_chain, seq_len, HEAD_DIM), _F32),
            pltpu.VMEM((n_chain * n_chunk, 2 * DN_CHUNK, HEAD_DIM), _BF16),
            pltpu.VMEM((n_chain * n_chunk, DN_CHUNK + HEAD_DIM, DN_CHUNK), _BF16),
            pltpu.VMEM((n_chain, HEAD_DIM, HEAD_DIM), _F32),
        ],
        compiler_params=_params(),
        name="deltanet_ctx" if emit_state else "deltanet_lat",
    )(*args)


def _softmax_av(scores, values, sink):
    m = sink
    for s in scores:
        m = jnp.maximum(m, jnp.max(s, axis=-1, keepdims=True))
    denom = jnp.exp(sink - m)
    acc = None
    for s, v in zip(scores, values):
        p = jnp.exp(s - m)
        denom = denom + jnp.sum(p, axis=-1, keepdims=True)
        pv = _dot(p, v)
        acc = pv if acc is None else acc + pv
    return acc / denom


def _ctx_attn_kernel(sink_ref, q_ref, k_ref, v_ref, o_ref):
    kvh = pl.program_id(1)
    k = k_ref[...]
    v = v_ref[...]
    for g in range(ATT_GROUP):
        q = q_ref[:, g * HEAD_DIM:(g + 1) * HEAD_DIM]
        sink = jnp.full((SEQ, 1), sink_ref[kvh * ATT_GROUP + g], _F32)
        o = _softmax_av([_dot_nt(q, k)], [v], sink)
        o_ref[:, g * HEAD_DIM:(g + 1) * HEAD_DIM] = o.astype(_BF16)


def _context_attention(sinks, aq, ak, av):
    return pl.pallas_call(
        _ctx_attn_kernel,
        grid=(BATCH, ATT_KV_HEADS),
        in_specs=[
            pl.BlockSpec(memory_space=pltpu.SMEM),
            pl.BlockSpec((SEQ, ATT_GROUP * HEAD_DIM), lambda b, h: (b, h)),
            pl.BlockSpec((SEQ, HEAD_DIM), lambda b, h: (b, h)),
            pl.BlockSpec((SEQ, HEAD_DIM), lambda b, h: (b, h)),
        ],
        out_specs=pl.BlockSpec((SEQ, ATT_GROUP * HEAD_DIM), lambda b, h: (b, h)),
        out_shape=jax.ShapeDtypeStruct((M_CTX, ATT_Q), _BF16),
        compiler_params=_params(2),
        name="context_attention",
    )(sinks, aq, ak, av)


LAT_Q_BLOCK = 2 * ATT_BLOCK


def _lat_attn_kernel(sink_ref, q_ref, kp_ref, kc_ref, kn_ref, vp_ref, vc_ref, vn_ref, ck_ref, cv_ref, o_ref):
    kvh = pl.program_id(1)
    i = pl.program_id(2)
    nb = pl.num_programs(2)
    B, Q = ATT_BLOCK, LAT_Q_BLOCK
    rows = ATT_GROUP * Q
    q = jnp.concatenate([q_ref[:, g * HEAD_DIM:(g + 1) * HEAD_DIM] for g in range(ATT_GROUP)], axis=0)
    r_b = lax.broadcasted_iota(jnp.int32, (rows, B), 0) % Q
    c_b = lax.broadcasted_iota(jnp.int32, (rows, B), 1)
    r_q = lax.broadcasted_iota(jnp.int32, (rows, Q), 0) % Q
    c_q = lax.broadcasted_iota(jnp.int32, (rows, Q), 1)
    s_prev = jnp.where(c_b >= r_b + jnp.where(i > 0, 0, Q), _dot_nt(q, kp_ref[...]), NEG_INF)
    s_cur = jnp.where(jnp.abs(r_q - c_q) <= B, _dot_nt(q, kc_ref[...]), NEG_INF)
    s_next = jnp.where(c_b <= r_b - B - jnp.where(i < nb - 1, 0, Q), _dot_nt(q, kn_ref[...]), NEG_INF)
    s_ctx = _dot_nt(q, ck_ref[...])
    head = lax.broadcasted_iota(jnp.int32, (rows, 1), 0) // Q
    sink = jnp.zeros((rows, 1), _F32)
    for g in range(ATT_GROUP):
        sink = jnp.where(head == g, sink_ref[kvh * ATT_GROUP + g], sink)
    o = _softmax_av([s_prev, s_cur, s_next, s_ctx], [vp_ref[...], vc_ref[...], vn_ref[...], cv_ref[...]], sink)
    for g in range(ATT_GROUP):
        o_ref[:, g * HEAD_DIM:(g + 1) * HEAD_DIM] = o[g * Q:(g + 1) * Q].astype(_BF16)


def _latent_attention(sinks, aq, ak, av, ctx_k, ctx_v):
    nq = DEC_SEQ // LAT_Q_BLOCK
    nb = DEC_SEQ // ATT_BLOCK
    q0 = M_CTX // LAT_Q_BLOCK
    b0 = M_CTX // ATT_BLOCK
    cur = lambda b, h, i: (q0 + b * nq + i, h)
    prev = lambda b, h, i: (b0 + b * nb + jnp.maximum(2 * i - 1, 0), h)
    nxt = lambda b, h, i: (b0 + b * nb + jnp.minimum(2 * i + 2, nb - 1), h)
    side_spec = lambda f: pl.BlockSpec((ATT_BLOCK, HEAD_DIM), f)
    cur_spec = pl.BlockSpec((LAT_Q_BLOCK, HEAD_DIM), cur)
    ctx_spec = pl.BlockSpec((PAST_LEN, HEAD_DIM), lambda b, h, i: (b, h))
    return pl.pallas_call(
        _lat_attn_kernel,
        grid=(DEC_BATCH, ATT_KV_HEADS, nq),
        in_specs=[
            pl.BlockSpec(memory_space=pltpu.SMEM),
            pl.BlockSpec((LAT_Q_BLOCK, ATT_GROUP * HEAD_DIM), cur),
            side_spec(prev), cur_spec, side_spec(nxt),
            side_spec(prev), cur_spec, side_spec(nxt),
            ctx_spec, ctx_spec,
        ],
        out_specs=pl.BlockSpec((LAT_Q_BLOCK, ATT_GROUP * HEAD_DIM), lambda b, h, i: (b * nq + i, h)),
        out_shape=jax.ShapeDtypeStruct((M_LAT, ATT_Q), _BF16),
        compiler_params=_params(3),
        name="latent_attention",
    )(sinks, aq, ak, ak, ak, av, av, av, ctx_k, ctx_v)


def _outproj_kernel(xp_ref, xs_ref, dnc_ref, dnl_ref, atc_ref, atl_ref, mod_ref, n2_ref, wo_dn_ref, wo_at_ref,
                    rw_ref, x1_ref, h2_ref, h2p_ref, lg_ref):
    x = jnp.where(_ctx_tile_mask((TOK_TILE, D_MODEL)), xp_ref[...], xs_ref[...])
    half_mask = _ctx_tile_mask((TOK_TILE, DN_WIDTH))
    dn = jnp.where(half_mask, dnc_ref[...], dnl_ref[...])
    at = jnp.where(half_mask, atc_ref[...], atl_ref[...])
    mixed = (jnp.dot(dn, wo_dn_ref[...], preferred_element_type=_F32)
             + jnp.dot(at, wo_at_ref[...], preferred_element_type=_F32))
    x1 = x + mod_ref[0, 2:3, :] * mixed
    x1_ref[...] = x1
    h = _rms(x1, n2_ref[...]) * (1 + mod_ref[0, 4:5, :]) + mod_ref[0, 3:4, :]
    h2_ref[...] = h.astype(_BF16)
    _store_token_tiles(h2p_ref, h, TOK_TILE)
    w = rw_ref[...]
    w_hi = w.astype(_BF16)
    w_lo = (w - w_hi.astype(_F32)).astype(_BF16)
    h_hi = h.astype(_BF16)
    h_lo = (h - h_hi.astype(_F32)).astype(_BF16)
    lg_ref[...] = (_dot_nt(w_hi, h_hi) + _dot_nt(w_hi, h_lo)) + _dot_nt(w_lo, h_hi)


def _output_projection(xp, xs, dn_c, dn_l, at_c, at_l, mod3, norm2, wo_dn, wo_at, router_wt):
    const = lambda i: (0, 0)
    row = lambda i: (i, 0)
    ctx_row = lambda i: (jnp.minimum(i, N_CTX_TILES - 1), 0)
    lat_row = lambda i: (jnp.maximum(i - N_CTX_TILES, 0), 0)
    half = (TOK_TILE, DN_WIDTH)
    return pl.pallas_call(
        _outproj_kernel,
        grid=(N_TILES,),
        in_specs=_x_specs() + [
            pl.BlockSpec(half, ctx_row), pl.BlockSpec(half, lat_row),
            pl.BlockSpec(half, ctx_row), pl.BlockSpec(half, lat_row),
            pl.BlockSpec((1, 6, D_MODEL), lambda i: (_tile_mod_row(i), 0, 0)),
            pl.BlockSpec((1, D_MODEL), const),
            pl.BlockSpec((DN_WIDTH, D_MODEL), const),
            pl.BlockSpec((ATT_Q, D_MODEL), const),
            pl.BlockSpec((N_EXPERTS, D_MODEL), const),
        ],
        out_specs=[
            pl.BlockSpec((TOK_TILE, D_MODEL), row),
            pl.BlockSpec((TOK_TILE, D_MODEL), row),
            pl.BlockSpec((TOK_TILE * TOKEN_TILE_ROWS, 128), row),
            pl.BlockSpec((N_EXPERTS, TOK_TILE), lambda i: (0, i)),
        ],
        out_shape=[
            jax.ShapeDtypeStruct((M_ALL, D_MODEL), _F32),
            jax.ShapeDtypeStruct((M_ALL, D_MODEL), _BF16),
            jax.ShapeDtypeStruct((M_ALL * TOKEN_TILE_ROWS, 128), _F32),
            jax.ShapeDtypeStruct((N_EXPERTS, M_ALL), _F32),
        ],
        compiler_params=_params(),
        name="output_projection",
    )(xp, xs, dn_c, dn_l, at_c, at_l, mod3, norm2, wo_dn, wo_at, router_wt)


def _first_index_of(values, target, index, limit):
    return jnp.min(jnp.where(values == target, index, limit), axis=0, keepdims=True)


def _route_kernel(lg_ref, bias_ref, idx_ref, rank_ref, gate_ref, cnt_ref, carry_scr):
    i = pl.program_id(0)
    tm = TOK_TILE

    @pl.when(i == 0)
    def _():
        carry_scr[...] = jnp.zeros_like(carry_scr)

    scores = jax.nn.sigmoid(lg_ref[...])
    biased = scores + bias_ref[...]
    row = lax.broadcasted_iota(jnp.int32, (N_EXPERTS, tm), 0).astype(_F32)
    grow = lax.broadcasted_iota(jnp.int32, (GROUP_SIZE, tm), 0).astype(_F32)

    group_rows = []
    for g in range(N_GROUPS):
        blk = biased[g * GROUP_SIZE:(g + 1) * GROUP_SIZE]
        m1 = jnp.max(blk, axis=0, keepdims=True)
        i1 = _first_index_of(blk, m1, grow, GROUP_SIZE)
        m2 = jnp.max(jnp.where(grow == i1, -jnp.inf, blk), axis=0, keepdims=True)
        group_rows.append(m1 + m2)
    gs = jnp.concatenate(group_rows, axis=0)
    gi = lax.broadcasted_iota(jnp.int32, (N_GROUPS, tm), 0).astype(_F32)
    gself = jnp.zeros((N_GROUPS, tm), _F32)
    for _ in range(TOPK_GROUPS):
        hit = gi == _first_index_of(gs, jnp.max(gs, axis=0, keepdims=True), gi, N_GROUPS)
        gself = jnp.where(hit, 1.0, gself)
        gs = jnp.where(hit, -jnp.inf, gs)
    emask = jnp.concatenate(
        [jnp.broadcast_to(gself[g:g + 1], (GROUP_SIZE, tm)) for g in range(N_GROUPS)], axis=0) > 0.5
    masked = jnp.where(emask, biased, NEG_INF)

    selected = jnp.zeros((N_EXPERTS, tm), _F32)
    idxs, gates = [], []
    for _ in range(TOP_K):
        ei = _first_index_of(masked, jnp.max(masked, axis=0, keepdims=True), row, N_EXPERTS)
        hit = row == ei
        idxs.append(ei)
        gates.append(jnp.sum(jnp.where(hit, scores, 0.0), axis=0, keepdims=True))
        masked = jnp.where(hit, -jnp.inf, masked)
        selected = jnp.where(hit, 1.0, selected)
    gsum = gates[0]
    for gk in gates[1:]:
        gsum = gsum + gk
    gates = [gk / gsum * ROUTED_SCALE for gk in gates]

    si = lax.broadcasted_iota(jnp.int32, (tm, tm), 0)
    ti = lax.broadcasted_iota(jnp.int32, (tm, tm), 1)
    earlier = (si < ti).astype(_BF16)
    ranks_all = _dot(selected, earlier) + carry_scr[...]
    ranks = [jnp.sum(jnp.where(row == ei, ranks_all, 0.0), axis=0, keepdims=True) for ei in idxs]
    carry_scr[...] = carry_scr[...] + jnp.sum(selected, axis=1, keepdims=True)

    idx_ref[...] = jnp.concatenate(idxs, axis=0).astype(jnp.int32)
    rank_ref[...] = jnp.concatenate(ranks, axis=0).astype(jnp.int32)
    gate_rows = jnp.concatenate(gates + [jnp.zeros((128 - TOP_K, tm), _F32)], axis=0)
    gate_ref[...] = gate_rows.T
    cnt_ref[...] = jnp.broadcast_to(carry_scr[...], (N_EXPERTS, 128))


def _route(logits_t, bias_col):
    return pl.pallas_call(
        _route_kernel,
        grid=(N_TILES,),
        in_specs=[pl.BlockSpec((N_EXPERTS, TOK_TILE), lambda i: (0, i)),
                  pl.BlockSpec((N_EXPERTS, 1), lambda i: (0, 0))],
        out_specs=[
            pl.BlockSpec((TOP_K, TOK_TILE), lambda i: (0, i)),
            pl.BlockSpec((TOP_K, TOK_TILE), lambda i: (0, i)),
            pl.BlockSpec((TOK_TILE, 128), lambda i: (i, 0)),
            pl.BlockSpec((N_EXPERTS, 128), lambda i: (0, 0)),
        ],
        out_shape=[
            jax.ShapeDtypeStruct((TOP_K, M_ALL), jnp.int32),
            jax.ShapeDtypeStruct((TOP_K, M_ALL), jnp.int32),
            jax.ShapeDtypeStruct((M_ALL, 128), _F32),
            jax.ShapeDtypeStruct((N_EXPERTS, 128), _F32),
        ],
        scratch_shapes=[pltpu.VMEM((N_EXPERTS, 1), _F32)],
        compiler_params=_params(),
        name="route",
    )(logits_t, bias_col)


def _slot_kernel(idx_ref, rank_ref, start_ref, pos_ref):
    row = lax.broadcasted_iota(jnp.int32, (N_EXPERTS, TOK_TILE), 0)
    start = start_ref[...]
    rows = []
    for k in range(TOP_K):
        base = jnp.sum(jnp.where(row == idx_ref[k:k + 1, :], start, 0.0), axis=0, keepdims=True)
        rows.append(base.astype(jnp.int32) + rank_ref[k:k + 1, :])
    pos_ref[...] = jnp.concatenate(rows, axis=0)


def _slots(idx, rank, start_col):
    spec = pl.BlockSpec((TOP_K, TOK_TILE), lambda i: (0, i))
    return pl.pallas_call(
        _slot_kernel,
        grid=(N_TILES,),
        in_specs=[spec, spec, pl.BlockSpec((N_EXPERTS, 1), lambda i: (0, 0))],
        out_specs=spec,
        out_shape=jax.ShapeDtypeStruct((TOP_K, M_ALL), jnp.int32),
        compiler_params=_params(),
        name="slots",
    )(idx, rank, start_col)


def _slot_tile(ref, slot):
    return ref.at[pl.ds(pl.multiple_of(slot * TOKEN_TILE_ROWS, TOKEN_TILE_ROWS), TOKEN_TILE_ROWS)]


def _dispatch_kernel(pad_lo_ref, pad_hi_ref, nused_ref, pos_ref, h_hbm, h2_ref, sw1_ref, sw3_ref, sw2_ref, xs_ref,
                     shared_ref, zero_scr, h_buf, in_sem, out_sem, zero_sem):
    i = pl.program_id(0)
    n_steps = pl.num_programs(0)
    block_rows = TOK_TILE * TOKEN_TILE_ROWS

    def fetch(step):
        src = h_hbm.at[pl.ds(pl.multiple_of(step * block_rows, block_rows), block_rows)]
        return pltpu.make_async_copy(src, h_buf.at[step % 3], in_sem.at[step % 3])

    def wait_copies(step):
        for _ in range(TOP_K):
            pltpu.make_async_copy(h_buf.at[0], xs_ref.at[pl.ds(0, block_rows)], out_sem.at[step % 2]).wait()

    @pl.when(i == 0)
    def _():
        fetch(i).start()

    @pl.when(i + 1 < n_steps)
    def _():
        fetch(i + 1).start()

    def zero_fill(start):
        def zero_copy(first_slot, n_slots):
            rows = n_slots * TOKEN_TILE_ROWS
            dst = xs_ref.at[pl.ds(pl.multiple_of(first_slot * TOKEN_TILE_ROWS, TOKEN_TILE_ROWS), rows)]
            return pltpu.make_async_copy(zero_scr.at[pl.ds(0, rows)], dst, zero_sem)

        def expert_padding(e, carry):
            first = pad_lo_ref[e]
            n = pad_hi_ref[e] - first
            size = EXPERT_ROWS // 2
            while size >= 1:
                chunk = n & size

                @pl.when(chunk != 0)
                def _(first=first, size=size):
                    copy = zero_copy(first, size)
                    copy.start() if start else copy.wait()

                first = first + chunk
                size //= 2
            return carry

        def tail_block(t, carry):
            j = N_BLOCKS - 1 - t

            @pl.when(j >= nused_ref[0])
            def _():
                copy = zero_copy(j * EXPERT_ROWS, EXPERT_ROWS)
                copy.start() if start else copy.wait()

            return carry

        lax.fori_loop(0, N_EXPERTS, expert_padding, 0)
        lax.fori_loop(0, N_BLOCKS - N_PAIRS // EXPERT_ROWS, tail_block, 0)

    @pl.when(i == 0)
    def _():
        zero_scr[...] = jnp.zeros_like(zero_scr)
        zero_fill(start=True)

    fetch(i).wait()
    h_tile = h_buf.at[i % 3]

    def body(t, carry):
        src = _slot_tile(h_tile, t)
        for k in range(TOP_K):
            pltpu.make_async_copy(src, _slot_tile(xs_ref, pos_ref[0, k, t]), out_sem.at[i % 2]).start(
                priority=k % 2)
        return carry

    lax.fori_loop(0, TOK_TILE, body, 0, unroll=4)

    hb = h2_ref[...]
    act = _silu(jnp.dot(hb, sw1_ref[...], preferred_element_type=_F32)) * jnp.dot(
        hb, sw3_ref[...], preferred_element_type=_F32)
    shared_ref[...] = _dot(act, sw2_ref[...])

    @pl.when(i >= 1)
    def _():
        wait_copies(i - 1)

    @pl.when(i == n_steps - 1)
    def _():
        wait_copies(i)
        zero_fill(start=False)


def _dispatch(pad_lo, pad_hi, n_used, pos3, h2p, h2, sw1, sw3, sw2):
    const = lambda i, *_: (0, 0)
    row = lambda i, *_: (i, 0)
    grid_spec = pltpu.PrefetchScalarGridSpec(
        num_scalar_prefetch=3,
        grid=(N_TILES,),
        in_specs=[pl.BlockSpec((1, TOP_K, TOK_TILE), lambda i, *_: (i, 0, 0), memory_space=pltpu.SMEM),
                  pl.BlockSpec(memory_space=pl.ANY),
                  pl.BlockSpec((TOK_TILE, D_MODEL), row),
                  pl.BlockSpec((D_MODEL, EXPERT_FF), const),
                  pl.BlockSpec((D_MODEL, EXPERT_FF), const),
                  pl.BlockSpec((EXPERT_FF, D_MODEL), const)],
        out_specs=[pl.BlockSpec(memory_space=pl.ANY), pl.BlockSpec((TOK_TILE, D_MODEL), row)],
        scratch_shapes=[pltpu.VMEM((EXPERT_ROWS * TOKEN_TILE_ROWS, 128), _F32),
                        pltpu.VMEM((3, TOK_TILE * TOKEN_TILE_ROWS, 128), _F32),
                        pltpu.SemaphoreType.DMA((3,)), pltpu.SemaphoreType.DMA((2,)),
                        pltpu.SemaphoreType.DMA(())],
    )
    return pl.pallas_call(
        _dispatch_kernel,
        grid_spec=grid_spec,
        out_shape=[jax.ShapeDtypeStruct((N_SLOTS * TOKEN_TILE_ROWS, 128), _F32),
                   jax.ShapeDtypeStruct((M_ALL, D_MODEL), _F32)],
        compiler_params=_params(),
        name="dispatch",
    )(pad_lo, pad_hi, n_used, pos3, h2p, h2, sw1, sw3, sw2)


def _expert_kernel(be_ref, nused_ref, first_ref, next_ref, parity_ref, xs_hbm, w1_hbm, w3_hbm, w2_hbm, ys_ref,
                   x_buf, w1_buf, w3_buf, w2_buf, w1_scr, w3_scr, w2_scr, x_sem, sem):
    j = pl.program_id(0)
    n_used = nused_ref[0]
    block_rows = EXPERT_ROWS * TOKEN_TILE_ROWS

    def weight_copies(expert, slot):
        return [pltpu.make_async_copy(hbm.at[expert], buf.at[slot], sem.at[slot])
                for hbm, buf in ((w1_hbm, w1_buf), (w3_hbm, w3_buf), (w2_hbm, w2_buf))]

    def row_copy(step):
        src = xs_hbm.at[pl.ds(pl.multiple_of(step * block_rows, block_rows), block_rows)]
        return pltpu.make_async_copy(src, x_buf.at[step % 3], x_sem.at[step % 3])

    @pl.when(j < n_used)
    def _():
        @pl.when(j == 0)
        def _():
            row_copy(0).start()

            @pl.when(n_used > 1)
            def _():
                row_copy(1).start()

            for copy in weight_copies(be_ref[0], 0):
                copy.start()

        @pl.when(j + 2 < n_used)
        def _():
            row_copy(j + 2).start()

        row_copy(j).wait()

        @pl.when(first_ref[j] == 1)
        def _():
            slot = parity_ref[j]
            for copy in weight_copies(be_ref[j], slot):
                copy.wait()

            @pl.when(next_ref[j] >= 0)
            def _():
                for copy in weight_copies(next_ref[j], 1 - slot):
                    copy.start(priority=1)

            w1_scr[...] = w1_buf[slot].astype(_BF16)
            w3_scr[...] = w3_buf[slot].astype(_BF16)
            w2_scr[...] = w2_buf[slot].astype(_BF16)

        x = _load_token_tiles(x_buf.at[j % 3], EXPERT_ROWS).astype(_BF16)
        h1 = jnp.dot(x, w1_scr[...], preferred_element_type=_F32)
        h3 = jnp.dot(x, w3_scr[...], preferred_element_type=_F32)
        act = (_silu(h1) * h3).astype(_BF16)
        _store_token_tiles(ys_ref, jnp.dot(act, w2_scr[...], preferred_element_type=_F32), EXPERT_ROWS)

    @pl.when(j >= nused_ref[0])
    def _():
        ys_ref[...] = jnp.zeros_like(ys_ref)


def _experts(block_expert, n_used, run_first, run_next, run_parity, xs, w1, w3, w2):
    up, down = (D_MODEL, EXPERT_FF), (EXPERT_FF, D_MODEL)
    block = (EXPERT_ROWS * TOKEN_TILE_ROWS, 128)
    grid_spec = pltpu.PrefetchScalarGridSpec(
        num_scalar_prefetch=5,
        grid=(N_BLOCKS,),
        in_specs=[pl.BlockSpec(memory_space=pl.ANY)] * 4,
        out_specs=pl.BlockSpec(block, lambda j, *_: (j, 0)),
        scratch_shapes=[pltpu.VMEM((3,) + block, _F32),
                        pltpu.VMEM((2,) + up, _F32), pltpu.VMEM((2,) + up, _F32), pltpu.VMEM((2,) + down, _F32),
                        pltpu.VMEM(up, _BF16), pltpu.VMEM(up, _BF16), pltpu.VMEM(down, _BF16),
                        pltpu.SemaphoreType.DMA((3,)), pltpu.SemaphoreType.DMA((2,))],
    )
    return pl.pallas_call(
        _expert_kernel,
        grid_spec=grid_spec,
        out_shape=jax.ShapeDtypeStruct((N_SLOTS * TOKEN_TILE_ROWS, 128), _F32),
        compiler_params=_params(),
        name="experts",
    )(block_expert, n_used, run_first, run_next, run_parity, xs, w1, w3, w2)


COMBINE_CHUNK = 8


def _combine_kernel(pos_ref, pos_next_ref, gate_ref, shared_ref, x1_ref, mod_ref, fn_ref,
                    ys_ref, y_ref, buf, routed_scr, sem):
    tm = COMBINE_TILE
    tc = COMBINE_CHUNK
    i = pl.program_id(0)
    slot = i % 2

    def issue(p_ref, dst_slot, t):
        for k in range(TOP_K):
            pltpu.make_async_copy(_slot_tile(ys_ref, p_ref[0, k, t]), _slot_tile(buf.at[dst_slot, k], t),
                                  sem.at[dst_slot]).start(priority=k % 2)

    @pl.when(i == 0)
    def _():
        def body(t, carry):
            issue(pos_ref, 0, t)
            return carry

        lax.fori_loop(0, tm, body, 0, unroll=4)

    for k in range(TOP_K):
        pltpu.make_async_copy(ys_ref.at[pl.ds(0, tm * TOKEN_TILE_ROWS)], buf.at[slot, k], sem.at[slot]).wait()

    def weighted_sum_loop(issue_next):
        def body(c, carry):
            t0 = pl.multiple_of(c * tc, tc)
            if issue_next:
                for tt in range(tc):
                    issue(pos_next_ref, 1 - slot, t0 + tt)
            gates = gate_ref[pl.ds(t0, tc), :]
            weights = [gates[:, k:k + 1] for k in range(TOP_K)]
            for s in range(TOKEN_TILE_ROWS):
                rows = pl.ds(t0 * TOKEN_TILE_ROWS + s, tc, stride=TOKEN_TILE_ROWS)
                acc = weights[0] * buf[slot, 0, rows, :]
                for k in range(1, TOP_K):
                    acc = acc + weights[k] * buf[slot, k, rows, :]
                routed_scr[pl.ds(t0, tc), s * 128:(s + 1) * 128] = acc
            return carry

        lax.fori_loop(0, tm // tc, body, 0)

    has_next = i + 1 < pl.num_programs(0)

    @pl.when(has_next)
    def _():
        weighted_sum_loop(True)

    @pl.when(jnp.logical_not(has_next))
    def _():
        weighted_sum_loop(False)

    y = x1_ref[...] + mod_ref[0, 5:6, :] * (routed_scr[...] + shared_ref[...])
    y_ref[...] = _rms(y, fn_ref[...])


def _combine(pos3, gate_t, shared, x1, mod3, final_norm, ys, *, n_rows, tile0, mod_row):
    tm = COMBINE_TILE
    const = lambda i: (0, 0)
    row = lambda i: (tile0 + i, 0)
    n_tiles = n_rows // tm
    return pl.pallas_call(
        _combine_kernel,
        grid=(n_tiles,),
        in_specs=[
            pl.BlockSpec((1, TOP_K, tm), lambda i: (tile0 + i, 0, 0), memory_space=pltpu.SMEM),
            pl.BlockSpec((1, TOP_K, tm), lambda i: (tile0 + jnp.minimum(i + 1, n_tiles - 1), 0, 0),
                         memory_space=pltpu.SMEM),
            pl.BlockSpec((tm, 128), row),
            pl.BlockSpec((tm, D_MODEL), row),
            pl.BlockSpec((tm, D_MODEL), row),
            pl.BlockSpec((1, 6, D_MODEL), lambda i: (mod_row(i), 0, 0)),
            pl.BlockSpec((1, D_MODEL), const),
            pl.BlockSpec(memory_space=pl.ANY),
        ],
        out_specs=pl.BlockSpec((tm, D_MODEL), lambda i: (i, 0)),
        out_shape=jax.ShapeDtypeStruct((n_rows, D_MODEL), _F32),
        scratch_shapes=[pltpu.VMEM((2, TOP_K, tm * TOKEN_TILE_ROWS, 128), _F32), pltpu.VMEM((tm, D_MODEL), _F32),
                        pltpu.SemaphoreType.DMA((2,))],
        compiler_params=_params(),
        name="combine",
    )(pos3, pos3, gate_t, shared, x1, mod3, final_norm, ys)


def _tile_major(a, tile):
    return a.reshape(TOP_K, -1, tile).transpose(1, 0, 2)


def kernel(x_prompt, x_sample, c, cache_k, cache_v, state_dn, c_ctx, w_ada, b_ada, norm1, norm2, w_in, dn_conv,
           dn_A_log, dn_dt_bias, dn_norm, attn_sinks, w_out, router_w, router_bias, expert_w1, expert_w3,
           expert_w2, shared_w1, shared_w3, shared_w2, final_norm):
    xp = x_prompt.reshape(M_CTX, D_MODEL)
    xs = x_sample.reshape(M_LAT, D_MODEL)

    cvec = jnp.concatenate([c_ctx[None, :], c, jnp.zeros((N_MOD - 1 - DEC_BATCH, D_MODEL), _F32)], axis=0)
    mod3 = _modulation(cvec, w_ada[0], b_ada).reshape(N_MOD, 6, D_MODEL)

    w = w_in[0]
    n_dn = 4 * DN_WIDTH
    w_dn = w[:, :n_dn].astype(_BF16)
    w_ab = jnp.pad(w[:, n_dn:n_dn + 4 * DN_HEADS], ((0, 0), (0, 128 - 4 * DN_HEADS))).astype(_BF16)
    w_att = w[:, n_dn + 4 * DN_HEADS:].astype(_BF16)
    cos, sin = _rope_tables()
    dn, aq, ak, av, ab = _input_projection(xp, xs, mod3, norm1, cos, sin, w_dn, w_att, w_ab)

    conv_w = jnp.pad(dn_conv[0], ((0, 8 - DN_CONV), (0, 0)))
    pad8 = lambda v: jnp.pad(v.reshape(1, 2 * DN_HEADS), ((0, 0), (0, 128 - 2 * DN_HEADS)))
    a_log = pad8(dn_A_log[0])
    dt_bias = pad8(dn_dt_bias[0])
    dn_c, new_state = _deltanet(dn, ab, conv_w, a_log, dt_bias, dn_norm, None,
                                n_seq=BATCH, seq_len=SEQ, row_block0=0, emit_state=True)
    (dn_l,) = _deltanet(dn, ab, conv_w, a_log, dt_bias, dn_norm, state_dn[:, 0],
                        n_seq=DEC_BATCH, seq_len=DEC_SEQ, row_block0=M_CTX // DEC_SEQ, emit_state=False)

    sinks = attn_sinks[0]
    at_c = _context_attention(sinks, aq, ak, av)
    ctx_k = cache_k[:, 0].reshape(DEC_BATCH * PAST_LEN, ATT_KV)
    ctx_v = cache_v[:, 0].reshape(DEC_BATCH * PAST_LEN, ATT_KV)
    at_l = _latent_attention(sinks, aq, ak, av, ctx_k, ctx_v)

    wo = w_out[0].astype(_BF16)
    x1, h2, h2p, logits_t = _output_projection(xp, xs, dn_c, dn_l, at_c, at_l, mod3, norm2,
                                               wo[:DN_WIDTH], wo[DN_WIDTH:], router_w[0].T)

    idx, rank, gate_t, counts = _route(logits_t, router_bias[0].reshape(N_EXPERTS, 1))
    cnt = counts[:, 0].astype(jnp.int32)
    padded = (cnt + EXPERT_ROWS - 1) // EXPERT_ROWS * EXPERT_ROWS
    pad_end = jnp.cumsum(padded)
    pos = _slots(idx, rank, (pad_end - padded).astype(_F32).reshape(N_EXPERTS, 1))
    n_used = (pad_end[-1] // EXPERT_ROWS).astype(jnp.int32).reshape(1)
    block_start = jnp.arange(N_BLOCKS, dtype=jnp.int32) * EXPERT_ROWS
    block_expert = jnp.minimum(
        jnp.sum((pad_end[None, :] <= block_start[:, None]).astype(jnp.int32), axis=1), N_EXPERTS - 1)

    pad_lo = (pad_end - padded + cnt).astype(jnp.int32)
    sw1, sw3, sw2 = shared_w1[0].astype(_BF16), shared_w3[0].astype(_BF16), shared_w2[0].astype(_BF16)
    x_sorted, shared = _dispatch(pad_lo, pad_end.astype(jnp.int32), n_used, _tile_major(pos, TOK_TILE), h2p,
                                 h2, sw1, sw3, sw2)
    run_first = jnp.concatenate([jnp.ones((1,), jnp.int32),
                                 (block_expert[1:] != block_expert[:-1]).astype(jnp.int32)])
    run_parity = (jnp.cumsum(run_first) - 1) % 2
    experts = jnp.arange(N_EXPERTS, dtype=jnp.int32)
    later = (experts[None, :] > experts[:, None]) & (cnt[None, :] > 0)
    next_expert = jnp.min(jnp.where(later, experts[None, :], N_EXPERTS), axis=1)
    next_expert = jnp.where(next_expert == N_EXPERTS, -1, next_expert)
    run_next = jnp.sum(jnp.where(block_expert[:, None] == experts[None, :], next_expert[None, :], 0), axis=1)
    ys = _experts(block_expert, n_used, run_first, run_next.astype(jnp.int32), run_parity.astype(jnp.int32),
                  x_sorted, expert_w1[0], expert_w3[0], expert_w2[0])

    pos_c = _tile_major(pos, COMBINE_TILE)
    fn = final_norm.reshape(1, D_MODEL)
    y_prompt = _combine(pos_c, gate_t, shared, x1, mod3, fn, ys,
                        n_rows=M_CTX, tile0=0, mod_row=lambda i: 0)
    lat_tiles = DEC_SEQ // COMBINE_TILE
    y_sample = _combine(pos_c, gate_t, shared, x1, mod3, fn, ys,
                        n_rows=M_LAT, tile0=M_CTX // COMBINE_TILE, mod_row=lambda i: 1 + i // lat_tiles)

    new_cache_k = ak[:M_CTX].reshape(BATCH, 1, SEQ, ATT_KV_HEADS, HEAD_DIM)
    new_cache_v = av[:M_CTX].reshape(BATCH, 1, SEQ, ATT_KV_HEADS, HEAD_DIM)
    return (y_prompt.reshape(BATCH, SEQ, D_MODEL), y_sample.reshape(DEC_BATCH, DEC_SEQ, D_MODEL),
            new_cache_k, new_cache_v, new_state.reshape(BATCH, 1, 2, DN_HEADS, HEAD_DIM, HEAD_DIM))
```
